```python
import jax, jax.numpy as jnp
from jax import lax
import numpy as np

D_MODEL = 1024
BATCH = 8
SEQ = 4096
DEPTH = 1

HEAD_DIM = 64
MOBA_HEADS = 8
MOBA_WIDTH = MOBA_HEADS * HEAD_DIM
MOBA_BLOCK = 256
MOBA_TOPK = 3
MOBA_Q_BLOCK = 64
ALIBI_MAX = 8.0
GLA_HEADS = 4
GLA_DK = 64
GLA_DV = 128
GLA_KEY_WIDTH = GLA_HEADS * GLA_DK
GLA_VAL_WIDTH = GLA_HEADS * GLA_DV
GLA_GATE_RANK = 16
GLA_GATE_TAU = 16.0
GLA_CHUNK = 64
MIX_WIDTH = MOBA_WIDTH + GLA_VAL_WIDTH
PROJ_WIDTH = 3 * MOBA_WIDTH + 2 * GLA_KEY_WIDTH + 2 * GLA_VAL_WIDTH + GLA_GATE_RANK
N_EXPERTS = 32
TOP_K = 4
D_EXPERT = D_MODEL
SWIGLU_ALPHA = 1.702
SWIGLU_LIMIT = 7.0
MOE_ROW_BLOCK = 128
PLE_DIM = 256
EPS = 1e-6

kernel_name = 'hybrid_moba_gla_moe_ple'


def rms_norm(x, g):
    xf = x.astype(jnp.float32)
    y = xf * lax.rsqrt(jnp.mean(xf * xf, axis=-1, keepdims=True) + EPS)
    return (y * g.astype(jnp.float32)).astype(x.dtype)


def moba_attention(q, k, v):
    B, H, S, Dh = q.shape
    f32 = jnp.float32
    nb = -(-S // MOBA_BLOCK)
    pad = nb * MOBA_BLOCK - S
    kb = jnp.pad(k, ((0, 0), (0, 0), (0, pad), (0, 0))).reshape(B, H, nb, MOBA_BLOCK, Dh)
    vb = jnp.pad(v, ((0, 0), (0, 0), (0, pad), (0, 0))).reshape(B, H, nb, MOBA_BLOCK, Dh)
    k_mean = jnp.mean(kb.astype(f32), axis=3)
    slopes = jnp.exp2(-ALIBI_MAX * jnp.arange(1, H + 1, dtype=f32) / H)[None, :, None, None]
    scale = Dh ** -0.5
    n_sel = min(MOBA_TOPK, nb - 1)
    nq = S // MOBA_Q_BLOCK
    q_blocks = q.reshape(B, H, nq, MOBA_Q_BLOCK, Dh).transpose(2, 0, 1, 3, 4)
    b_idx = jnp.arange(B)[:, None, None]
    h_idx = jnp.arange(H)[None, :, None]
    offs = jnp.arange(MOBA_BLOCK)

    def attend_block(args):
        qb, c = args
        qf = qb.astype(f32) * scale
        q_pos = c * MOBA_Q_BLOCK + jnp.arange(MOBA_Q_BLOCK)
        own = (c * MOBA_Q_BLOCK) // MOBA_BLOCK
        scores = []
        sel = None
        if n_sel > 0:
            gate = jnp.einsum('bhqd,bhnd->bhqn', qf, k_mean)
            gate = jnp.where(jnp.arange(nb) < own, gate, -jnp.inf)
            _, sel = lax.top_k(gate, n_sel)
            for j in range(n_sel):
                idx = sel[..., j]
                kj = kb[b_idx, h_idx, idx].astype(f32)
                s = jnp.einsum('bhqd,bhqkd->bhqk', qf, kj)
                dist = (q_pos[:, None] - (idx[..., None] * MOBA_BLOCK + offs)).astype(f32)
                s = jnp.where(j < own, s - slopes * dist, -jnp.inf)
                scores.append(s)
        ko = lax.dynamic_index_in_dim(kb, own, axis=2, keepdims=False).astype(f32)
        vo = lax.dynamic_index_in_dim(vb, own, axis=2, keepdims=False).astype(f32)
        dist = (q_pos[:, None] - (own * MOBA_BLOCK + offs)[None, :]).astype(f32)
        s = jnp.einsum('bhqd,bhkd->bhqk', qf, ko)
        s = jnp.where(dist >= 0, s - slopes * dist, -jnp.inf)
        scores.append(s)
        probs = jax.nn.softmax(jnp.concatenate(scores, axis=-1), axis=-1)
        probs = probs.reshape(B, H, MOBA_Q_BLOCK, n_sel + 1, MOBA_BLOCK)
        out = jnp.einsum('bhqk,bhkd->bhqd', probs[:, :, :, n_sel], vo)
        for j in range(n_sel):
            vj = vb[b_idx, h_idx, sel[..., j]].astype(f32)
            out = out + jnp.einsum('bhqk,bhqkd->bhqd', probs[:, :, :, j], vj)
        return out.astype(q.dtype)

    out = lax.map(attend_block, (q_blocks, jnp.arange(nq)))
    return out.transpose(1, 2, 0, 3, 4).reshape(B, H, S, Dh)


def gla_chunked(q, k, v, log_a):
    B, H, S, DK = q.shape
    DV = v.shape[-1]
    f32 = jnp.float32
    C = GLA_CHUNK
    nc = S // C
    qf = q.astype(f32).reshape(B, H, nc, C, DK) * DK ** -0.5
    kf = k.astype(f32).reshape(B, H, nc, C, DK)
    vf = v.astype(f32).reshape(B, H, nc, C, DV)
    G = jnp.cumsum(log_a.astype(f32).reshape(B, H, nc, C, DK), axis=3)
    G_last = G[:, :, :, -1:, :]
    q_dec = qf * jnp.exp(G)
    k_dec = kf * jnp.exp(-G)
    k_tail = kf * jnp.exp(G_last - G)
    causal = jnp.tril(jnp.ones((C, C), dtype=bool))
    A = jnp.where(causal, jnp.einsum('bhnqd,bhnkd->bhnqk', q_dec, k_dec), 0.0)
    o_intra = jnp.einsum('bhnqk,bhnkv->bhnqv', A, vf)
    kv = jnp.einsum('bhnkd,bhnkv->bhndv', k_tail, vf)
    decay = jnp.exp(G_last[:, :, :, 0, :])

    def step(state, inp):
        kv_c, d_c = inp
        return d_c[..., None] * state + kv_c, state

    _, s_prev = lax.scan(step, jnp.zeros((B, H, DK, DV), f32),
                         (kv.transpose(2, 0, 1, 3, 4), decay.transpose(2, 0, 1, 3)))
    s_prev = s_prev.transpose(1, 2, 0, 3, 4)
    o_inter = jnp.einsum('bhnqd,bhndv->bhnqv', q_dec, s_prev)
    return (o_intra + o_inter).reshape(B, H, S, DV).astype(v.dtype)


def clamped_swiglu(hid):
    x_glu = jnp.minimum(hid[..., ::2], SWIGLU_LIMIT)
    x_lin = jnp.clip(hid[..., 1::2], -SWIGLU_LIMIT, SWIGLU_LIMIT)
    return x_glu * jax.nn.sigmoid(SWIGLU_ALPHA * x_glu) * (x_lin + 1.0)


def moe_ffn(h, w_router, b_router, w1, b1, w2, b2):
    T, D = h.shape
    M = MOE_ROW_BLOCK
    logits = (h @ w_router + b_router).astype(jnp.float32)
    top_vals, top_idx = lax.top_k(logits, TOP_K)
    gates = jax.nn.softmax(top_vals, axis=-1).astype(h.dtype)
    n_assign = T * TOP_K
    flat_e = top_idx.reshape(-1)
    order = jnp.argsort(flat_e, stable=True)
    sorted_e = flat_e[order]
    sorted_tok = order // TOP_K
    sorted_gate = gates.reshape(-1)[order]
    counts = jnp.bincount(flat_e, length=N_EXPERTS)
    padded = (counts + M - 1) // M * M
    grp_start = jnp.cumsum(counts) - counts
    pad_end = jnp.cumsum(padded)
    pad_start = pad_end - padded
    dest = pad_start[sorted_e] + jnp.arange(n_assign) - grp_start[sorted_e]
    n_blocks = -(-n_assign // M) + N_EXPERTS
    n_rows = n_blocks * M
    row_tok = jnp.zeros((n_rows,), jnp.int32).at[dest].set(sorted_tok.astype(jnp.int32))
    row_gate = jnp.zeros((n_rows,), h.dtype).at[dest].set(sorted_gate)
    block_expert = jnp.minimum(jnp.searchsorted(pad_end, jnp.arange(n_blocks) * M, side='right'),
                               N_EXPERTS - 1)
    x_rows = h[row_tok].reshape(n_blocks, M, D)

    def expert_block(args):
        xb, e = args
        hid = xb @ w1[e] + b1[e]
        return clamped_swiglu(hid) @ w2[e] + b2[e]

    y = lax.map(expert_block, (x_rows, block_expert)).reshape(n_rows, D)
    return jnp.zeros_like(h).at[row_tok].add(y * row_gate[:, None])


def setup_inputs(seed: int = 0) -> dict:
    key = jax.random.key(seed)
    ks = jax.random.split(key, 19)
    L = DEPTH

    def nrm(k, shape, scale):
        return jax.random.normal(k, shape, jnp.float32) * scale

    return {
        'x': nrm(ks[0], (BATCH, SEQ, D_MODEL), 1.0),
        'p': nrm(ks[1], (DEPTH, BATCH, SEQ, PLE_DIM), 1.0),
        'g_mix': 1.0 + nrm(ks[2], (L, D_MODEL), 0.01),
        'w_in': nrm(ks[3], (L, D_MODEL, PROJ_WIDTH), D_MODEL ** -0.5),
        'w_gla_gate': nrm(ks[4], (L, GLA_GATE_RANK, GLA_KEY_WIDTH), GLA_GATE_RANK ** -0.5),
        'b_gla_gate': nrm(ks[5], (L, GLA_KEY_WIDTH), 0.1),
        'g_gla_norm': 1.0 + nrm(ks[6], (L, GLA_DV), 0.01),
        'w_out': nrm(ks[7], (L, MIX_WIDTH, D_MODEL), MIX_WIDTH ** -0.5),
        'g_ffn': 1.0 + nrm(ks[8], (L, D_MODEL), 0.01),
        'w_router': nrm(ks[9], (L, D_MODEL, N_EXPERTS), D_MODEL ** -0.5),
        'b_router': nrm(ks[10], (L, N_EXPERTS), 0.01),
        'w1': nrm(ks[11], (L, N_EXPERTS, D_MODEL, 2 * D_EXPERT), D_MODEL ** -0.5),
        'b1': nrm(ks[12], (L, N_EXPERTS, 2 * D_EXPERT), 0.01),
        'w2': nrm(ks[13], (L, N_EXPERTS, D_EXPERT, D_MODEL), D_EXPERT ** -0.5),
        'b2': nrm(ks[14], (L, N_EXPERTS, D_MODEL), 0.01),
        'w_ple_gate': nrm(ks[15], (L, D_MODEL, D_MODEL), D_MODEL ** -0.5),
        'w_ple_proj': nrm(ks[16], (L, PLE_DIM, D_MODEL), PLE_DIM ** -0.5),
        'g_ple': 1.0 + nrm(ks[17], (L, D_MODEL), 0.01),
        'g_final': 1.0 + nrm(ks[18], (D_MODEL,), 0.01),
    }


def reference(x, p, g_mix, w_in, w_gla_gate, b_gla_gate, g_gla_norm, w_out, g_ffn,
              w_router, b_router, w1, b1, w2, b2, w_ple_gate, w_ple_proj, g_ple, g_final):
    B, S, D = x.shape
    splits = list(np.cumsum([MOBA_WIDTH, MOBA_WIDTH, MOBA_WIDTH, GLA_KEY_WIDTH, GLA_KEY_WIDTH,
                             GLA_VAL_WIDTH, GLA_VAL_WIDTH])[:])
    splits = [int(s) for s in splits]
    h = x
    for i in range(DEPTH):
        a = rms_norm(h, g_mix[i])
        proj = a @ w_in[i]
        qa, ka, va, qg, kg, vg, rg, ag = jnp.split(proj, splits, axis=-1)
        to_heads = lambda t, nh: t.reshape(B, S, nh, -1).transpose(0, 2, 1, 3)
        o_moba = moba_attention(to_heads(qa, MOBA_HEADS), to_heads(ka, MOBA_HEADS),
                                to_heads(va, MOBA_HEADS))
        o_moba = o_moba.transpose(0, 2, 1, 3).reshape(B, S, MOBA_WIDTH)
        log_a = jax.nn.log_sigmoid((ag @ w_gla_gate[i] + b_gla_gate[i]).astype(jnp.float32)) / GLA_GATE_TAU
        o_gla = gla_chunked(to_heads(qg, GLA_HEADS), to_heads(kg, GLA_HEADS),
                            to_heads(vg, GLA_HEADS), to_heads(log_a, GLA_HEADS))
        o_gla = rms_norm(o_gla, g_gla_norm[i]).transpose(0, 2, 1, 3).reshape(B, S, GLA_VAL_WIDTH)
        o_gla = o_gla * jax.nn.silu(rg)
        h = h + jnp.concatenate([o_moba, o_gla], axis=-1) @ w_out[i]
        f = rms_norm(h, g_ffn[i]).reshape(B * S, D)
        h = h + moe_ffn(f, w_router[i], b_router[i], w1[i], b1[i], w2[i], b2[i]).reshape(B, S, D)
        ple = rms_norm(p[i] @ w_ple_proj[i], g_ple[i])
        h = h + jax.nn.sigmoid(h @ w_ple_gate[i]) * ple
    return rms_norm(h, g_final)
```

```python
import functools

import jax
import jax.numpy as jnp
from jax import lax
from jax.experimental import pallas as pl
from jax.experimental.pallas import tpu as pltpu

HEAD_DIM = 64
MOBA_HEADS = 8
MOBA_WIDTH = MOBA_HEADS * HEAD_DIM
MOBA_BLOCK = 256
MOBA_TOPK = 3
ALIBI_MAX = 8.0
GLA_HEADS = 4
GLA_DK = 64
GLA_DV = 128
GLA_KEY_WIDTH = GLA_HEADS * GLA_DK
GLA_VAL_WIDTH = GLA_HEADS * GLA_DV
GLA_GATE_RANK = 16
GLA_GATE_TAU = 16.0
GLA_CHUNK = 64
N_EXPERTS = 32
TOP_K = 4
SWIGLU_ALPHA = 1.702
SWIGLU_LIMIT = 7.0
EPS = 1e-6

LANES = 128
MASKED = -1e30
VMEM_LIMIT = 56 * 1024 * 1024

BF16 = jnp.bfloat16
F32 = jnp.float32

_NT = (((1,), (1,)), ((), ()))
_TN = (((0,), (0,)), ((), ()))


def _dot(a, b, dims=None, precision=None):
    if dims is None:
        return jnp.dot(a, b, preferred_element_type=F32, precision=precision)
    return lax.dot_general(a, b, dims, preferred_element_type=F32, precision=precision)


def _rms(x, g):
    return x * lax.rsqrt(jnp.mean(x * x, axis=-1, keepdims=True) + EPS) * g


def _params(*sem):
    return pltpu.CompilerParams(dimension_semantics=sem, vmem_limit_bytes=VMEM_LIMIT)


def _inproj_kernel(x_ref, g_ref, wq_ref, wk_ref, wvt_ref, wgla_ref, wag_ref, wgate_ref, bgate_ref,
                   qa_ref, ka_ref, vt_ref, km_ref, qg_ref, kg_ref, vg_ref, rg_ref, la_ref):
    i = pl.program_id(1)
    tm = x_ref.shape[0]
    nblk = tm // MOBA_BLOCK
    ab = _rms(x_ref[...], g_ref[...]).astype(BF16)
    qa_ref[...] = (_dot(ab, wq_ref[...]) * (HEAD_DIM ** -0.5)).astype(BF16)
    ka = _dot(ab, wk_ref[...])
    ka_ref[...] = ka.astype(BF16)
    @pl.when(i == 0)
    def _():
        km_ref[...] = jnp.zeros_like(km_ref)

    km = km_ref[...]
    blk_row = lax.broadcasted_iota(jnp.int32, km.shape, 0)
    for j in range(nblk):
        mean_j = jnp.mean(ka[j * MOBA_BLOCK:(j + 1) * MOBA_BLOCK], axis=0, keepdims=True)
        km = jnp.where(blk_row == i * nblk + j, mean_j, km)
    km_ref[...] = km
    vt = _dot(wvt_ref[...], ab, _NT).astype(BF16)
    for j in range(nblk):
        vt_ref[j] = vt[:, j * MOBA_BLOCK:(j + 1) * MOBA_BLOCK]
    gla = _dot(ab, wgla_ref[...])
    o = 0
    for ref in (qg_ref, kg_ref, vg_ref, rg_ref):
        w = ref.shape[1]
        ref[...] = gla[:, o:o + w]
        o += w
    ag = _dot(ab, wag_ref[...]).astype(BF16)
    z = _dot(ag, wgate_ref[...]) + bgate_ref[...]
    la_ref[...] = jax.nn.log_sigmoid(z) / GLA_GATE_TAU


def _inproj(x2d, g_mix, w_in, w_gla_gate, b_gla_gate, *, batch, seq, tm=512):
    t, d = x2d.shape
    nb = seq // MOBA_BLOCK
    o_k, o_v, o_qg = MOBA_WIDTH, 2 * MOBA_WIDTH, 3 * MOBA_WIDTH
    o_ag = o_qg + 2 * GLA_KEY_WIDTH + 2 * GLA_VAL_WIDTH
    wq = w_in[:, :o_k].astype(BF16)
    wk = w_in[:, o_k:o_v].astype(BF16)
    wvt = w_in[:, o_v:o_qg].T.astype(BF16)
    wgla = w_in[:, o_qg:o_ag].astype(BF16)
    wag = jnp.pad(w_in[:, o_ag:], ((0, 0), (0, LANES - GLA_GATE_RANK))).astype(BF16)
    wgate = jnp.pad(w_gla_gate, ((0, LANES - GLA_GATE_RANK), (0, 0))).astype(BF16)
    steps = seq // tm
    row = lambda b, i: (b * steps + i, 0)
    const = lambda b, i: (0, 0)
    full = lambda a: pl.BlockSpec(a.shape, const)
    args = (x2d, g_mix.reshape(1, d), wq, wk, wvt, wgla, wag, wgate, b_gla_gate.reshape(1, -1))
    in_specs = [pl.BlockSpec((tm, d), row)] + [full(a) for a in args[1:]]
    out_shape = (
        jax.ShapeDtypeStruct((t, MOBA_WIDTH), BF16),
        jax.ShapeDtypeStruct((t, MOBA_WIDTH), BF16),
        jax.ShapeDtypeStruct((t // MOBA_BLOCK, MOBA_WIDTH, MOBA_BLOCK), BF16),
        jax.ShapeDtypeStruct((batch * nb, MOBA_WIDTH), F32),
        jax.ShapeDtypeStruct((t, GLA_KEY_WIDTH), F32),
        jax.ShapeDtypeStruct((t, GLA_KEY_WIDTH), F32),
        jax.ShapeDtypeStruct((t, GLA_VAL_WIDTH), F32),
        jax.ShapeDtypeStruct((t, GLA_VAL_WIDTH), F32),
        jax.ShapeDtypeStruct((t, GLA_KEY_WIDTH), F32),
    )
    out_specs = (
        pl.BlockSpec((tm, MOBA_WIDTH), row),
        pl.BlockSpec((tm, MOBA_WIDTH), row),
        pl.BlockSpec((tm // MOBA_BLOCK, MOBA_WIDTH, MOBA_BLOCK), lambda b, i: (b * steps + i, 0, 0)),
        pl.BlockSpec((nb, MOBA_WIDTH), lambda b, i: (b, 0)),
        pl.BlockSpec((tm, GLA_KEY_WIDTH), row),
        pl.BlockSpec((tm, GLA_KEY_WIDTH), row),
        pl.BlockSpec((tm, GLA_VAL_WIDTH), row),
        pl.BlockSpec((tm, GLA_VAL_WIDTH), row),
        pl.BlockSpec((tm, GLA_KEY_WIDTH), row),
    )
    return pl.pallas_call(
        _inproj_kernel, grid=(batch, steps), in_specs=in_specs, out_specs=out_specs, out_shape=out_shape,
        compiler_params=_params("arbitrary", "arbitrary"), name="inproj",
    )(*args)


def _moba_kernel(slopes_ref, q_ref, k_ref, vt_ref, km_ref, o_ref, bias_ref):
    hp = pl.program_id(1)
    c = pl.program_id(2)
    blk = MOBA_BLOCK
    nb = km_ref.shape[0]
    q2 = q_ref[...]
    lane = lax.broadcasted_iota(jnp.int32, (1, 2 * HEAD_DIM), 1)
    kidx = lax.broadcasted_iota(jnp.int32, (blk, blk), 0)
    qidx = lax.broadcasted_iota(jnp.int32, (blk, blk), 1)
    rel = (kidx - qidx).astype(F32)
    causal = kidx <= qidx
    nidx = lax.broadcasted_iota(jnp.int32, (nb, blk), 0)
    kmb = km_ref[...].astype(BF16)
    k_own = k_ref[pl.ds(pl.multiple_of(c * blk, blk), blk), :]
    vt_own = vt_ref[c]

    qms, slopes, state = [], [], []
    for hh in range(2):
        slope = slopes_ref[hp * 2 + hh]
        qm = jnp.where(lane // HEAD_DIM == hh, q2, jnp.zeros_like(q2))
        gt = _dot(kmb, qm, _NT)
        valid = nidx < c
        g = jnp.where(valid, gt, -jnp.inf)
        rank = jnp.zeros((nb, blk), jnp.int32)
        for m in range(nb - 1):
            gm = g[m:m + 1, :]
            beats = (gm > g) | ((gm == g) & (m < nidx))
            rank = rank + beats.astype(jnp.int32)
        sel = valid & (rank < MOBA_TOPK)
        dist0 = ((c - nidx) * blk).astype(F32)
        bias_ref[hh] = jnp.where(sel, -slope * dist0, MASKED)
        s = _dot(k_own, qm, _NT) + slope * rel
        s = jnp.where(causal, s, MASKED)
        m0 = jnp.max(s, axis=0, keepdims=True)
        p = jnp.exp(s - m0)
        l0 = jnp.sum(p, axis=0, keepdims=True)
        acc0 = _dot(vt_own[hh * HEAD_DIM:(hh + 1) * HEAD_DIM, :], p.astype(BF16))
        qms.append(qm)
        slopes.append(slope)
        state += [m0, l0, acc0]

    def body(j, carry):
        kj = k_ref[pl.ds(pl.multiple_of(j * blk, blk), blk), :]
        vtj = vt_ref[j]
        out = []
        for hh in range(2):
            m_i, l_i, acc = carry[3 * hh:3 * hh + 3]
            s = _dot(kj, qms[hh], _NT) + slopes[hh] * rel + bias_ref[hh, pl.ds(j, 1), :]
            m_new = jnp.maximum(m_i, jnp.max(s, axis=0, keepdims=True))
            alpha = jnp.exp(m_i - m_new)
            p = jnp.exp(s - m_new)
            l_new = alpha * l_i + jnp.sum(p, axis=0, keepdims=True)
            acc = alpha * acc + _dot(vtj[hh * HEAD_DIM:(hh + 1) * HEAD_DIM, :], p.astype(BF16))
            out += [m_new, l_new, acc]
        return tuple(out)

    state = lax.fori_loop(0, c, body, tuple(state))
    out_t = jnp.concatenate([state[2] / state[1], state[5] / state[4]], axis=0)
    o_ref[...] = out_t.T.astype(o_ref.dtype)


def _moba(qa, ka, vt, kmean, slopes, *, batch, seq):
    t = qa.shape[0]
    nb = seq // MOBA_BLOCK
    hw = 2 * HEAD_DIM
    grid_spec = pltpu.PrefetchScalarGridSpec(
        num_scalar_prefetch=1,
        grid=(batch, MOBA_HEADS // 2, nb),
        in_specs=[
            pl.BlockSpec((MOBA_BLOCK, hw), lambda b, h, c, s: (b * nb + c, h)),
            pl.BlockSpec((seq, hw), lambda b, h, c, s: (b, h)),
            pl.BlockSpec((nb, hw, MOBA_BLOCK), lambda b, h, c, s: (b, h, 0)),
            pl.BlockSpec((nb, hw), lambda b, h, c, s: (b, h)),
        ],
        out_specs=pl.BlockSpec((MOBA_BLOCK, hw), lambda b, h, c, s: (b * nb + c, h)),
        scratch_shapes=[pltpu.VMEM((2, nb, MOBA_BLOCK), F32)],
    )
    return pl.pallas_call(
        _moba_kernel, grid_spec=grid_spec, out_shape=jax.ShapeDtypeStruct((t, MOBA_WIDTH), BF16),
        compiler_params=_params("arbitrary", "arbitrary", "arbitrary"), name="moba",
    )(slopes, qa, ka, vt, kmean)


def _gla_kernel(q_ref, k_ref, v_ref, r_ref, la_ref, gn_ref, o_ref, st_ref):
    @pl.when(pl.program_id(1) == 0)
    def _():
        st_ref[...] = jnp.zeros_like(st_ref)

    ch = GLA_CHUNK
    n_chunks = q_ref.shape[0] // ch
    row = lax.broadcasted_iota(jnp.int32, (ch, ch), 0)
    col = lax.broadcasted_iota(jnp.int32, (ch, ch), 1)
    tri = (row >= col).astype(F32)
    causal = row >= col
    lane = lax.broadcasted_iota(jnp.int32, (1, GLA_KEY_WIDTH), 1)
    gn = gn_ref[...]
    state = st_ref[...]
    for ci in range(n_chunks):
        rows = slice(ci * ch, (ci + 1) * ch)
        la = la_ref[rows, :]
        g = _dot(tri, la, precision=lax.Precision.HIGHEST)
        g_last = g[ch - 1:ch, :]
        qd = q_ref[rows, :] * (GLA_DK ** -0.5) * jnp.exp(g)
        k = k_ref[rows, :]
        kd = (k * jnp.exp(-g)).astype(BF16)
        kt = k * jnp.exp(g_last - g)
        state_b = state.astype(BF16)
        new_state = state * jnp.exp(g_last)
        for h in range(GLA_HEADS):
            hm = lane // GLA_DK == h
            qh = jnp.where(hm, qd, 0.0).astype(BF16)
            kth = jnp.where(hm, kt, 0.0).astype(BF16)
            vh = v_ref[rows, h * GLA_DV:(h + 1) * GLA_DV].astype(BF16)
            a = jnp.where(causal, _dot(qh, kd, _NT), 0.0)
            o = _dot(a.astype(BF16), vh) + _dot(qh, state_b, _NT)
            new_state = new_state + _dot(vh, kth, _TN)
            rh = r_ref[rows, h * GLA_DV:(h + 1) * GLA_DV]
            o_ref[rows, h * GLA_DV:(h + 1) * GLA_DV] = (_rms(o, gn) * (rh * jax.nn.sigmoid(rh))).astype(o_ref.dtype)
        state = new_state
    st_ref[...] = state


def _gla(qg, kg, vg, rg, la, g_norm, *, batch, seq, tc=256):
    t = qg.shape[0]
    steps = seq // tc
    row = lambda b, i: (b * steps + i, 0)
    kspec = pl.BlockSpec((tc, GLA_KEY_WIDTH), row)
    vspec = pl.BlockSpec((tc, GLA_VAL_WIDTH), row)
    return pl.pallas_call(
        _gla_kernel, grid=(batch, steps),
        in_specs=[kspec, kspec, vspec, vspec, kspec, pl.BlockSpec((1, GLA_DV), lambda b, i: (0, 0))],
        out_specs=vspec, out_shape=jax.ShapeDtypeStruct((t, GLA_VAL_WIDTH), BF16),
        scratch_shapes=[pltpu.VMEM((GLA_DV, GLA_KEY_WIDTH), F32)],
        compiler_params=_params("arbitrary", "arbitrary"), name="gla",
    )(qg, kg, vg, rg, la, g_norm.reshape(1, GLA_DV))


def _outproj_kernel(x_ref, om_ref, og_ref, wm_ref, wg_ref, gf_ref, wr_ref, br_ref, h_ref, f_ref, lg_ref):
    h = x_ref[...] + _dot(om_ref[...], wm_ref[...]) + _dot(og_ref[...], wg_ref[...])
    h_ref[...] = h
    f = _rms(h, gf_ref[...]).astype(BF16)
    f_ref[...] = f
    lg_ref[...] = _dot(f, wr_ref[...]) + br_ref[...]


def _outproj(x2d, o_moba, o_gla, w_out, g_ffn, w_router, b_router, *, tm=512):
    t, d = x2d.shape
    wm = w_out[:MOBA_WIDTH].astype(BF16)
    wg = w_out[MOBA_WIDTH:].astype(BF16)
    wr = jnp.pad(w_router, ((0, 0), (0, LANES - N_EXPERTS))).astype(BF16)
    br = jnp.pad(b_router, (0, LANES - N_EXPERTS)).reshape(1, LANES)
    row = lambda i: (i, 0)
    const = lambda i: (0, 0)
    full = lambda a: pl.BlockSpec(a.shape, const)
    args = (x2d, o_moba, o_gla, wm, wg, g_ffn.reshape(1, d), wr, br)
    in_specs = [pl.BlockSpec((tm, d), row), pl.BlockSpec((tm, MOBA_WIDTH), row),
                pl.BlockSpec((tm, GLA_VAL_WIDTH), row)] + [full(a) for a in args[3:]]
    return pl.pallas_call(
        _outproj_kernel, grid=(t // tm,), in_specs=in_specs,
        out_specs=(pl.BlockSpec((tm, d), row), pl.BlockSpec((tm, d), row), pl.BlockSpec((tm, LANES), row)),
        out_shape=(jax.ShapeDtypeStruct((t, d), F32), jax.ShapeDtypeStruct((t, d), BF16),
                   jax.ShapeDtypeStruct((t, LANES), F32)),
        compiler_params=_params("arbitrary"), name="outproj",
    )(*args)


def _moe_kernel(be_ref, nused_ref, x_ref, w1_ref, b1_ref, w2_ref, b2_ref, o_ref):
    i = pl.program_id(0)

    @pl.when(i < nused_ref[0])
    def _():
        f = w2_ref.shape[1]
        hid = _dot(x_ref[...], w1_ref[0]) + b1_ref[0]
        glu = jnp.minimum(hid[:, :f], SWIGLU_LIMIT)
        lin = jnp.clip(hid[:, f:], -SWIGLU_LIMIT, SWIGLU_LIMIT)
        act = glu * jax.nn.sigmoid(SWIGLU_ALPHA * glu) * (lin + 1.0)
        o_ref[...] = (_dot(act.astype(BF16), w2_ref[0]) + b2_ref[0]).astype(o_ref.dtype)

    @pl.when(i >= nused_ref[0])
    def _():
        o_ref[...] = jnp.zeros_like(o_ref)


def _moe(x_rows, block_expert, n_used, w1, b1, w2, b2, *, tm):
    n_rows, d = x_rows.shape
    e, _, f2 = w1.shape
    f = f2 // 2
    w1p = jnp.concatenate([w1[:, :, 0::2], w1[:, :, 1::2]], axis=-1).astype(BF16)
    b1p = jnp.concatenate([b1[:, 0::2], b1[:, 1::2]], axis=-1).reshape(e, 1, f2)
    w2b = w2.astype(BF16)
    b2r = b2.reshape(e, 1, d)
    grid_spec = pltpu.PrefetchScalarGridSpec(
        num_scalar_prefetch=2,
        grid=(n_rows // tm,),
        in_specs=[
            pl.BlockSpec((tm, d), lambda i, be, nu: (i, 0)),
            pl.BlockSpec((1, d, f2), lambda i, be, nu: (be[i], 0, 0)),
            pl.BlockSpec((1, 1, f2), lambda i, be, nu: (be[i], 0, 0)),
            pl.BlockSpec((1, f, d), lambda i, be, nu: (be[i], 0, 0)),
            pl.BlockSpec((1, 1, d), lambda i, be, nu: (be[i], 0, 0)),
        ],
        out_specs=pl.BlockSpec((tm, d), lambda i, be, nu: (i, 0)),
    )
    return pl.pallas_call(
        _moe_kernel, grid_spec=grid_spec, out_shape=jax.ShapeDtypeStruct((n_rows, d), BF16),
        compiler_params=_params("arbitrary"), name="moe",
    )(block_expert, n_used, x_rows, w1p, b1p, w2b, b2r)


def _route(logits, *, tm):
    t = logits.shape[0]
    top_vals, top_idx = lax.top_k(logits, TOP_K)
    gates = jax.nn.softmax(top_vals, axis=-1)
    n_assign = t * TOP_K
    flat_e = top_idx.reshape(-1)
    order = jnp.argsort(flat_e, stable=True)
    sorted_e = flat_e[order]
    counts = jnp.sum(flat_e[:, None] == jnp.arange(N_EXPERTS)[None, :], axis=0, dtype=jnp.int32)
    padded = (counts + tm - 1) // tm * tm
    pad_end = jnp.cumsum(padded)
    pad_start = pad_end - padded
    grp_start = jnp.cumsum(counts) - counts
    dest = pad_start[sorted_e] + jnp.arange(n_assign, dtype=jnp.int32) - grp_start[sorted_e]
    n_blocks = -(-n_assign // tm) + N_EXPERTS
    row_tok = jnp.zeros((n_blocks * tm,), jnp.int32).at[dest].set((order // TOP_K).astype(jnp.int32))
    slot = jnp.zeros((n_assign,), jnp.int32).at[order].set(dest.astype(jnp.int32))
    n_used = (pad_end[-1] // tm).astype(jnp.int32)
    blk_start = jnp.minimum(jnp.arange(n_blocks, dtype=jnp.int32), n_used - 1) * tm
    block_expert = jnp.minimum(jnp.searchsorted(pad_end, blk_start, side="right"), N_EXPERTS - 1).astype(jnp.int32)
    return gates, row_tok, slot.reshape(t, TOP_K), block_expert, n_used.reshape(1)


def _final_kernel(h_ref, y_ref, gate_ref, p_ref, wp_ref, gp_ref, wgate_ref, gfin_ref, o_ref, *, last_layer):
    h = h_ref[...]
    gates = gate_ref[...]
    for k in range(TOP_K):
        h = h + gates[:, k:k + 1] * y_ref[k].astype(F32)
    ple = _rms(_dot(p_ref[...].astype(BF16), wp_ref[...]), gp_ref[...])
    h = h + jax.nn.sigmoid(_dot(h.astype(BF16), wgate_ref[...])) * ple
    o_ref[...] = _rms(h, gfin_ref[...]) if last_layer else h


def _final(h1, y_tok, gates, p2d, w_ple_proj, g_ple, w_ple_gate, g_final, *, last_layer, tm=512):
    t, d = h1.shape
    row = lambda i: (i, 0)
    const = lambda i: (0, 0)
    full = lambda a: pl.BlockSpec(a.shape, const)
    args = (h1, y_tok, gates, p2d, w_ple_proj.astype(BF16), g_ple.reshape(1, d), w_ple_gate.astype(BF16),
            g_final.reshape(1, d))
    in_specs = [pl.BlockSpec((tm, d), row), pl.BlockSpec((TOP_K, tm, d), lambda i: (0, i, 0)),
                pl.BlockSpec((tm, TOP_K), row), pl.BlockSpec((tm, p2d.shape[1]), row)] + [full(a) for a in args[4:]]
    return pl.pallas_call(
        functools.partial(_final_kernel, last_layer=last_layer), grid=(t // tm,), in_specs=in_specs,
        out_specs=pl.BlockSpec((tm, d), row),
        out_shape=jax.ShapeDtypeStruct((t, d), F32),
        compiler_params=_params("arbitrary"), name="final",
    )(*args)


def kernel(x, p, g_mix, w_in, w_gla_gate, b_gla_gate, g_gla_norm, w_out, g_ffn, w_router, b_router,
           w1, b1, w2, b2, w_ple_gate, w_ple_proj, g_ple, g_final, *, moe_tm=256):
    batch, seq, d = x.shape
    depth = p.shape[0]
    t = batch * seq
    slopes = jnp.exp2(-ALIBI_MAX * jnp.arange(1, MOBA_HEADS + 1, dtype=F32) / MOBA_HEADS)
    h = x.reshape(t, d)
    for i in range(depth):
        qa, ka, vt, kmean, qg, kg, vg, rg, la = _inproj(h, g_mix[i], w_in[i], w_gla_gate[i], b_gla_gate[i],
                                                        batch=batch, seq=seq)
        o_moba = _moba(qa, ka, vt, kmean, slopes, batch=batch, seq=seq)
        o_gla = _gla(qg, kg, vg, rg, la, g_gla_norm[i], batch=batch, seq=seq)
        h1, f, logits = _outproj(h, o_moba, o_gla, w_out[i], g_ffn[i], w_router[i], b_router[i])
        gates, row_tok, slot, block_expert, n_used = _route(logits[:, :N_EXPERTS], tm=moe_tm)
        y = _moe(f[row_tok], block_expert, n_used, w1[i], b1[i], w2[i], b2[i], tm=moe_tm)
        y_tok = y[slot.T]
        h = _final(h1, y_tok, gates, p[i].reshape(t, -1), w_ple_proj[i], g_ple[i], w_ple_gate[i],
                   g_final, last_layer=i == depth - 1)
    return h.reshape(batch, seq, d)
```

```python
import functools

import jax
import jax.numpy as jnp
from jax import lax
from jax.experimental import pallas as pl
from jax.experimental.pallas import tpu as pltpu

HEAD_DIM = 64
MOBA_HEADS = 8
MOBA_WIDTH = MOBA_HEADS * HEAD_DIM
MOBA_BLOCK = 256
MOBA_TOPK = 3
ALIBI_MAX = 8.0
GLA_HEADS = 4
GLA_DK = 64
GLA_DV = 128
GLA_KEY_WIDTH = GLA_HEADS * GLA_DK
GLA_VAL_WIDTH = GLA_HEADS * GLA_DV
GLA_GATE_RANK = 16
GLA_GATE_TAU = 16.0
GLA_CHUNK = 64
N_EXPERTS = 32
TOP_K = 4
SWIGLU_ALPHA = 1.702
SWIGLU_LIMIT = 7.0
EPS = 1e-6

LANES = 128
MASKED = -1e30
VMEM_LIMIT = 56 * 1024 * 1024

BF16 = jnp.bfloat16
F32 = jnp.float32

_NT = (((1,), (1,)), ((), ()))
_TN = (((0,), (0,)), ((), ()))


def _dot(a, b, dims=None, precision=None):
    if dims is None:
        return jnp.dot(a, b, preferred_element_type=F32, precision=precision)
    return lax.dot_general(a, b, dims, preferred_element_type=F32, precision=precision)


def _rms(x, g):
    return x * lax.rsqrt(jnp.mean(x * x, axis=-1, keepdims=True) + EPS) * g


def _params(*sem):
    return pltpu.CompilerParams(dimension_semantics=sem, vmem_limit_bytes=VMEM_LIMIT)


def _inproj_kernel(x_ref, g_ref, wq_ref, wk_ref, wvt_ref, wgla_ref, wag_ref, wgate_ref, bgate_ref,
                   qa_ref, ka_ref, vt_ref, km_ref, qg_ref, kg_ref, vg_ref, rg_ref, la_ref):
    i = pl.program_id(1)
    tm = x_ref.shape[0]
    nblk = tm // MOBA_BLOCK
    ab = _rms(x_ref[...], g_ref[...]).astype(BF16)
    qa_ref[...] = (_dot(ab, wq_ref[...]) * (HEAD_DIM ** -0.5)).astype(BF16)
    ka = _dot(ab, wk_ref[...])
    ka_ref[...] = ka.astype(BF16)
    @pl.when(i == 0)
    def _():
        km_ref[...] = jnp.zeros_like(km_ref)

    km = km_ref[...]
    blk_row = lax.broadcasted_iota(jnp.int32, km.shape, 0)
    for j in range(nblk):
        mean_j = jnp.mean(ka[j * MOBA_BLOCK:(j + 1) * MOBA_BLOCK], axis=0, keepdims=True)
        km = jnp.where(blk_row == i * nblk + j, mean_j, km)
    km_ref[...] = km
    vt = _dot(wvt_ref[...], ab, _NT).astype(BF16)
    for j in range(nblk):
        vt_ref[j] = vt[:, j * MOBA_BLOCK:(j + 1) * MOBA_BLOCK]
    gla = _dot(ab, wgla_ref[...])
    o = 0
    for ref in (qg_ref, kg_ref, vg_ref, rg_ref):
        w = ref.shape[1]
        ref[...] = gla[:, o:o + w]
        o += w
    ag = _dot(ab, wag_ref[...]).astype(BF16)
    z = _dot(ag, wgate_ref[...]) + bgate_ref[...]
    la_ref[...] = jax.nn.log_sigmoid(z) / GLA_GATE_TAU


def _inproj(x2d, g_mix, w_in, w_gla_gate, b_gla_gate, *, batch, seq, tm=512):
    t, d = x2d.shape
    nb = seq // MOBA_BLOCK
    o_k, o_v, o_qg = MOBA_WIDTH, 2 * MOBA_WIDTH, 3 * MOBA_WIDTH
    o_ag = o_qg + 2 * GLA_KEY_WIDTH + 2 * GLA_VAL_WIDTH
    wq = w_in[:, :o_k].astype(BF16)
    wk = w_in[:, o_k:o_v].astype(BF16)
    wvt = w_in[:, o_v:o_qg].T.astype(BF16)
    wgla = w_in[:, o_qg:o_ag].astype(BF16)
    wag = jnp.pad(w_in[:, o_ag:], ((0, 0), (0, LANES - GLA_GATE_RANK))).astype(BF16)
    wgate = jnp.pad(w_gla_gate, ((0, LANES - GLA_GATE_RANK), (0, 0))).astype(BF16)
    steps = seq // tm
    row = lambda b, i: (b * steps + i, 0)
    const = lambda b, i: (0, 0)
    full = lambda a: pl.BlockSpec(a.shape, const)
    args = (x2d, g_mix.reshape(1, d), wq, wk, wvt, wgla, wag, wgate, b_gla_gate.reshape(1, -1))
    in_specs = [pl.BlockSpec((tm, d), row)] + [full(a) for a in args[1:]]
    out_shape = (
        jax.ShapeDtypeStruct((t, MOBA_WIDTH), BF16),
        jax.ShapeDtypeStruct((t, MOBA_WIDTH), BF16),
        jax.ShapeDtypeStruct((t // MOBA_BLOCK, MOBA_WIDTH, MOBA_BLOCK), BF16),
        jax.ShapeDtypeStruct((batch * nb, MOBA_WIDTH), F32),
        jax.ShapeDtypeStruct((t, GLA_KEY_WIDTH), F32),
        jax.ShapeDtypeStruct((t, GLA_KEY_WIDTH), F32),
        jax.ShapeDtypeStruct((t, GLA_VAL_WIDTH), F32),
        jax.ShapeDtypeStruct((t, GLA_VAL_WIDTH), F32),
        jax.ShapeDtypeStruct((t, GLA_KEY_WIDTH), F32),
    )
    out_specs = (
        pl.BlockSpec((tm, MOBA_WIDTH), row),
        pl.BlockSpec((tm, MOBA_WIDTH), row),
        pl.BlockSpec((tm // MOBA_BLOCK, MOBA_WIDTH, MOBA_BLOCK), lambda b, i: (b * steps + i, 0, 0)),
        pl.BlockSpec((nb, MOBA_WIDTH), lambda b, i: (b, 0)),
        pl.BlockSpec((tm, GLA_KEY_WIDTH), row),
        pl.BlockSpec((tm, GLA_KEY_WIDTH), row),
        pl.BlockSpec((tm, GLA_VAL_WIDTH), row),
        pl.BlockSpec((tm, GLA_VAL_WIDTH), row),
        pl.BlockSpec((tm, GLA_KEY_WIDTH), row),
    )
    return pl.pallas_call(
        _inproj_kernel, grid=(batch, steps), in_specs=in_specs, out_specs=out_specs, out_shape=out_shape,
        compiler_params=_params("arbitrary", "arbitrary"), name="inproj",
    )(*args)


def _moba_kernel(slopes_ref, q_ref, k_ref, vt_ref, km_ref, o_ref, bias_ref):
    hp = pl.program_id(1)
    c = pl.program_id(2)
    blk = MOBA_BLOCK
    nb = km_ref.shape[0]
    q2 = q_ref[...]
    lane = lax.broadcasted_iota(jnp.int32, (1, 2 * HEAD_DIM), 1)
    kidx = lax.broadcasted_iota(jnp.int32, (blk, blk), 0)
    qidx = lax.broadcasted_iota(jnp.int32, (blk, blk), 1)
    rel = (kidx - qidx).astype(F32)
    causal = kidx <= qidx
    nidx = lax.broadcasted_iota(jnp.int32, (nb, blk), 0)
    kmb = km_ref[...].astype(BF16)
    k_own = k_ref[pl.ds(pl.multiple_of(c * blk, blk), blk), :]
    vt_own = vt_ref[c]

    qms, slopes, state = [], [], []
    for hh in range(2):
        slope = slopes_ref[hp * 2 + hh]
        qm = jnp.where(lane // HEAD_DIM == hh, q2, jnp.zeros_like(q2))
        gt = _dot(kmb, qm, _NT)
        valid = nidx < c
        g = jnp.where(valid, gt, -jnp.inf)
        rank = jnp.zeros((nb, blk), jnp.int32)
        for m in range(nb - 1):
            gm = g[m:m + 1, :]
            beats = (gm > g) | ((gm == g) & (m < nidx))
            rank = rank + beats.astype(jnp.int32)
        sel = valid & (rank < MOBA_TOPK)
        dist0 = ((c - nidx) * blk).astype(F32)
        bias_ref[hh] = jnp.where(sel, -slope * dist0, MASKED)
        s = _dot(k_own, qm, _NT) + slope * rel
        s = jnp.where(causal, s, MASKED)
        m0 = jnp.max(s, axis=0, keepdims=True)
        p = jnp.exp(s - m0)
        l0 = jnp.sum(p, axis=0, keepdims=True)
        acc0 = _dot(vt_own[hh * HEAD_DIM:(hh + 1) * HEAD_DIM, :], p.astype(BF16))
        qms.append(qm)
        slopes.append(slope)
        state += [m0, l0, acc0]

    def body(j, carry):
        kj = k_ref[pl.ds(pl.multiple_of(j * blk, blk), blk), :]
        vtj = vt_ref[j]
        out = []
        for hh in range(2):
            m_i, l_i, acc = carry[3 * hh:3 * hh + 3]
            s = _dot(kj, qms[hh], _NT) + slopes[hh] * rel + bias_ref[hh, pl.ds(j, 1), :]
            m_new = jnp.maximum(m_i, jnp.max(s, axis=0, keepdims=True))
            alpha = jnp.exp(m_i - m_new)
            p = jnp.exp(s - m_new)
            l_new = alpha * l_i + jnp.sum(p, axis=0, keepdims=True)
            acc = alpha * acc + _dot(vtj[hh * HEAD_DIM:(hh + 1) * HEAD_DIM, :], p.astype(BF16))
            out += [m_new, l_new, acc]
        return tuple(out)

    state = lax.fori_loop(0, c, body, tuple(state))
    out_t = jnp.concatenate([state[2] / state[1], state[5] / state[4]], axis=0)
    o_ref[...] = out_t.T.astype(o_ref.dtype)


def _moba(qa, ka, vt, kmean, slopes, *, batch, seq):
    t = qa.shape[0]
    nb = seq // MOBA_BLOCK
    hw = 2 * HEAD_DIM
    grid_spec = pltpu.PrefetchScalarGridSpec(
        num_scalar_prefetch=1,
        grid=(batch, MOBA_HEADS // 2, nb),
        in_specs=[
            pl.BlockSpec((MOBA_BLOCK, hw), lambda b, h, c, s: (b * nb + c, h)),
            pl.BlockSpec((seq, hw), lambda b, h, c, s: (b, h)),
            pl.BlockSpec((nb, hw, MOBA_BLOCK), lambda b, h, c, s: (b, h, 0)),
            pl.BlockSpec((nb, hw), lambda b, h, c, s: (b, h)),
        ],
        out_specs=pl.BlockSpec((MOBA_BLOCK, hw), lambda b, h, c, s: (b * nb + c, h)),
        scratch_shapes=[pltpu.VMEM((2, nb, MOBA_BLOCK), F32)],
    )
    return pl.pallas_call(
        _moba_kernel, grid_spec=grid_spec, out_shape=jax.ShapeDtypeStruct((t, MOBA_WIDTH), BF16),
        compiler_params=_params("arbitrary", "arbitrary", "arbitrary"), name="moba",
    )(slopes, qa, ka, vt, kmean)


def _gla_kernel(q_ref, k_ref, v_ref, r_ref, la_ref, gn_ref, o_ref, st_ref):
    @pl.when(pl.program_id(1) == 0)
    def _():
        st_ref[...] = jnp.zeros_like(st_ref)

    ch = GLA_CHUNK
    n_chunks = q_ref.shape[0] // ch
    row = lax.broadcasted_iota(jnp.int32, (ch, ch), 0)
    col = lax.broadcasted_iota(jnp.int32, (ch, ch), 1)
    tri = (row >= col).astype(F32)
    causal = row >= col
    lane = lax.broadcasted_iota(jnp.int32, (1, GLA_KEY_WIDTH), 1)
    gn = gn_ref[...]
    state = st_ref[...]
    for ci in range(n_chunks):
        rows = slice(ci * ch, (ci + 1) * ch)
        la = la_ref[rows, :]
        g = _dot(tri, la, precision=lax.Precision.HIGHEST)
        g_last = g[ch - 1:ch, :]
        qd = q_ref[rows, :] * (GLA_DK ** -0.5) * jnp.exp(g)
        k = k_ref[rows, :]
        kd = (k * jnp.exp(-g)).astype(BF16)
        kt = k * jnp.exp(g_last - g)
        state_b = state.astype(BF16)
        new_state = state * jnp.exp(g_last)
        for h in range(GLA_HEADS):
            hm = lane // GLA_DK == h
            qh = jnp.where(hm, qd, 0.0).astype(BF16)
            kth = jnp.where(hm, kt, 0.0).astype(BF16)
            vh = v_ref[rows, h * GLA_DV:(h + 1) * GLA_DV].astype(BF16)
            a = jnp.where(causal, _dot(qh, kd, _NT), 0.0)
            o = _dot(a.astype(BF16), vh) + _dot(qh, state_b, _NT)
            new_state = new_state + _dot(vh, kth, _TN)
            rh = r_ref[rows, h * GLA_DV:(h + 1) * GLA_DV]
            o_ref[rows, h * GLA_DV:(h + 1) * GLA_DV] = (_rms(o, gn) * (rh * jax.nn.sigmoid(rh))).astype(o_ref.dtype)
        state = new_state
    st_ref[...] = state


def _gla(qg, kg, vg, rg, la, g_norm, *, batch, seq, tc=256):
    t = qg.shape[0]
    steps = seq // tc
    row = lambda b, i: (b * steps + i, 0)
    kspec = pl.BlockSpec((tc, GLA_KEY_WIDTH), row)
    vspec = pl.BlockSpec((tc, GLA_VAL_WIDTH), row)
    return pl.pallas_call(
        _gla_kernel, grid=(batch, steps),
        in_specs=[kspec, kspec, vspec, vspec, kspec, pl.BlockSpec((1, GLA_DV), lambda b, i: (0, 0))],
        out_specs=vspec, out_shape=jax.ShapeDtypeStruct((t, GLA_VAL_WIDTH), BF16),
        scratch_shapes=[pltpu.VMEM((GLA_DV, GLA_KEY_WIDTH), F32)],
        compiler_params=_params("arbitrary", "arbitrary"), name="gla",
    )(qg, kg, vg, rg, la, g_norm.reshape(1, GLA_DV))


def _outproj_kernel(x_ref, om_ref, og_ref, wm_ref, wg_ref, gf_ref, wr_ref, br_ref, h_ref, f_ref, lg_ref):
    h = x_ref[...] + _dot(om_ref[...], wm_ref[...]) + _dot(og_ref[...], wg_ref[...])
    h_ref[...] = h
    f = _rms(h, gf_ref[...]).astype(BF16)
    f_ref[...] = f
    lg_ref[...] = _dot(f, wr_ref[...]) + br_ref[...]


def _outproj(x2d, o_moba, o_gla, w_out, g_ffn, w_router, b_router, *, tm=512):
    t, d = x2d.shape
    wm = w_out[:MOBA_WIDTH].astype(BF16)
    wg = w_out[MOBA_WIDTH:].astype(BF16)
    wr = jnp.pad(w_router, ((0, 0), (0, LANES - N_EXPERTS))).astype(BF16)
    br = jnp.pad(b_router, (0, LANES - N_EXPERTS)).reshape(1, LANES)
    row = lambda i: (i, 0)
    const = lambda i: (0, 0)
    full = lambda a: pl.BlockSpec(a.shape, const)
    args = (x2d, o_moba, o_gla, wm, wg, g_ffn.reshape(1, d), wr, br)
    in_specs = [pl.BlockSpec((tm, d), row), pl.BlockSpec((tm, MOBA_WIDTH), row),
                pl.BlockSpec((tm, GLA_VAL_WIDTH), row)] + [full(a) for a in args[3:]]
    return pl.pallas_call(
        _outproj_kernel, grid=(t // tm,), in_specs=in_specs,
        out_specs=(pl.BlockSpec((tm, d), row), pl.BlockSpec((tm, d), row), pl.BlockSpec((tm, LANES), row)),
        out_shape=(jax.ShapeDtypeStruct((t, d), F32), jax.ShapeDtypeStruct((t, d), BF16),
                   jax.ShapeDtypeStruct((t, LANES), F32)),
        compiler_params=_params("arbitrary"), name="outproj",
    )(*args)


PAIR_CHUNK = 2 * LANES


def _moe_kernel(be_ref, nused_ref, x_ref, w1_ref, b1_ref, w2_ref, b2_ref, o_ref, w1s_ref, w2s_ref):
    i = pl.program_id(0)
    f = w2_ref.shape[1]

    @pl.when((i == 0) | (be_ref[i] != be_ref[jnp.maximum(i - 1, 0)]))
    def _():
        r = lax.broadcasted_iota(jnp.int32, (PAIR_CHUNK, PAIR_CHUNK), 0)
        c = lax.broadcasted_iota(jnp.int32, (PAIR_CHUNK, PAIR_CHUNK), 1)
        src_col = jnp.where(c < LANES, 2 * c, 2 * (c - LANES) + 1)
        perm = jnp.where(r == src_col, 1.0, 0.0).astype(BF16)
        for cc in range(2 * f // PAIR_CHUNK):
            chunk = w1_ref[0, :, cc * PAIR_CHUNK:(cc + 1) * PAIR_CHUNK].astype(BF16)
            res = _dot(chunk, perm).astype(BF16)
            w1s_ref[:, cc * LANES:(cc + 1) * LANES] = res[:, :LANES]
            w1s_ref[:, f + cc * LANES:f + (cc + 1) * LANES] = res[:, LANES:]
        w2s_ref[...] = w2_ref[0].astype(BF16)

    @pl.when(i < nused_ref[0])
    def _():
        hid = _dot(x_ref[...], w1s_ref[...]) + b1_ref[0]
        glu = jnp.minimum(hid[:, :f], SWIGLU_LIMIT)
        lin = jnp.clip(hid[:, f:], -SWIGLU_LIMIT, SWIGLU_LIMIT)
        act = glu * jax.nn.sigmoid(SWIGLU_ALPHA * glu) * (lin + 1.0)
        o_ref[...] = (_dot(act.astype(BF16), w2s_ref[...]) + b2_ref[0]).astype(o_ref.dtype)

    @pl.when(i >= nused_ref[0])
    def _():
        o_ref[...] = jnp.zeros_like(o_ref)


def _moe(x_rows, block_expert, n_used, w1, b1, w2, b2, *, tm):
    n_rows, d = x_rows.shape
    e, _, f2 = w1.shape
    f = f2 // 2
    b1p = jnp.concatenate([b1[:, 0::2], b1[:, 1::2]], axis=-1).reshape(e, 1, f2)
    b2r = b2.reshape(e, 1, d)
    grid_spec = pltpu.PrefetchScalarGridSpec(
        num_scalar_prefetch=2,
        grid=(n_rows // tm,),
        in_specs=[
            pl.BlockSpec((tm, d), lambda i, be, nu: (i, 0)),
            pl.BlockSpec((1, d, f2), lambda i, be, nu: (be[i], 0, 0)),
            pl.BlockSpec((1, 1, f2), lambda i, be, nu: (be[i], 0, 0)),
            pl.BlockSpec((1, f, d), lambda i, be, nu: (be[i], 0, 0)),
            pl.BlockSpec((1, 1, d), lambda i, be, nu: (be[i], 0, 0)),
        ],
        out_specs=pl.BlockSpec((tm, d), lambda i, be, nu: (i, 0)),
        scratch_shapes=[pltpu.VMEM((d, f2), BF16), pltpu.VMEM((f, d), BF16)],
    )
    return pl.pallas_call(
        _moe_kernel, grid_spec=grid_spec, out_shape=jax.ShapeDtypeStruct((n_rows, d), BF16),
        compiler_params=_params("arbitrary"), name="moe",
    )(block_expert, n_used, x_rows, w1, b1p, w2, b2r)


def _prefix_counts(onehot):
    n, e = onehot.shape
    g = 256
    tri = jnp.tril(jnp.ones((g, g), BF16))
    inner = jnp.einsum("ij,bje->bie", tri, onehot.reshape(n // g, g, e).astype(BF16), preferred_element_type=F32)
    totals = inner[:, -1, :]
    tri_o = jnp.tril(jnp.ones((n // g, n // g), BF16), -1)
    before = jnp.dot(tri_o, totals.astype(BF16), preferred_element_type=F32)
    return (inner + before[:, None, :]).reshape(n, e).astype(jnp.int32)


def _route(logits, *, tm):
    t = logits.shape[0]
    top_vals, top_idx = lax.top_k(logits, TOP_K)
    gates = jax.nn.softmax(top_vals, axis=-1)
    n_assign = t * TOP_K
    experts = jnp.arange(N_EXPERTS, dtype=jnp.int32)
    flat_e = top_idx.reshape(-1).astype(jnp.int32)
    onehot = flat_e[:, None] == experts[None, :]
    csum = _prefix_counts(onehot)
    counts = csum[-1]
    padded = (counts + tm - 1) // tm * tm
    pad_end = jnp.cumsum(padded)
    pad_start = pad_end - padded
    grp_start = jnp.cumsum(counts) - counts
    pick = lambda table, oh: jnp.sum(jnp.where(oh, table[None, :], 0), axis=1)
    slot = pick(pad_start, onehot) + jnp.sum(jnp.where(onehot, csum, 0), axis=1) - 1
    order = jnp.sort(flat_e * n_assign + jnp.arange(n_assign, dtype=jnp.int32)) % n_assign
    n_blocks = -(-n_assign // tm) + N_EXPERTS
    r = jnp.arange(n_blocks * tm, dtype=jnp.int32)
    e_r = jnp.minimum(jnp.sum(pad_end[None, :] <= r[:, None], axis=1), N_EXPERTS - 1)
    oh_r = e_r[:, None] == experts[None, :]
    off = r - pick(pad_start, oh_r)
    valid = (off < pick(counts, oh_r)) & (r < pad_end[-1])
    src = jnp.clip(pick(grp_start, oh_r) + off, 0, n_assign - 1)
    row_tok = jnp.where(valid, order[src] // TOP_K, 0)
    n_used = pad_end[-1] // tm
    blk_start = jnp.minimum(jnp.arange(n_blocks, dtype=jnp.int32), n_used - 1) * tm
    block_expert = jnp.minimum(jnp.sum(pad_end[None, :] <= blk_start[:, None], axis=1), N_EXPERTS - 1)
    return gates, row_tok, slot.reshape(t, TOP_K), block_expert.astype(jnp.int32), n_used.reshape(1).astype(jnp.int32)


def _final_kernel(h_ref, y_ref, gate_ref, p_ref, wp_ref, gp_ref, wgate_ref, gfin_ref, o_ref, *, last_layer):
    h = h_ref[...]
    gates = gate_ref[...]
    for k in range(TOP_K):
        h = h + gates[:, k:k + 1] * y_ref[k].astype(F32)
    ple = _rms(_dot(p_ref[...].astype(BF16), wp_ref[...]), gp_ref[...])
    h = h + jax.nn.sigmoid(_dot(h.astype(BF16), wgate_ref[...])) * ple
    o_ref[...] = _rms(h, gfin_ref[...]) if last_layer else h


def _final(h1, y_tok, gates, p2d, w_ple_proj, g_ple, w_ple_gate, g_final, *, last_layer, tm=512):
    t, d = h1.shape
    row = lambda i: (i, 0)
    const = lambda i: (0, 0)
    full = lambda a: pl.BlockSpec(a.shape, const)
    args = (h1, y_tok, gates, p2d, w_ple_proj.astype(BF16), g_ple.reshape(1, d), w_ple_gate.astype(BF16),
            g_final.reshape(1, d))
    in_specs = [pl.BlockSpec((tm, d), row), pl.BlockSpec((TOP_K, tm, d), lambda i: (0, i, 0)),
                pl.BlockSpec((tm, TOP_K), row), pl.BlockSpec((tm, p2d.shape[1]), row)] + [full(a) for a in args[4:]]
    return pl.pallas_call(
        functools.partial(_final_kernel, last_layer=last_layer), grid=(t // tm,), in_specs=in_specs,
        out_specs=pl.BlockSpec((tm, d), row),
        out_shape=jax.ShapeDtypeStruct((t, d), F32),
        compiler_params=_params("arbitrary"), name="final",
    )(*args)


def kernel(x, p, g_mix, w_in, w_gla_gate, b_gla_gate, g_gla_norm, w_out, g_ffn, w_router, b_router,
           w1, b1, w2, b2, w_ple_gate, w_ple_proj, g_ple, g_final, *, moe_tm=256):
    batch, seq, d = x.shape
    depth = p.shape[0]
    t = batch * seq
    slopes = jnp.exp2(-ALIBI_MAX * jnp.arange(1, MOBA_HEADS + 1, dtype=F32) / MOBA_HEADS)
    h = x.reshape(t, d)
    for i in range(depth):
        qa, ka, vt, kmean, qg, kg, vg, rg, la = _inproj(h, g_mix[i], w_in[i], w_gla_gate[i], b_gla_gate[i],
                                                        batch=batch, seq=seq)
        o_moba = _moba(qa, ka, vt, kmean, slopes, batch=batch, seq=seq)
        o_gla = _gla(qg, kg, vg, rg, la, g_gla_norm[i], batch=batch, seq=seq)
        h1, f, logits = _outproj(h, o_moba, o_gla, w_out[i], g_ffn[i], w_router[i], b_router[i])
        gates, row_tok, slot, block_expert, n_used = _route(logits[:, :N_EXPERTS], tm=moe_tm)
        y = _moe(f[row_tok], block_expert, n_used, w1[i], b1[i], w2[i], b2[i], tm=moe_tm)
        y_tok = y[slot.T]
        h = _final(h1, y_tok, gates, p[i].reshape(t, -1), w_ple_proj[i], g_ple[i], w_ple_gate[i],
                   g_final, last_layer=i == depth - 1)
    return h.reshape(batch, seq, d)
```

```python
import functools

import jax
import jax.numpy as jnp
import numpy as np
from jax import lax
from jax.experimental import pallas as pl
from jax.experimental.pallas import tpu as pltpu

HEAD_DIM = 64
MOBA_HEADS = 8
MOBA_WIDTH = MOBA_HEADS * HEAD_DIM
MOBA_BLOCK = 256
MOBA_TOPK = 3
ALIBI_MAX = 8.0
GLA_HEADS = 4
GLA_DK = 64
GLA_DV = 128
GLA_KEY_WIDTH = GLA_HEADS * GLA_DK
GLA_VAL_WIDTH = GLA_HEADS * GLA_DV
GLA_GATE_RANK = 16
GLA_GATE_TAU = 16.0
GLA_CHUNK = 64
N_EXPERTS = 32
TOP_K = 4
SWIGLU_ALPHA = 1.702
SWIGLU_LIMIT = 7.0
EPS = 1e-6

LANES = 128
MASKED = -1e30
VMEM_LIMIT = 56 * 1024 * 1024

BF16 = jnp.bfloat16
F32 = jnp.float32

_NT = (((1,), (1,)), ((), ()))
_TN = (((0,), (0,)), ((), ()))


def _dot(a, b, dims=None, precision=None):
    if dims is None:
        return jnp.dot(a, b, preferred_element_type=F32, precision=precision)
    return lax.dot_general(a, b, dims, preferred_element_type=F32, precision=precision)


def _rms(x, g):
    return x * lax.rsqrt(jnp.mean(x * x, axis=-1, keepdims=True) + EPS) * g


def _col_reduce(x, pair_op, final_op, rows=64):
    parts = [x[i:i + rows] for i in range(0, x.shape[0], rows)]
    while len(parts) > 1:
        parts = [pair_op(parts[i], parts[i + 1]) for i in range(0, len(parts), 2)]
    return final_op(parts[0], axis=0, keepdims=True)


def _params(*sem):
    return pltpu.CompilerParams(dimension_semantics=sem, vmem_limit_bytes=VMEM_LIMIT)


def _inproj_kernel(x_ref, g_ref, wq_ref, wk_ref, wvt_ref, wgla_ref, wag_ref, wgate_ref, bgate_ref,
                   qa_ref, ka_ref, vt_ref, km_ref, qg_ref, kg_ref, vg_ref, rg_ref, la_ref):
    i = pl.program_id(1)
    tm = x_ref.shape[0]
    nblk = tm // MOBA_BLOCK
    ab = _rms(x_ref[...], g_ref[...]).astype(BF16)
    qa_ref[...] = (_dot(ab, wq_ref[...]) * (HEAD_DIM ** -0.5)).astype(BF16)
    ka = _dot(ab, wk_ref[...])
    ka_ref[...] = ka.astype(BF16)
    @pl.when(i == 0)
    def _():
        km_ref[...] = jnp.zeros_like(km_ref)

    km = km_ref[...]
    blk_row = lax.broadcasted_iota(jnp.int32, km.shape, 0)
    for j in range(nblk):
        mean_j = jnp.mean(ka[j * MOBA_BLOCK:(j + 1) * MOBA_BLOCK], axis=0, keepdims=True)
        km = jnp.where(blk_row == i * nblk + j, mean_j, km)
    km_ref[...] = km
    vt = _dot(wvt_ref[...], ab, _NT).astype(BF16)
    for j in range(nblk):
        vt_ref[j] = vt[:, j * MOBA_BLOCK:(j + 1) * MOBA_BLOCK]
    gla = _dot(ab, wgla_ref[...])
    o = 0
    for ref in (qg_ref, kg_ref, vg_ref, rg_ref):
        w = ref.shape[1]
        ref[...] = gla[:, o:o + w]
        o += w
    ag = _dot(ab, wag_ref[...]).astype(BF16)
    z = _dot(ag, wgate_ref[...]) + bgate_ref[...]
    la_ref[...] = jax.nn.log_sigmoid(z) / GLA_GATE_TAU


def _inproj(x2d, g_mix, w_in, w_gla_gate, b_gla_gate, *, batch, seq, tm=512):
    t, d = x2d.shape
    nb = seq // MOBA_BLOCK
    o_k, o_v, o_qg = MOBA_WIDTH, 2 * MOBA_WIDTH, 3 * MOBA_WIDTH
    o_ag = o_qg + 2 * GLA_KEY_WIDTH + 2 * GLA_VAL_WIDTH
    wq = w_in[:, :o_k].astype(BF16)
    wk = w_in[:, o_k:o_v].astype(BF16)
    wvt = w_in[:, o_v:o_qg].T.astype(BF16)
    wgla = w_in[:, o_qg:o_ag].astype(BF16)
    wag = jnp.pad(w_in[:, o_ag:], ((0, 0), (0, LANES - GLA_GATE_RANK))).astype(BF16)
    wgate = jnp.pad(w_gla_gate, ((0, LANES - GLA_GATE_RANK), (0, 0))).astype(BF16)
    steps = seq // tm
    row = lambda b, i: (b * steps + i, 0)
    const = lambda b, i: (0, 0)
    full = lambda a: pl.BlockSpec(a.shape, const)
    args = (x2d, g_mix.reshape(1, d), wq, wk, wvt, wgla, wag, wgate, b_gla_gate.reshape(1, -1))
    in_specs = [pl.BlockSpec((tm, d), row)] + [full(a) for a in args[1:]]
    out_shape = (
        jax.ShapeDtypeStruct((t, MOBA_WIDTH), BF16),
        jax.ShapeDtypeStruct((t, MOBA_WIDTH), BF16),
        jax.ShapeDtypeStruct((t // MOBA_BLOCK, MOBA_WIDTH, MOBA_BLOCK), BF16),
        jax.ShapeDtypeStruct((batch * nb, MOBA_WIDTH), F32),
        jax.ShapeDtypeStruct((t, GLA_KEY_WIDTH), F32),
        jax.ShapeDtypeStruct((t, GLA_KEY_WIDTH), F32),
        jax.ShapeDtypeStruct((t, GLA_VAL_WIDTH), F32),
        jax.ShapeDtypeStruct((t, GLA_VAL_WIDTH), F32),
        jax.ShapeDtypeStruct((t, GLA_KEY_WIDTH), F32),
    )
    out_specs = (
        pl.BlockSpec((tm, MOBA_WIDTH), row),
        pl.BlockSpec((tm, MOBA_WIDTH), row),
        pl.BlockSpec((tm // MOBA_BLOCK, MOBA_WIDTH, MOBA_BLOCK), lambda b, i: (b * steps + i, 0, 0)),
        pl.BlockSpec((nb, MOBA_WIDTH), lambda b, i: (b, 0)),
        pl.BlockSpec((tm, GLA_KEY_WIDTH), row),
        pl.BlockSpec((tm, GLA_KEY_WIDTH), row),
        pl.BlockSpec((tm, GLA_VAL_WIDTH), row),
        pl.BlockSpec((tm, GLA_VAL_WIDTH), row),
        pl.BlockSpec((tm, GLA_KEY_WIDTH), row),
    )
    return pl.pallas_call(
        _inproj_kernel, grid=(batch, steps), in_specs=in_specs, out_specs=out_specs, out_shape=out_shape,
        compiler_params=_params("arbitrary", "arbitrary"), name="inproj",
    )(*args)


def _moba_kernel(slopes_ref, q_ref, k_ref, vt_ref, km_ref, o_ref, kaug_ref, s_ref):
    hp = pl.program_id(1)
    c = pl.program_id(2)
    blk = MOBA_BLOCK
    nb = km_ref.shape[0]
    hw = 2 * HEAD_DIM
    feat_rows = nb + 8

    @pl.when(c == 0)
    def _():
        lane = lax.broadcasted_iota(jnp.int32, (blk, hw), 1)
        koff = lax.broadcasted_iota(jnp.int32, (blk, hw), 0).astype(F32)
        for hh in range(2):
            slope = slopes_ref[hp * 2 + hh]
            base = HEAD_DIM * (1 - hh)
            for jb in range(nb):
                feat = jnp.where(lane == base + nb, slope * koff, 0.0)
                feat = jnp.where((lane == base + jb) | (lane == base + nb + 1), 1.0, feat)
                kb = k_ref[jb * blk:(jb + 1) * blk, :]
                kaug_ref[hh, jb * blk:(jb + 1) * blk, :] = jnp.where(lane // HEAD_DIM == hh, kb, feat.astype(BF16))

    q2t = q_ref[...].astype(F32).T
    q2t_b = q2t.astype(BF16)
    kidx = lax.broadcasted_iota(jnp.int32, (blk, blk), 0)
    qidx = lax.broadcasted_iota(jnp.int32, (blk, blk), 1)
    causal = kidx <= qidx
    nidx = lax.broadcasted_iota(jnp.int32, (nb, blk), 0)
    r8 = lax.broadcasted_iota(jnp.int32, (8, blk), 0)
    qoff = lax.broadcasted_iota(jnp.int32, (8, blk), 1).astype(F32)
    km_lane = lax.broadcasted_iota(jnp.int32, (nb, hw), 1)
    km = km_ref[...]
    own_rows = pl.ds(pl.multiple_of(c * blk, blk), blk)
    vt_own = vt_ref[c]

    q_past, state = [], []
    for hh in range(2):
        slope = slopes_ref[hp * 2 + hh]
        kmh = jnp.where(km_lane // HEAD_DIM == hh, km, 0.0).astype(BF16)
        g = jnp.where(nidx < c, _dot(kmh, q2t_b), -jnp.inf)
        rank = jnp.zeros((nb, blk), jnp.int32)
        for m in range(nb - 1):
            gm = g[m:m + 1, :]
            beats = (gm > g) | ((gm == g) & (m < nidx))
            rank = rank + beats.astype(jnp.int32)
        sel = (nidx < c) & (rank < MOBA_TOPK)
        dist0 = -slope * ((c - nidx) * blk).astype(F32)
        tail = jnp.where(r8 == 0, 1.0, jnp.where(r8 == 1, -slope * qoff, 0.0))
        fill = jnp.zeros((HEAD_DIM - feat_rows, blk), F32)
        qh = q2t[hh * HEAD_DIM:(hh + 1) * HEAD_DIM, :]

        def q_aug(bias):
            feats = [jnp.where(bias, dist0, MASKED), tail, fill]
            return jnp.concatenate([qh] + feats if hh == 0 else feats + [qh], axis=0).astype(BF16)

        q_past.append(q_aug(sel))
        s = _dot(kaug_ref[hh, own_rows, :], q_aug(nidx == c))
        s = jnp.where(causal, s, MASKED)
        m0 = jnp.max(s, axis=0, keepdims=True)
        p = jnp.exp(s - m0)
        l0 = jnp.sum(p, axis=0, keepdims=True)
        acc0 = _dot(vt_own[hh * HEAD_DIM:(hh + 1) * HEAD_DIM, :], p.astype(BF16))
        state += [m0, l0, acc0]

    def scores(pair, slot):
        rows = pl.ds(pl.multiple_of(pair * 2 * blk, 2 * blk), 2 * blk)
        tops = []
        for hh in range(2):
            s = _dot(kaug_ref[hh, rows, :], q_past[hh])
            s_ref[slot, hh] = s
            tops.append(_col_reduce(s, jnp.maximum, jnp.max))
        return tops

    def absorb(pair, slot, tops, carry):
        vt0 = vt_ref[2 * pair]
        vt1 = vt_ref[2 * pair + 1]
        out = []
        for hh in range(2):
            m_i, l_i, acc = carry[3 * hh:3 * hh + 3]
            hrows = slice(hh * HEAD_DIM, (hh + 1) * HEAD_DIM)
            m_new = jnp.maximum(m_i, tops[hh])
            alpha = jnp.exp(m_i - m_new)
            p = jnp.exp(s_ref[slot, hh] - m_new)
            l_new = alpha * l_i + _col_reduce(p, jnp.add, jnp.sum)
            pb = p.astype(BF16)
            acc = alpha * acc + _dot(vt0[hrows, :], pb[:blk]) + _dot(vt1[hrows, :], pb[blk:])
            out += [m_new, l_new, acc]
        return out

    def step(i, carry, slot):
        tops = scores(i + 1, 1 - slot)
        return tuple(absorb(i, slot, carry[6:], carry[:6]) + tops)

    def body(i, carry):
        return lax.cond(i % 2 == 0, lambda: step(i, carry, 0), lambda: step(i, carry, 1))

    last = jnp.maximum((c + 1) // 2 - 1, 0)
    carry = lax.fori_loop(0, last, body, tuple(state + scores(0, 0)))
    state = lax.cond(last % 2 == 0, lambda: tuple(absorb(last, 0, carry[6:], carry[:6])),
                     lambda: tuple(absorb(last, 1, carry[6:], carry[:6])))
    out_t = jnp.concatenate([state[2] / state[1], state[5] / state[4]], axis=0)
    o_ref[...] = out_t.T.astype(o_ref.dtype)


def _alibi_slopes():
    slopes = np.exp2(-ALIBI_MAX * np.arange(1, MOBA_HEADS + 1, dtype=np.float32) / MOBA_HEADS).astype(np.float32)
    worst = slopes * np.float32(MOBA_BLOCK - 1)
    assert np.all(worst.astype(jnp.bfloat16).astype(np.float32) == worst), "ALiBi slopes not bf16-exact"
    return slopes


def _moba(qa, ka, vt, kmean, *, batch, seq):
    t = qa.shape[0]
    nb = seq // MOBA_BLOCK
    hw = 2 * HEAD_DIM
    assert nb % 8 == 0 and nb + 8 <= HEAD_DIM, "bias features must fit the spare contraction lanes"
    grid_spec = pltpu.PrefetchScalarGridSpec(
        num_scalar_prefetch=1,
        grid=(batch, MOBA_HEADS // 2, nb),
        in_specs=[
            pl.BlockSpec((MOBA_BLOCK, hw), lambda b, h, c, s: (b * nb + c, h)),
            pl.BlockSpec((seq, hw), lambda b, h, c, s: (b, h)),
            pl.BlockSpec((nb, hw, MOBA_BLOCK), lambda b, h, c, s: (b, h, 0)),
            pl.BlockSpec((nb, hw), lambda b, h, c, s: (b, h)),
        ],
        out_specs=pl.BlockSpec((MOBA_BLOCK, hw), lambda b, h, c, s: (b * nb + c, h)),
        scratch_shapes=[pltpu.VMEM((2, seq, hw), BF16), pltpu.VMEM((2, 2, 2 * MOBA_BLOCK, MOBA_BLOCK), F32)],
    )
    return pl.pallas_call(
        _moba_kernel, grid_spec=grid_spec, out_shape=jax.ShapeDtypeStruct((t, MOBA_WIDTH), BF16),
        compiler_params=_params("arbitrary", "arbitrary", "arbitrary"), name="moba",
    )(jnp.asarray(_alibi_slopes()), qa, ka, vt, kmean)


def _gla_kernel(q_ref, k_ref, v_ref, r_ref, la_ref, gn_ref, o_ref, st_ref):
    @pl.when(pl.program_id(1) == 0)
    def _():
        st_ref[...] = jnp.zeros_like(st_ref)

    ch = GLA_CHUNK
    n_chunks = q_ref.shape[0] // ch
    row = lax.broadcasted_iota(jnp.int32, (ch, ch), 0)
    col = lax.broadcasted_iota(jnp.int32, (ch, ch), 1)
    tri = (row >= col).astype(F32)
    causal = row >= col
    lane = lax.broadcasted_iota(jnp.int32, (1, GLA_KEY_WIDTH), 1)
    gn = gn_ref[...]
    state = st_ref[...]
    for ci in range(n_chunks):
        rows = slice(ci * ch, (ci + 1) * ch)
        la = la_ref[rows, :]
        g = _dot(tri, la, precision=lax.Precision.HIGHEST)
        g_last = g[ch - 1:ch, :]
        qd = q_ref[rows, :] * (GLA_DK ** -0.5) * jnp.exp(g)
        k = k_ref[rows, :]
        kd = (k * jnp.exp(-g)).astype(BF16)
        kt = k * jnp.exp(g_last - g)
        state_b = state.astype(BF16)
        new_state = state * jnp.exp(g_last)
        for h in range(GLA_HEADS):
            hm = lane // GLA_DK == h
            qh = jnp.where(hm, qd, 0.0).astype(BF16)
            kth = jnp.where(hm, kt, 0.0).astype(BF16)
            vh = v_ref[rows, h * GLA_DV:(h + 1) * GLA_DV].astype(BF16)
            a = jnp.where(causal, _dot(qh, kd, _NT), 0.0)
            o = _dot(a.astype(BF16), vh) + _dot(qh, state_b, _NT)
            new_state = new_state + _dot(vh, kth, _TN)
            rh = r_ref[rows, h * GLA_DV:(h + 1) * GLA_DV]
            o_ref[rows, h * GLA_DV:(h + 1) * GLA_DV] = (_rms(o, gn) * (rh * jax.nn.sigmoid(rh))).astype(o_ref.dtype)
        state = new_state
    st_ref[...] = state


def _gla(qg, kg, vg, rg, la, g_norm, *, batch, seq, tc=256):
    t = qg.shape[0]
    steps = seq // tc
    row = lambda b, i: (b * steps + i, 0)
    kspec = pl.BlockSpec((tc, GLA_KEY_WIDTH), row)
    vspec = pl.BlockSpec((tc, GLA_VAL_WIDTH), row)
    return pl.pallas_call(
        _gla_kernel, grid=(batch, steps),
        in_specs=[kspec, kspec, vspec, vspec, kspec, pl.BlockSpec((1, GLA_DV), lambda b, i: (0, 0))],
        out_specs=vspec, out_shape=jax.ShapeDtypeStruct((t, GLA_VAL_WIDTH), BF16),
        scratch_shapes=[pltpu.VMEM((GLA_DV, GLA_KEY_WIDTH), F32)],
        compiler_params=_params("arbitrary", "arbitrary"), name="gla",
    )(qg, kg, vg, rg, la, g_norm.reshape(1, GLA_DV))


def _outproj_kernel(x_ref, om_ref, og_ref, wm_ref, wg_ref, gf_ref, wr_ref, br_ref, h_ref, f_ref, lg_ref):
    h = x_ref[...] + _dot(om_ref[...], wm_ref[...]) + _dot(og_ref[...], wg_ref[...])
    h_ref[...] = h
    f = _rms(h, gf_ref[...]).astype(BF16)
    f_ref[...] = f
    lg_ref[...] = _dot(f, wr_ref[...]) + br_ref[...]


def _outproj(x2d, o_moba, o_gla, w_out, g_ffn, w_router, b_router, *, tm=512):
    t, d = x2d.shape
    wm = w_out[:MOBA_WIDTH].astype(BF16)
    wg = w_out[MOBA_WIDTH:].astype(BF16)
    wr = jnp.pad(w_router, ((0, 0), (0, LANES - N_EXPERTS))).astype(BF16)
    br = jnp.pad(b_router, (0, LANES - N_EXPERTS)).reshape(1, LANES)
    row = lambda i: (i, 0)
    const = lambda i: (0, 0)
    full = lambda a: pl.BlockSpec(a.shape, const)
    args = (x2d, o_moba, o_gla, wm, wg, g_ffn.reshape(1, d), wr, br)
    in_specs = [pl.BlockSpec((tm, d), row), pl.BlockSpec((tm, MOBA_WIDTH), row),
                pl.BlockSpec((tm, GLA_VAL_WIDTH), row)] + [full(a) for a in args[3:]]
    return pl.pallas_call(
        _outproj_kernel, grid=(t // tm,), in_specs=in_specs,
        out_specs=(pl.BlockSpec((tm, d), row), pl.BlockSpec((tm, d), row), pl.BlockSpec((tm, LANES), row)),
        out_shape=(jax.ShapeDtypeStruct((t, d), F32), jax.ShapeDtypeStruct((t, d), BF16),
                   jax.ShapeDtypeStruct((t, LANES), F32)),
        compiler_params=_params("arbitrary"), name="outproj",
    )(*args)


PAIR_CHUNK = 2 * LANES


def _moe_kernel(be_ref, nused_ref, x_ref, w1_ref, b1_ref, w2_ref, b2_ref, o_ref, w1s_ref, w2s_ref):
    i = pl.program_id(0)
    f = w2_ref.shape[1]

    @pl.when((i == 0) | (be_ref[i] != be_ref[jnp.maximum(i - 1, 0)]))
    def _():
        r = lax.broadcasted_iota(jnp.int32, (PAIR_CHUNK, PAIR_CHUNK), 0)
        c = lax.broadcasted_iota(jnp.int32, (PAIR_CHUNK, PAIR_CHUNK), 1)
        src_col = jnp.where(c < LANES, 2 * c, 2 * (c - LANES) + 1)
        perm = jnp.where(r == src_col, 1.0, 0.0).astype(BF16)
        for cc in range(2 * f // PAIR_CHUNK):
            chunk = w1_ref[0, :, cc * PAIR_CHUNK:(cc + 1) * PAIR_CHUNK].astype(BF16)
            res = _dot(chunk, perm).astype(BF16)
            w1s_ref[:, cc * LANES:(cc + 1) * LANES] = res[:, :LANES]
            w1s_ref[:, f + cc * LANES:f + (cc + 1) * LANES] = res[:, LANES:]
        w2s_ref[...] = w2_ref[0].astype(BF16)

    @pl.when(i < nused_ref[0])
    def _():
        hid = _dot(x_ref[...], w1s_ref[...]) + b1_ref[0]
        glu = jnp.minimum(hid[:, :f], SWIGLU_LIMIT)
        lin = jnp.clip(hid[:, f:], -SWIGLU_LIMIT, SWIGLU_LIMIT)
        act = glu * jax.nn.sigmoid(SWIGLU_ALPHA * glu) * (lin + 1.0)
        o_ref[...] = (_dot(act.astype(BF16), w2s_ref[...]) + b2_ref[0]).astype(o_ref.dtype)

    @pl.when(i >= nused_ref[0])
    def _():
        o_ref[...] = jnp.zeros_like(o_ref)


def _moe(x_rows, block_expert, n_used, w1, b1, w2, b2, *, tm):
    n_rows, d = x_rows.shape
    e, _, f2 = w1.shape
    f = f2 // 2
    b1p = jnp.concatenate([b1[:, 0::2], b1[:, 1::2]], axis=-1).reshape(e, 1, f2)
    b2r = b2.reshape(e, 1, d)
    grid_spec = pltpu.PrefetchScalarGridSpec(
        num_scalar_prefetch=2,
        grid=(n_rows // tm,),
        in_specs=[
            pl.BlockSpec((tm, d), lambda i, be, nu: (i, 0)),
            pl.BlockSpec((1, d, f2), lambda i, be, nu: (be[i], 0, 0)),
            pl.BlockSpec((1, 1, f2), lambda i, be, nu: (be[i], 0, 0)),
            pl.BlockSpec((1, f, d), lambda i, be, nu: (be[i], 0, 0)),
            pl.BlockSpec((1, 1, d), lambda i, be, nu: (be[i], 0, 0)),
        ],
        out_specs=pl.BlockSpec((tm, d), lambda i, be, nu: (i, 0)),
        scratch_shapes=[pltpu.VMEM((d, f2), BF16), pltpu.VMEM((f, d), BF16)],
    )
    return pl.pallas_call(
        _moe_kernel, grid_spec=grid_spec, out_shape=jax.ShapeDtypeStruct((n_rows, d), BF16),
        compiler_params=_params("arbitrary"), name="moe",
    )(block_expert, n_used, x_rows, w1, b1p, w2, b2r)


def _prefix_counts(onehot):
    n, e = onehot.shape
    g = 256
    tri = jnp.tril(jnp.ones((g, g), BF16))
    inner = jnp.einsum("ij,bje->bie", tri, onehot.reshape(n // g, g, e).astype(BF16), preferred_element_type=F32)
    totals = inner[:, -1, :]
    tri_o = jnp.tril(jnp.ones((n // g, n // g), BF16), -1)
    before = jnp.dot(tri_o, totals.astype(BF16), preferred_element_type=F32)
    return (inner + before[:, None, :]).reshape(n, e).astype(jnp.int32)


def _route(logits, *, tm):
    t = logits.shape[0]
    top_vals, top_idx = lax.top_k(logits, TOP_K)
    gates = jax.nn.softmax(top_vals, axis=-1)
    n_assign = t * TOP_K
    experts = jnp.arange(N_EXPERTS, dtype=jnp.int32)
    flat_e = top_idx.reshape(-1).astype(jnp.int32)
    onehot = flat_e[:, None] == experts[None, :]
    csum = _prefix_counts(onehot)
    counts = csum[-1]
    padded = (counts + tm - 1) // tm * tm
    pad_end = jnp.cumsum(padded)
    pad_start = pad_end - padded
    grp_start = jnp.cumsum(counts) - counts
    pick = lambda table, oh: jnp.sum(jnp.where(oh, table[None, :], 0), axis=1)
    slot = pick(pad_start, onehot) + jnp.sum(jnp.where(onehot, csum, 0), axis=1) - 1
    order = jnp.sort(flat_e * n_assign + jnp.arange(n_assign, dtype=jnp.int32)) % n_assign
    n_blocks = -(-n_assign // tm) + N_EXPERTS
    r = jnp.arange(n_blocks * tm, dtype=jnp.int32)
    e_r = jnp.minimum(jnp.sum(pad_end[None, :] <= r[:, None], axis=1), N_EXPERTS - 1)
    oh_r = e_r[:, None] == experts[None, :]
    off = r - pick(pad_start, oh_r)
    valid = (off < pick(counts, oh_r)) & (r < pad_end[-1])
    src = jnp.clip(pick(grp_start, oh_r) + off, 0, n_assign - 1)
    row_tok = jnp.where(valid, order[src] // TOP_K, 0)
    n_used = pad_end[-1] // tm
    blk_start = jnp.minimum(jnp.arange(n_blocks, dtype=jnp.int32), n_used - 1) * tm
    block_expert = jnp.minimum(jnp.sum(pad_end[None, :] <= blk_start[:, None], axis=1), N_EXPERTS - 1)
    return gates, row_tok, slot.reshape(t, TOP_K), block_expert.astype(jnp.int32), n_used.reshape(1).astype(jnp.int32)


def _final_kernel(h_ref, y_ref, gate_ref, p_ref, wp_ref, gp_ref, wgate_ref, gfin_ref, o_ref, *, last_layer):
    h = h_ref[...]
    gates = gate_ref[...]
    for k in range(TOP_K):
        h = h + gates[:, k:k + 1] * y_ref[k].astype(F32)
    ple = _rms(_dot(p_ref[...].astype(BF16), wp_ref[...]), gp_ref[...])
    h = h + jax.nn.sigmoid(_dot(h.astype(BF16), wgate_ref[...])) * ple
    o_ref[...] = _rms(h, gfin_ref[...]) if last_layer else h


def _final(h1, y_tok, gates, p2d, w_ple_proj, g_ple, w_ple_gate, g_final, *, last_layer, tm=512):
    t, d = h1.shape
    row = lambda i: (i, 0)
    const = lambda i: (0, 0)
    full = lambda a: pl.BlockSpec(a.shape, const)
    args = (h1, y_tok, gates, p2d, w_ple_proj.astype(BF16), g_ple.reshape(1, d), w_ple_gate.astype(BF16),
            g_final.reshape(1, d))
    in_specs = [pl.BlockSpec((tm, d), row), pl.BlockSpec((TOP_K, tm, d), lambda i: (0, i, 0)),
                pl.BlockSpec((tm, TOP_K), row), pl.BlockSpec((tm, p2d.shape[1]), row)] + [full(a) for a in args[4:]]
    return pl.pallas_call(
        functools.partial(_final_kernel, last_layer=last_layer), grid=(t // tm,), in_specs=in_specs,
        out_specs=pl.BlockSpec((tm, d), row),
        out_shape=jax.ShapeDtypeStruct((t, d), F32),
        compiler_params=_params("arbitrary"), name="final",
    )(*args)


def kernel(x, p, g_mix, w_in, w_gla_gate, b_gla_gate, g_gla_norm, w_out, g_ffn, w_router, b_router,
           w1, b1, w2, b2, w_ple_gate, w_ple_proj, g_ple, g_final, *, moe_tm=256):
    batch, seq, d = x.shape
    depth = p.shape[0]
    t = batch * seq
    h = x.reshape(t, d)
    for i in range(depth):
        qa, ka, vt, kmean, qg, kg, vg, rg, la = _inproj(h, g_mix[i], w_in[i], w_gla_gate[i], b_gla_gate[i],
                                                        batch=batch, seq=seq)
        o_moba = _moba(qa, ka, vt, kmean, batch=batch, seq=seq)
        o_gla = _gla(qg, kg, vg, rg, la, g_gla_norm[i], batch=batch, seq=seq)
        h1, f, logits = _outproj(h, o_moba, o_gla, w_out[i], g_ffn[i], w_router[i], b_router[i])
        gates, row_tok, slot, block_expert, n_used = _route(logits[:, :N_EXPERTS], tm=moe_tm)
        y = _moe(f[row_tok], block_expert, n_used, w1[i], b1[i], w2[i], b2[i], tm=moe_tm)
        y_tok = y[slot.T]
        h = _final(h1, y_tok, gates, p[i].reshape(t, -1), w_ple_proj[i], g_ple[i], w_ple_gate[i],
                   g_final, last_layer=i == depth - 1)
    return h.reshape(batch, seq, d)
```

```python
import functools

import jax
import jax.numpy as jnp
import numpy as np
from jax import lax
from jax.experimental import pallas as pl
from jax.experimental.pallas import tpu as pltpu

HEAD_DIM = 64
MOBA_HEADS = 8
MOBA_WIDTH = MOBA_HEADS * HEAD_DIM
MOBA_BLOCK = 256
MOBA_TOPK = 3
ALIBI_MAX = 8.0
GLA_HEADS = 4
GLA_DK = 64
GLA_DV = 128
GLA_KEY_WIDTH = GLA_HEADS * GLA_DK
GLA_VAL_WIDTH = GLA_HEADS * GLA_DV
GLA_GATE_RANK = 16
GLA_GATE_TAU = 16.0
GLA_CHUNK = 64
N_EXPERTS = 32
TOP_K = 4
SWIGLU_ALPHA = 1.702
SWIGLU_LIMIT = 7.0
EPS = 1e-6

LANES = 128
MASKED = -1e30
VMEM_LIMIT = 56 * 1024 * 1024

BF16 = jnp.bfloat16
F32 = jnp.float32

_NT = (((1,), (1,)), ((), ()))
_TN = (((0,), (0,)), ((), ()))


def _dot(a, b, dims=None, precision=None):
    if dims is None:
        return jnp.dot(a, b, preferred_element_type=F32, precision=precision)
    return lax.dot_general(a, b, dims, preferred_element_type=F32, precision=precision)


def _rms(x, g):
    return x * lax.rsqrt(jnp.mean(x * x, axis=-1, keepdims=True) + EPS) * g


def _col_reduce(x, pair_op, final_op, rows=64):
    parts = [x[i:i + rows] for i in range(0, x.shape[0], rows)]
    while len(parts) > 1:
        parts = [pair_op(parts[i], parts[i + 1]) for i in range(0, len(parts), 2)]
    return final_op(parts[0], axis=0, keepdims=True)


def _params(*sem):
    return pltpu.CompilerParams(dimension_semantics=sem, vmem_limit_bytes=VMEM_LIMIT)


def _inproj_kernel(x_ref, g_ref, wq_ref, wk_ref, wvt_ref, wgla_ref, wag_ref, wgate_ref, bgate_ref,
                   qa_ref, ka_ref, vt_ref, km_ref, qg_ref, kg_ref, vg_ref, rg_ref, la_ref):
    i = pl.program_id(1)
    tm = x_ref.shape[0]
    nblk = tm // MOBA_BLOCK
    ab = _rms(x_ref[...], g_ref[...]).astype(BF16)
    qa_ref[...] = (_dot(ab, wq_ref[...]) * (HEAD_DIM ** -0.5)).astype(BF16)
    ka = _dot(ab, wk_ref[...])
    ka_ref[...] = ka.astype(BF16)
    @pl.when(i == 0)
    def _():
        km_ref[...] = jnp.zeros_like(km_ref)

    km = km_ref[...]
    blk_row = lax.broadcasted_iota(jnp.int32, km.shape, 0)
    for j in range(nblk):
        mean_j = jnp.mean(ka[j * MOBA_BLOCK:(j + 1) * MOBA_BLOCK], axis=0, keepdims=True)
        km = jnp.where(blk_row == i * nblk + j, mean_j, km)
    km_ref[...] = km
    vt = _dot(wvt_ref[...], ab, _NT).astype(BF16)
    for j in range(nblk):
        vt_ref[j] = vt[:, j * MOBA_BLOCK:(j + 1) * MOBA_BLOCK]
    gla = _dot(ab, wgla_ref[...])
    o = 0
    for ref in (qg_ref, kg_ref, vg_ref, rg_ref):
        w = ref.shape[1]
        ref[...] = gla[:, o:o + w]
        o += w
    ag = _dot(ab, wag_ref[...]).astype(BF16)
    z = _dot(ag, wgate_ref[...]) + bgate_ref[...]
    la_ref[...] = jax.nn.log_sigmoid(z) / GLA_GATE_TAU


def _inproj(x2d, g_mix, w_in, w_gla_gate, b_gla_gate, *, batch, seq, tm=512):
    t, d = x2d.shape
    nb = seq // MOBA_BLOCK
    o_k, o_v, o_qg = MOBA_WIDTH, 2 * MOBA_WIDTH, 3 * MOBA_WIDTH
    o_ag = o_qg + 2 * GLA_KEY_WIDTH + 2 * GLA_VAL_WIDTH
    wq = w_in[:, :o_k].astype(BF16)
    wk = w_in[:, o_k:o_v].astype(BF16)
    wvt = w_in[:, o_v:o_qg].T.astype(BF16)
    wgla = w_in[:, o_qg:o_ag].astype(BF16)
    wag = jnp.pad(w_in[:, o_ag:], ((0, 0), (0, LANES - GLA_GATE_RANK))).astype(BF16)
    wgate = jnp.pad(w_gla_gate, ((0, LANES - GLA_GATE_RANK), (0, 0))).astype(BF16)
    steps = seq // tm
    row = lambda b, i: (b * steps + i, 0)
    const = lambda b, i: (0, 0)
    full = lambda a: pl.BlockSpec(a.shape, const)
    args = (x2d, g_mix.reshape(1, d), wq, wk, wvt, wgla, wag, wgate, b_gla_gate.reshape(1, -1))
    in_specs = [pl.BlockSpec((tm, d), row)] + [full(a) for a in args[1:]]
    out_shape = (
        jax.ShapeDtypeStruct((t, MOBA_WIDTH), BF16),
        jax.ShapeDtypeStruct((t, MOBA_WIDTH), BF16),
        jax.ShapeDtypeStruct((t // MOBA_BLOCK, MOBA_WIDTH, MOBA_BLOCK), BF16),
        jax.ShapeDtypeStruct((batch * nb, MOBA_WIDTH), F32),
        jax.ShapeDtypeStruct((t, GLA_KEY_WIDTH), F32),
        jax.ShapeDtypeStruct((t, GLA_KEY_WIDTH), F32),
        jax.ShapeDtypeStruct((t, GLA_VAL_WIDTH), F32),
        jax.ShapeDtypeStruct((t, GLA_VAL_WIDTH), F32),
        jax.ShapeDtypeStruct((t, GLA_KEY_WIDTH), F32),
    )
    out_specs = (
        pl.BlockSpec((tm, MOBA_WIDTH), row),
        pl.BlockSpec((tm, MOBA_WIDTH), row),
        pl.BlockSpec((tm // MOBA_BLOCK, MOBA_WIDTH, MOBA_BLOCK), lambda b, i: (b * steps + i, 0, 0)),
        pl.BlockSpec((nb, MOBA_WIDTH), lambda b, i: (b, 0)),
        pl.BlockSpec((tm, GLA_KEY_WIDTH), row),
        pl.BlockSpec((tm, GLA_KEY_WIDTH), row),
        pl.BlockSpec((tm, GLA_VAL_WIDTH), row),
        pl.BlockSpec((tm, GLA_VAL_WIDTH), row),
        pl.BlockSpec((tm, GLA_KEY_WIDTH), row),
    )
    return pl.pallas_call(
        _inproj_kernel, grid=(batch, steps), in_specs=in_specs, out_specs=out_specs, out_shape=out_shape,
        compiler_params=_params("arbitrary", "arbitrary"), name="inproj",
    )(*args)


def _moba_kernel(slopes_ref, q_ref, k_ref, vt_ref, km_ref, o_ref, kaug_ref, s_ref):
    hp = pl.program_id(1)
    c = pl.program_id(2)
    blk = MOBA_BLOCK
    nb = km_ref.shape[0]
    hw = 2 * HEAD_DIM
    feat_rows = nb + 8

    @pl.when(c == 0)
    def _():
        lane = lax.broadcasted_iota(jnp.int32, (blk, hw), 1)
        koff = lax.broadcasted_iota(jnp.int32, (blk, hw), 0).astype(F32)
        for hh in range(2):
            slope = slopes_ref[hp * 2 + hh]
            base = HEAD_DIM * (1 - hh)
            for jb in range(nb):
                feat = jnp.where(lane == base + nb, slope * koff, 0.0)
                feat = jnp.where((lane == base + jb) | (lane == base + nb + 1), 1.0, feat)
                kb = k_ref[jb * blk:(jb + 1) * blk, :]
                kaug_ref[hh, jb * blk:(jb + 1) * blk, :] = jnp.where(lane // HEAD_DIM == hh, kb, feat.astype(BF16))

    q2t = q_ref[...].astype(F32).T
    q2t_b = q2t.astype(BF16)
    kidx = lax.broadcasted_iota(jnp.int32, (blk, blk), 0)
    qidx = lax.broadcasted_iota(jnp.int32, (blk, blk), 1)
    causal = kidx <= qidx
    nidx = lax.broadcasted_iota(jnp.int32, (nb, blk), 0)
    r8 = lax.broadcasted_iota(jnp.int32, (8, blk), 0)
    qoff = lax.broadcasted_iota(jnp.int32, (8, blk), 1).astype(F32)
    km_lane = lax.broadcasted_iota(jnp.int32, (nb, hw), 1)
    km = km_ref[...]
    own_rows = pl.ds(pl.multiple_of(c * blk, blk), blk)
    vt_own = vt_ref[c]

    q_past, state = [], []
    for hh in range(2):
        slope = slopes_ref[hp * 2 + hh]
        kmh = jnp.where(km_lane // HEAD_DIM == hh, km, 0.0).astype(BF16)
        g = jnp.where(nidx < c, _dot(kmh, q2t_b), -jnp.inf)
        rank = jnp.zeros((nb, blk), jnp.int32)
        for m in range(nb - 1):
            gm = g[m:m + 1, :]
            beats = (gm > g) | ((gm == g) & (m < nidx))
            rank = rank + beats.astype(jnp.int32)
        sel = (nidx < c) & (rank < MOBA_TOPK)
        dist0 = -slope * ((c - nidx) * blk).astype(F32)
        tail = jnp.where(r8 == 0, 1.0, jnp.where(r8 == 1, -slope * qoff, 0.0))
        fill = jnp.zeros((HEAD_DIM - feat_rows, blk), F32)
        qh = q2t[hh * HEAD_DIM:(hh + 1) * HEAD_DIM, :]

        def q_aug(bias):
            feats = [jnp.where(bias, dist0, MASKED), tail, fill]
            return jnp.concatenate([qh] + feats if hh == 0 else feats + [qh], axis=0).astype(BF16)

        q_past.append(q_aug(sel))
        s = _dot(kaug_ref[hh, own_rows, :], q_aug(nidx == c))
        s = jnp.where(causal, s, MASKED)
        m0 = jnp.max(s, axis=0, keepdims=True)
        p = jnp.exp(s - m0)
        l0 = jnp.sum(p, axis=0, keepdims=True)
        acc0 = _dot(vt_own[hh * HEAD_DIM:(hh + 1) * HEAD_DIM, :], p.astype(BF16))
        state += [m0, l0, acc0]

    def scores(pair, slot):
        rows = pl.ds(pl.multiple_of(pair * 2 * blk, 2 * blk), 2 * blk)
        tops = []
        for hh in range(2):
            s = _dot(kaug_ref[hh, rows, :], q_past[hh])
            s_ref[slot, hh] = s
            tops.append(_col_reduce(s, jnp.maximum, jnp.max))
        return tops

    def absorb(pair, slot, tops, carry):
        vt0 = vt_ref[2 * pair]
        vt1 = vt_ref[2 * pair + 1]
        out = []
        for hh in range(2):
            m_i, l_i, acc = carry[3 * hh:3 * hh + 3]
            hrows = slice(hh * HEAD_DIM, (hh + 1) * HEAD_DIM)
            m_new = jnp.maximum(m_i, tops[hh])
            alpha = jnp.exp(m_i - m_new)
            p = jnp.exp(s_ref[slot, hh] - m_new)
            l_new = alpha * l_i + _col_reduce(p, jnp.add, jnp.sum)
            pb = p.astype(BF16)
            acc = alpha * acc + _dot(vt0[hrows, :], pb[:blk]) + _dot(vt1[hrows, :], pb[blk:])
            out += [m_new, l_new, acc]
        return out

    def step(i, carry, slot):
        tops = scores(i + 1, 1 - slot)
        return tuple(absorb(i, slot, carry[6:], carry[:6]) + tops)

    def body(i, carry):
        return lax.cond(i % 2 == 0, lambda: step(i, carry, 0), lambda: step(i, carry, 1))

    last = jnp.maximum((c + 1) // 2 - 1, 0)
    carry = lax.fori_loop(0, last, body, tuple(state + scores(0, 0)))
    state = lax.cond(last % 2 == 0, lambda: tuple(absorb(last, 0, carry[6:], carry[:6])),
                     lambda: tuple(absorb(last, 1, carry[6:], carry[:6])))
    out_t = jnp.concatenate([state[2] / state[1], state[5] / state[4]], axis=0)
    o_ref[...] = out_t.T.astype(o_ref.dtype)


def _alibi_slopes():
    slopes = np.exp2(-ALIBI_MAX * np.arange(1, MOBA_HEADS + 1, dtype=np.float32) / MOBA_HEADS).astype(np.float32)
    worst = slopes * np.float32(MOBA_BLOCK - 1)
    assert np.all(worst.astype(jnp.bfloat16).astype(np.float32) == worst), "ALiBi slopes not bf16-exact"
    return slopes


def _moba(qa, ka, vt, kmean, *, batch, seq):
    t = qa.shape[0]
    nb = seq // MOBA_BLOCK
    hw = 2 * HEAD_DIM
    assert nb % 8 == 0 and nb + 8 <= HEAD_DIM, "bias features must fit the spare contraction lanes"
    grid_spec = pltpu.PrefetchScalarGridSpec(
        num_scalar_prefetch=1,
        grid=(batch, MOBA_HEADS // 2, nb),
        in_specs=[
            pl.BlockSpec((MOBA_BLOCK, hw), lambda b, h, c, s: (b * nb + c, h)),
            pl.BlockSpec((seq, hw), lambda b, h, c, s: (b, h)),
            pl.BlockSpec((nb, hw, MOBA_BLOCK), lambda b, h, c, s: (b, h, 0)),
            pl.BlockSpec((nb, hw), lambda b, h, c, s: (b, h)),
        ],
        out_specs=pl.BlockSpec((MOBA_BLOCK, hw), lambda b, h, c, s: (b * nb + c, h)),
        scratch_shapes=[pltpu.VMEM((2, seq, hw), BF16), pltpu.VMEM((2, 2, 2 * MOBA_BLOCK, MOBA_BLOCK), F32)],
    )
    return pl.pallas_call(
        _moba_kernel, grid_spec=grid_spec, out_shape=jax.ShapeDtypeStruct((t, MOBA_WIDTH), BF16),
        compiler_params=_params("arbitrary", "arbitrary", "arbitrary"), name="moba",
    )(jnp.asarray(_alibi_slopes()), qa, ka, vt, kmean)


def _gla_kernel(q_ref, k_ref, v_ref, r_ref, la_ref, gn_ref, o_ref, st_ref):
    @pl.when(pl.program_id(1) == 0)
    def _():
        st_ref[...] = jnp.zeros_like(st_ref)

    ch = GLA_CHUNK
    n_chunks = q_ref.shape[0] // ch
    row = lax.broadcasted_iota(jnp.int32, (ch, ch), 0)
    col = lax.broadcasted_iota(jnp.int32, (ch, ch), 1)
    tri = (row >= col).astype(F32)
    causal = row >= col
    lane = lax.broadcasted_iota(jnp.int32, (1, GLA_KEY_WIDTH), 1)
    gn = gn_ref[...]
    state = st_ref[...]
    for ci in range(n_chunks):
        rows = slice(ci * ch, (ci + 1) * ch)
        la = la_ref[rows, :]
        g = _dot(tri, la, precision=lax.Precision.HIGHEST)
        g_last = g[ch - 1:ch, :]
        qd = q_ref[rows, :] * (GLA_DK ** -0.5) * jnp.exp(g)
        k = k_ref[rows, :]
        kd = (k * jnp.exp(-g)).astype(BF16)
        kt = k * jnp.exp(g_last - g)
        state_b = state.astype(BF16)
        new_state = state * jnp.exp(g_last)
        for h in range(GLA_HEADS):
            hm = lane // GLA_DK == h
            qh = jnp.where(hm, qd, 0.0).astype(BF16)
            kth = jnp.where(hm, kt, 0.0).astype(BF16)
            vh = v_ref[rows, h * GLA_DV:(h + 1) * GLA_DV].astype(BF16)
            a = jnp.where(causal, _dot(qh, kd, _NT), 0.0)
            o = _dot(a.astype(BF16), vh) + _dot(qh, state_b, _NT)
            new_state = new_state + _dot(vh, kth, _TN)
            rh = r_ref[rows, h * GLA_DV:(h + 1) * GLA_DV]
            o_ref[rows, h * GLA_DV:(h + 1) * GLA_DV] = (_rms(o, gn) * (rh * jax.nn.sigmoid(rh))).astype(o_ref.dtype)
        state = new_state
    st_ref[...] = state


def _gla(qg, kg, vg, rg, la, g_norm, *, batch, seq, tc=256):
    t = qg.shape[0]
    steps = seq // tc
    row = lambda b, i: (b * steps + i, 0)
    kspec = pl.BlockSpec((tc, GLA_KEY_WIDTH), row)
    vspec = pl.BlockSpec((tc, GLA_VAL_WIDTH), row)
    return pl.pallas_call(
        _gla_kernel, grid=(batch, steps),
        in_specs=[kspec, kspec, vspec, vspec, kspec, pl.BlockSpec((1, GLA_DV), lambda b, i: (0, 0))],
        out_specs=vspec, out_shape=jax.ShapeDtypeStruct((t, GLA_VAL_WIDTH), BF16),
        scratch_shapes=[pltpu.VMEM((GLA_DV, GLA_KEY_WIDTH), F32)],
        compiler_params=_params("arbitrary", "arbitrary"), name="gla",
    )(qg, kg, vg, rg, la, g_norm.reshape(1, GLA_DV))


def _outproj_kernel(x_ref, om_ref, og_ref, wm_ref, wg_ref, gf_ref, wr_ref, br_ref, h_ref, f_ref, lg_ref):
    h = x_ref[...] + _dot(om_ref[...], wm_ref[...]) + _dot(og_ref[...], wg_ref[...])
    h_ref[...] = h
    f = _rms(h, gf_ref[...]).astype(BF16)
    f_ref[...] = f
    lg_ref[...] = _dot(f, wr_ref[...]) + br_ref[...]


def _outproj(x2d, o_moba, o_gla, w_out, g_ffn, w_router, b_router, *, tm=512):
    t, d = x2d.shape
    wm = w_out[:MOBA_WIDTH].astype(BF16)
    wg = w_out[MOBA_WIDTH:].astype(BF16)
    wr = jnp.pad(w_router, ((0, 0), (0, LANES - N_EXPERTS))).astype(BF16)
    br = jnp.pad(b_router, (0, LANES - N_EXPERTS)).reshape(1, LANES)
    row = lambda i: (i, 0)
    const = lambda i: (0, 0)
    full = lambda a: pl.BlockSpec(a.shape, const)
    args = (x2d, o_moba, o_gla, wm, wg, g_ffn.reshape(1, d), wr, br)
    in_specs = [pl.BlockSpec((tm, d), row), pl.BlockSpec((tm, MOBA_WIDTH), row),
                pl.BlockSpec((tm, GLA_VAL_WIDTH), row)] + [full(a) for a in args[3:]]
    return pl.pallas_call(
        _outproj_kernel, grid=(t // tm,), in_specs=in_specs,
        out_specs=(pl.BlockSpec((tm, d), row), pl.BlockSpec((tm, d), row), pl.BlockSpec((tm, LANES), row)),
        out_shape=(jax.ShapeDtypeStruct((t, d), F32), jax.ShapeDtypeStruct((t, d), BF16),
                   jax.ShapeDtypeStruct((t, LANES), F32)),
        compiler_params=_params("arbitrary"), name="outproj",
    )(*args)


PAIR_CHUNK = 2 * LANES


def _moe_kernel(be_ref, nused_ref, x_ref, w1_ref, b1_ref, w2_ref, b2_ref, o_ref, w1s_ref, w2s_ref):
    i = pl.program_id(0)
    f = w2_ref.shape[1]

    @pl.when((i == 0) | (be_ref[i] != be_ref[jnp.maximum(i - 1, 0)]))
    def _():
        r = lax.broadcasted_iota(jnp.int32, (PAIR_CHUNK, PAIR_CHUNK), 0)
        c = lax.broadcasted_iota(jnp.int32, (PAIR_CHUNK, PAIR_CHUNK), 1)
        src_col = jnp.where(c < LANES, 2 * c, 2 * (c - LANES) + 1)
        perm = jnp.where(r == src_col, 1.0, 0.0).astype(BF16)
        for cc in range(2 * f // PAIR_CHUNK):
            chunk = w1_ref[0, :, cc * PAIR_CHUNK:(cc + 1) * PAIR_CHUNK].astype(BF16)
            res = _dot(chunk, perm).astype(BF16)
            w1s_ref[:, cc * LANES:(cc + 1) * LANES] = res[:, :LANES]
            w1s_ref[:, f + cc * LANES:f + (cc + 1) * LANES] = res[:, LANES:]
        w2s_ref[...] = w2_ref[0].astype(BF16)

    @pl.when(i < nused_ref[0])
    def _():
        hid = _dot(x_ref[...], w1s_ref[...]) + b1_ref[0]
        glu = jnp.minimum(hid[:, :f], SWIGLU_LIMIT)
        lin = jnp.clip(hid[:, f:], -SWIGLU_LIMIT, SWIGLU_LIMIT)
        act = glu * jax.nn.sigmoid(SWIGLU_ALPHA * glu) * (lin + 1.0)
        o_ref[...] = (_dot(act.astype(BF16), w2s_ref[...]) + b2_ref[0]).astype(o_ref.dtype)

    @pl.when(i >= nused_ref[0])
    def _():
        o_ref[...] = jnp.zeros_like(o_ref)


def _moe(x_rows, block_expert, n_used, w1, b1, w2, b2, *, tm):
    n_rows, d = x_rows.shape
    e, _, f2 = w1.shape
    f = f2 // 2
    b1p = jnp.concatenate([b1[:, 0::2], b1[:, 1::2]], axis=-1).reshape(e, 1, f2)
    b2r = b2.reshape(e, 1, d)
    grid_spec = pltpu.PrefetchScalarGridSpec(
        num_scalar_prefetch=2,
        grid=(n_rows // tm,),
        in_specs=[
            pl.BlockSpec((tm, d), lambda i, be, nu: (i, 0)),
            pl.BlockSpec((1, d, f2), lambda i, be, nu: (be[i], 0, 0)),
            pl.BlockSpec((1, 1, f2), lambda i, be, nu: (be[i], 0, 0)),
            pl.BlockSpec((1, f, d), lambda i, be, nu: (be[i], 0, 0)),
            pl.BlockSpec((1, 1, d), lambda i, be, nu: (be[i], 0, 0)),
        ],
        out_specs=pl.BlockSpec((tm, d), lambda i, be, nu: (i, 0)),
        scratch_shapes=[pltpu.VMEM((d, f2), BF16), pltpu.VMEM((f, d), BF16)],
    )
    return pl.pallas_call(
        _moe_kernel, grid_spec=grid_spec, out_shape=jax.ShapeDtypeStruct((n_rows, d), BF16),
        compiler_params=_params("arbitrary"), name="moe",
    )(block_expert, n_used, x_rows, w1, b1p, w2, b2r)


def _prefix_counts(onehot):
    n, e = onehot.shape
    g = 256
    tri = jnp.tril(jnp.ones((g, g), BF16))
    inner = jnp.einsum("ij,bje->bie", tri, onehot.reshape(n // g, g, e).astype(BF16), preferred_element_type=F32)
    totals = inner[:, -1, :]
    tri_o = jnp.tril(jnp.ones((n // g, n // g), BF16), -1)
    before = jnp.dot(tri_o, totals.astype(BF16), preferred_element_type=F32)
    return (inner + before[:, None, :]).reshape(n, e).astype(jnp.int32)


def _route(logits, *, tm):
    t = logits.shape[0]
    top_vals, top_idx = lax.top_k(logits, TOP_K)
    gates = jax.nn.softmax(top_vals, axis=-1)
    n_assign = t * TOP_K
    experts = jnp.arange(N_EXPERTS, dtype=jnp.int32)
    flat_e = top_idx.reshape(-1).astype(jnp.int32)
    onehot = flat_e[:, None] == experts[None, :]
    csum = _prefix_counts(onehot)
    counts = csum[-1]
    padded = (counts + tm - 1) // tm * tm
    pad_end = jnp.cumsum(padded)
    pad_start = pad_end - padded
    grp_start = jnp.cumsum(counts) - counts
    pick = lambda table, oh: jnp.sum(jnp.where(oh, table[None, :], 0), axis=1)
    slot = pick(pad_start, onehot) + jnp.sum(jnp.where(onehot, csum, 0), axis=1) - 1
    order = jnp.sort(flat_e * n_assign + jnp.arange(n_assign, dtype=jnp.int32)) % n_assign
    n_blocks = -(-n_assign // tm) + N_EXPERTS
    r = jnp.arange(n_blocks * tm, dtype=jnp.int32)
    e_r = jnp.minimum(jnp.sum(pad_end[None, :] <= r[:, None], axis=1), N_EXPERTS - 1)
    oh_r = e_r[:, None] == experts[None, :]
    off = r - pick(pad_start, oh_r)
    valid = (off < pick(counts, oh_r)) & (r < pad_end[-1])
    src = jnp.clip(pick(grp_start, oh_r) + off, 0, n_assign - 1)
    row_tok = jnp.where(valid, order[src] // TOP_K, r % t)
    n_used = pad_end[-1] // tm
    blk_start = jnp.minimum(jnp.arange(n_blocks, dtype=jnp.int32), n_used - 1) * tm
    block_expert = jnp.minimum(jnp.sum(pad_end[None, :] <= blk_start[:, None], axis=1), N_EXPERTS - 1)
    return gates, row_tok, slot.reshape(t, TOP_K), block_expert.astype(jnp.int32), n_used.reshape(1).astype(jnp.int32)


def _final_kernel(h_ref, y_ref, gate_ref, p_ref, wp_ref, gp_ref, wgate_ref, gfin_ref, o_ref, *, last_layer):
    h = h_ref[...]
    gates = gate_ref[...]
    for k in range(TOP_K):
        h = h + gates[:, k:k + 1] * y_ref[k].astype(F32)
    ple = _rms(_dot(p_ref[...].astype(BF16), wp_ref[...]), gp_ref[...])
    h = h + jax.nn.sigmoid(_dot(h.astype(BF16), wgate_ref[...])) * ple
    o_ref[...] = _rms(h, gfin_ref[...]) if last_layer else h


def _final(h1, y_tok, gates, p2d, w_ple_proj, g_ple, w_ple_gate, g_final, *, last_layer, tm=512):
    t, d = h1.shape
    row = lambda i: (i, 0)
    const = lambda i: (0, 0)
    full = lambda a: pl.BlockSpec(a.shape, const)
    args = (h1, y_tok, gates, p2d, w_ple_proj.astype(BF16), g_ple.reshape(1, d), w_ple_gate.astype(BF16),
            g_final.reshape(1, d))
    in_specs = [pl.BlockSpec((tm, d), row), pl.BlockSpec((TOP_K, tm, d), lambda i: (0, i, 0)),
                pl.BlockSpec((tm, TOP_K), row), pl.BlockSpec((tm, p2d.shape[1]), row)] + [full(a) for a in args[4:]]
    return pl.pallas_call(
        functools.partial(_final_kernel, last_layer=last_layer), grid=(t // tm,), in_specs=in_specs,
        out_specs=pl.BlockSpec((tm, d), row),
        out_shape=jax.ShapeDtypeStruct((t, d), F32),
        compiler_params=_params("arbitrary"), name="final",
    )(*args)


def kernel(x, p, g_mix, w_in, w_gla_gate, b_gla_gate, g_gla_norm, w_out, g_ffn, w_router, b_router,
           w1, b1, w2, b2, w_ple_gate, w_ple_proj, g_ple, g_final, *, moe_tm=256):
    batch, seq, d = x.shape
    depth = p.shape[0]
    t = batch * seq
    h = x.reshape(t, d)
    for i in range(depth):
        qa, ka, vt, kmean, qg, kg, vg, rg, la = _inproj(h, g_mix[i], w_in[i], w_gla_gate[i], b_gla_gate[i],
                                                        batch=batch, seq=seq)
        o_moba = _moba(qa, ka, vt, kmean, batch=batch, seq=seq)
        o_gla = _gla(qg, kg, vg, rg, la, g_gla_norm[i], batch=batch, seq=seq)
        h1, f, logits = _outproj(h, o_moba, o_gla, w_out[i], g_ffn[i], w_router[i], b_router[i])
        gates, row_tok, slot, block_expert, n_used = _route(logits[:, :N_EXPERTS], tm=moe_tm)
        x_rows = jnp.take(f, row_tok, axis=0, mode="clip")
        y = _moe(x_rows, block_expert, n_used, w1[i], b1[i], w2[i], b2[i], tm=moe_tm)
        y_tok = jnp.take(y, slot.T, axis=0, mode="clip")
        h = _final(h1, y_tok, gates, p[i].reshape(t, -1), w_ple_proj[i], g_ple[i], w_ple_gate[i],
                   g_final, last_layer=i == depth - 1)
    return h.reshape(batch, seq, d)
```

```python
import functools

import jax
import jax.numpy as jnp
import numpy as np
from jax import lax
from jax.experimental import pallas as pl
from jax.experimental.pallas import tpu as pltpu

HEAD_DIM = 64
MOBA_HEADS = 8
MOBA_WIDTH = MOBA_HEADS * HEAD_DIM
MOBA_BLOCK = 256
MOBA_TOPK = 3
MOBA_GROUP = 4
ALIBI_MAX = 8.0
GLA_HEADS = 4
GLA_DK = 64
GLA_DV = 128
GLA_KEY_WIDTH = GLA_HEADS * GLA_DK
GLA_VAL_WIDTH = GLA_HEADS * GLA_DV
GLA_GATE_RANK = 16
GLA_GATE_TAU = 16.0
GLA_CHUNK = 64
N_EXPERTS = 32
TOP_K = 4
SWIGLU_ALPHA = 1.702
SWIGLU_LIMIT = 7.0
EPS = 1e-6

LANES = 128
MASKED = -1e30
VMEM_LIMIT = 56 * 1024 * 1024

BF16 = jnp.bfloat16
F32 = jnp.float32

_NT = (((1,), (1,)), ((), ()))
_TN = (((0,), (0,)), ((), ()))


def _dot(a, b, dims=None, precision=None):
    if dims is None:
        return jnp.dot(a, b, preferred_element_type=F32, precision=precision)
    return lax.dot_general(a, b, dims, preferred_element_type=F32, precision=precision)


def _rms(x, g):
    return x * lax.rsqrt(jnp.mean(x * x, axis=-1, keepdims=True) + EPS) * g


def _col_reduce(x, pair_op, final_op, rows=64):
    parts = [x[i:i + rows] for i in range(0, x.shape[0], rows)]
    while len(parts) > 1:
        parts = [pair_op(parts[i], parts[i + 1]) for i in range(0, len(parts), 2)]
    return final_op(parts[0], axis=0, keepdims=True)


def _params(*sem):
    return pltpu.CompilerParams(dimension_semantics=sem, vmem_limit_bytes=VMEM_LIMIT)


def _inproj_kernel(x_ref, g_ref, wq_ref, wk_ref, wvt_ref, wgla_ref, wag_ref, wgate_ref, bgate_ref,
                   qa_ref, ka_ref, vt_ref, km_ref, qg_ref, kg_ref, vg_ref, rg_ref, la_ref):
    i = pl.program_id(1)
    tm = x_ref.shape[0]
    nblk = tm // MOBA_BLOCK
    ab = _rms(x_ref[...], g_ref[...]).astype(BF16)
    qa_ref[...] = (_dot(ab, wq_ref[...]) * (HEAD_DIM ** -0.5)).astype(BF16)
    ka = _dot(ab, wk_ref[...])
    ka_ref[...] = ka.astype(BF16)
    @pl.when(i == 0)
    def _():
        km_ref[...] = jnp.zeros_like(km_ref)

    km = km_ref[...]
    blk_row = lax.broadcasted_iota(jnp.int32, km.shape, 0)
    for j in range(nblk):
        mean_j = jnp.mean(ka[j * MOBA_BLOCK:(j + 1) * MOBA_BLOCK], axis=0, keepdims=True)
        km = jnp.where(blk_row == i * nblk + j, mean_j, km)
    km_ref[...] = km
    vt = _dot(wvt_ref[...], ab, _NT).astype(BF16)
    for j in range(nblk):
        vt_ref[j] = vt[:, j * MOBA_BLOCK:(j + 1) * MOBA_BLOCK]
    gla = _dot(ab, wgla_ref[...])
    o = 0
    for ref in (qg_ref, kg_ref, vg_ref, rg_ref):
        w = ref.shape[1]
        ref[...] = gla[:, o:o + w]
        o += w
    ag = _dot(ab, wag_ref[...]).astype(BF16)
    z = _dot(ag, wgate_ref[...]) + bgate_ref[...]
    la_ref[...] = jax.nn.log_sigmoid(z) / GLA_GATE_TAU


def _inproj(x2d, g_mix, w_in, w_gla_gate, b_gla_gate, *, batch, seq, tm=512):
    t, d = x2d.shape
    nb = seq // MOBA_BLOCK
    o_k, o_v, o_qg = MOBA_WIDTH, 2 * MOBA_WIDTH, 3 * MOBA_WIDTH
    o_ag = o_qg + 2 * GLA_KEY_WIDTH + 2 * GLA_VAL_WIDTH
    wq = w_in[:, :o_k].astype(BF16)
    wk = w_in[:, o_k:o_v].astype(BF16)
    wvt = w_in[:, o_v:o_qg].T.astype(BF16)
    wgla = w_in[:, o_qg:o_ag].astype(BF16)
    wag = jnp.pad(w_in[:, o_ag:], ((0, 0), (0, LANES - GLA_GATE_RANK))).astype(BF16)
    wgate = jnp.pad(w_gla_gate, ((0, LANES - GLA_GATE_RANK), (0, 0))).astype(BF16)
    steps = seq // tm
    row = lambda b, i: (b * steps + i, 0)
    const = lambda b, i: (0, 0)
    full = lambda a: pl.BlockSpec(a.shape, const)
    args = (x2d, g_mix.reshape(1, d), wq, wk, wvt, wgla, wag, wgate, b_gla_gate.reshape(1, -1))
    in_specs = [pl.BlockSpec((tm, d), row)] + [full(a) for a in args[1:]]
    out_shape = (
        jax.ShapeDtypeStruct((t, MOBA_WIDTH), BF16),
        jax.ShapeDtypeStruct((t, MOBA_WIDTH), BF16),
        jax.ShapeDtypeStruct((t // MOBA_BLOCK, MOBA_WIDTH, MOBA_BLOCK), BF16),
        jax.ShapeDtypeStruct((batch * nb, MOBA_WIDTH), F32),
        jax.ShapeDtypeStruct((t, GLA_KEY_WIDTH), F32),
        jax.ShapeDtypeStruct((t, GLA_KEY_WIDTH), F32),
        jax.ShapeDtypeStruct((t, GLA_VAL_WIDTH), F32),
        jax.ShapeDtypeStruct((t, GLA_VAL_WIDTH), F32),
        jax.ShapeDtypeStruct((t, GLA_KEY_WIDTH), F32),
    )
    out_specs = (
        pl.BlockSpec((tm, MOBA_WIDTH), row),
        pl.BlockSpec((tm, MOBA_WIDTH), row),
        pl.BlockSpec((tm // MOBA_BLOCK, MOBA_WIDTH, MOBA_BLOCK), lambda b, i: (b * steps + i, 0, 0)),
        pl.BlockSpec((nb, MOBA_WIDTH), lambda b, i: (b, 0)),
        pl.BlockSpec((tm, GLA_KEY_WIDTH), row),
        pl.BlockSpec((tm, GLA_KEY_WIDTH), row),
        pl.BlockSpec((tm, GLA_VAL_WIDTH), row),
        pl.BlockSpec((tm, GLA_VAL_WIDTH), row),
        pl.BlockSpec((tm, GLA_KEY_WIDTH), row),
    )
    return pl.pallas_call(
        _inproj_kernel, grid=(batch, steps), in_specs=in_specs, out_specs=out_specs, out_shape=out_shape,
        compiler_params=_params("arbitrary", "arbitrary"), name="inproj",
    )(*args)


def _moba_kernel(slopes_ref, q_ref, k_ref, vt_ref, km_ref, o_ref, kaug_ref, s_ref):
    hg = pl.program_id(1)
    c = pl.program_id(2)
    blk = MOBA_BLOCK
    nb = km_ref.shape[0]
    hw = 2 * HEAD_DIM
    heads = q_ref.shape[1] // HEAD_DIM
    feat_rows = nb + 8
    head_rows = lambda g: slice(g * HEAD_DIM, (g + 1) * HEAD_DIM)

    @pl.when(c == 0)
    def _():
        lane = lax.broadcasted_iota(jnp.int32, (blk, hw), 1)
        koff = lax.broadcasted_iota(jnp.int32, (blk, hw), 0).astype(F32)
        for g in range(heads):
            tile, hh = divmod(g, 2)
            slope = slopes_ref[hg * heads + g]
            base = HEAD_DIM * (1 - hh)
            for jb in range(nb):
                feat = jnp.where(lane == base + nb, slope * koff, 0.0)
                feat = jnp.where((lane == base + jb) | (lane == base + nb + 1), 1.0, feat)
                kb = k_ref[jb * blk:(jb + 1) * blk, tile * hw:(tile + 1) * hw]
                kaug_ref[g, jb * blk:(jb + 1) * blk, :] = jnp.where(lane // HEAD_DIM == hh, kb, feat.astype(BF16))

    qt = q_ref[...].astype(F32).T
    qt_b = qt.astype(BF16)
    kidx = lax.broadcasted_iota(jnp.int32, (blk, blk), 0)
    qidx = lax.broadcasted_iota(jnp.int32, (blk, blk), 1)
    causal = kidx <= qidx
    nidx = lax.broadcasted_iota(jnp.int32, (nb, blk), 0)
    r8 = lax.broadcasted_iota(jnp.int32, (8, blk), 0)
    qoff = lax.broadcasted_iota(jnp.int32, (8, blk), 1).astype(F32)
    km_lane = lax.broadcasted_iota(jnp.int32, km_ref.shape, 1)
    km = km_ref[...]
    own_rows = pl.ds(pl.multiple_of(c * blk, blk), blk)
    vt_own = vt_ref[c]

    q_past, state = [], []
    for g in range(heads):
        slope = slopes_ref[hg * heads + g]
        kmh = jnp.where(km_lane // HEAD_DIM == g, km, 0.0).astype(BF16)
        gate = jnp.where(nidx < c, _dot(kmh, qt_b), -jnp.inf)
        rank = jnp.zeros((nb, blk), jnp.int32)
        for m in range(nb - 1):
            gm = gate[m:m + 1, :]
            beats = (gm > gate) | ((gm == gate) & (m < nidx))
            rank = rank + beats.astype(jnp.int32)
        sel = (nidx < c) & (rank < MOBA_TOPK)
        dist0 = -slope * ((c - nidx) * blk).astype(F32)
        tail = jnp.where(r8 == 0, 1.0, jnp.where(r8 == 1, -slope * qoff, 0.0))
        fill = jnp.zeros((HEAD_DIM - feat_rows, blk), F32)
        qh = qt[head_rows(g), :]

        def q_aug(bias):
            feats = [jnp.where(bias, dist0, MASKED), tail, fill]
            return jnp.concatenate([qh] + feats if g % 2 == 0 else feats + [qh], axis=0).astype(BF16)

        q_past.append(q_aug(sel))
        s = _dot(kaug_ref[g, own_rows, :], q_aug(nidx == c))
        s = jnp.where(causal, s, MASKED)
        m0 = jnp.max(s, axis=0, keepdims=True)
        p = jnp.exp(s - m0)
        l0 = jnp.sum(p, axis=0, keepdims=True)
        acc0 = _dot(vt_own[head_rows(g), :], p.astype(BF16))
        state += [m0, l0, acc0]

    def scores(pair, slot):
        rows = pl.ds(pl.multiple_of(pair * 2 * blk, 2 * blk), 2 * blk)
        tops = []
        for g in range(heads):
            s = _dot(kaug_ref[g, rows, :], q_past[g])
            s_ref[slot, g] = s
            tops.append(_col_reduce(s, jnp.maximum, jnp.max))
        return tops

    def absorb(pair, slot, tops, carry):
        vt0 = vt_ref[2 * pair]
        vt1 = vt_ref[2 * pair + 1]
        out = []
        for g in range(heads):
            m_i, l_i, acc = carry[3 * g:3 * g + 3]
            m_new = jnp.maximum(m_i, tops[g])
            alpha = jnp.exp(m_i - m_new)
            p = jnp.exp(s_ref[slot, g] - m_new)
            l_new = alpha * l_i + _col_reduce(p, jnp.add, jnp.sum)
            pb = p.astype(BF16)
            acc = alpha * acc + _dot(vt0[head_rows(g), :], pb[:blk]) + _dot(vt1[head_rows(g), :], pb[blk:])
            out += [m_new, l_new, acc]
        return out

    n_state = 3 * heads

    def step(i, carry, slot):
        tops = scores(i + 1, 1 - slot)
        return tuple(absorb(i, slot, carry[n_state:], carry[:n_state]) + tops)

    def body(i, carry):
        return lax.cond(i % 2 == 0, lambda: step(i, carry, 0), lambda: step(i, carry, 1))

    last = jnp.maximum((c + 1) // 2 - 1, 0)
    carry = lax.fori_loop(0, last, body, tuple(state + scores(0, 0)))
    state = lax.cond(last % 2 == 0, lambda: tuple(absorb(last, 0, carry[n_state:], carry[:n_state])),
                     lambda: tuple(absorb(last, 1, carry[n_state:], carry[:n_state])))
    out_t = jnp.concatenate([state[3 * g + 2] / state[3 * g + 1] for g in range(heads)], axis=0)
    o_ref[...] = out_t.T.astype(o_ref.dtype)


def _alibi_slopes():
    slopes = np.exp2(-ALIBI_MAX * np.arange(1, MOBA_HEADS + 1, dtype=np.float32) / MOBA_HEADS).astype(np.float32)
    worst = slopes * np.float32(MOBA_BLOCK - 1)
    assert np.all(worst.astype(jnp.bfloat16).astype(np.float32) == worst), "ALiBi slopes not bf16-exact"
    return slopes


def _moba(qa, ka, vt, kmean, *, batch, seq):
    t = qa.shape[0]
    nb = seq // MOBA_BLOCK
    gw = MOBA_GROUP * HEAD_DIM
    assert nb % 8 == 0 and nb + 8 <= HEAD_DIM, "bias features must fit the spare contraction lanes"
    grid_spec = pltpu.PrefetchScalarGridSpec(
        num_scalar_prefetch=1,
        grid=(batch, MOBA_HEADS // MOBA_GROUP, nb),
        in_specs=[
            pl.BlockSpec((MOBA_BLOCK, gw), lambda b, h, c, s: (b * nb + c, h)),
            pl.BlockSpec((seq, gw), lambda b, h, c, s: (b, h)),
            pl.BlockSpec((nb, gw, MOBA_BLOCK), lambda b, h, c, s: (b, h, 0)),
            pl.BlockSpec((nb, gw), lambda b, h, c, s: (b, h)),
        ],
        out_specs=pl.BlockSpec((MOBA_BLOCK, gw), lambda b, h, c, s: (b * nb + c, h)),
        scratch_shapes=[pltpu.VMEM((MOBA_GROUP, seq, 2 * HEAD_DIM), BF16),
                        pltpu.VMEM((2, MOBA_GROUP, 2 * MOBA_BLOCK, MOBA_BLOCK), F32)],
    )
    return pl.pallas_call(
        _moba_kernel, grid_spec=grid_spec, out_shape=jax.ShapeDtypeStruct((t, MOBA_WIDTH), BF16),
        compiler_params=_params("arbitrary", "arbitrary", "arbitrary"), name="moba",
    )(jnp.asarray(_alibi_slopes()), qa, ka, vt, kmean)


def _gla_kernel(q_ref, k_ref, v_ref, r_ref, la_ref, gn_ref, o_ref, st_ref):
    @pl.when(pl.program_id(1) == 0)
    def _():
        st_ref[...] = jnp.zeros_like(st_ref)

    ch = GLA_CHUNK
    tc = q_ref.shape[0]
    n_chunks = tc // ch
    row = lax.broadcasted_iota(jnp.int32, (tc, tc), 0)
    col = lax.broadcasted_iota(jnp.int32, (tc, tc), 1)
    causal = (row // ch == col // ch) & (row >= col)
    tri = jnp.where(causal, 1.0, 0.0).astype(BF16)
    lane = lax.broadcasted_iota(jnp.int32, (1, GLA_KEY_WIDTH), 1)
    la = la_ref[...]
    la_hi = la.astype(BF16)
    rest = la - la_hi.astype(F32)
    la_mid = rest.astype(BF16)
    la_lo = (rest - la_mid.astype(F32)).astype(BF16)
    g = _dot(tri, la_hi) + _dot(tri, la_mid) + _dot(tri, la_lo)
    ends = [g[(ci + 1) * ch - 1:(ci + 1) * ch, :] for ci in range(n_chunks)]
    g_last = jnp.concatenate([jnp.broadcast_to(e, (ch, GLA_KEY_WIDTH)) for e in ends], axis=0)
    qd = q_ref[...] * (GLA_DK ** -0.5) * jnp.exp(g)
    k = k_ref[...]
    kd = (k * jnp.exp(-g)).astype(BF16)
    kt = k * jnp.exp(g_last - g)
    qhs, kths, vhs, intra = [], [], [], []
    for h in range(GLA_HEADS):
        hm = lane // GLA_DK == h
        qhs.append(jnp.where(hm, qd, 0.0).astype(BF16))
        kths.append(jnp.where(hm, kt, 0.0).astype(BF16))
        vhs.append(v_ref[:, h * GLA_DV:(h + 1) * GLA_DV].astype(BF16))
        a = jnp.where(causal, _dot(qhs[h], kd, _NT), 0.0)
        intra.append(_dot(a.astype(BF16), vhs[h]))
    state = st_ref[...]
    inter = [[] for _ in range(GLA_HEADS)]
    for ci in range(n_chunks):
        rows = slice(ci * ch, (ci + 1) * ch)
        state_b = state.astype(BF16)
        new_state = state * jnp.exp(ends[ci])
        for h in range(GLA_HEADS):
            inter[h].append(_dot(qhs[h][rows], state_b, _NT))
            new_state = new_state + _dot(vhs[h][rows], kths[h][rows], _TN)
        state = new_state
    st_ref[...] = state
    gn = gn_ref[...]
    for h in range(GLA_HEADS):
        o = intra[h] + jnp.concatenate(inter[h], axis=0)
        rh = r_ref[:, h * GLA_DV:(h + 1) * GLA_DV]
        o_ref[:, h * GLA_DV:(h + 1) * GLA_DV] = (_rms(o, gn) * (rh * jax.nn.sigmoid(rh))).astype(o_ref.dtype)


def _gla(qg, kg, vg, rg, la, g_norm, *, batch, seq, tc=256):
    t = qg.shape[0]
    steps = seq // tc
    row = lambda b, i: (b * steps + i, 0)
    kspec = pl.BlockSpec((tc, GLA_KEY_WIDTH), row)
    vspec = pl.BlockSpec((tc, GLA_VAL_WIDTH), row)
    return pl.pallas_call(
        _gla_kernel, grid=(batch, steps),
        in_specs=[kspec, kspec, vspec, vspec, kspec, pl.BlockSpec((1, GLA_DV), lambda b, i: (0, 0))],
        out_specs=vspec, out_shape=jax.ShapeDtypeStruct((t, GLA_VAL_WIDTH), BF16),
        scratch_shapes=[pltpu.VMEM((GLA_DV, GLA_KEY_WIDTH), F32)],
        compiler_params=_params("arbitrary", "arbitrary"), name="gla",
    )(qg, kg, vg, rg, la, g_norm.reshape(1, GLA_DV))


def _outproj_kernel(x_ref, om_ref, og_ref, wm_ref, wg_ref, gf_ref, wr_ref, br_ref, h_ref, f_ref, lg_ref):
    h = x_ref[...] + _dot(om_ref[...], wm_ref[...]) + _dot(og_ref[...], wg_ref[...])
    h_ref[...] = h
    f = _rms(h, gf_ref[...]).astype(BF16)
    f_ref[...] = f
    lg_ref[...] = _dot(f, wr_ref[...]) + br_ref[...]


def _outproj(x2d, o_moba, o_gla, w_out, g_ffn, w_router, b_router, *, tm=512):
    t, d = x2d.shape
    wm = w_out[:MOBA_WIDTH].astype(BF16)
    wg = w_out[MOBA_WIDTH:].astype(BF16)
    wr = jnp.pad(w_router, ((0, 0), (0, LANES - N_EXPERTS))).astype(BF16)
    br = jnp.pad(b_router, (0, LANES - N_EXPERTS)).reshape(1, LANES)
    row = lambda i: (i, 0)
    const = lambda i: (0, 0)
    full = lambda a: pl.BlockSpec(a.shape, const)
    args = (x2d, o_moba, o_gla, wm, wg, g_ffn.reshape(1, d), wr, br)
    in_specs = [pl.BlockSpec((tm, d), row), pl.BlockSpec((tm, MOBA_WIDTH), row),
                pl.BlockSpec((tm, GLA_VAL_WIDTH), row)] + [full(a) for a in args[3:]]
    return pl.pallas_call(
        _outproj_kernel, grid=(t // tm,), in_specs=in_specs,
        out_specs=(pl.BlockSpec((tm, d), row), pl.BlockSpec((tm, d), row), pl.BlockSpec((tm, LANES), row)),
        out_shape=(jax.ShapeDtypeStruct((t, d), F32), jax.ShapeDtypeStruct((t, d), BF16),
                   jax.ShapeDtypeStruct((t, LANES), F32)),
        compiler_params=_params("arbitrary"), name="outproj",
    )(*args)


PAIR_CHUNK = 2 * LANES


def _moe_kernel(be_ref, nused_ref, x_ref, w1_ref, b1_ref, w2_ref, b2_ref, o_ref, w1s_ref, w2s_ref):
    i = pl.program_id(0)
    f = w2_ref.shape[1]

    @pl.when((i == 0) | (be_ref[i] != be_ref[jnp.maximum(i - 1, 0)]))
    def _():
        r = lax.broadcasted_iota(jnp.int32, (PAIR_CHUNK, PAIR_CHUNK), 0)
        c = lax.broadcasted_iota(jnp.int32, (PAIR_CHUNK, PAIR_CHUNK), 1)
        src_col = jnp.where(c < LANES, 2 * c, 2 * (c - LANES) + 1)
        perm = jnp.where(r == src_col, 1.0, 0.0).astype(BF16)
        for cc in range(2 * f // PAIR_CHUNK):
            chunk = w1_ref[0, :, cc * PAIR_CHUNK:(cc + 1) * PAIR_CHUNK].astype(BF16)
            res = _dot(chunk, perm).astype(BF16)
            w1s_ref[:, cc * LANES:(cc + 1) * LANES] = res[:, :LANES]
            w1s_ref[:, f + cc * LANES:f + (cc + 1) * LANES] = res[:, LANES:]
        w2s_ref[...] = w2_ref[0].astype(BF16)

    @pl.when(i < nused_ref[0])
    def _():
        hid = _dot(x_ref[...], w1s_ref[...]) + b1_ref[0]
        glu = jnp.minimum(hid[:, :f], SWIGLU_LIMIT)
        lin = jnp.clip(hid[:, f:], -SWIGLU_LIMIT, SWIGLU_LIMIT)
        act = glu * jax.nn.sigmoid(SWIGLU_ALPHA * glu) * (lin + 1.0)
        o_ref[...] = (_dot(act.astype(BF16), w2s_ref[...]) + b2_ref[0]).astype(o_ref.dtype)

    @pl.when(i >= nused_ref[0])
    def _():
        o_ref[...] = jnp.zeros_like(o_ref)


def _moe(x_rows, block_expert, n_used, w1, b1, w2, b2, *, tm):
    n_rows, d = x_rows.shape
    e, _, f2 = w1.shape
    f = f2 // 2
    b1p = jnp.concatenate([b1[:, 0::2], b1[:, 1::2]], axis=-1).reshape(e, 1, f2)
    b2r = b2.reshape(e, 1, d)
    grid_spec = pltpu.PrefetchScalarGridSpec(
        num_scalar_prefetch=2,
        grid=(n_rows // tm,),
        in_specs=[
            pl.BlockSpec((tm, d), lambda i, be, nu: (i, 0)),
            pl.BlockSpec((1, d, f2), lambda i, be, nu: (be[i], 0, 0)),
            pl.BlockSpec((1, 1, f2), lambda i, be, nu: (be[i], 0, 0)),
            pl.BlockSpec((1, f, d), lambda i, be, nu: (be[i], 0, 0)),
            pl.BlockSpec((1, 1, d), lambda i, be, nu: (be[i], 0, 0)),
        ],
        out_specs=pl.BlockSpec((tm, d), lambda i, be, nu: (i, 0)),
        scratch_shapes=[pltpu.VMEM((d, f2), BF16), pltpu.VMEM((f, d), BF16)],
    )
    return pl.pallas_call(
        _moe_kernel, grid_spec=grid_spec, out_shape=jax.ShapeDtypeStruct((n_rows, d), BF16),
        compiler_params=_params("arbitrary"), name="moe",
    )(block_expert, n_used, x_rows, w1, b1p, w2, b2r)


def _prefix_counts(onehot):
    n, e = onehot.shape
    g = 256
    tri = jnp.tril(jnp.ones((g, g), BF16))
    inner = jnp.einsum("ij,bje->bie", tri, onehot.reshape(n // g, g, e).astype(BF16), preferred_element_type=F32)
    totals = inner[:, -1, :]
    tri_o = jnp.tril(jnp.ones((n // g, n // g), BF16), -1)
    before = jnp.dot(tri_o, totals.astype(BF16), preferred_element_type=F32)
    return (inner + before[:, None, :]).reshape(n, e).astype(jnp.int32)


def _route(logits, *, tm):
    t = logits.shape[0]
    top_vals, top_idx = lax.top_k(logits, TOP_K)
    gates = jax.nn.softmax(top_vals, axis=-1)
    n_assign = t * TOP_K
    experts = jnp.arange(N_EXPERTS, dtype=jnp.int32)
    flat_e = top_idx.reshape(-1).astype(jnp.int32)
    onehot = flat_e[:, None] == experts[None, :]
    csum = _prefix_counts(onehot)
    counts = csum[-1]
    padded = (counts + tm - 1) // tm * tm
    pad_end = jnp.cumsum(padded)
    pad_start = pad_end - padded
    grp_start = jnp.cumsum(counts) - counts
    pick = lambda table, oh: jnp.sum(jnp.where(oh, table[None, :], 0), axis=1)
    slot = pick(pad_start, onehot) + jnp.sum(jnp.where(onehot, csum, 0), axis=1) - 1
    order = jnp.sort(flat_e * n_assign + jnp.arange(n_assign, dtype=jnp.int32)) % n_assign
    n_blocks = -(-n_assign // tm) + N_EXPERTS
    r = jnp.arange(n_blocks * tm, dtype=jnp.int32)
    e_r = jnp.minimum(jnp.sum(pad_end[None, :] <= r[:, None], axis=1), N_EXPERTS - 1)
    oh_r = e_r[:, None] == experts[None, :]
    off = r - pick(pad_start, oh_r)
    valid = (off < pick(counts, oh_r)) & (r < pad_end[-1])
    src = jnp.clip(pick(grp_start, oh_r) + off, 0, n_assign - 1)
    row_tok = jnp.where(valid, order[src] // TOP_K, r % t)
    n_used = pad_end[-1] // tm
    blk_start = jnp.minimum(jnp.arange(n_blocks, dtype=jnp.int32), n_used - 1) * tm
    block_expert = jnp.minimum(jnp.sum(pad_end[None, :] <= blk_start[:, None], axis=1), N_EXPERTS - 1)
    return gates, row_tok, slot.reshape(t, TOP_K), block_expert.astype(jnp.int32), n_used.reshape(1).astype(jnp.int32)


def _final_kernel(h_ref, y_ref, gate_ref, p_ref, wp_ref, gp_ref, wgate_ref, gfin_ref, o_ref, *, last_layer):
    h = h_ref[...]
    gates = gate_ref[...]
    for k in range(TOP_K):
        h = h + gates[:, k:k + 1] * y_ref[k].astype(F32)
    ple = _rms(_dot(p_ref[...].astype(BF16), wp_ref[...]), gp_ref[...])
    h = h + jax.nn.sigmoid(_dot(h.astype(BF16), wgate_ref[...])) * ple
    o_ref[...] = _rms(h, gfin_ref[...]) if last_layer else h


def _final(h1, y_tok, gates, p2d, w_ple_proj, g_ple, w_ple_gate, g_final, *, last_layer, tm=512):
    t, d = h1.shape
    row = lambda i: (i, 0)
    const = lambda i: (0, 0)
    full = lambda a: pl.BlockSpec(a.shape, const)
    args = (h1, y_tok, gates, p2d, w_ple_proj.astype(BF16), g_ple.reshape(1, d), w_ple_gate.astype(BF16),
            g_final.reshape(1, d))
    in_specs = [pl.BlockSpec((tm, d), row), pl.BlockSpec((TOP_K, tm, d), lambda i: (0, i, 0)),
                pl.BlockSpec((tm, TOP_K), row), pl.BlockSpec((tm, p2d.shape[1]), row)] + [full(a) for a in args[4:]]
    return pl.pallas_call(
        functools.partial(_final_kernel, last_layer=last_layer), grid=(t // tm,), in_specs=in_specs,
        out_specs=pl.BlockSpec((tm, d), row),
        out_shape=jax.ShapeDtypeStruct((t, d), F32),
        compiler_params=_params("arbitrary"), name="final",
    )(*args)


def kernel(x, p, g_mix, w_in, w_gla_gate, b_gla_gate, g_gla_norm, w_out, g_ffn, w_router, b_router,
           w1, b1, w2, b2, w_ple_gate, w_ple_proj, g_ple, g_final, *, moe_tm=512):
    batch, seq, d = x.shape
    depth = p.shape[0]
    t = batch * seq
    h = x.reshape(t, d)
    for i in range(depth):
        qa, ka, vt, kmean, qg, kg, vg, rg, la = _inproj(h, g_mix[i], w_in[i], w_gla_gate[i], b_gla_gate[i],
                                                        batch=batch, seq=seq)
        o_moba = _moba(qa, ka, vt, kmean, batch=batch, seq=seq)
        o_gla = _gla(qg, kg, vg, rg, la, g_gla_norm[i], batch=batch, seq=seq)
        h1, f, logits = _outproj(h, o_moba, o_gla, w_out[i], g_ffn[i], w_router[i], b_router[i])
        gates, row_tok, slot, block_expert, n_used = _route(logits[:, :N_EXPERTS], tm=moe_tm)
        x_rows = jnp.take(f, row_tok, axis=0, mode="clip")
        y = _moe(x_rows, block_expert, n_used, w1[i], b1[i], w2[i], b2[i], tm=moe_tm)
        y_tok = jnp.take(y, slot.T, axis=0, mode="clip")
        h = _final(h1, y_tok, gates, p[i].reshape(t, -1), w_ple_proj[i], g_ple[i], w_ple_gate[i],
                   g_final, last_layer=i == depth - 1)
    return h.reshape(batch, seq, d)
```

```python
import functools

import jax
import jax.numpy as jnp
import numpy as np
from jax import lax
from jax.experimental import pallas as pl
from jax.experimental.pallas import tpu as pltpu

HEAD_DIM = 64
MOBA_HEADS = 8
MOBA_WIDTH = MOBA_HEADS * HEAD_DIM
MOBA_BLOCK = 256
MOBA_TOPK = 3
MOBA_GROUP = 4
ALIBI_MAX = 8.0
GLA_HEADS = 4
GLA_DK = 64
GLA_DV = 128
GLA_KEY_WIDTH = GLA_HEADS * GLA_DK
GLA_VAL_WIDTH = GLA_HEADS * GLA_DV
GLA_GATE_RANK = 16
GLA_GATE_TAU = 16.0
GLA_CHUNK = 64
N_EXPERTS = 32
TOP_K = 4
SWIGLU_ALPHA = 1.702
SWIGLU_LIMIT = 7.0
EPS = 1e-6

LANES = 128
MASKED = -1e30
VMEM_LIMIT = 56 * 1024 * 1024

BF16 = jnp.bfloat16
F32 = jnp.float32

_NT = (((1,), (1,)), ((), ()))
_TN = (((0,), (0,)), ((), ()))


def _dot(a, b, dims=None, precision=None):
    if dims is None:
        return jnp.dot(a, b, preferred_element_type=F32, precision=precision)
    return lax.dot_general(a, b, dims, preferred_element_type=F32, precision=precision)


def _rms(x, g):
    return x * lax.rsqrt(jnp.mean(x * x, axis=-1, keepdims=True) + EPS) * g


def _col_reduce(x, pair_op, final_op, rows=64):
    parts = [x[i:i + rows] for i in range(0, x.shape[0], rows)]
    while len(parts) > 1:
        parts = [pair_op(parts[i], parts[i + 1]) for i in range(0, len(parts), 2)]
    return final_op(parts[0], axis=0, keepdims=True)


def _params(*sem):
    return pltpu.CompilerParams(dimension_semantics=sem, vmem_limit_bytes=VMEM_LIMIT)


def _inproj_kernel(x_ref, g_ref, wq_ref, wk_ref, wvt_ref, wgla_ref, wag_ref, wgate_ref, bgate_ref,
                   qa_ref, ka_ref, vt_ref, km_ref, qg_ref, kg_ref, vg_ref, rg_ref, la_ref):
    i = pl.program_id(1)
    tm = x_ref.shape[0]
    nblk = tm // MOBA_BLOCK
    ab = _rms(x_ref[...], g_ref[...]).astype(BF16)
    qa_ref[...] = (_dot(ab, wq_ref[...]) * (HEAD_DIM ** -0.5)).astype(BF16)
    ka = _dot(ab, wk_ref[...])
    ka_ref[...] = ka.astype(BF16)
    @pl.when(i == 0)
    def _():
        km_ref[...] = jnp.zeros_like(km_ref)

    km = km_ref[...]
    blk_row = lax.broadcasted_iota(jnp.int32, km.shape, 0)
    for j in range(nblk):
        mean_j = jnp.mean(ka[j * MOBA_BLOCK:(j + 1) * MOBA_BLOCK], axis=0, keepdims=True)
        km = jnp.where(blk_row == i * nblk + j, mean_j, km)
    km_ref[...] = km
    vt = _dot(wvt_ref[...], ab, _NT).astype(BF16)
    for j in range(nblk):
        vt_ref[j] = vt[:, j * MOBA_BLOCK:(j + 1) * MOBA_BLOCK]
    gla = _dot(ab, wgla_ref[...])
    o = 0
    for ref in (qg_ref, kg_ref, vg_ref, rg_ref):
        w = ref.shape[1]
        ref[...] = gla[:, o:o + w]
        o += w
    ag = _dot(ab, wag_ref[...]).astype(BF16)
    z = _dot(ag, wgate_ref[...]) + bgate_ref[...]
    la_ref[...] = jax.nn.log_sigmoid(z) / GLA_GATE_TAU


def _inproj(x2d, g_mix, w_in, w_gla_gate, b_gla_gate, *, batch, seq, tm=512):
    t, d = x2d.shape
    nb = seq // MOBA_BLOCK
    o_k, o_v, o_qg = MOBA_WIDTH, 2 * MOBA_WIDTH, 3 * MOBA_WIDTH
    o_ag = o_qg + 2 * GLA_KEY_WIDTH + 2 * GLA_VAL_WIDTH
    wq = w_in[:, :o_k].astype(BF16)
    wk = w_in[:, o_k:o_v].astype(BF16)
    wvt = w_in[:, o_v:o_qg].T.astype(BF16)
    wgla = w_in[:, o_qg:o_ag].astype(BF16)
    wag = jnp.pad(w_in[:, o_ag:], ((0, 0), (0, LANES - GLA_GATE_RANK))).astype(BF16)
    wgate = jnp.pad(w_gla_gate, ((0, LANES - GLA_GATE_RANK), (0, 0))).astype(BF16)
    steps = seq // tm
    row = lambda b, i: (b * steps + i, 0)
    const = lambda b, i: (0, 0)
    full = lambda a: pl.BlockSpec(a.shape, const)
    args = (x2d, g_mix.reshape(1, d), wq, wk, wvt, wgla, wag, wgate, b_gla_gate.reshape(1, -1))
    in_specs = [pl.BlockSpec((tm, d), row)] + [full(a) for a in args[1:]]
    out_shape = (
        jax.ShapeDtypeStruct((t, MOBA_WIDTH), BF16),
        jax.ShapeDtypeStruct((t, MOBA_WIDTH), BF16),
        jax.ShapeDtypeStruct((t // MOBA_BLOCK, MOBA_WIDTH, MOBA_BLOCK), BF16),
        jax.ShapeDtypeStruct((batch * nb, MOBA_WIDTH), F32),
        jax.ShapeDtypeStruct((t, GLA_KEY_WIDTH), F32),
        jax.ShapeDtypeStruct((t, GLA_KEY_WIDTH), F32),
        jax.ShapeDtypeStruct((t, GLA_VAL_WIDTH), F32),
        jax.ShapeDtypeStruct((t, GLA_VAL_WIDTH), F32),
        jax.ShapeDtypeStruct((t, GLA_KEY_WIDTH), F32),
    )
    out_specs = (
        pl.BlockSpec((tm, MOBA_WIDTH), row),
        pl.BlockSpec((tm, MOBA_WIDTH), row),
        pl.BlockSpec((tm // MOBA_BLOCK, MOBA_WIDTH, MOBA_BLOCK), lambda b, i: (b * steps + i, 0, 0)),
        pl.BlockSpec((nb, MOBA_WIDTH), lambda b, i: (b, 0)),
        pl.BlockSpec((tm, GLA_KEY_WIDTH), row),
        pl.BlockSpec((tm, GLA_KEY_WIDTH), row),
        pl.BlockSpec((tm, GLA_VAL_WIDTH), row),
        pl.BlockSpec((tm, GLA_VAL_WIDTH), row),
        pl.BlockSpec((tm, GLA_KEY_WIDTH), row),
    )
    return pl.pallas_call(
        _inproj_kernel, grid=(batch, steps), in_specs=in_specs, out_specs=out_specs, out_shape=out_shape,
        compiler_params=_params("arbitrary", "arbitrary"), name="inproj",
    )(*args)


def _moba_kernel(slopes_ref, q_ref, k_ref, vt_ref, km_ref, o_ref, kaug_ref, s_ref):
    hg = pl.program_id(1)
    c = pl.program_id(2)
    blk = MOBA_BLOCK
    nb = km_ref.shape[0]
    hw = 2 * HEAD_DIM
    heads = q_ref.shape[1] // HEAD_DIM
    feat_rows = nb + 8
    head_rows = lambda g: slice(g * HEAD_DIM, (g + 1) * HEAD_DIM)

    @pl.when(c == 0)
    def _():
        lane = lax.broadcasted_iota(jnp.int32, (blk, hw), 1)
        koff = lax.broadcasted_iota(jnp.int32, (blk, hw), 0).astype(F32)
        for g in range(heads):
            tile, hh = divmod(g, 2)
            slope = slopes_ref[hg * heads + g]
            base = HEAD_DIM * (1 - hh)
            for jb in range(nb):
                feat = jnp.where(lane == base + nb, slope * koff, 0.0)
                feat = jnp.where((lane == base + jb) | (lane == base + nb + 1), 1.0, feat)
                kb = k_ref[jb * blk:(jb + 1) * blk, tile * hw:(tile + 1) * hw]
                kaug_ref[g, jb * blk:(jb + 1) * blk, :] = jnp.where(lane // HEAD_DIM == hh, kb, feat.astype(BF16))

    qt = q_ref[...].astype(F32).T
    qt_b = qt.astype(BF16)
    kidx = lax.broadcasted_iota(jnp.int32, (blk, blk), 0)
    qidx = lax.broadcasted_iota(jnp.int32, (blk, blk), 1)
    causal = kidx <= qidx
    nidx = lax.broadcasted_iota(jnp.int32, (nb, blk), 0)
    r8 = lax.broadcasted_iota(jnp.int32, (8, blk), 0)
    qoff = lax.broadcasted_iota(jnp.int32, (8, blk), 1).astype(F32)
    km_lane = lax.broadcasted_iota(jnp.int32, km_ref.shape, 1)
    km = km_ref[...]
    own_rows = pl.ds(pl.multiple_of(c * blk, blk), blk)
    vt_own = vt_ref[c]

    q_past, state = [], []
    for g in range(heads):
        slope = slopes_ref[hg * heads + g]
        kmh = jnp.where(km_lane // HEAD_DIM == g, km, 0.0).astype(BF16)
        gate = jnp.where(nidx < c, _dot(kmh, qt_b), -jnp.inf)
        rank = jnp.zeros((nb, blk), jnp.int32)
        for m in range(nb - 1):
            gm = gate[m:m + 1, :]
            beats = (gm > gate) | ((gm == gate) & (m < nidx))
            rank = rank + beats.astype(jnp.int32)
        sel = (nidx < c) & (rank < MOBA_TOPK)
        dist0 = -slope * ((c - nidx) * blk).astype(F32)
        tail = jnp.where(r8 == 0, 1.0, jnp.where(r8 == 1, -slope * qoff, 0.0))
        fill = jnp.zeros((HEAD_DIM - feat_rows, blk), F32)
        qh = qt[head_rows(g), :]

        def q_aug(bias):
            feats = [jnp.where(bias, dist0, MASKED), tail, fill]
            return jnp.concatenate([qh] + feats if g % 2 == 0 else feats + [qh], axis=0).astype(BF16)

        q_past.append(q_aug(sel))
        s = _dot(kaug_ref[g, own_rows, :], q_aug(nidx == c))
        s = jnp.where(causal, s, MASKED)
        m0 = jnp.max(s, axis=0, keepdims=True)
        p = jnp.exp(s - m0)
        l0 = jnp.sum(p, axis=0, keepdims=True)
        acc0 = _dot(vt_own[head_rows(g), :], p.astype(BF16))
        state += [m0, l0, acc0]

    def scores(pair, slot):
        rows = pl.ds(pl.multiple_of(pair * 2 * blk, 2 * blk), 2 * blk)
        tops = []
        for g in range(heads):
            s = _dot(kaug_ref[g, rows, :], q_past[g])
            s_ref[slot, g] = s
            tops.append(_col_reduce(s, jnp.maximum, jnp.max))
        return tops

    def absorb(pair, slot, tops, carry):
        vt0 = vt_ref[2 * pair]
        vt1 = vt_ref[2 * pair + 1]
        out = []
        for g in range(heads):
            m_i, l_i, acc = carry[3 * g:3 * g + 3]
            m_new = jnp.maximum(m_i, tops[g])
            alpha = jnp.exp(m_i - m_new)
            p = jnp.exp(s_ref[slot, g] - m_new)
            l_new = alpha * l_i + _col_reduce(p, jnp.add, jnp.sum)
            pb = p.astype(BF16)
            acc = alpha * acc + _dot(vt0[head_rows(g), :], pb[:blk]) + _dot(vt1[head_rows(g), :], pb[blk:])
            out += [m_new, l_new, acc]
        return out

    n_state = 3 * heads

    def step(i, carry, slot):
        tops = scores(i + 1, 1 - slot)
        return tuple(absorb(i, slot, carry[n_state:], carry[:n_state]) + tops)

    def body(i, carry):
        return lax.cond(i % 2 == 0, lambda: step(i, carry, 0), lambda: step(i, carry, 1))

    last = jnp.maximum((c + 1) // 2 - 1, 0)
    carry = lax.fori_loop(0, last, body, tuple(state + scores(0, 0)))
    state = lax.cond(last % 2 == 0, lambda: tuple(absorb(last, 0, carry[n_state:], carry[:n_state])),
                     lambda: tuple(absorb(last, 1, carry[n_state:], carry[:n_state])))
    out_t = jnp.concatenate([state[3 * g + 2] / state[3 * g + 1] for g in range(heads)], axis=0)
    o_ref[...] = out_t.T.astype(o_ref.dtype)


def _alibi_slopes():
    slopes = np.exp2(-ALIBI_MAX * np.arange(1, MOBA_HEADS + 1, dtype=np.float32) / MOBA_HEADS).astype(np.float32)
    worst = slopes * np.float32(MOBA_BLOCK - 1)
    assert np.all(worst.astype(jnp.bfloat16).astype(np.float32) == worst), "ALiBi slopes not bf16-exact"
    return slopes


def _moba(qa, ka, vt, kmean, *, batch, seq):
    t = qa.shape[0]
    nb = seq // MOBA_BLOCK
    gw = MOBA_GROUP * HEAD_DIM
    assert nb % 8 == 0 and nb + 8 <= HEAD_DIM, "bias features must fit the spare contraction lanes"
    grid_spec = pltpu.PrefetchScalarGridSpec(
        num_scalar_prefetch=1,
        grid=(batch, MOBA_HEADS // MOBA_GROUP, nb),
        in_specs=[
            pl.BlockSpec((MOBA_BLOCK, gw), lambda b, h, c, s: (b * nb + c, h)),
            pl.BlockSpec((seq, gw), lambda b, h, c, s: (b, h)),
            pl.BlockSpec((nb, gw, MOBA_BLOCK), lambda b, h, c, s: (b, h, 0)),
            pl.BlockSpec((nb, gw), lambda b, h, c, s: (b, h)),
        ],
        out_specs=pl.BlockSpec((MOBA_BLOCK, gw), lambda b, h, c, s: (b * nb + c, h)),
        scratch_shapes=[pltpu.VMEM((MOBA_GROUP, seq, 2 * HEAD_DIM), BF16),
                        pltpu.VMEM((2, MOBA_GROUP, 2 * MOBA_BLOCK, MOBA_BLOCK), F32)],
    )
    return pl.pallas_call(
        _moba_kernel, grid_spec=grid_spec, out_shape=jax.ShapeDtypeStruct((t, MOBA_WIDTH), BF16),
        compiler_params=_params("arbitrary", "arbitrary", "arbitrary"), name="moba",
    )(jnp.asarray(_alibi_slopes()), qa, ka, vt, kmean)


def _gla_kernel(q_ref, k_ref, v_ref, r_ref, la_ref, gn_ref, o_ref, st_ref):
    @pl.when(pl.program_id(1) == 0)
    def _():
        st_ref[...] = jnp.zeros_like(st_ref)

    ch = GLA_CHUNK
    tc = q_ref.shape[0]
    n_chunks = tc // ch
    row = lax.broadcasted_iota(jnp.int32, (tc, tc), 0)
    col = lax.broadcasted_iota(jnp.int32, (tc, tc), 1)
    causal = (row // ch == col // ch) & (row >= col)
    tri = jnp.where(causal, 1.0, 0.0).astype(BF16)
    lane = lax.broadcasted_iota(jnp.int32, (1, GLA_KEY_WIDTH), 1)
    la = la_ref[...]
    la_hi = la.astype(BF16)
    rest = la - la_hi.astype(F32)
    la_mid = rest.astype(BF16)
    la_lo = (rest - la_mid.astype(F32)).astype(BF16)
    g = _dot(tri, la_hi) + _dot(tri, la_mid) + _dot(tri, la_lo)
    ends = [g[(ci + 1) * ch - 1:(ci + 1) * ch, :] for ci in range(n_chunks)]
    g_last = jnp.concatenate([jnp.broadcast_to(e, (ch, GLA_KEY_WIDTH)) for e in ends], axis=0)
    qd = q_ref[...] * (GLA_DK ** -0.5) * jnp.exp(g)
    k = k_ref[...]
    kd = (k * jnp.exp(-g)).astype(BF16)
    kt = k * jnp.exp(g_last - g)
    qhs, kths, vhs, intra = [], [], [], []
    for h in range(GLA_HEADS):
        hm = lane // GLA_DK == h
        qhs.append(jnp.where(hm, qd, 0.0).astype(BF16))
        kths.append(jnp.where(hm, kt, 0.0).astype(BF16))
        vhs.append(v_ref[:, h * GLA_DV:(h + 1) * GLA_DV].astype(BF16))
        a = jnp.where(causal, _dot(qhs[h], kd, _NT), 0.0)
        intra.append(_dot(a.astype(BF16), vhs[h]))
    state = st_ref[...]
    inter = [[] for _ in range(GLA_HEADS)]
    for ci in range(n_chunks):
        rows = slice(ci * ch, (ci + 1) * ch)
        state_b = state.astype(BF16)
        new_state = state * jnp.exp(ends[ci])
        for h in range(GLA_HEADS):
            inter[h].append(_dot(qhs[h][rows], state_b, _NT))
            new_state = new_state + _dot(vhs[h][rows], kths[h][rows], _TN)
        state = new_state
    st_ref[...] = state
    gn = gn_ref[...]
    for h in range(GLA_HEADS):
        o = intra[h] + jnp.concatenate(inter[h], axis=0)
        rh = r_ref[:, h * GLA_DV:(h + 1) * GLA_DV]
        o_ref[:, h * GLA_DV:(h + 1) * GLA_DV] = (_rms(o, gn) * (rh * jax.nn.sigmoid(rh))).astype(o_ref.dtype)


def _gla(qg, kg, vg, rg, la, g_norm, *, batch, seq, tc=256):
    t = qg.shape[0]
    steps = seq // tc
    row = lambda b, i: (b * steps + i, 0)
    kspec = pl.BlockSpec((tc, GLA_KEY_WIDTH), row)
    vspec = pl.BlockSpec((tc, GLA_VAL_WIDTH), row)
    return pl.pallas_call(
        _gla_kernel, grid=(batch, steps),
        in_specs=[kspec, kspec, vspec, vspec, kspec, pl.BlockSpec((1, GLA_DV), lambda b, i: (0, 0))],
        out_specs=vspec, out_shape=jax.ShapeDtypeStruct((t, GLA_VAL_WIDTH), BF16),
        scratch_shapes=[pltpu.VMEM((GLA_DV, GLA_KEY_WIDTH), F32)],
        compiler_params=_params("arbitrary", "arbitrary"), name="gla",
    )(qg, kg, vg, rg, la, g_norm.reshape(1, GLA_DV))


def _outproj_kernel(x_ref, om_ref, og_ref, wm_ref, wg_ref, gf_ref, wr_ref, br_ref, h_ref, f_ref, lg_ref):
    h = x_ref[...] + _dot(om_ref[...], wm_ref[...]) + _dot(og_ref[...], wg_ref[...])
    h_ref[...] = h
    f = _rms(h, gf_ref[...]).astype(BF16)
    f_ref[...] = f
    lg_ref[...] = _dot(wr_ref[...], f, _NT) + br_ref[...]


def _outproj(x2d, o_moba, o_gla, w_out, g_ffn, w_router, b_router, *, tm=512):
    t, d = x2d.shape
    wm = w_out[:MOBA_WIDTH].astype(BF16)
    wg = w_out[MOBA_WIDTH:].astype(BF16)
    wr = w_router.T.astype(BF16)
    br = b_router.reshape(N_EXPERTS, 1)
    row = lambda i: (i, 0)
    const = lambda i: (0, 0)
    full = lambda a: pl.BlockSpec(a.shape, const)
    args = (x2d, o_moba, o_gla, wm, wg, g_ffn.reshape(1, d), wr, br)
    in_specs = [pl.BlockSpec((tm, d), row), pl.BlockSpec((tm, MOBA_WIDTH), row),
                pl.BlockSpec((tm, GLA_VAL_WIDTH), row)] + [full(a) for a in args[3:]]
    return pl.pallas_call(
        _outproj_kernel, grid=(t // tm,), in_specs=in_specs,
        out_specs=(pl.BlockSpec((tm, d), row), pl.BlockSpec((tm, d), row),
                   pl.BlockSpec((N_EXPERTS, tm), lambda i: (0, i))),
        out_shape=(jax.ShapeDtypeStruct((t, d), F32), jax.ShapeDtypeStruct((t, d), BF16),
                   jax.ShapeDtypeStruct((N_EXPERTS, t), F32)),
        compiler_params=_params("arbitrary"), name="outproj",
    )(*args)


PAIR_CHUNK = 2 * LANES


def _moe_kernel(be_ref, nused_ref, x_ref, w1_ref, b1_ref, w2_ref, b2_ref, o_ref, w1s_ref, w2s_ref):
    i = pl.program_id(0)
    f = w2_ref.shape[1]

    @pl.when((i == 0) | (be_ref[i] != be_ref[jnp.maximum(i - 1, 0)]))
    def _():
        r = lax.broadcasted_iota(jnp.int32, (PAIR_CHUNK, PAIR_CHUNK), 0)
        c = lax.broadcasted_iota(jnp.int32, (PAIR_CHUNK, PAIR_CHUNK), 1)
        src_col = jnp.where(c < LANES, 2 * c, 2 * (c - LANES) + 1)
        perm = jnp.where(r == src_col, 1.0, 0.0).astype(BF16)
        for cc in range(2 * f // PAIR_CHUNK):
            chunk = w1_ref[0, :, cc * PAIR_CHUNK:(cc + 1) * PAIR_CHUNK].astype(BF16)
            res = _dot(chunk, perm).astype(BF16)
            w1s_ref[:, cc * LANES:(cc + 1) * LANES] = res[:, :LANES]
            w1s_ref[:, f + cc * LANES:f + (cc + 1) * LANES] = res[:, LANES:]
        w2s_ref[...] = w2_ref[0].astype(BF16)

    @pl.when(i < nused_ref[0])
    def _():
        hid = _dot(x_ref[...], w1s_ref[...]) + b1_ref[0]
        glu = jnp.minimum(hid[:, :f], SWIGLU_LIMIT)
        lin = jnp.clip(hid[:, f:], -SWIGLU_LIMIT, SWIGLU_LIMIT)
        act = glu * jax.nn.sigmoid(SWIGLU_ALPHA * glu) * (lin + 1.0)
        o_ref[...] = (_dot(act.astype(BF16), w2s_ref[...]) + b2_ref[0]).astype(o_ref.dtype)

    @pl.when(i >= nused_ref[0])
    def _():
        o_ref[...] = jnp.zeros_like(o_ref)


def _moe(x_rows, block_expert, n_used, w1, b1, w2, b2, *, tm):
    n_rows, d = x_rows.shape
    e, _, f2 = w1.shape
    f = f2 // 2
    b1p = jnp.concatenate([b1[:, 0::2], b1[:, 1::2]], axis=-1).reshape(e, 1, f2)
    b2r = b2.reshape(e, 1, d)
    grid_spec = pltpu.PrefetchScalarGridSpec(
        num_scalar_prefetch=2,
        grid=(n_rows // tm,),
        in_specs=[
            pl.BlockSpec((tm, d), lambda i, be, nu: (i, 0)),
            pl.BlockSpec((1, d, f2), lambda i, be, nu: (be[i], 0, 0)),
            pl.BlockSpec((1, 1, f2), lambda i, be, nu: (be[i], 0, 0)),
            pl.BlockSpec((1, f, d), lambda i, be, nu: (be[i], 0, 0)),
            pl.BlockSpec((1, 1, d), lambda i, be, nu: (be[i], 0, 0)),
        ],
        out_specs=pl.BlockSpec((tm, d), lambda i, be, nu: (i, 0)),
        scratch_shapes=[pltpu.VMEM((d, f2), BF16), pltpu.VMEM((f, d), BF16)],
    )
    return pl.pallas_call(
        _moe_kernel, grid_spec=grid_spec, out_shape=jax.ShapeDtypeStruct((n_rows, d), BF16),
        compiler_params=_params("arbitrary"), name="moe",
    )(block_expert, n_used, x_rows, w1, b1p, w2, b2r)


ROUTE_CHUNK = 2 * LANES


def _router_kernel(lg_ref, idx_ref, gate_ref, rank_ref, cnt_ref, carry_ref):
    @pl.when(pl.program_id(0) == 0)
    def _():
        carry_ref[...] = jnp.zeros_like(carry_ref)

    x = lg_ref[...]
    ne, tt = x.shape
    eidx = lax.broadcasted_iota(jnp.int32, (ne, tt), 0)
    member = jnp.zeros((ne, tt), F32)
    vals, ids = [], []
    for _ in range(TOP_K):
        m = jnp.max(x, axis=0, keepdims=True)
        sel = jnp.min(jnp.where(x == m, eidx, ne), axis=0, keepdims=True)
        hit = eidx == sel
        member = jnp.where(hit, 1.0, member)
        x = jnp.where(hit, -jnp.inf, x)
        vals.append(m)
        ids.append(sel)
    e = jnp.exp(jnp.concatenate(vals, axis=0) - vals[0])
    gate_ref[...] = e / jnp.sum(e, axis=0, keepdims=True)
    idx_ref[...] = jnp.concatenate(ids, axis=0)
    r = lax.broadcasted_iota(jnp.int32, (ROUTE_CHUNK, ROUTE_CHUNK), 0)
    c = lax.broadcasted_iota(jnp.int32, (ROUTE_CHUNK, ROUTE_CHUNK), 1)
    earlier = jnp.where(r < c, 1.0, 0.0).astype(BF16)
    carry = carry_ref[:, :1]
    before = []
    for ch in range(tt // ROUTE_CHUNK):
        mc = member[:, ch * ROUTE_CHUNK:(ch + 1) * ROUTE_CHUNK]
        before.append(_dot(mc.astype(BF16), earlier) + carry)
        carry = carry + jnp.sum(mc, axis=1, keepdims=True)
    before = jnp.concatenate(before, axis=1)
    ranks = [jnp.sum(jnp.where(eidx == ids[k], before, 0.0), axis=0, keepdims=True) for k in range(TOP_K)]
    rank_ref[...] = jnp.concatenate(ranks, axis=0).astype(jnp.int32)
    carry_ref[...] = jnp.broadcast_to(carry, carry_ref.shape)
    cnt_ref[...] = jnp.broadcast_to(carry, cnt_ref.shape).astype(jnp.int32)


def _router(logits_t, *, tt=2048):
    ne, t = logits_t.shape
    tok = lambda i: (0, i)
    kspec = pl.BlockSpec((TOP_K, tt), tok)
    return pl.pallas_call(
        _router_kernel, grid=(t // tt,), in_specs=[pl.BlockSpec((ne, tt), tok)],
        out_specs=(kspec, kspec, kspec, pl.BlockSpec((ne, LANES), lambda i: (0, 0))),
        out_shape=(jax.ShapeDtypeStruct((TOP_K, t), jnp.int32), jax.ShapeDtypeStruct((TOP_K, t), F32),
                   jax.ShapeDtypeStruct((TOP_K, t), jnp.int32), jax.ShapeDtypeStruct((ne, LANES), jnp.int32)),
        scratch_shapes=[pltpu.VMEM((ne, LANES), F32)],
        compiler_params=_params("arbitrary"), name="router",
    )(logits_t)


def _route(logits_t, *, tm):
    ne, t = logits_t.shape
    idx, gates, rank, cnt = _router(logits_t)
    counts = cnt[:, 0]
    padded = (counts + tm - 1) // tm * tm
    pad_end = jnp.cumsum(padded)
    pad_start = pad_end - padded
    grp_start = jnp.cumsum(counts) - counts
    experts = jnp.arange(ne, dtype=jnp.int32)

    def pick(table, ids):
        lead = (ne,) + (1,) * ids.ndim
        return jnp.sum(jnp.where(ids[None] == experts.reshape(lead), table.reshape(lead), 0), axis=0)

    slot = pick(pad_start, idx) + rank
    n_assign = TOP_K * t
    skey = jnp.sort((idx * t + jnp.arange(t, dtype=jnp.int32)[None, :]).reshape(-1))
    n_blocks = -(-n_assign // tm) + ne
    r = jnp.arange(n_blocks * tm, dtype=jnp.int32)
    e_r = jnp.minimum(jnp.sum(pad_end[:, None] <= r[None, :], axis=0), ne - 1)
    off = r - pick(pad_start, e_r)
    valid = (off < pick(counts, e_r)) & (r < pad_end[-1])
    src = jnp.clip(pick(grp_start, e_r) + off, 0, n_assign - 1)
    row_tok = jnp.where(valid, skey[src] % t, r % t)
    n_used = pad_end[-1] // tm
    blk_start = jnp.minimum(jnp.arange(n_blocks, dtype=jnp.int32), n_used - 1) * tm
    block_expert = jnp.minimum(jnp.sum(pad_end[:, None] <= blk_start[None, :], axis=0), ne - 1)
    return gates, row_tok, slot, block_expert.astype(jnp.int32), n_used.reshape(1).astype(jnp.int32)


def _final_kernel(h_ref, y_ref, gate_ref, p_ref, wp_ref, gp_ref, wgate_ref, gfin_ref, o_ref, *, last_layer):
    h = h_ref[...]
    g = gate_ref[...]
    gates = jnp.concatenate([g, jnp.zeros((8 - TOP_K, g.shape[1]), F32)], axis=0).T
    for k in range(TOP_K):
        h = h + gates[:, k:k + 1] * y_ref[k].astype(F32)
    ple = _rms(_dot(p_ref[...].astype(BF16), wp_ref[...]), gp_ref[...])
    h = h + jax.nn.sigmoid(_dot(h.astype(BF16), wgate_ref[...])) * ple
    o_ref[...] = _rms(h, gfin_ref[...]) if last_layer else h


def _final(h1, y_tok, gates, p2d, w_ple_proj, g_ple, w_ple_gate, g_final, *, last_layer, tm=512):
    t, d = h1.shape
    row = lambda i: (i, 0)
    const = lambda i: (0, 0)
    full = lambda a: pl.BlockSpec(a.shape, const)
    args = (h1, y_tok, gates, p2d, w_ple_proj.astype(BF16), g_ple.reshape(1, d), w_ple_gate.astype(BF16),
            g_final.reshape(1, d))
    in_specs = [pl.BlockSpec((tm, d), row), pl.BlockSpec((TOP_K, tm, d), lambda i: (0, i, 0)),
                pl.BlockSpec((TOP_K, tm), lambda i: (0, i)), pl.BlockSpec((tm, p2d.shape[1]), row)]
    in_specs += [full(a) for a in args[4:]]
    return pl.pallas_call(
        functools.partial(_final_kernel, last_layer=last_layer), grid=(t // tm,), in_specs=in_specs,
        out_specs=pl.BlockSpec((tm, d), row),
        out_shape=jax.ShapeDtypeStruct((t, d), F32),
        compiler_params=_params("arbitrary"), name="final",
    )(*args)


def kernel(x, p, g_mix, w_in, w_gla_gate, b_gla_gate, g_gla_norm, w_out, g_ffn, w_router, b_router,
           w1, b1, w2, b2, w_ple_gate, w_ple_proj, g_ple, g_final, *, moe_tm=512):
    batch, seq, d = x.shape
    depth = p.shape[0]
    t = batch * seq
    h = x.reshape(t, d)
    for i in range(depth):
        qa, ka, vt, kmean, qg, kg, vg, rg, la = _inproj(h, g_mix[i], w_in[i], w_gla_gate[i], b_gla_gate[i],
                                                        batch=batch, seq=seq)
        o_moba = _moba(qa, ka, vt, kmean, batch=batch, seq=seq)
        o_gla = _gla(qg, kg, vg, rg, la, g_gla_norm[i], batch=batch, seq=seq)
        h1, f, logits = _outproj(h, o_moba, o_gla, w_out[i], g_ffn[i], w_router[i], b_router[i])
        gates, row_tok, slot, block_expert, n_used = _route(logits, tm=moe_tm)
        x_rows = jnp.take(f, row_tok, axis=0, mode="clip")
        y = _moe(x_rows, block_expert, n_used, w1[i], b1[i], w2[i], b2[i], tm=moe_tm)
        y_tok = jnp.take(y, slot, axis=0, mode="clip")
        h = _final(h1, y_tok, gates, p[i].reshape(t, -1), w_ple_proj[i], g_ple[i], w_ple_gate[i],
                   g_final, last_layer=i == depth - 1)
    return h.reshape(batch, seq, d)
```

```python
import functools

import jax
import jax.numpy as jnp
import numpy as np
from jax import lax
from jax.experimental import pallas as pl
from jax.experimental.pallas import tpu as pltpu

HEAD_DIM = 64
MOBA_HEADS = 8
MOBA_WIDTH = MOBA_HEADS * HEAD_DIM
MOBA_BLOCK = 256
MOBA_TOPK = 3
MOBA_GROUP = 4
ALIBI_MAX = 8.0
GLA_HEADS = 4
GLA_DK = 64
GLA_DV = 128
GLA_KEY_WIDTH = GLA_HEADS * GLA_DK
GLA_VAL_WIDTH = GLA_HEADS * GLA_DV
GLA_GATE_RANK = 16
GLA_GATE_TAU = 16.0
GLA_CHUNK = 64
N_EXPERTS = 32
TOP_K = 4
SWIGLU_ALPHA = 1.702
SWIGLU_LIMIT = 7.0
EPS = 1e-6

LANES = 128
MASKED = -1e30
VMEM_LIMIT = 56 * 1024 * 1024

BF16 = jnp.bfloat16
F32 = jnp.float32

_NT = (((1,), (1,)), ((), ()))
_TN = (((0,), (0,)), ((), ()))


def _dot(a, b, dims=None, precision=None):
    if dims is None:
        return jnp.dot(a, b, preferred_element_type=F32, precision=precision)
    return lax.dot_general(a, b, dims, preferred_element_type=F32, precision=precision)


def _rms(x, g):
    return x * lax.rsqrt(jnp.mean(x * x, axis=-1, keepdims=True) + EPS) * g


def _col_reduce(x, pair_op, final_op, rows=64):
    parts = [x[i:i + rows] for i in range(0, x.shape[0], rows)]
    while len(parts) > 1:
        parts = [pair_op(parts[i], parts[i + 1]) for i in range(0, len(parts), 2)]
    return final_op(parts[0], axis=0, keepdims=True)


def _params(*sem):
    return pltpu.CompilerParams(dimension_semantics=sem, vmem_limit_bytes=VMEM_LIMIT)


def _inproj_kernel(x_ref, g_ref, wq_ref, wk_ref, wvt_ref, wgla_ref, wag_ref, wgate_ref, bgate_ref,
                   qa_ref, ka_ref, vt_ref, km_ref, qg_ref, kg_ref, vg_ref, rg_ref, la_ref):
    i = pl.program_id(1)
    tm = x_ref.shape[0]
    nblk = tm // MOBA_BLOCK
    ab = _rms(x_ref[...], g_ref[...]).astype(BF16)
    qa_ref[...] = (_dot(ab, wq_ref[...]) * (HEAD_DIM ** -0.5)).astype(BF16)
    ka = _dot(ab, wk_ref[...])
    ka_ref[...] = ka.astype(BF16)
    @pl.when(i == 0)
    def _():
        km_ref[...] = jnp.zeros_like(km_ref)

    km = km_ref[...]
    blk_row = lax.broadcasted_iota(jnp.int32, km.shape, 0)
    for j in range(nblk):
        mean_j = jnp.mean(ka[j * MOBA_BLOCK:(j + 1) * MOBA_BLOCK], axis=0, keepdims=True)
        km = jnp.where(blk_row == i * nblk + j, mean_j, km)
    km_ref[...] = km
    vt = _dot(wvt_ref[...], ab, _NT).astype(BF16)
    for j in range(nblk):
        vt_ref[j] = vt[:, j * MOBA_BLOCK:(j + 1) * MOBA_BLOCK]
    gla = _dot(ab, wgla_ref[...])
    o = 0
    for ref in (qg_ref, kg_ref, vg_ref, rg_ref):
        w = ref.shape[1]
        ref[...] = gla[:, o:o + w]
        o += w
    ag = _dot(ab, wag_ref[...]).astype(BF16)
    z = _dot(ag, wgate_ref[...]) + bgate_ref[...]
    la_ref[...] = jax.nn.log_sigmoid(z) / GLA_GATE_TAU


def _inproj(x2d, g_mix, w_in, w_gla_gate, b_gla_gate, *, batch, seq, row0, tm=512):
    t, d = batch * seq, x2d.shape[1]
    nb = seq // MOBA_BLOCK
    o_k, o_v, o_qg = MOBA_WIDTH, 2 * MOBA_WIDTH, 3 * MOBA_WIDTH
    o_ag = o_qg + 2 * GLA_KEY_WIDTH + 2 * GLA_VAL_WIDTH
    wq = w_in[:, :o_k].astype(BF16)
    wk = w_in[:, o_k:o_v].astype(BF16)
    wvt = w_in[:, o_v:o_qg].T.astype(BF16)
    wgla = w_in[:, o_qg:o_ag].astype(BF16)
    wag = jnp.pad(w_in[:, o_ag:], ((0, 0), (0, LANES - GLA_GATE_RANK))).astype(BF16)
    wgate = jnp.pad(w_gla_gate, ((0, LANES - GLA_GATE_RANK), (0, 0))).astype(BF16)
    steps = seq // tm
    row = lambda b, i: (b * steps + i, 0)
    const = lambda b, i: (0, 0)
    full = lambda a: pl.BlockSpec(a.shape, const)
    args = (x2d, g_mix.reshape(1, d), wq, wk, wvt, wgla, wag, wgate, b_gla_gate.reshape(1, -1))
    in_specs = [pl.BlockSpec((tm, d), lambda b, i: (row0 // tm + b * steps + i, 0))] + [full(a) for a in args[1:]]
    out_shape = (
        jax.ShapeDtypeStruct((t, MOBA_WIDTH), BF16),
        jax.ShapeDtypeStruct((t, MOBA_WIDTH), BF16),
        jax.ShapeDtypeStruct((t // MOBA_BLOCK, MOBA_WIDTH, MOBA_BLOCK), BF16),
        jax.ShapeDtypeStruct((batch * nb, MOBA_WIDTH), F32),
        jax.ShapeDtypeStruct((t, GLA_KEY_WIDTH), F32),
        jax.ShapeDtypeStruct((t, GLA_KEY_WIDTH), F32),
        jax.ShapeDtypeStruct((t, GLA_VAL_WIDTH), F32),
        jax.ShapeDtypeStruct((t, GLA_VAL_WIDTH), F32),
        jax.ShapeDtypeStruct((t, GLA_KEY_WIDTH), F32),
    )
    out_specs = (
        pl.BlockSpec((tm, MOBA_WIDTH), row),
        pl.BlockSpec((tm, MOBA_WIDTH), row),
        pl.BlockSpec((tm // MOBA_BLOCK, MOBA_WIDTH, MOBA_BLOCK), lambda b, i: (b * steps + i, 0, 0)),
        pl.BlockSpec((nb, MOBA_WIDTH), lambda b, i: (b, 0)),
        pl.BlockSpec((tm, GLA_KEY_WIDTH), row),
        pl.BlockSpec((tm, GLA_KEY_WIDTH), row),
        pl.BlockSpec((tm, GLA_VAL_WIDTH), row),
        pl.BlockSpec((tm, GLA_VAL_WIDTH), row),
        pl.BlockSpec((tm, GLA_KEY_WIDTH), row),
    )
    return pl.pallas_call(
        _inproj_kernel, grid=(batch, steps), in_specs=in_specs, out_specs=out_specs, out_shape=out_shape,
        compiler_params=_params("arbitrary", "arbitrary"), name="inproj",
    )(*args)


def _moba_kernel(slopes_ref, q_ref, k_ref, vt_ref, km_ref, o_ref, kaug_ref, s_ref):
    hg = pl.program_id(1)
    c = pl.program_id(2)
    blk = MOBA_BLOCK
    nb = km_ref.shape[0]
    hw = 2 * HEAD_DIM
    heads = q_ref.shape[1] // HEAD_DIM
    feat_rows = nb + 8
    head_rows = lambda g: slice(g * HEAD_DIM, (g + 1) * HEAD_DIM)

    @pl.when(c == 0)
    def _():
        lane = lax.broadcasted_iota(jnp.int32, (blk, hw), 1)
        koff = lax.broadcasted_iota(jnp.int32, (blk, hw), 0).astype(F32)
        for g in range(heads):
            tile, hh = divmod(g, 2)
            slope = slopes_ref[hg * heads + g]
            base = HEAD_DIM * (1 - hh)
            for jb in range(nb):
                feat = jnp.where(lane == base + nb, slope * koff, 0.0)
                feat = jnp.where((lane == base + jb) | (lane == base + nb + 1), 1.0, feat)
                kb = k_ref[jb * blk:(jb + 1) * blk, tile * hw:(tile + 1) * hw]
                kaug_ref[g, jb * blk:(jb + 1) * blk, :] = jnp.where(lane // HEAD_DIM == hh, kb, feat.astype(BF16))

    qt = q_ref[...].astype(F32).T
    qt_b = qt.astype(BF16)
    kidx = lax.broadcasted_iota(jnp.int32, (blk, blk), 0)
    qidx = lax.broadcasted_iota(jnp.int32, (blk, blk), 1)
    causal = kidx <= qidx
    nidx = lax.broadcasted_iota(jnp.int32, (nb, blk), 0)
    r8 = lax.broadcasted_iota(jnp.int32, (8, blk), 0)
    qoff = lax.broadcasted_iota(jnp.int32, (8, blk), 1).astype(F32)
    km_lane = lax.broadcasted_iota(jnp.int32, km_ref.shape, 1)
    km = km_ref[...]
    own_rows = pl.ds(pl.multiple_of(c * blk, blk), blk)
    vt_own = vt_ref[c]

    q_past, state = [], []
    for g in range(heads):
        slope = slopes_ref[hg * heads + g]
        kmh = jnp.where(km_lane // HEAD_DIM == g, km, 0.0).astype(BF16)
        gate = jnp.where(nidx < c, _dot(kmh, qt_b), -jnp.inf)
        rank = jnp.zeros((nb, blk), jnp.int32)
        for m in range(nb - 1):
            gm = gate[m:m + 1, :]
            beats = (gm > gate) | ((gm == gate) & (m < nidx))
            rank = rank + beats.astype(jnp.int32)
        sel = (nidx < c) & (rank < MOBA_TOPK)
        dist0 = -slope * ((c - nidx) * blk).astype(F32)
        tail = jnp.where(r8 == 0, 1.0, jnp.where(r8 == 1, -slope * qoff, 0.0))
        fill = jnp.zeros((HEAD_DIM - feat_rows, blk), F32)
        qh = qt[head_rows(g), :]

        def q_aug(bias):
            feats = [jnp.where(bias, dist0, MASKED), tail, fill]
            return jnp.concatenate([qh] + feats if g % 2 == 0 else feats + [qh], axis=0).astype(BF16)

        q_past.append(q_aug(sel))
        s = _dot(kaug_ref[g, own_rows, :], q_aug(nidx == c))
        s = jnp.where(causal, s, MASKED)
        m0 = jnp.max(s, axis=0, keepdims=True)
        p = jnp.exp(s - m0)
        l0 = jnp.sum(p, axis=0, keepdims=True)
        acc0 = _dot(vt_own[head_rows(g), :], p.astype(BF16))
        state += [m0, l0, acc0]

    def scores(pair, slot):
        rows = pl.ds(pl.multiple_of(pair * 2 * blk, 2 * blk), 2 * blk)
        tops = []
        for g in range(heads):
            s = _dot(kaug_ref[g, rows, :], q_past[g])
            s_ref[slot, g] = s
            tops.append(_col_reduce(s, jnp.maximum, jnp.max))
        return tops

    def absorb(pair, slot, tops, carry):
        vt0 = vt_ref[2 * pair]
        vt1 = vt_ref[2 * pair + 1]
        out = []
        for g in range(heads):
            m_i, l_i, acc = carry[3 * g:3 * g + 3]
            m_new = jnp.maximum(m_i, tops[g])
            alpha = jnp.exp(m_i - m_new)
            p = jnp.exp(s_ref[slot, g] - m_new)
            l_new = alpha * l_i + _col_reduce(p, jnp.add, jnp.sum)
            pb = p.astype(BF16)
            acc = alpha * acc + _dot(vt0[head_rows(g), :], pb[:blk]) + _dot(vt1[head_rows(g), :], pb[blk:])
            out += [m_new, l_new, acc]
        return out

    n_state = 3 * heads

    def step(i, carry, slot):
        tops = scores(i + 1, 1 - slot)
        return tuple(absorb(i, slot, carry[n_state:], carry[:n_state]) + tops)

    def body(i, carry):
        return lax.cond(i % 2 == 0, lambda: step(i, carry, 0), lambda: step(i, carry, 1))

    last = jnp.maximum((c + 1) // 2 - 1, 0)
    carry = lax.fori_loop(0, last, body, tuple(state + scores(0, 0)))
    state = lax.cond(last % 2 == 0, lambda: tuple(absorb(last, 0, carry[n_state:], carry[:n_state])),
                     lambda: tuple(absorb(last, 1, carry[n_state:], carry[:n_state])))
    out_t = jnp.concatenate([state[3 * g + 2] / state[3 * g + 1] for g in range(heads)], axis=0)
    o_ref[...] = out_t.T.astype(o_ref.dtype)


def _alibi_slopes():
    slopes = np.exp2(-ALIBI_MAX * np.arange(1, MOBA_HEADS + 1, dtype=np.float32) / MOBA_HEADS).astype(np.float32)
    worst = slopes * np.float32(MOBA_BLOCK - 1)
    assert np.all(worst.astype(jnp.bfloat16).astype(np.float32) == worst), "ALiBi slopes not bf16-exact"
    return slopes


def _moba(qa, ka, vt, kmean, *, batch, seq):
    t = qa.shape[0]
    nb = seq // MOBA_BLOCK
    gw = MOBA_GROUP * HEAD_DIM
    assert nb % 8 == 0 and nb + 8 <= HEAD_DIM, "bias features must fit the spare contraction lanes"
    grid_spec = pltpu.PrefetchScalarGridSpec(
        num_scalar_prefetch=1,
        grid=(batch, MOBA_HEADS // MOBA_GROUP, nb),
        in_specs=[
            pl.BlockSpec((MOBA_BLOCK, gw), lambda b, h, c, s: (b * nb + c, h)),
            pl.BlockSpec((seq, gw), lambda b, h, c, s: (b, h)),
            pl.BlockSpec((nb, gw, MOBA_BLOCK), lambda b, h, c, s: (b, h, 0)),
            pl.BlockSpec((nb, gw), lambda b, h, c, s: (b, h)),
        ],
        out_specs=pl.BlockSpec((MOBA_BLOCK, gw), lambda b, h, c, s: (b * nb + c, h)),
        scratch_shapes=[pltpu.VMEM((MOBA_GROUP, seq, 2 * HEAD_DIM), BF16),
                        pltpu.VMEM((2, MOBA_GROUP, 2 * MOBA_BLOCK, MOBA_BLOCK), F32)],
    )
    return pl.pallas_call(
        _moba_kernel, grid_spec=grid_spec, out_shape=jax.ShapeDtypeStruct((t, MOBA_WIDTH), BF16),
        compiler_params=_params("arbitrary", "arbitrary", "arbitrary"), name="moba",
    )(jnp.asarray(_alibi_slopes()), qa, ka, vt, kmean)


def _gla_kernel(q_ref, k_ref, v_ref, r_ref, la_ref, gn_ref, o_ref, st_ref):
    @pl.when(pl.program_id(1) == 0)
    def _():
        st_ref[...] = jnp.zeros_like(st_ref)

    ch = GLA_CHUNK
    tc = q_ref.shape[0]
    n_chunks = tc // ch
    row = lax.broadcasted_iota(jnp.int32, (tc, tc), 0)
    col = lax.broadcasted_iota(jnp.int32, (tc, tc), 1)
    causal = (row // ch == col // ch) & (row >= col)
    tri = jnp.where(causal, 1.0, 0.0).astype(BF16)
    lane = lax.broadcasted_iota(jnp.int32, (1, GLA_KEY_WIDTH), 1)
    la = la_ref[...]
    la_hi = la.astype(BF16)
    rest = la - la_hi.astype(F32)
    la_mid = rest.astype(BF16)
    la_lo = (rest - la_mid.astype(F32)).astype(BF16)
    g = _dot(tri, la_hi) + _dot(tri, la_mid) + _dot(tri, la_lo)
    ends = [g[(ci + 1) * ch - 1:(ci + 1) * ch, :] for ci in range(n_chunks)]
    g_last = jnp.concatenate([jnp.broadcast_to(e, (ch, GLA_KEY_WIDTH)) for e in ends], axis=0)
    qd = q_ref[...] * (GLA_DK ** -0.5) * jnp.exp(g)
    k = k_ref[...]
    kd = (k * jnp.exp(-g)).astype(BF16)
    kt = k * jnp.exp(g_last - g)
    qhs, kths, vhs, intra = [], [], [], []
    for h in range(GLA_HEADS):
        hm = lane // GLA_DK == h
        qhs.append(jnp.where(hm, qd, 0.0).astype(BF16))
        kths.append(jnp.where(hm, kt, 0.0).astype(BF16))
        vhs.append(v_ref[:, h * GLA_DV:(h + 1) * GLA_DV].astype(BF16))
        a = jnp.where(causal, _dot(qhs[h], kd, _NT), 0.0)
        intra.append(_dot(a.astype(BF16), vhs[h]))
    state = st_ref[...]
    inter = [[] for _ in range(GLA_HEADS)]
    for ci in range(n_chunks):
        rows = slice(ci * ch, (ci + 1) * ch)
        state_b = state.astype(BF16)
        new_state = state * jnp.exp(ends[ci])
        for h in range(GLA_HEADS):
            inter[h].append(_dot(qhs[h][rows], state_b, _NT))
            new_state = new_state + _dot(vhs[h][rows], kths[h][rows], _TN)
        state = new_state
    st_ref[...] = state
    gn = gn_ref[...]
    for h in range(GLA_HEADS):
        o = intra[h] + jnp.concatenate(inter[h], axis=0)
        rh = r_ref[:, h * GLA_DV:(h + 1) * GLA_DV]
        o_ref[:, h * GLA_DV:(h + 1) * GLA_DV] = (_rms(o, gn) * (rh * jax.nn.sigmoid(rh))).astype(o_ref.dtype)


def _gla(qg, kg, vg, rg, la, g_norm, *, batch, seq, tc=256):
    t = qg.shape[0]
    steps = seq // tc
    row = lambda b, i: (b * steps + i, 0)
    kspec = pl.BlockSpec((tc, GLA_KEY_WIDTH), row)
    vspec = pl.BlockSpec((tc, GLA_VAL_WIDTH), row)
    return pl.pallas_call(
        _gla_kernel, grid=(batch, steps),
        in_specs=[kspec, kspec, vspec, vspec, kspec, pl.BlockSpec((1, GLA_DV), lambda b, i: (0, 0))],
        out_specs=vspec, out_shape=jax.ShapeDtypeStruct((t, GLA_VAL_WIDTH), BF16),
        scratch_shapes=[pltpu.VMEM((GLA_DV, GLA_KEY_WIDTH), F32)],
        compiler_params=_params("arbitrary", "arbitrary"), name="gla",
    )(qg, kg, vg, rg, la, g_norm.reshape(1, GLA_DV))


def _outproj_kernel(x_ref, om_ref, og_ref, wm_ref, wg_ref, gf_ref, wr_ref, br_ref, h_ref, f_ref, lg_ref):
    h = x_ref[...] + _dot(om_ref[...], wm_ref[...]) + _dot(og_ref[...], wg_ref[...])
    h_ref[...] = h
    f = _rms(h, gf_ref[...]).astype(BF16)
    f_ref[...] = f
    lg_ref[...] = _dot(wr_ref[...], f, _NT) + br_ref[...]


def _outproj(x2d, o_moba, o_gla, w_out, g_ffn, w_router, b_router, *, row0, tm=512):
    t, d = o_moba.shape[0], x2d.shape[1]
    wm = w_out[:MOBA_WIDTH].astype(BF16)
    wg = w_out[MOBA_WIDTH:].astype(BF16)
    wr = w_router.T.astype(BF16)
    br = b_router.reshape(N_EXPERTS, 1)
    row = lambda i: (i, 0)
    const = lambda i: (0, 0)
    full = lambda a: pl.BlockSpec(a.shape, const)
    args = (x2d, o_moba, o_gla, wm, wg, g_ffn.reshape(1, d), wr, br)
    in_specs = [pl.BlockSpec((tm, d), lambda i: (row0 // tm + i, 0)), pl.BlockSpec((tm, MOBA_WIDTH), row),
                pl.BlockSpec((tm, GLA_VAL_WIDTH), row)] + [full(a) for a in args[3:]]
    return pl.pallas_call(
        _outproj_kernel, grid=(t // tm,), in_specs=in_specs,
        out_specs=(pl.BlockSpec((tm, d), row), pl.BlockSpec((tm, d), row),
                   pl.BlockSpec((N_EXPERTS, tm), lambda i: (0, i))),
        out_shape=(jax.ShapeDtypeStruct((t, d), F32), jax.ShapeDtypeStruct((t, d), BF16),
                   jax.ShapeDtypeStruct((N_EXPERTS, t), F32)),
        compiler_params=_params("arbitrary"), name="outproj",
    )(*args)


PAIR_CHUNK = 2 * LANES


def _moe_kernel(be_ref, nused_ref, x_ref, w1_ref, b1_ref, w2_ref, b2_ref, o_ref, w1s_ref, w2s_ref):
    i = pl.program_id(0)
    f = w2_ref.shape[1]

    @pl.when((i == 0) | (be_ref[i] != be_ref[jnp.maximum(i - 1, 0)]))
    def _():
        r = lax.broadcasted_iota(jnp.int32, (PAIR_CHUNK, PAIR_CHUNK), 0)
        c = lax.broadcasted_iota(jnp.int32, (PAIR_CHUNK, PAIR_CHUNK), 1)
        src_col = jnp.where(c < LANES, 2 * c, 2 * (c - LANES) + 1)
        perm = jnp.where(r == src_col, 1.0, 0.0).astype(BF16)
        for cc in range(2 * f // PAIR_CHUNK):
            chunk = w1_ref[0, :, cc * PAIR_CHUNK:(cc + 1) * PAIR_CHUNK].astype(BF16)
            res = _dot(chunk, perm).astype(BF16)
            w1s_ref[:, cc * LANES:(cc + 1) * LANES] = res[:, :LANES]
            w1s_ref[:, f + cc * LANES:f + (cc + 1) * LANES] = res[:, LANES:]
        w2s_ref[...] = w2_ref[0].astype(BF16)

    @pl.when(i < nused_ref[0])
    def _():
        hid = _dot(x_ref[...], w1s_ref[...]) + b1_ref[0]
        glu = jnp.minimum(hid[:, :f], SWIGLU_LIMIT)
        lin = jnp.clip(hid[:, f:], -SWIGLU_LIMIT, SWIGLU_LIMIT)
        act = glu * jax.nn.sigmoid(SWIGLU_ALPHA * glu) * (lin + 1.0)
        o_ref[...] = (_dot(act.astype(BF16), w2s_ref[...]) + b2_ref[0]).astype(o_ref.dtype)

    @pl.when(i >= nused_ref[0])
    def _():
        o_ref[...] = jnp.zeros_like(o_ref)


def _moe(x_rows, block_expert, n_used, w1, b1, w2, b2, *, tm):
    n_rows, d = x_rows.shape
    e, _, f2 = w1.shape
    f = f2 // 2
    b1p = jnp.concatenate([b1[:, 0::2], b1[:, 1::2]], axis=-1).reshape(e, 1, f2)
    b2r = b2.reshape(e, 1, d)
    grid_spec = pltpu.PrefetchScalarGridSpec(
        num_scalar_prefetch=2,
        grid=(n_rows // tm,),
        in_specs=[
            pl.BlockSpec((tm, d), lambda i, be, nu: (i, 0)),
            pl.BlockSpec((1, d, f2), lambda i, be, nu: (be[i], 0, 0)),
            pl.BlockSpec((1, 1, f2), lambda i, be, nu: (be[i], 0, 0)),
            pl.BlockSpec((1, f, d), lambda i, be, nu: (be[i], 0, 0)),
            pl.BlockSpec((1, 1, d), lambda i, be, nu: (be[i], 0, 0)),
        ],
        out_specs=pl.BlockSpec((tm, d), lambda i, be, nu: (i, 0)),
        scratch_shapes=[pltpu.VMEM((d, f2), BF16), pltpu.VMEM((f, d), BF16)],
    )
    return pl.pallas_call(
        _moe_kernel, grid_spec=grid_spec, out_shape=jax.ShapeDtypeStruct((n_rows, d), BF16),
        compiler_params=_params("arbitrary"), name="moe",
    )(block_expert, n_used, x_rows, w1, b1p, w2, b2r)


ROUTE_CHUNK = 2 * LANES


def _router_kernel(lg_ref, idx_ref, gate_ref, rank_ref, cnt_ref, carry_ref):
    @pl.when(pl.program_id(0) == 0)
    def _():
        carry_ref[...] = jnp.zeros_like(carry_ref)

    x = lg_ref[...]
    ne, tt = x.shape
    eidx = lax.broadcasted_iota(jnp.int32, (ne, tt), 0)
    member = jnp.zeros((ne, tt), F32)
    vals, ids = [], []
    for _ in range(TOP_K):
        m = jnp.max(x, axis=0, keepdims=True)
        sel = jnp.min(jnp.where(x == m, eidx, ne), axis=0, keepdims=True)
        hit = eidx == sel
        member = jnp.where(hit, 1.0, member)
        x = jnp.where(hit, -jnp.inf, x)
        vals.append(m)
        ids.append(sel)
    e = jnp.exp(jnp.concatenate(vals, axis=0) - vals[0])
    gate_ref[...] = e / jnp.sum(e, axis=0, keepdims=True)
    idx_ref[...] = jnp.concatenate(ids, axis=0)
    r = lax.broadcasted_iota(jnp.int32, (ROUTE_CHUNK, ROUTE_CHUNK), 0)
    c = lax.broadcasted_iota(jnp.int32, (ROUTE_CHUNK, ROUTE_CHUNK), 1)
    earlier = jnp.where(r < c, 1.0, 0.0).astype(BF16)
    carry = carry_ref[:, :1]
    before = []
    for ch in range(tt // ROUTE_CHUNK):
        mc = member[:, ch * ROUTE_CHUNK:(ch + 1) * ROUTE_CHUNK]
        before.append(_dot(mc.astype(BF16), earlier) + carry)
        carry = carry + jnp.sum(mc, axis=1, keepdims=True)
    before = jnp.concatenate(before, axis=1)
    ranks = [jnp.sum(jnp.where(eidx == ids[k], before, 0.0), axis=0, keepdims=True) for k in range(TOP_K)]
    rank_ref[...] = jnp.concatenate(ranks, axis=0).astype(jnp.int32)
    carry_ref[...] = jnp.broadcast_to(carry, carry_ref.shape)
    cnt_ref[...] = jnp.broadcast_to(carry, cnt_ref.shape).astype(jnp.int32)


def _router(logits_t, *, tt=2048):
    ne, t = logits_t.shape
    tok = lambda i: (0, i)
    kspec = pl.BlockSpec((TOP_K, tt), tok)
    return pl.pallas_call(
        _router_kernel, grid=(t // tt,), in_specs=[pl.BlockSpec((ne, tt), tok)],
        out_specs=(kspec, kspec, kspec, pl.BlockSpec((ne, LANES), lambda i: (0, 0))),
        out_shape=(jax.ShapeDtypeStruct((TOP_K, t), jnp.int32), jax.ShapeDtypeStruct((TOP_K, t), F32),
                   jax.ShapeDtypeStruct((TOP_K, t), jnp.int32), jax.ShapeDtypeStruct((ne, LANES), jnp.int32)),
        scratch_shapes=[pltpu.VMEM((ne, LANES), F32)],
        compiler_params=_params("arbitrary"), name="router",
    )(logits_t)


def _route(logits_t, *, tm):
    ne, t = logits_t.shape
    idx, gates, rank, cnt = _router(logits_t)
    counts = cnt[:, 0]
    padded = (counts + tm - 1) // tm * tm
    pad_end = jnp.cumsum(padded)
    pad_start = pad_end - padded
    grp_start = jnp.cumsum(counts) - counts
    experts = jnp.arange(ne, dtype=jnp.int32)

    def pick(table, ids):
        lead = (ne,) + (1,) * ids.ndim
        return jnp.sum(jnp.where(ids[None] == experts.reshape(lead), table.reshape(lead), 0), axis=0)

    slot = pick(pad_start, idx) + rank
    n_assign = TOP_K * t
    skey = jnp.sort((idx * t + jnp.arange(t, dtype=jnp.int32)[None, :]).reshape(-1))
    n_blocks = -(-n_assign // tm) + ne
    r = jnp.arange(n_blocks * tm, dtype=jnp.int32)
    e_r = jnp.minimum(jnp.sum(pad_end[:, None] <= r[None, :], axis=0), ne - 1)
    off = r - pick(pad_start, e_r)
    valid = (off < pick(counts, e_r)) & (r < pad_end[-1])
    src = jnp.clip(pick(grp_start, e_r) + off, 0, n_assign - 1)
    row_tok = jnp.where(valid, skey[src] % t, r % t)
    n_used = pad_end[-1] // tm
    blk_start = jnp.minimum(jnp.arange(n_blocks, dtype=jnp.int32), n_used - 1) * tm
    block_expert = jnp.minimum(jnp.sum(pad_end[:, None] <= blk_start[None, :], axis=0), ne - 1)
    return gates, row_tok, slot, block_expert.astype(jnp.int32), n_used.reshape(1).astype(jnp.int32)


def _final_kernel(h_ref, y_ref, gate_ref, p_ref, wp_ref, gp_ref, wgate_ref, gfin_ref, o_ref, *, last_layer):
    h = h_ref[...]
    g = gate_ref[...]
    gates = jnp.concatenate([g, jnp.zeros((8 - TOP_K, g.shape[1]), F32)], axis=0).T
    for k in range(TOP_K):
        h = h + gates[:, k:k + 1] * y_ref[k].astype(F32)
    ple = _rms(_dot(p_ref[...].astype(BF16), wp_ref[...]), gp_ref[...])
    h = h + jax.nn.sigmoid(_dot(h.astype(BF16), wgate_ref[...])) * ple
    o_ref[...] = _rms(h, gfin_ref[...]) if last_layer else h


def _final(h1, y_tok, gates, p2d, w_ple_proj, g_ple, w_ple_gate, g_final, *, last_layer, row0, tm=512):
    t, d = h1.shape
    row = lambda i: (i, 0)
    const = lambda i: (0, 0)
    full = lambda a: pl.BlockSpec(a.shape, const)
    args = (h1, y_tok, gates, p2d, w_ple_proj.astype(BF16), g_ple.reshape(1, d), w_ple_gate.astype(BF16),
            g_final.reshape(1, d))
    in_specs = [pl.BlockSpec((tm, d), row), pl.BlockSpec((TOP_K, tm, d), lambda i: (0, i, 0)),
                pl.BlockSpec((TOP_K, tm), lambda i: (0, i)),
                pl.BlockSpec((tm, p2d.shape[1]), lambda i: (row0 // tm + i, 0))]
    in_specs += [full(a) for a in args[4:]]
    return pl.pallas_call(
        functools.partial(_final_kernel, last_layer=last_layer), grid=(t // tm,), in_specs=in_specs,
        out_specs=pl.BlockSpec((tm, d), row),
        out_shape=jax.ShapeDtypeStruct((t, d), F32),
        compiler_params=_params("arbitrary"), name="final",
    )(*args)


BATCH_SPLIT = 2


def kernel(x, p, g_mix, w_in, w_gla_gate, b_gla_gate, g_gla_norm, w_out, g_ffn, w_router, b_router,
           w1, b1, w2, b2, w_ple_gate, w_ple_proj, g_ple, g_final, *, moe_tm=512):
    batch, seq, d = x.shape
    depth = p.shape[0]
    n_groups = BATCH_SPLIT if batch % BATCH_SPLIT == 0 else 1
    gb = batch // n_groups
    gt = gb * seq
    groups = range(n_groups)
    hs = [x.reshape(batch * seq, d)] * n_groups
    for i in range(depth):
        p2d = p[i].reshape(batch * seq, -1)
        row0 = [j * gt if i == 0 else 0 for j in groups]
        proj = [_inproj(hs[j], g_mix[i], w_in[i], w_gla_gate[i], b_gla_gate[i], batch=gb, seq=seq, row0=row0[j])
                for j in groups]
        o_moba = [_moba(*proj[j][:4], batch=gb, seq=seq) for j in groups]
        o_gla = [_gla(*proj[j][4:], g_gla_norm[i], batch=gb, seq=seq) for j in groups]
        mixed = [_outproj(hs[j], o_moba[j], o_gla[j], w_out[i], g_ffn[i], w_router[i], b_router[i], row0=row0[j])
                 for j in groups]
        routes = [_route(mixed[j][2], tm=moe_tm) for j in groups]
        x_rows = [jnp.take(mixed[j][1], routes[j][1], axis=0, mode="clip") for j in groups]
        y = [_moe(x_rows[j], routes[j][3], routes[j][4], w1[i], b1[i], w2[i], b2[i], tm=moe_tm) for j in groups]
        y_tok = [jnp.take(y[j], routes[j][2], axis=0, mode="clip") for j in groups]
        hs = [_final(mixed[j][0], y_tok[j], routes[j][0], p2d, w_ple_proj[i], g_ple[i], w_ple_gate[i], g_final,
                     last_layer=i == depth - 1, row0=j * gt) for j in groups]
    return jnp.concatenate(hs, axis=0).reshape(batch, seq, d)
```

```python
import functools

import jax
import jax.numpy as jnp
import numpy as np
from jax import lax
from jax.experimental import pallas as pl
from jax.experimental.compute_on import compute_on
from jax.experimental.pallas import tpu as pltpu

HEAD_DIM = 64
MOBA_HEADS = 8
MOBA_WIDTH = MOBA_HEADS * HEAD_DIM
MOBA_BLOCK = 256
MOBA_TOPK = 3
MOBA_GROUP = 4
ALIBI_MAX = 8.0
GLA_HEADS = 4
GLA_DK = 64
GLA_DV = 128
GLA_KEY_WIDTH = GLA_HEADS * GLA_DK
GLA_VAL_WIDTH = GLA_HEADS * GLA_DV
GLA_GATE_RANK = 16
GLA_GATE_TAU = 16.0
GLA_CHUNK = 64
N_EXPERTS = 32
TOP_K = 4
SWIGLU_ALPHA = 1.702
SWIGLU_LIMIT = 7.0
EPS = 1e-6

LANES = 128
MASKED = -1e30
VMEM_LIMIT = 56 * 1024 * 1024

BF16 = jnp.bfloat16
F32 = jnp.float32

_NT = (((1,), (1,)), ((), ()))
_TN = (((0,), (0,)), ((), ()))


def _dot(a, b, dims=None, precision=None):
    if dims is None:
        return jnp.dot(a, b, preferred_element_type=F32, precision=precision)
    return lax.dot_general(a, b, dims, preferred_element_type=F32, precision=precision)


def _rms(x, g):
    return x * lax.rsqrt(jnp.mean(x * x, axis=-1, keepdims=True) + EPS) * g


def _col_reduce(x, pair_op, final_op, rows=64):
    parts = [x[i:i + rows] for i in range(0, x.shape[0], rows)]
    while len(parts) > 1:
        parts = [pair_op(parts[i], parts[i + 1]) for i in range(0, len(parts), 2)]
    return final_op(parts[0], axis=0, keepdims=True)


def _params(*sem):
    return pltpu.CompilerParams(dimension_semantics=sem, vmem_limit_bytes=VMEM_LIMIT)


def _inproj_kernel(x_ref, g_ref, wq_ref, wk_ref, wvt_ref, wgla_ref, wag_ref, wgate_ref, bgate_ref,
                   qa_ref, ka_ref, vt_ref, km_ref, qg_ref, kg_ref, vg_ref, rg_ref, la_ref):
    i = pl.program_id(1)
    tm = x_ref.shape[0]
    nblk = tm // MOBA_BLOCK
    ab = _rms(x_ref[...], g_ref[...]).astype(BF16)
    qa_ref[...] = (_dot(ab, wq_ref[...]) * (HEAD_DIM ** -0.5)).astype(BF16)
    ka = _dot(ab, wk_ref[...])
    ka_ref[...] = ka.astype(BF16)
    @pl.when(i == 0)
    def _():
        km_ref[...] = jnp.zeros_like(km_ref)

    km = km_ref[...]
    blk_row = lax.broadcasted_iota(jnp.int32, km.shape, 0)
    for j in range(nblk):
        mean_j = jnp.mean(ka[j * MOBA_BLOCK:(j + 1) * MOBA_BLOCK], axis=0, keepdims=True)
        km = jnp.where(blk_row == i * nblk + j, mean_j, km)
    km_ref[...] = km
    vt = _dot(wvt_ref[...], ab, _NT).astype(BF16)
    for j in range(nblk):
        vt_ref[j] = vt[:, j * MOBA_BLOCK:(j + 1) * MOBA_BLOCK]
    gla = _dot(ab, wgla_ref[...])
    o = 0
    for ref in (qg_ref, kg_ref, vg_ref, rg_ref):
        w = ref.shape[1]
        ref[...] = gla[:, o:o + w]
        o += w
    ag = _dot(ab, wag_ref[...]).astype(BF16)
    z = _dot(ag, wgate_ref[...]) + bgate_ref[...]
    la_ref[...] = jax.nn.log_sigmoid(z) / GLA_GATE_TAU


def _inproj(x2d, g_mix, w_in, w_gla_gate, b_gla_gate, *, batch, seq, row0, tm=512):
    t, d = batch * seq, x2d.shape[1]
    nb = seq // MOBA_BLOCK
    o_k, o_v, o_qg = MOBA_WIDTH, 2 * MOBA_WIDTH, 3 * MOBA_WIDTH
    o_ag = o_qg + 2 * GLA_KEY_WIDTH + 2 * GLA_VAL_WIDTH
    wq = w_in[:, :o_k].astype(BF16)
    wk = w_in[:, o_k:o_v].astype(BF16)
    wvt = w_in[:, o_v:o_qg].T.astype(BF16)
    wgla = w_in[:, o_qg:o_ag].astype(BF16)
    wag = jnp.pad(w_in[:, o_ag:], ((0, 0), (0, LANES - GLA_GATE_RANK))).astype(BF16)
    wgate = jnp.pad(w_gla_gate, ((0, LANES - GLA_GATE_RANK), (0, 0))).astype(BF16)
    steps = seq // tm
    row = lambda b, i: (b * steps + i, 0)
    const = lambda b, i: (0, 0)
    full = lambda a: pl.BlockSpec(a.shape, const)
    args = (x2d, g_mix.reshape(1, d), wq, wk, wvt, wgla, wag, wgate, b_gla_gate.reshape(1, -1))
    in_specs = [pl.BlockSpec((tm, d), lambda b, i: (row0 // tm + b * steps + i, 0))] + [full(a) for a in args[1:]]
    out_shape = (
        jax.ShapeDtypeStruct((t, MOBA_WIDTH), BF16),
        jax.ShapeDtypeStruct((t, MOBA_WIDTH), BF16),
        jax.ShapeDtypeStruct((t // MOBA_BLOCK, MOBA_WIDTH, MOBA_BLOCK), BF16),
        jax.ShapeDtypeStruct((batch * nb, MOBA_WIDTH), F32),
        jax.ShapeDtypeStruct((t, GLA_KEY_WIDTH), F32),
        jax.ShapeDtypeStruct((t, GLA_KEY_WIDTH), F32),
        jax.ShapeDtypeStruct((t, GLA_VAL_WIDTH), F32),
        jax.ShapeDtypeStruct((t, GLA_VAL_WIDTH), F32),
        jax.ShapeDtypeStruct((t, GLA_KEY_WIDTH), F32),
    )
    out_specs = (
        pl.BlockSpec((tm, MOBA_WIDTH), row),
        pl.BlockSpec((tm, MOBA_WIDTH), row),
        pl.BlockSpec((tm // MOBA_BLOCK, MOBA_WIDTH, MOBA_BLOCK), lambda b, i: (b * steps + i, 0, 0)),
        pl.BlockSpec((nb, MOBA_WIDTH), lambda b, i: (b, 0)),
        pl.BlockSpec((tm, GLA_KEY_WIDTH), row),
        pl.BlockSpec((tm, GLA_KEY_WIDTH), row),
        pl.BlockSpec((tm, GLA_VAL_WIDTH), row),
        pl.BlockSpec((tm, GLA_VAL_WIDTH), row),
        pl.BlockSpec((tm, GLA_KEY_WIDTH), row),
    )
    return pl.pallas_call(
        _inproj_kernel, grid=(batch, steps), in_specs=in_specs, out_specs=out_specs, out_shape=out_shape,
        compiler_params=_params("arbitrary", "arbitrary"), name="inproj",
    )(*args)


def _moba_kernel(slopes_ref, q_ref, k_ref, vt_ref, km_ref, o_ref, kaug_ref, s_ref):
    hg = pl.program_id(1)
    c = pl.program_id(2)
    blk = MOBA_BLOCK
    nb = km_ref.shape[0]
    hw = 2 * HEAD_DIM
    heads = q_ref.shape[1] // HEAD_DIM
    feat_rows = nb + 8
    head_rows = lambda g: slice(g * HEAD_DIM, (g + 1) * HEAD_DIM)

    @pl.when(c == 0)
    def _():
        lane = lax.broadcasted_iota(jnp.int32, (blk, hw), 1)
        koff = lax.broadcasted_iota(jnp.int32, (blk, hw), 0).astype(F32)
        for g in range(heads):
            tile, hh = divmod(g, 2)
            slope = slopes_ref[hg * heads + g]
            base = HEAD_DIM * (1 - hh)
            for jb in range(nb):
                feat = jnp.where(lane == base + nb, slope * koff, 0.0)
                feat = jnp.where((lane == base + jb) | (lane == base + nb + 1), 1.0, feat)
                kb = k_ref[jb * blk:(jb + 1) * blk, tile * hw:(tile + 1) * hw]
                kaug_ref[g, jb * blk:(jb + 1) * blk, :] = jnp.where(lane // HEAD_DIM == hh, kb, feat.astype(BF16))

    qt = q_ref[...].astype(F32).T
    qt_b = qt.astype(BF16)
    kidx = lax.broadcasted_iota(jnp.int32, (blk, blk), 0)
    qidx = lax.broadcasted_iota(jnp.int32, (blk, blk), 1)
    causal = kidx <= qidx
    nidx = lax.broadcasted_iota(jnp.int32, (nb, blk), 0)
    r8 = lax.broadcasted_iota(jnp.int32, (8, blk), 0)
    qoff = lax.broadcasted_iota(jnp.int32, (8, blk), 1).astype(F32)
    km_lane = lax.broadcasted_iota(jnp.int32, km_ref.shape, 1)
    km = km_ref[...]
    own_rows = pl.ds(pl.multiple_of(c * blk, blk), blk)
    vt_own = vt_ref[c]

    q_past, state = [], []
    for g in range(heads):
        slope = slopes_ref[hg * heads + g]
        kmh = jnp.where(km_lane // HEAD_DIM == g, km, 0.0).astype(BF16)
        gate = jnp.where(nidx < c, _dot(kmh, qt_b), -jnp.inf)
        rank = jnp.zeros((nb, blk), jnp.int32)
        for m in range(nb - 1):
            gm = gate[m:m + 1, :]
            beats = (gm > gate) | ((gm == gate) & (m < nidx))
            rank = rank + beats.astype(jnp.int32)
        sel = (nidx < c) & (rank < MOBA_TOPK)
        dist0 = -slope * ((c - nidx) * blk).astype(F32)
        tail = jnp.where(r8 == 0, 1.0, jnp.where(r8 == 1, -slope * qoff, 0.0))
        fill = jnp.zeros((HEAD_DIM - feat_rows, blk), F32)
        qh = qt[head_rows(g), :]

        def q_aug(bias):
            feats = [jnp.where(bias, dist0, MASKED), tail, fill]
            return jnp.concatenate([qh] + feats if g % 2 == 0 else feats + [qh], axis=0).astype(BF16)

        q_past.append(q_aug(sel))
        s = _dot(kaug_ref[g, own_rows, :], q_aug(nidx == c))
        s = jnp.where(causal, s, MASKED)
        m0 = jnp.max(s, axis=0, keepdims=True)
        p = jnp.exp(s - m0)
        l0 = jnp.sum(p, axis=0, keepdims=True)
        acc0 = _dot(vt_own[head_rows(g), :], p.astype(BF16))
        state += [m0, l0, acc0]

    def scores(pair, slot):
        rows = pl.ds(pl.multiple_of(pair * 2 * blk, 2 * blk), 2 * blk)
        tops = []
        for g in range(heads):
            s = _dot(kaug_ref[g, rows, :], q_past[g])
            s_ref[slot, g] = s
            tops.append(_col_reduce(s, jnp.maximum, jnp.max))
        return tops

    def absorb(pair, slot, tops, carry):
        vt0 = vt_ref[2 * pair]
        vt1 = vt_ref[2 * pair + 1]
        out = []
        for g in range(heads):
            m_i, l_i, acc = carry[3 * g:3 * g + 3]
            m_new = jnp.maximum(m_i, tops[g])
            alpha = jnp.exp(m_i - m_new)
            p = jnp.exp(s_ref[slot, g] - m_new)
            l_new = alpha * l_i + _col_reduce(p, jnp.add, jnp.sum)
            pb = p.astype(BF16)
            acc = alpha * acc + _dot(vt0[head_rows(g), :], pb[:blk]) + _dot(vt1[head_rows(g), :], pb[blk:])
            out += [m_new, l_new, acc]
        return out

    n_state = 3 * heads

    def step(i, carry, slot):
        tops = scores(i + 1, 1 - slot)
        return tuple(absorb(i, slot, carry[n_state:], carry[:n_state]) + tops)

    def body(i, carry):
        return lax.cond(i % 2 == 0, lambda: step(i, carry, 0), lambda: step(i, carry, 1))

    last = jnp.maximum((c + 1) // 2 - 1, 0)
    carry = lax.fori_loop(0, last, body, tuple(state + scores(0, 0)))
    state = lax.cond(last % 2 == 0, lambda: tuple(absorb(last, 0, carry[n_state:], carry[:n_state])),
                     lambda: tuple(absorb(last, 1, carry[n_state:], carry[:n_state])))
    out_t = jnp.concatenate([state[3 * g + 2] / state[3 * g + 1] for g in range(heads)], axis=0)
    o_ref[...] = out_t.T.astype(o_ref.dtype)


def _alibi_slopes():
    slopes = np.exp2(-ALIBI_MAX * np.arange(1, MOBA_HEADS + 1, dtype=np.float32) / MOBA_HEADS).astype(np.float32)
    worst = slopes * np.float32(MOBA_BLOCK - 1)
    assert np.all(worst.astype(jnp.bfloat16).astype(np.float32) == worst), "ALiBi slopes not bf16-exact"
    return slopes


def _moba(qa, ka, vt, kmean, *, batch, seq):
    t = qa.shape[0]
    nb = seq // MOBA_BLOCK
    gw = MOBA_GROUP * HEAD_DIM
    assert nb % 8 == 0 and nb + 8 <= HEAD_DIM, "bias features must fit the spare contraction lanes"
    grid_spec = pltpu.PrefetchScalarGridSpec(
        num_scalar_prefetch=1,
        grid=(batch, MOBA_HEADS // MOBA_GROUP, nb),
        in_specs=[
            pl.BlockSpec((MOBA_BLOCK, gw), lambda b, h, c, s: (b * nb + c, h)),
            pl.BlockSpec((seq, gw), lambda b, h, c, s: (b, h)),
            pl.BlockSpec((nb, gw, MOBA_BLOCK), lambda b, h, c, s: (b, h, 0)),
            pl.BlockSpec((nb, gw), lambda b, h, c, s: (b, h)),
        ],
        out_specs=pl.BlockSpec((MOBA_BLOCK, gw), lambda b, h, c, s: (b * nb + c, h)),
        scratch_shapes=[pltpu.VMEM((MOBA_GROUP, seq, 2 * HEAD_DIM), BF16),
                        pltpu.VMEM((2, MOBA_GROUP, 2 * MOBA_BLOCK, MOBA_BLOCK), F32)],
    )
    return pl.pallas_call(
        _moba_kernel, grid_spec=grid_spec, out_shape=jax.ShapeDtypeStruct((t, MOBA_WIDTH), BF16),
        compiler_params=_params("arbitrary", "arbitrary", "arbitrary"), name="moba",
    )(jnp.asarray(_alibi_slopes()), qa, ka, vt, kmean)


def _gla_kernel(q_ref, k_ref, v_ref, r_ref, la_ref, gn_ref, o_ref, st_ref):
    @pl.when(pl.program_id(1) == 0)
    def _():
        st_ref[...] = jnp.zeros_like(st_ref)

    ch = GLA_CHUNK
    tc = q_ref.shape[0]
    n_chunks = tc // ch
    row = lax.broadcasted_iota(jnp.int32, (tc, tc), 0)
    col = lax.broadcasted_iota(jnp.int32, (tc, tc), 1)
    causal = (row // ch == col // ch) & (row >= col)
    tri = jnp.where(causal, 1.0, 0.0).astype(BF16)
    lane = lax.broadcasted_iota(jnp.int32, (1, GLA_KEY_WIDTH), 1)
    la = la_ref[...]
    la_hi = la.astype(BF16)
    rest = la - la_hi.astype(F32)
    la_mid = rest.astype(BF16)
    la_lo = (rest - la_mid.astype(F32)).astype(BF16)
    g = _dot(tri, la_hi) + _dot(tri, la_mid) + _dot(tri, la_lo)
    ends = [g[(ci + 1) * ch - 1:(ci + 1) * ch, :] for ci in range(n_chunks)]
    g_last = jnp.concatenate([jnp.broadcast_to(e, (ch, GLA_KEY_WIDTH)) for e in ends], axis=0)
    qd = q_ref[...] * (GLA_DK ** -0.5) * jnp.exp(g)
    k = k_ref[...]
    kd = (k * jnp.exp(-g)).astype(BF16)
    kt = k * jnp.exp(g_last - g)
    qhs, kths, vhs, intra = [], [], [], []
    for h in range(GLA_HEADS):
        hm = lane // GLA_DK == h
        qhs.append(jnp.where(hm, qd, 0.0).astype(BF16))
        kths.append(jnp.where(hm, kt, 0.0).astype(BF16))
        vhs.append(v_ref[:, h * GLA_DV:(h + 1) * GLA_DV].astype(BF16))
        a = jnp.where(causal, _dot(qhs[h], kd, _NT), 0.0)
        intra.append(_dot(a.astype(BF16), vhs[h]))
    state = st_ref[...]
    inter = [[] for _ in range(GLA_HEADS)]
    for ci in range(n_chunks):
        rows = slice(ci * ch, (ci + 1) * ch)
        state_b = state.astype(BF16)
        new_state = state * jnp.exp(ends[ci])
        for h in range(GLA_HEADS):
            inter[h].append(_dot(qhs[h][rows], state_b, _NT))
            new_state = new_state + _dot(vhs[h][rows], kths[h][rows], _TN)
        state = new_state
    st_ref[...] = state
    gn = gn_ref[...]
    for h in range(GLA_HEADS):
        o = intra[h] + jnp.concatenate(inter[h], axis=0)
        rh = r_ref[:, h * GLA_DV:(h + 1) * GLA_DV]
        o_ref[:, h * GLA_DV:(h + 1) * GLA_DV] = (_rms(o, gn) * (rh * jax.nn.sigmoid(rh))).astype(o_ref.dtype)


def _gla(qg, kg, vg, rg, la, g_norm, *, batch, seq, tc=256):
    t = qg.shape[0]
    steps = seq // tc
    row = lambda b, i: (b * steps + i, 0)
    kspec = pl.BlockSpec((tc, GLA_KEY_WIDTH), row)
    vspec = pl.BlockSpec((tc, GLA_VAL_WIDTH), row)
    return pl.pallas_call(
        _gla_kernel, grid=(batch, steps),
        in_specs=[kspec, kspec, vspec, vspec, kspec, pl.BlockSpec((1, GLA_DV), lambda b, i: (0, 0))],
        out_specs=vspec, out_shape=jax.ShapeDtypeStruct((t, GLA_VAL_WIDTH), BF16),
        scratch_shapes=[pltpu.VMEM((GLA_DV, GLA_KEY_WIDTH), F32)],
        compiler_params=_params("arbitrary", "arbitrary"), name="gla",
    )(qg, kg, vg, rg, la, g_norm.reshape(1, GLA_DV))


def _outproj_kernel(x_ref, om_ref, og_ref, wm_ref, wg_ref, gf_ref, wr_ref, br_ref, h_ref, f_ref, lg_ref):
    h = x_ref[...] + _dot(om_ref[...], wm_ref[...]) + _dot(og_ref[...], wg_ref[...])
    h_ref[...] = h
    f = _rms(h, gf_ref[...]).astype(BF16)
    f_ref[...] = f
    lg_ref[...] = _dot(wr_ref[...], f, _NT) + br_ref[...]


def _outproj(x2d, o_moba, o_gla, w_out, g_ffn, w_router, b_router, *, row0, tm=512):
    t, d = o_moba.shape[0], x2d.shape[1]
    wm = w_out[:MOBA_WIDTH].astype(BF16)
    wg = w_out[MOBA_WIDTH:].astype(BF16)
    wr = w_router.T.astype(BF16)
    br = b_router.reshape(N_EXPERTS, 1)
    row = lambda i: (i, 0)
    const = lambda i: (0, 0)
    full = lambda a: pl.BlockSpec(a.shape, const)
    args = (x2d, o_moba, o_gla, wm, wg, g_ffn.reshape(1, d), wr, br)
    in_specs = [pl.BlockSpec((tm, d), lambda i: (row0 // tm + i, 0)), pl.BlockSpec((tm, MOBA_WIDTH), row),
                pl.BlockSpec((tm, GLA_VAL_WIDTH), row)] + [full(a) for a in args[3:]]
    return pl.pallas_call(
        _outproj_kernel, grid=(t // tm,), in_specs=in_specs,
        out_specs=(pl.BlockSpec((tm, d), row), pl.BlockSpec((tm, d), row),
                   pl.BlockSpec((N_EXPERTS, tm), lambda i: (0, i))),
        out_shape=(jax.ShapeDtypeStruct((t, d), F32), jax.ShapeDtypeStruct((t, d), BF16),
                   jax.ShapeDtypeStruct((N_EXPERTS, t), F32)),
        compiler_params=_params("arbitrary"), name="outproj",
    )(*args)


PAIR_CHUNK = 2 * LANES


def _moe_kernel(be_ref, nused_ref, x_ref, w1_ref, b1_ref, w2_ref, b2_ref, o_ref, w1s_ref, w2s_ref):
    i = pl.program_id(0)
    f = w2_ref.shape[1]

    @pl.when((i == 0) | (be_ref[i] != be_ref[jnp.maximum(i - 1, 0)]))
    def _():
        r = lax.broadcasted_iota(jnp.int32, (PAIR_CHUNK, PAIR_CHUNK), 0)
        c = lax.broadcasted_iota(jnp.int32, (PAIR_CHUNK, PAIR_CHUNK), 1)
        src_col = jnp.where(c < LANES, 2 * c, 2 * (c - LANES) + 1)
        perm = jnp.where(r == src_col, 1.0, 0.0).astype(BF16)
        for cc in range(2 * f // PAIR_CHUNK):
            chunk = w1_ref[0, :, cc * PAIR_CHUNK:(cc + 1) * PAIR_CHUNK].astype(BF16)
            res = _dot(chunk, perm).astype(BF16)
            w1s_ref[:, cc * LANES:(cc + 1) * LANES] = res[:, :LANES]
            w1s_ref[:, f + cc * LANES:f + (cc + 1) * LANES] = res[:, LANES:]
        w2s_ref[...] = w2_ref[0].astype(BF16)

    @pl.when(i < nused_ref[0])
    def _():
        hid = _dot(x_ref[...], w1s_ref[...]) + b1_ref[0]
        glu = jnp.minimum(hid[:, :f], SWIGLU_LIMIT)
        lin = jnp.clip(hid[:, f:], -SWIGLU_LIMIT, SWIGLU_LIMIT)
        act = glu * jax.nn.sigmoid(SWIGLU_ALPHA * glu) * (lin + 1.0)
        o_ref[...] = (_dot(act.astype(BF16), w2s_ref[...]) + b2_ref[0]).astype(o_ref.dtype)

    @pl.when(i >= nused_ref[0])
    def _():
        o_ref[...] = jnp.zeros_like(o_ref)


def _moe(x_rows, block_expert, n_used, w1, b1, w2, b2, *, tm):
    n_rows, d = x_rows.shape
    e, _, f2 = w1.shape
    f = f2 // 2
    b1p = jnp.concatenate([b1[:, 0::2], b1[:, 1::2]], axis=-1).reshape(e, 1, f2)
    b2r = b2.reshape(e, 1, d)
    grid_spec = pltpu.PrefetchScalarGridSpec(
        num_scalar_prefetch=2,
        grid=(n_rows // tm,),
        in_specs=[
            pl.BlockSpec((tm, d), lambda i, be, nu: (i, 0)),
            pl.BlockSpec((1, d, f2), lambda i, be, nu: (be[i], 0, 0)),
            pl.BlockSpec((1, 1, f2), lambda i, be, nu: (be[i], 0, 0)),
            pl.BlockSpec((1, f, d), lambda i, be, nu: (be[i], 0, 0)),
            pl.BlockSpec((1, 1, d), lambda i, be, nu: (be[i], 0, 0)),
        ],
        out_specs=pl.BlockSpec((tm, d), lambda i, be, nu: (i, 0)),
        scratch_shapes=[pltpu.VMEM((d, f2), BF16), pltpu.VMEM((f, d), BF16)],
    )
    return pl.pallas_call(
        _moe_kernel, grid_spec=grid_spec, out_shape=jax.ShapeDtypeStruct((n_rows, d), BF16),
        compiler_params=_params("arbitrary"), name="moe",
    )(block_expert, n_used, x_rows, w1, b1p, w2, b2r)


ROUTE_CHUNK = 2 * LANES


def _router_kernel(lg_ref, idx_ref, gate_ref, rank_ref, cnt_ref, carry_ref):
    @pl.when(pl.program_id(0) == 0)
    def _():
        carry_ref[...] = jnp.zeros_like(carry_ref)

    x = lg_ref[...]
    ne, tt = x.shape
    eidx = lax.broadcasted_iota(jnp.int32, (ne, tt), 0)
    member = jnp.zeros((ne, tt), F32)
    vals, ids = [], []
    for _ in range(TOP_K):
        m = jnp.max(x, axis=0, keepdims=True)
        sel = jnp.min(jnp.where(x == m, eidx, ne), axis=0, keepdims=True)
        hit = eidx == sel
        member = jnp.where(hit, 1.0, member)
        x = jnp.where(hit, -jnp.inf, x)
        vals.append(m)
        ids.append(sel)
    e = jnp.exp(jnp.concatenate(vals, axis=0) - vals[0])
    gate_ref[...] = e / jnp.sum(e, axis=0, keepdims=True)
    idx_ref[...] = jnp.concatenate(ids, axis=0)
    r = lax.broadcasted_iota(jnp.int32, (ROUTE_CHUNK, ROUTE_CHUNK), 0)
    c = lax.broadcasted_iota(jnp.int32, (ROUTE_CHUNK, ROUTE_CHUNK), 1)
    earlier = jnp.where(r < c, 1.0, 0.0).astype(BF16)
    carry = carry_ref[:, :1]
    before = []
    for ch in range(tt // ROUTE_CHUNK):
        mc = member[:, ch * ROUTE_CHUNK:(ch + 1) * ROUTE_CHUNK]
        before.append(_dot(mc.astype(BF16), earlier) + carry)
        carry = carry + jnp.sum(mc, axis=1, keepdims=True)
    before = jnp.concatenate(before, axis=1)
    ranks = [jnp.sum(jnp.where(eidx == ids[k], before, 0.0), axis=0, keepdims=True) for k in range(TOP_K)]
    rank_ref[...] = jnp.concatenate(ranks, axis=0).astype(jnp.int32)
    carry_ref[...] = jnp.broadcast_to(carry, carry_ref.shape)
    cnt_ref[...] = jnp.broadcast_to(carry, cnt_ref.shape).astype(jnp.int32)


def _router(logits_t, *, tt=2048):
    ne, t = logits_t.shape
    tok = lambda i: (0, i)
    kspec = pl.BlockSpec((TOP_K, tt), tok)
    return pl.pallas_call(
        _router_kernel, grid=(t // tt,), in_specs=[pl.BlockSpec((ne, tt), tok)],
        out_specs=(kspec, kspec, kspec, pl.BlockSpec((ne, LANES), lambda i: (0, 0))),
        out_shape=(jax.ShapeDtypeStruct((TOP_K, t), jnp.int32), jax.ShapeDtypeStruct((TOP_K, t), F32),
                   jax.ShapeDtypeStruct((TOP_K, t), jnp.int32), jax.ShapeDtypeStruct((ne, LANES), jnp.int32)),
        scratch_shapes=[pltpu.VMEM((ne, LANES), F32)],
        compiler_params=_params("arbitrary"), name="router",
    )(logits_t)


def _route(logits_t, *, tm):
    ne, t = logits_t.shape
    idx, gates, rank, cnt = _router(logits_t)
    counts = cnt[:, 0]
    padded = (counts + tm - 1) // tm * tm
    pad_end = jnp.cumsum(padded)
    pad_start = pad_end - padded
    grp_start = jnp.cumsum(counts) - counts
    experts = jnp.arange(ne, dtype=jnp.int32)

    def pick(table, ids):
        lead = (ne,) + (1,) * ids.ndim
        return jnp.sum(jnp.where(ids[None] == experts.reshape(lead), table.reshape(lead), 0), axis=0)

    slot = pick(pad_start, idx) + rank
    n_assign = TOP_K * t
    skey = jnp.sort((idx * t + jnp.arange(t, dtype=jnp.int32)[None, :]).reshape(-1))
    n_blocks = -(-n_assign // tm) + ne
    r = jnp.arange(n_blocks * tm, dtype=jnp.int32)
    e_r = jnp.minimum(jnp.sum(pad_end[:, None] <= r[None, :], axis=0), ne - 1)
    off = r - pick(pad_start, e_r)
    valid = (off < pick(counts, e_r)) & (r < pad_end[-1])
    src = jnp.clip(pick(grp_start, e_r) + off, 0, n_assign - 1)
    row_tok = jnp.where(valid, skey[src] % t, r % t)
    n_used = pad_end[-1] // tm
    blk_start = jnp.minimum(jnp.arange(n_blocks, dtype=jnp.int32), n_used - 1) * tm
    block_expert = jnp.minimum(jnp.sum(pad_end[:, None] <= blk_start[None, :], axis=0), ne - 1)
    return gates, row_tok, slot, block_expert.astype(jnp.int32), n_used.reshape(1).astype(jnp.int32)


def _final_kernel(h_ref, y_ref, gate_ref, p_ref, wp_ref, gp_ref, wgate_ref, gfin_ref, o_ref, *, last_layer):
    h = h_ref[...]
    g = gate_ref[...]
    gates = jnp.concatenate([g, jnp.zeros((8 - TOP_K, g.shape[1]), F32)], axis=0).T
    for k in range(TOP_K):
        h = h + gates[:, k:k + 1] * y_ref[k].astype(F32)
    ple = _rms(_dot(p_ref[...].astype(BF16), wp_ref[...]), gp_ref[...])
    h = h + jax.nn.sigmoid(_dot(h.astype(BF16), wgate_ref[...])) * ple
    o_ref[...] = _rms(h, gfin_ref[...]) if last_layer else h


def _final(h1, y_tok, gates, p2d, w_ple_proj, g_ple, w_ple_gate, g_final, *, last_layer, row0, tm=512):
    t, d = h1.shape
    row = lambda i: (i, 0)
    const = lambda i: (0, 0)
    full = lambda a: pl.BlockSpec(a.shape, const)
    args = (h1, y_tok, gates, p2d, w_ple_proj.astype(BF16), g_ple.reshape(1, d), w_ple_gate.astype(BF16),
            g_final.reshape(1, d))
    in_specs = [pl.BlockSpec((tm, d), row), pl.BlockSpec((TOP_K, tm, d), lambda i: (0, i, 0)),
                pl.BlockSpec((TOP_K, tm), lambda i: (0, i)),
                pl.BlockSpec((tm, p2d.shape[1]), lambda i: (row0 // tm + i, 0))]
    in_specs += [full(a) for a in args[4:]]
    return pl.pallas_call(
        functools.partial(_final_kernel, last_layer=last_layer), grid=(t // tm,), in_specs=in_specs,
        out_specs=pl.BlockSpec((tm, d), row),
        out_shape=jax.ShapeDtypeStruct((t, d), F32),
        compiler_params=_params("arbitrary"), name="final",
    )(*args)


@compute_on("tpu_sparsecore")
@jax.jit
def _take_rows(table, idx):
    return jnp.take(table, idx, axis=0, mode="clip")


BATCH_SPLIT = 2


def kernel(x, p, g_mix, w_in, w_gla_gate, b_gla_gate, g_gla_norm, w_out, g_ffn, w_router, b_router,
           w1, b1, w2, b2, w_ple_gate, w_ple_proj, g_ple, g_final, *, moe_tm=512):
    batch, seq, d = x.shape
    depth = p.shape[0]
    n_groups = BATCH_SPLIT if batch % BATCH_SPLIT == 0 else 1
    gb = batch // n_groups
    gt = gb * seq
    groups = range(n_groups)
    hs = [x.reshape(batch * seq, d)] * n_groups
    for i in range(depth):
        p2d = p[i].reshape(batch * seq, -1)
        row0 = [j * gt if i == 0 else 0 for j in groups]
        mixed, routes, x_rows = [], [], []
        for j in groups:
            proj = _inproj(hs[j], g_mix[i], w_in[i], w_gla_gate[i], b_gla_gate[i], batch=gb, seq=seq, row0=row0[j])
            o_moba = _moba(*proj[:4], batch=gb, seq=seq)
            o_gla = _gla(*proj[4:], g_gla_norm[i], batch=gb, seq=seq)
            mixed.append(_outproj(hs[j], o_moba, o_gla, w_out[i], g_ffn[i], w_router[i], b_router[i],
                                  row0=row0[j]))
            routes.append(_route(mixed[j][2], tm=moe_tm))
            x_rows.append(_take_rows(mixed[j][1], routes[j][1]))
        y = [_moe(x_rows[j], routes[j][3], routes[j][4], w1[i], b1[i], w2[i], b2[i], tm=moe_tm) for j in groups]
        y_tok = [_take_rows(y[j], routes[j][2]) for j in groups]
        hs = [_final(mixed[j][0], y_tok[j], routes[j][0], p2d, w_ple_proj[i], g_ple[i], w_ple_gate[i], g_final,
                     last_layer=i == depth - 1, row0=j * gt) for j in groups]
    return jnp.concatenate(hs, axis=0).reshape(batch, seq, d)
```

```python
import functools

import jax
import jax.numpy as jnp
import numpy as np
from jax import lax
from jax.experimental import pallas as pl
from jax.experimental.pallas import tpu as pltpu

HEAD_DIM = 64
MOBA_HEADS = 8
MOBA_WIDTH = MOBA_HEADS * HEAD_DIM
MOBA_BLOCK = 256
MOBA_TOPK = 3
MOBA_GROUP = 4
ALIBI_MAX = 8.0
GLA_HEADS = 4
GLA_DK = 64
GLA_DV = 128
GLA_KEY_WIDTH = GLA_HEADS * GLA_DK
GLA_VAL_WIDTH = GLA_HEADS * GLA_DV
GLA_GATE_RANK = 16
GLA_GATE_TAU = 16.0
GLA_CHUNK = 64
N_EXPERTS = 32
TOP_K = 4
SWIGLU_ALPHA = 1.702
SWIGLU_LIMIT = 7.0
EPS = 1e-6

LANES = 128
MASKED = -1e30
VMEM_LIMIT = 56 * 1024 * 1024

BF16 = jnp.bfloat16
F32 = jnp.float32

_NT = (((1,), (1,)), ((), ()))
_TN = (((0,), (0,)), ((), ()))


def _dot(a, b, dims=None, precision=None):
    if dims is None:
        return jnp.dot(a, b, preferred_element_type=F32, precision=precision)
    return lax.dot_general(a, b, dims, preferred_element_type=F32, precision=precision)


def _rms(x, g):
    return x * lax.rsqrt(jnp.mean(x * x, axis=-1, keepdims=True) + EPS) * g


def _col_reduce(x, pair_op, final_op, rows=64):
    parts = [x[i:i + rows] for i in range(0, x.shape[0], rows)]
    while len(parts) > 1:
        parts = [pair_op(parts[i], parts[i + 1]) for i in range(0, len(parts), 2)]
    return final_op(parts[0], axis=0, keepdims=True)


def _params(*sem):
    return pltpu.CompilerParams(dimension_semantics=sem, vmem_limit_bytes=VMEM_LIMIT)


def _inproj_kernel(x_ref, g_ref, wq_ref, wk_ref, wvt_ref, wgla_ref, wag_ref, wgate_ref, bgate_ref,
                   qa_ref, ka_ref, vt_ref, km_ref, qg_ref, kg_ref, vg_ref, rg_ref, la_ref):
    i = pl.program_id(1)
    tm = x_ref.shape[0]
    nblk = tm // MOBA_BLOCK
    ab = _rms(x_ref[...], g_ref[...]).astype(BF16)
    qa_ref[...] = (_dot(ab, wq_ref[...]) * (HEAD_DIM ** -0.5)).astype(BF16)
    ka = _dot(ab, wk_ref[...])
    ka_ref[...] = ka.astype(BF16)
    @pl.when(i == 0)
    def _():
        km_ref[...] = jnp.zeros_like(km_ref)

    km = km_ref[...]
    blk_row = lax.broadcasted_iota(jnp.int32, km.shape, 0)
    for j in range(nblk):
        mean_j = jnp.mean(ka[j * MOBA_BLOCK:(j + 1) * MOBA_BLOCK], axis=0, keepdims=True)
        km = jnp.where(blk_row == i * nblk + j, mean_j, km)
    km_ref[...] = km
    vt = _dot(wvt_ref[...], ab, _NT).astype(BF16)
    for j in range(nblk):
        vt_ref[j] = vt[:, j * MOBA_BLOCK:(j + 1) * MOBA_BLOCK]
    gla = _dot(ab, wgla_ref[...])
    o = 0
    for ref in (qg_ref, kg_ref, vg_ref, rg_ref):
        w = ref.shape[1]
        ref[...] = gla[:, o:o + w]
        o += w
    ag = _dot(ab, wag_ref[...]).astype(BF16)
    z = _dot(ag, wgate_ref[...]) + bgate_ref[...]
    la_ref[...] = jax.nn.log_sigmoid(z) / GLA_GATE_TAU


def _inproj(x2d, g_mix, w_in, w_gla_gate, b_gla_gate, *, batch, seq, tm=512):
    t, d = x2d.shape
    nb = seq // MOBA_BLOCK
    o_k, o_v, o_qg = MOBA_WIDTH, 2 * MOBA_WIDTH, 3 * MOBA_WIDTH
    o_ag = o_qg + 2 * GLA_KEY_WIDTH + 2 * GLA_VAL_WIDTH
    wq = w_in[:, :o_k].astype(BF16)
    wk = w_in[:, o_k:o_v].astype(BF16)
    wvt = w_in[:, o_v:o_qg].T.astype(BF16)
    wgla = w_in[:, o_qg:o_ag].astype(BF16)
    wag = jnp.pad(w_in[:, o_ag:], ((0, 0), (0, LANES - GLA_GATE_RANK))).astype(BF16)
    wgate = jnp.pad(w_gla_gate, ((0, LANES - GLA_GATE_RANK), (0, 0))).astype(BF16)
    steps = seq // tm
    row = lambda b, i: (b * steps + i, 0)
    const = lambda b, i: (0, 0)
    full = lambda a: pl.BlockSpec(a.shape, const)
    args = (x2d, g_mix.reshape(1, d), wq, wk, wvt, wgla, wag, wgate, b_gla_gate.reshape(1, -1))
    in_specs = [pl.BlockSpec((tm, d), row)] + [full(a) for a in args[1:]]
    out_shape = (
        jax.ShapeDtypeStruct((t, MOBA_WIDTH), BF16),
        jax.ShapeDtypeStruct((t, MOBA_WIDTH), BF16),
        jax.ShapeDtypeStruct((t // MOBA_BLOCK, MOBA_WIDTH, MOBA_BLOCK), BF16),
        jax.ShapeDtypeStruct((batch * nb, MOBA_WIDTH), F32),
        jax.ShapeDtypeStruct((t, GLA_KEY_WIDTH), F32),
        jax.ShapeDtypeStruct((t, GLA_KEY_WIDTH), F32),
        jax.ShapeDtypeStruct((t, GLA_VAL_WIDTH), F32),
        jax.ShapeDtypeStruct((t, GLA_VAL_WIDTH), F32),
        jax.ShapeDtypeStruct((t, GLA_KEY_WIDTH), F32),
    )
    out_specs = (
        pl.BlockSpec((tm, MOBA_WIDTH), row),
        pl.BlockSpec((tm, MOBA_WIDTH), row),
        pl.BlockSpec((tm // MOBA_BLOCK, MOBA_WIDTH, MOBA_BLOCK), lambda b, i: (b * steps + i, 0, 0)),
        pl.BlockSpec((nb, MOBA_WIDTH), lambda b, i: (b, 0)),
        pl.BlockSpec((tm, GLA_KEY_WIDTH), row),
        pl.BlockSpec((tm, GLA_KEY_WIDTH), row),
        pl.BlockSpec((tm, GLA_VAL_WIDTH), row),
        pl.BlockSpec((tm, GLA_VAL_WIDTH), row),
        pl.BlockSpec((tm, GLA_KEY_WIDTH), row),
    )
    return pl.pallas_call(
        _inproj_kernel, grid=(batch, steps), in_specs=in_specs, out_specs=out_specs, out_shape=out_shape,
        compiler_params=_params("arbitrary", "arbitrary"), name="inproj",
    )(*args)


def _moba_kernel(slopes_ref, q_ref, k_ref, vt_ref, km_ref, o_ref, kaug_ref, s_ref):
    hg = pl.program_id(1)
    c = pl.program_id(2)
    blk = MOBA_BLOCK
    nb = km_ref.shape[0]
    hw = 2 * HEAD_DIM
    heads = q_ref.shape[1] // HEAD_DIM
    feat_rows = nb + 8
    head_rows = lambda g: slice(g * HEAD_DIM, (g + 1) * HEAD_DIM)

    @pl.when(c == 0)
    def _():
        lane = lax.broadcasted_iota(jnp.int32, (blk, hw), 1)
        koff = lax.broadcasted_iota(jnp.int32, (blk, hw), 0).astype(F32)
        for g in range(heads):
            tile, hh = divmod(g, 2)
            slope = slopes_ref[hg * heads + g]
            base = HEAD_DIM * (1 - hh)
            for jb in range(nb):
                feat = jnp.where(lane == base + nb, slope * koff, 0.0)
                feat = jnp.where((lane == base + jb) | (lane == base + nb + 1), 1.0, feat)
                kb = k_ref[jb * blk:(jb + 1) * blk, tile * hw:(tile + 1) * hw]
                kaug_ref[g, jb * blk:(jb + 1) * blk, :] = jnp.where(lane // HEAD_DIM == hh, kb, feat.astype(BF16))

    qt = q_ref[...].astype(F32).T
    qt_b = qt.astype(BF16)
    kidx = lax.broadcasted_iota(jnp.int32, (blk, blk), 0)
    qidx = lax.broadcasted_iota(jnp.int32, (blk, blk), 1)
    causal = kidx <= qidx
    nidx = lax.broadcasted_iota(jnp.int32, (nb, blk), 0)
    r8 = lax.broadcasted_iota(jnp.int32, (8, blk), 0)
    qoff = lax.broadcasted_iota(jnp.int32, (8, blk), 1).astype(F32)
    km_lane = lax.broadcasted_iota(jnp.int32, km_ref.shape, 1)
    km = km_ref[...]
    own_rows = pl.ds(pl.multiple_of(c * blk, blk), blk)
    vt_own = vt_ref[c]

    q_past, state = [], []
    for g in range(heads):
        slope = slopes_ref[hg * heads + g]
        kmh = jnp.where(km_lane // HEAD_DIM == g, km, 0.0).astype(BF16)
        gate = jnp.where(nidx < c, _dot(kmh, qt_b), -jnp.inf)
        rank = jnp.zeros((nb, blk), jnp.int32)
        for m in range(nb - 1):
            gm = gate[m:m + 1, :]
            beats = (gm > gate) | ((gm == gate) & (m < nidx))
            rank = rank + beats.astype(jnp.int32)
        sel = (nidx < c) & (rank < MOBA_TOPK)
        dist0 = -slope * ((c - nidx) * blk).astype(F32)
        tail = jnp.where(r8 == 0, 1.0, jnp.where(r8 == 1, -slope * qoff, 0.0))
        fill = jnp.zeros((HEAD_DIM - feat_rows, blk), F32)
        qh = qt[head_rows(g), :]

        def q_aug(bias):
            feats = [jnp.where(bias, dist0, MASKED), tail, fill]
            return jnp.concatenate([qh] + feats if g % 2 == 0 else feats + [qh], axis=0).astype(BF16)

        q_past.append(q_aug(sel))
        s = _dot(kaug_ref[g, own_rows, :], q_aug(nidx == c))
        s = jnp.where(causal, s, MASKED)
        m0 = jnp.max(s, axis=0, keepdims=True)
        p = jnp.exp(s - m0)
        l0 = jnp.sum(p, axis=0, keepdims=True)
        acc0 = _dot(vt_own[head_rows(g), :], p.astype(BF16))
        state += [m0, l0, acc0]

    def scores(pair, slot):
        rows = pl.ds(pl.multiple_of(pair * 2 * blk, 2 * blk), 2 * blk)
        tops = []
        for g in range(heads):
            s = _dot(kaug_ref[g, rows, :], q_past[g])
            s_ref[slot, g] = s
            tops.append(_col_reduce(s, jnp.maximum, jnp.max))
        return tops

    def absorb(pair, slot, tops, carry):
        vt0 = vt_ref[2 * pair]
        vt1 = vt_ref[2 * pair + 1]
        out = []
        for g in range(heads):
            m_i, l_i, acc = carry[3 * g:3 * g + 3]
            m_new = jnp.maximum(m_i, tops[g])
            alpha = jnp.exp(m_i - m_new)
            p = jnp.exp(s_ref[slot, g] - m_new)
            l_new = alpha * l_i + _col_reduce(p, jnp.add, jnp.sum)
            pb = p.astype(BF16)
            acc = alpha * acc + _dot(vt0[head_rows(g), :], pb[:blk]) + _dot(vt1[head_rows(g), :], pb[blk:])
            out += [m_new, l_new, acc]
        return out

    n_state = 3 * heads

    def step(i, carry, slot):
        tops = scores(i + 1, 1 - slot)
        return tuple(absorb(i, slot, carry[n_state:], carry[:n_state]) + tops)

    def body(i, carry):
        return lax.cond(i % 2 == 0, lambda: step(i, carry, 0), lambda: step(i, carry, 1))

    last = jnp.maximum((c + 1) // 2 - 1, 0)
    carry = lax.fori_loop(0, last, body, tuple(state + scores(0, 0)))
    state = lax.cond(last % 2 == 0, lambda: tuple(absorb(last, 0, carry[n_state:], carry[:n_state])),
                     lambda: tuple(absorb(last, 1, carry[n_state:], carry[:n_state])))
    out_t = jnp.concatenate([state[3 * g + 2] / state[3 * g + 1] for g in range(heads)], axis=0)
    o_ref[...] = out_t.T.astype(o_ref.dtype)


def _alibi_slopes():
    slopes = np.exp2(-ALIBI_MAX * np.arange(1, MOBA_HEADS + 1, dtype=np.float32) / MOBA_HEADS).astype(np.float32)
    worst = slopes * np.float32(MOBA_BLOCK - 1)
    assert np.all(worst.astype(jnp.bfloat16).astype(np.float32) == worst), "ALiBi slopes not bf16-exact"
    return slopes


def _moba(qa, ka, vt, kmean, *, batch, seq):
    t = qa.shape[0]
    nb = seq // MOBA_BLOCK
    gw = MOBA_GROUP * HEAD_DIM
    assert nb % 8 == 0 and nb + 8 <= HEAD_DIM, "bias features must fit the spare contraction lanes"
    grid_spec = pltpu.PrefetchScalarGridSpec(
        num_scalar_prefetch=1,
        grid=(batch, MOBA_HEADS // MOBA_GROUP, nb),
        in_specs=[
            pl.BlockSpec((MOBA_BLOCK, gw), lambda b, h, c, s: (b * nb + c, h)),
            pl.BlockSpec((seq, gw), lambda b, h, c, s: (b, h)),
            pl.BlockSpec((nb, gw, MOBA_BLOCK), lambda b, h, c, s: (b, h, 0)),
            pl.BlockSpec((nb, gw), lambda b, h, c, s: (b, h)),
        ],
        out_specs=pl.BlockSpec((MOBA_BLOCK, gw), lambda b, h, c, s: (b * nb + c, h)),
        scratch_shapes=[pltpu.VMEM((MOBA_GROUP, seq, 2 * HEAD_DIM), BF16),
                        pltpu.VMEM((2, MOBA_GROUP, 2 * MOBA_BLOCK, MOBA_BLOCK), F32)],
    )
    return pl.pallas_call(
        _moba_kernel, grid_spec=grid_spec, out_shape=jax.ShapeDtypeStruct((t, MOBA_WIDTH), BF16),
        compiler_params=_params("arbitrary", "arbitrary", "arbitrary"), name="moba",
    )(jnp.asarray(_alibi_slopes()), qa, ka, vt, kmean)


def _gla_kernel(q_ref, k_ref, v_ref, r_ref, la_ref, gn_ref, o_ref, st_ref):
    @pl.when(pl.program_id(1) == 0)
    def _():
        st_ref[...] = jnp.zeros_like(st_ref)

    ch = GLA_CHUNK
    tc = q_ref.shape[0]
    n_chunks = tc // ch
    row = lax.broadcasted_iota(jnp.int32, (tc, tc), 0)
    col = lax.broadcasted_iota(jnp.int32, (tc, tc), 1)
    causal = (row // ch == col // ch) & (row >= col)
    tri = jnp.where(causal, 1.0, 0.0).astype(BF16)
    lane = lax.broadcasted_iota(jnp.int32, (1, GLA_KEY_WIDTH), 1)
    la = la_ref[...]
    la_hi = la.astype(BF16)
    rest = la - la_hi.astype(F32)
    la_mid = rest.astype(BF16)
    la_lo = (rest - la_mid.astype(F32)).astype(BF16)
    g = _dot(tri, la_hi) + _dot(tri, la_mid) + _dot(tri, la_lo)
    ends = [g[(ci + 1) * ch - 1:(ci + 1) * ch, :] for ci in range(n_chunks)]
    g_last = jnp.concatenate([jnp.broadcast_to(e, (ch, GLA_KEY_WIDTH)) for e in ends], axis=0)
    qd = q_ref[...] * (GLA_DK ** -0.5) * jnp.exp(g)
    k = k_ref[...]
    kd = (k * jnp.exp(-g)).astype(BF16)
    kt = k * jnp.exp(g_last - g)
    qhs, kths, vhs, intra = [], [], [], []
    for h in range(GLA_HEADS):
        hm = lane // GLA_DK == h
        qhs.append(jnp.where(hm, qd, 0.0).astype(BF16))
        kths.append(jnp.where(hm, kt, 0.0).astype(BF16))
        vhs.append(v_ref[:, h * GLA_DV:(h + 1) * GLA_DV].astype(BF16))
        a = jnp.where(causal, _dot(qhs[h], kd, _NT), 0.0)
        intra.append(_dot(a.astype(BF16), vhs[h]))
    state = st_ref[...]
    inter = [[] for _ in range(GLA_HEADS)]
    for ci in range(n_chunks):
        rows = slice(ci * ch, (ci + 1) * ch)
        state_b = state.astype(BF16)
        new_state = state * jnp.exp(ends[ci])
        for h in range(GLA_HEADS):
            inter[h].append(_dot(qhs[h][rows], state_b, _NT))
            new_state = new_state + _dot(vhs[h][rows], kths[h][rows], _TN)
        state = new_state
    st_ref[...] = state
    gn = gn_ref[...]
    for h in range(GLA_HEADS):
        o = intra[h] + jnp.concatenate(inter[h], axis=0)
        rh = r_ref[:, h * GLA_DV:(h + 1) * GLA_DV]
        o_ref[:, h * GLA_DV:(h + 1) * GLA_DV] = (_rms(o, gn) * (rh * jax.nn.sigmoid(rh))).astype(o_ref.dtype)


def _gla(qg, kg, vg, rg, la, g_norm, *, batch, seq, tc=256):
    t = qg.shape[0]
    steps = seq // tc
    row = lambda b, i: (b * steps + i, 0)
    kspec = pl.BlockSpec((tc, GLA_KEY_WIDTH), row)
    vspec = pl.BlockSpec((tc, GLA_VAL_WIDTH), row)
    return pl.pallas_call(
        _gla_kernel, grid=(batch, steps),
        in_specs=[kspec, kspec, vspec, vspec, kspec, pl.BlockSpec((1, GLA_DV), lambda b, i: (0, 0))],
        out_specs=vspec, out_shape=jax.ShapeDtypeStruct((t, GLA_VAL_WIDTH), BF16),
        scratch_shapes=[pltpu.VMEM((GLA_DV, GLA_KEY_WIDTH), F32)],
        compiler_params=_params("arbitrary", "arbitrary"), name="gla",
    )(qg, kg, vg, rg, la, g_norm.reshape(1, GLA_DV))


def _outproj_kernel(x_ref, om_ref, og_ref, wm_ref, wg_ref, gf_ref, wr_ref, br_ref, h_ref, f_ref, lg_ref):
    h = x_ref[...] + _dot(om_ref[...], wm_ref[...]) + _dot(og_ref[...], wg_ref[...])
    h_ref[...] = h
    f = _rms(h, gf_ref[...]).astype(BF16)
    f_ref[...] = f
    lg_ref[...] = _dot(wr_ref[...], f, _NT) + br_ref[...]


def _outproj(x2d, o_moba, o_gla, w_out, g_ffn, w_router, b_router, *, tm=512):
    t, d = x2d.shape
    wm = w_out[:MOBA_WIDTH].astype(BF16)
    wg = w_out[MOBA_WIDTH:].astype(BF16)
    wr = w_router.T.astype(BF16)
    br = b_router.reshape(N_EXPERTS, 1)
    row = lambda i: (i, 0)
    const = lambda i: (0, 0)
    full = lambda a: pl.BlockSpec(a.shape, const)
    args = (x2d, o_moba, o_gla, wm, wg, g_ffn.reshape(1, d), wr, br)
    in_specs = [pl.BlockSpec((tm, d), row), pl.BlockSpec((tm, MOBA_WIDTH), row),
                pl.BlockSpec((tm, GLA_VAL_WIDTH), row)] + [full(a) for a in args[3:]]
    return pl.pallas_call(
        _outproj_kernel, grid=(t // tm,), in_specs=in_specs,
        out_specs=(pl.BlockSpec((tm, d), row), pl.BlockSpec((tm, d), row),
                   pl.BlockSpec((N_EXPERTS, tm), lambda i: (0, i))),
        out_shape=(jax.ShapeDtypeStruct((t, d), F32), jax.ShapeDtypeStruct((t, d), BF16),
                   jax.ShapeDtypeStruct((N_EXPERTS, t), F32)),
        compiler_params=_params("arbitrary"), name="outproj",
    )(*args)


PAIR_CHUNK = 2 * LANES


def _moe_kernel(be_ref, nused_ref, x_ref, w1_ref, b1_ref, w2_ref, b2_ref, o_ref, w1s_ref, w2s_ref):
    i = pl.program_id(0)
    f = w2_ref.shape[1]

    @pl.when((i == 0) | (be_ref[i] != be_ref[jnp.maximum(i - 1, 0)]))
    def _():
        r = lax.broadcasted_iota(jnp.int32, (PAIR_CHUNK, PAIR_CHUNK), 0)
        c = lax.broadcasted_iota(jnp.int32, (PAIR_CHUNK, PAIR_CHUNK), 1)
        src_col = jnp.where(c < LANES, 2 * c, 2 * (c - LANES) + 1)
        perm = jnp.where(r == src_col, 1.0, 0.0).astype(BF16)
        for cc in range(2 * f // PAIR_CHUNK):
            chunk = w1_ref[0, :, cc * PAIR_CHUNK:(cc + 1) * PAIR_CHUNK].astype(BF16)
            res = _dot(chunk, perm).astype(BF16)
            w1s_ref[:, cc * LANES:(cc + 1) * LANES] = res[:, :LANES]
            w1s_ref[:, f + cc * LANES:f + (cc + 1) * LANES] = res[:, LANES:]
        w2s_ref[...] = w2_ref[0].astype(BF16)

    @pl.when(i < nused_ref[0])
    def _():
        hid = _dot(x_ref[...], w1s_ref[...]) + b1_ref[0]
        glu = jnp.minimum(hid[:, :f], SWIGLU_LIMIT)
        lin = jnp.clip(hid[:, f:], -SWIGLU_LIMIT, SWIGLU_LIMIT)
        act = glu * jax.nn.sigmoid(SWIGLU_ALPHA * glu) * (lin + 1.0)
        o_ref[...] = (_dot(act.astype(BF16), w2s_ref[...]) + b2_ref[0]).astype(o_ref.dtype)

    @pl.when(i >= nused_ref[0])
    def _():
        o_ref[...] = jnp.zeros_like(o_ref)


def _moe(x_rows, block_expert, n_used, w1, b1, w2, b2, *, tm):
    n_rows, d = x_rows.shape
    e, _, f2 = w1.shape
    f = f2 // 2
    b1p = jnp.concatenate([b1[:, 0::2], b1[:, 1::2]], axis=-1).reshape(e, 1, f2)
    b2r = b2.reshape(e, 1, d)
    grid_spec = pltpu.PrefetchScalarGridSpec(
        num_scalar_prefetch=2,
        grid=(n_rows // tm,),
        in_specs=[
            pl.BlockSpec((tm, d), lambda i, be, nu: (i, 0)),
            pl.BlockSpec((1, d, f2), lambda i, be, nu: (be[i], 0, 0)),
            pl.BlockSpec((1, 1, f2), lambda i, be, nu: (be[i], 0, 0)),
            pl.BlockSpec((1, f, d), lambda i, be, nu: (be[i], 0, 0)),
            pl.BlockSpec((1, 1, d), lambda i, be, nu: (be[i], 0, 0)),
        ],
        out_specs=pl.BlockSpec((tm, d), lambda i, be, nu: (i, 0)),
        scratch_shapes=[pltpu.VMEM((d, f2), BF16), pltpu.VMEM((f, d), BF16)],
    )
    return pl.pallas_call(
        _moe_kernel, grid_spec=grid_spec, out_shape=jax.ShapeDtypeStruct((n_rows, d), BF16),
        compiler_params=_params("arbitrary"), name="moe",
    )(block_expert, n_used, x_rows, w1, b1p, w2, b2r)


ROUTE_CHUNK = 2 * LANES


def _router_kernel(lg_ref, idx_ref, gate_ref, rank_ref, cnt_ref, carry_ref):
    @pl.when(pl.program_id(0) == 0)
    def _():
        carry_ref[...] = jnp.zeros_like(carry_ref)

    x = lg_ref[...]
    ne, tt = x.shape
    eidx = lax.broadcasted_iota(jnp.int32, (ne, tt), 0)
    member = jnp.zeros((ne, tt), F32)
    vals, ids = [], []
    for _ in range(TOP_K):
        m = jnp.max(x, axis=0, keepdims=True)
        sel = jnp.min(jnp.where(x == m, eidx, ne), axis=0, keepdims=True)
        hit = eidx == sel
        member = jnp.where(hit, 1.0, member)
        x = jnp.where(hit, -jnp.inf, x)
        vals.append(m)
        ids.append(sel)
    e = jnp.exp(jnp.concatenate(vals, axis=0) - vals[0])
    gate_ref[...] = e / jnp.sum(e, axis=0, keepdims=True)
    idx_ref[...] = jnp.concatenate(ids, axis=0)
    r = lax.broadcasted_iota(jnp.int32, (ROUTE_CHUNK, ROUTE_CHUNK), 0)
    c = lax.broadcasted_iota(jnp.int32, (ROUTE_CHUNK, ROUTE_CHUNK), 1)
    earlier = jnp.where(r < c, 1.0, 0.0).astype(BF16)
    carry = carry_ref[:, :1]
    before = []
    for ch in range(tt // ROUTE_CHUNK):
        mc = member[:, ch * ROUTE_CHUNK:(ch + 1) * ROUTE_CHUNK]
        before.append(_dot(mc.astype(BF16), earlier) + carry)
        carry = carry + jnp.sum(mc, axis=1, keepdims=True)
    before = jnp.concatenate(before, axis=1)
    ranks = [jnp.sum(jnp.where(eidx == ids[k], before, 0.0), axis=0, keepdims=True) for k in range(TOP_K)]
    rank_ref[...] = jnp.concatenate(ranks, axis=0).astype(jnp.int32)
    carry_ref[...] = jnp.broadcast_to(carry, carry_ref.shape)
    cnt_ref[...] = jnp.broadcast_to(carry, cnt_ref.shape).astype(jnp.int32)


def _router(logits_t, *, tt=2048):
    ne, t = logits_t.shape
    tok = lambda i: (0, i)
    kspec = pl.BlockSpec((TOP_K, tt), tok)
    return pl.pallas_call(
        _router_kernel, grid=(t // tt,), in_specs=[pl.BlockSpec((ne, tt), tok)],
        out_specs=(kspec, kspec, kspec, pl.BlockSpec((ne, LANES), lambda i: (0, 0))),
        out_shape=(jax.ShapeDtypeStruct((TOP_K, t), jnp.int32), jax.ShapeDtypeStruct((TOP_K, t), F32),
                   jax.ShapeDtypeStruct((TOP_K, t), jnp.int32), jax.ShapeDtypeStruct((ne, LANES), jnp.int32)),
        scratch_shapes=[pltpu.VMEM((ne, LANES), F32)],
        compiler_params=_params("arbitrary"), name="router",
    )(logits_t)


def _route(logits_t, *, tm):
    ne, t = logits_t.shape
    idx, gates, rank, cnt = _router(logits_t)
    counts = cnt[:, 0]
    padded = (counts + tm - 1) // tm * tm
    pad_end = jnp.cumsum(padded)
    pad_start = pad_end - padded
    experts = jnp.arange(ne, dtype=jnp.int32).reshape(ne, 1, 1)
    slot = jnp.sum(jnp.where(idx[None] == experts, pad_start.reshape(ne, 1, 1), 0), axis=0) + rank
    n_blocks = -(-TOP_K * t // tm) + ne
    flat_slot = slot.reshape(-1)
    tok = jnp.broadcast_to(jnp.arange(t, dtype=jnp.int32), (TOP_K, t)).reshape(-1)
    row_tok = (jnp.arange(n_blocks * tm, dtype=jnp.int32) % t).at[flat_slot].add(
        tok - flat_slot % t, unique_indices=True, mode="promise_in_bounds")
    n_used = pad_end[-1] // tm
    blk_start = jnp.minimum(jnp.arange(n_blocks, dtype=jnp.int32), n_used - 1) * tm
    block_expert = jnp.minimum(jnp.sum(pad_end[:, None] <= blk_start[None, :], axis=0), ne - 1)
    return gates, row_tok, slot, block_expert.astype(jnp.int32), n_used.reshape(1).astype(jnp.int32)


def _final_kernel(h_ref, y_ref, gate_ref, p_ref, wp_ref, gp_ref, wgate_ref, gfin_ref, o_ref, *, last_layer):
    h = h_ref[...]
    g = gate_ref[...]
    gates = jnp.concatenate([g, jnp.zeros((8 - TOP_K, g.shape[1]), F32)], axis=0).T
    for k in range(TOP_K):
        h = h + gates[:, k:k + 1] * y_ref[k].astype(F32)
    ple = _rms(_dot(p_ref[...].astype(BF16), wp_ref[...]), gp_ref[...])
    h = h + jax.nn.sigmoid(_dot(h.astype(BF16), wgate_ref[...])) * ple
    o_ref[...] = _rms(h, gfin_ref[...]) if last_layer else h


def _final(h1, y_tok, gates, p2d, w_ple_proj, g_ple, w_ple_gate, g_final, *, last_layer, tm=512):
    t, d = h1.shape
    row = lambda i: (i, 0)
    const = lambda i: (0, 0)
    full = lambda a: pl.BlockSpec(a.shape, const)
    args = (h1, y_tok, gates, p2d, w_ple_proj.astype(BF16), g_ple.reshape(1, d), w_ple_gate.astype(BF16),
            g_final.reshape(1, d))
    in_specs = [pl.BlockSpec((tm, d), row), pl.BlockSpec((TOP_K, tm, d), lambda i: (0, i, 0)),
                pl.BlockSpec((TOP_K, tm), lambda i: (0, i)), pl.BlockSpec((tm, p2d.shape[1]), row)]
    in_specs += [full(a) for a in args[4:]]
    return pl.pallas_call(
        functools.partial(_final_kernel, last_layer=last_layer), grid=(t // tm,), in_specs=in_specs,
        out_specs=pl.BlockSpec((tm, d), row),
        out_shape=jax.ShapeDtypeStruct((t, d), F32),
        compiler_params=_params("arbitrary"), name="final",
    )(*args)


def kernel(x, p, g_mix, w_in, w_gla_gate, b_gla_gate, g_gla_norm, w_out, g_ffn, w_router, b_router,
           w1, b1, w2, b2, w_ple_gate, w_ple_proj, g_ple, g_final, *, moe_tm=512):
    batch, seq, d = x.shape
    depth = p.shape[0]
    t = batch * seq
    h = x.reshape(t, d)
    for i in range(depth):
        qa, ka, vt, kmean, qg, kg, vg, rg, la = _inproj(h, g_mix[i], w_in[i], w_gla_gate[i], b_gla_gate[i],
                                                        batch=batch, seq=seq)
        o_moba = _moba(qa, ka, vt, kmean, batch=batch, seq=seq)
        o_gla = _gla(qg, kg, vg, rg, la, g_gla_norm[i], batch=batch, seq=seq)
        h1, f, logits = _outproj(h, o_moba, o_gla, w_out[i], g_ffn[i], w_router[i], b_router[i])
        gates, row_tok, slot, block_expert, n_used = _route(logits, tm=moe_tm)
        x_rows = jnp.take(f, row_tok, axis=0, mode="clip")
        y = _moe(x_rows, block_expert, n_used, w1[i], b1[i], w2[i], b2[i], tm=moe_tm)
        y_tok = jnp.take(y, slot, axis=0, mode="clip")
        h = _final(h1, y_tok, gates, p[i].reshape(t, -1), w_ple_proj[i], g_ple[i], w_ple_gate[i],
                   g_final, last_layer=i == depth - 1)
    return h.reshape(batch, seq, d)
```

```python
import functools

import jax
import jax.numpy as jnp
import numpy as np
from jax import lax
from jax.experimental import pallas as pl
from jax.experimental.pallas import tpu as pltpu

HEAD_DIM = 64
MOBA_HEADS = 8
MOBA_WIDTH = MOBA_HEADS * HEAD_DIM
MOBA_BLOCK = 256
MOBA_TOPK = 3
MOBA_GROUP = 4
ALIBI_MAX = 8.0
GLA_HEADS = 4
GLA_DK = 64
GLA_DV = 128
GLA_KEY_WIDTH = GLA_HEADS * GLA_DK
GLA_VAL_WIDTH = GLA_HEADS * GLA_DV
GLA_GATE_RANK = 16
GLA_GATE_TAU = 16.0
GLA_CHUNK = 64
N_EXPERTS = 32
TOP_K = 4
SWIGLU_ALPHA = 1.702
SWIGLU_LIMIT = 7.0
EPS = 1e-6

LANES = 128
MASKED = -1e30
VMEM_LIMIT = 56 * 1024 * 1024

BF16 = jnp.bfloat16
F32 = jnp.float32

_NT = (((1,), (1,)), ((), ()))
_TN = (((0,), (0,)), ((), ()))


def _dot(a, b, dims=None, precision=None):
    if dims is None:
        return jnp.dot(a, b, preferred_element_type=F32, precision=precision)
    return lax.dot_general(a, b, dims, preferred_element_type=F32, precision=precision)


def _rms(x, g):
    return x * lax.rsqrt(jnp.mean(x * x, axis=-1, keepdims=True) + EPS) * g


def _col_reduce(x, pair_op, final_op, rows=64):
    parts = [x[i:i + rows] for i in range(0, x.shape[0], rows)]
    while len(parts) > 1:
        parts = [pair_op(parts[i], parts[i + 1]) for i in range(0, len(parts), 2)]
    return final_op(parts[0], axis=0, keepdims=True)


def _params(*sem):
    return pltpu.CompilerParams(dimension_semantics=sem, vmem_limit_bytes=VMEM_LIMIT)


def _inproj_kernel(x_ref, g_ref, wq_ref, wk_ref, wvt_ref, wgla_ref, wag_ref, wgate_ref, bgate_ref,
                   qa_ref, ka_ref, vt_ref, km_ref, qg_ref, kg_ref, vg_ref, rg_ref, la_ref):
    i = pl.program_id(1)
    tm = x_ref.shape[0]
    nblk = tm // MOBA_BLOCK
    @pl.when(i == 0)
    def _():
        km_ref[...] = jnp.zeros_like(km_ref)

    ab = _rms(x_ref[...], g_ref[...]).astype(BF16)
    ag = _dot(ab, wag_ref[...]).astype(BF16)
    z = _dot(ag, wgate_ref[...]) + bgate_ref[...]
    la_ref[...] = jax.nn.log_sigmoid(z) / GLA_GATE_TAU
    qa_ref[...] = (_dot(ab, wq_ref[...]) * (HEAD_DIM ** -0.5)).astype(BF16)
    ka = _dot(ab, wk_ref[...])
    ka_ref[...] = ka.astype(BF16)
    km = km_ref[...]
    blk_row = lax.broadcasted_iota(jnp.int32, km.shape, 0)
    for j in range(nblk):
        mean_j = jnp.mean(ka[j * MOBA_BLOCK:(j + 1) * MOBA_BLOCK], axis=0, keepdims=True)
        km = jnp.where(blk_row == i * nblk + j, mean_j, km)
    km_ref[...] = km
    vt = _dot(wvt_ref[...], ab, _NT).astype(BF16)
    for j in range(nblk):
        vt_ref[j] = vt[:, j * MOBA_BLOCK:(j + 1) * MOBA_BLOCK]
    gla = _dot(ab, wgla_ref[...])
    o = 0
    for ref in (qg_ref, kg_ref, vg_ref, rg_ref):
        w = ref.shape[1]
        ref[...] = gla[:, o:o + w]
        o += w


def _inproj(x2d, g_mix, w_in, w_gla_gate, b_gla_gate, *, batch, seq, tm=512):
    t, d = x2d.shape
    nb = seq // MOBA_BLOCK
    o_k, o_v, o_qg = MOBA_WIDTH, 2 * MOBA_WIDTH, 3 * MOBA_WIDTH
    o_ag = o_qg + 2 * GLA_KEY_WIDTH + 2 * GLA_VAL_WIDTH
    wq = w_in[:, :o_k].astype(BF16)
    wk = w_in[:, o_k:o_v].astype(BF16)
    wvt = w_in[:, o_v:o_qg].T.astype(BF16)
    wgla = w_in[:, o_qg:o_ag].astype(BF16)
    wag = jnp.pad(w_in[:, o_ag:], ((0, 0), (0, LANES - GLA_GATE_RANK))).astype(BF16)
    wgate = jnp.pad(w_gla_gate, ((0, LANES - GLA_GATE_RANK), (0, 0))).astype(BF16)
    steps = seq // tm
    row = lambda b, i: (b * steps + i, 0)
    const = lambda b, i: (0, 0)
    full = lambda a: pl.BlockSpec(a.shape, const)
    args = (x2d, g_mix.reshape(1, d), wq, wk, wvt, wgla, wag, wgate, b_gla_gate.reshape(1, -1))
    in_specs = [pl.BlockSpec((tm, d), row)] + [full(a) for a in args[1:]]
    out_shape = (
        jax.ShapeDtypeStruct((t, MOBA_WIDTH), BF16),
        jax.ShapeDtypeStruct((t, MOBA_WIDTH), BF16),
        jax.ShapeDtypeStruct((t // MOBA_BLOCK, MOBA_WIDTH, MOBA_BLOCK), BF16),
        jax.ShapeDtypeStruct((batch * nb, MOBA_WIDTH), F32),
        jax.ShapeDtypeStruct((t, GLA_KEY_WIDTH), F32),
        jax.ShapeDtypeStruct((t, GLA_KEY_WIDTH), F32),
        jax.ShapeDtypeStruct((t, GLA_VAL_WIDTH), F32),
        jax.ShapeDtypeStruct((t, GLA_VAL_WIDTH), F32),
        jax.ShapeDtypeStruct((t, GLA_KEY_WIDTH), F32),
    )
    out_specs = (
        pl.BlockSpec((tm, MOBA_WIDTH), row),
        pl.BlockSpec((tm, MOBA_WIDTH), row),
        pl.BlockSpec((tm // MOBA_BLOCK, MOBA_WIDTH, MOBA_BLOCK), lambda b, i: (b * steps + i, 0, 0)),
        pl.BlockSpec((nb, MOBA_WIDTH), lambda b, i: (b, 0)),
        pl.BlockSpec((tm, GLA_KEY_WIDTH), row),
        pl.BlockSpec((tm, GLA_KEY_WIDTH), row),
        pl.BlockSpec((tm, GLA_VAL_WIDTH), row),
        pl.BlockSpec((tm, GLA_VAL_WIDTH), row),
        pl.BlockSpec((tm, GLA_KEY_WIDTH), row),
    )
    return pl.pallas_call(
        _inproj_kernel, grid=(batch, steps), in_specs=in_specs, out_specs=out_specs, out_shape=out_shape,
        compiler_params=_params("arbitrary", "arbitrary"), name="inproj",
    )(*args)


def _moba_kernel(slopes_ref, q_ref, k_ref, vt_ref, km_ref, o_ref, kaug_ref, s_ref):
    hg = pl.program_id(1)
    c = pl.program_id(2)
    blk = MOBA_BLOCK
    nb = km_ref.shape[0]
    hw = 2 * HEAD_DIM
    heads = q_ref.shape[1] // HEAD_DIM
    feat_rows = nb + 8
    head_rows = lambda g: slice(g * HEAD_DIM, (g + 1) * HEAD_DIM)

    @pl.when(c == 0)
    def _():
        lane = lax.broadcasted_iota(jnp.int32, (blk, hw), 1)
        koff = lax.broadcasted_iota(jnp.int32, (blk, hw), 0).astype(F32)
        for g in range(heads):
            tile, hh = divmod(g, 2)
            slope = slopes_ref[hg * heads + g]
            base = HEAD_DIM * (1 - hh)
            for jb in range(nb):
                feat = jnp.where(lane == base + nb, slope * koff, 0.0)
                feat = jnp.where((lane == base + jb) | (lane == base + nb + 1), 1.0, feat)
                kb = k_ref[jb * blk:(jb + 1) * blk, tile * hw:(tile + 1) * hw]
                kaug_ref[g, jb * blk:(jb + 1) * blk, :] = jnp.where(lane // HEAD_DIM == hh, kb, feat.astype(BF16))

    qt = q_ref[...].astype(F32).T
    qt_b = qt.astype(BF16)
    kidx = lax.broadcasted_iota(jnp.int32, (blk, blk), 0)
    qidx = lax.broadcasted_iota(jnp.int32, (blk, blk), 1)
    causal = kidx <= qidx
    nidx = lax.broadcasted_iota(jnp.int32, (nb, blk), 0)
    r8 = lax.broadcasted_iota(jnp.int32, (8, blk), 0)
    qoff = lax.broadcasted_iota(jnp.int32, (8, blk), 1).astype(F32)
    km_lane = lax.broadcasted_iota(jnp.int32, km_ref.shape, 1)
    km = km_ref[...]
    own_rows = pl.ds(pl.multiple_of(c * blk, blk), blk)
    vt_own = vt_ref[c]
    ones = jnp.ones((16, blk), BF16)

    def values_and_ones(vt_blk, g):
        return jnp.concatenate([vt_blk[head_rows(g), :], ones], axis=0)

    q_past, state = [], []
    for g in range(heads):
        slope = slopes_ref[hg * heads + g]
        kmh = jnp.where(km_lane // HEAD_DIM == g, km, 0.0).astype(BF16)
        gate = jnp.where(nidx < c, _dot(kmh, qt_b), -jnp.inf)
        rank = jnp.zeros((nb, blk), jnp.int32)
        for m in range(nb - 1):
            gm = gate[m:m + 1, :]
            beats = (gm > gate) | ((gm == gate) & (m < nidx))
            rank = rank + beats.astype(jnp.int32)
        sel = (nidx < c) & (rank < MOBA_TOPK)
        dist0 = -slope * ((c - nidx) * blk).astype(F32)
        tail = jnp.where(r8 == 0, 1.0, jnp.where(r8 == 1, -slope * qoff, 0.0))
        fill = jnp.zeros((HEAD_DIM - feat_rows, blk), F32)
        qh = qt[head_rows(g), :]

        def q_aug(bias):
            feats = [jnp.where(bias, dist0, MASKED), tail, fill]
            return jnp.concatenate([qh] + feats if g % 2 == 0 else feats + [qh], axis=0).astype(BF16)

        q_past.append(q_aug(sel))
        s = _dot(kaug_ref[g, own_rows, :], q_aug(nidx == c))
        s = jnp.where(causal, s, MASKED)
        m0 = jnp.max(s, axis=0, keepdims=True)
        state += [m0, _dot(values_and_ones(vt_own, g), jnp.exp((s - m0).astype(BF16)))]

    def scores(pair, slot):
        rows = pl.ds(pl.multiple_of(pair * 2 * blk, 2 * blk), 2 * blk)
        tops = []
        for g in range(heads):
            s = _dot(kaug_ref[g, rows, :], q_past[g])
            s_ref[slot, g] = s
            tops.append(_col_reduce(s, jnp.maximum, jnp.max))
        return tops

    def absorb(pair, slot, tops, carry):
        vt0 = vt_ref[2 * pair]
        vt1 = vt_ref[2 * pair + 1]
        out = []
        for g in range(heads):
            m_i, acc = carry[2 * g:2 * g + 2]
            m_new = jnp.maximum(m_i, tops[g])
            p = jnp.exp((s_ref[slot, g] - m_new).astype(BF16))
            acc = (jnp.exp(m_i - m_new) * acc + _dot(values_and_ones(vt0, g), p[:blk])
                   + _dot(values_and_ones(vt1, g), p[blk:]))
            out += [m_new, acc]
        return out

    n_state = 2 * heads

    def step(i, carry, slot):
        tops = scores(i + 1, 1 - slot)
        return tuple(absorb(i, slot, carry[n_state:], carry[:n_state]) + tops)

    def body(i, carry):
        return lax.cond(i % 2 == 0, lambda: step(i, carry, 0), lambda: step(i, carry, 1))

    last = jnp.maximum((c + 1) // 2 - 1, 0)
    carry = lax.fori_loop(0, last, body, tuple(state + scores(0, 0)))
    state = lax.cond(last % 2 == 0, lambda: tuple(absorb(last, 0, carry[n_state:], carry[:n_state])),
                     lambda: tuple(absorb(last, 1, carry[n_state:], carry[:n_state])))
    accs = state[1::2]
    out_t = jnp.concatenate([a[:HEAD_DIM] / a[HEAD_DIM:HEAD_DIM + 1] for a in accs], axis=0)
    o_ref[...] = out_t.T.astype(o_ref.dtype)


def _alibi_slopes():
    slopes = np.exp2(-ALIBI_MAX * np.arange(1, MOBA_HEADS + 1, dtype=np.float32) / MOBA_HEADS).astype(np.float32)
    worst = slopes * np.float32(MOBA_BLOCK - 1)
    assert np.all(worst.astype(jnp.bfloat16).astype(np.float32) == worst), "ALiBi slopes not bf16-exact"
    return slopes


def _moba(qa, ka, vt, kmean, *, batch, seq):
    t = qa.shape[0]
    nb = seq // MOBA_BLOCK
    gw = MOBA_GROUP * HEAD_DIM
    assert nb % 8 == 0 and nb + 8 <= HEAD_DIM, "bias features must fit the spare contraction lanes"
    grid_spec = pltpu.PrefetchScalarGridSpec(
        num_scalar_prefetch=1,
        grid=(batch, MOBA_HEADS // MOBA_GROUP, nb),
        in_specs=[
            pl.BlockSpec((MOBA_BLOCK, gw), lambda b, h, c, s: (b * nb + c, h)),
            pl.BlockSpec((seq, gw), lambda b, h, c, s: (b, h)),
            pl.BlockSpec((nb, gw, MOBA_BLOCK), lambda b, h, c, s: (b, h, 0)),
            pl.BlockSpec((nb, gw), lambda b, h, c, s: (b, h)),
        ],
        out_specs=pl.BlockSpec((MOBA_BLOCK, gw), lambda b, h, c, s: (b * nb + c, h)),
        scratch_shapes=[pltpu.VMEM((MOBA_GROUP, seq, 2 * HEAD_DIM), BF16),
                        pltpu.VMEM((2, MOBA_GROUP, 2 * MOBA_BLOCK, MOBA_BLOCK), F32)],
    )
    return pl.pallas_call(
        _moba_kernel, grid_spec=grid_spec, out_shape=jax.ShapeDtypeStruct((t, MOBA_WIDTH), BF16),
        compiler_params=_params("arbitrary", "arbitrary", "arbitrary"), name="moba",
    )(jnp.asarray(_alibi_slopes()), qa, ka, vt, kmean)


def _gla_kernel(q_ref, k_ref, v_ref, r_ref, la_ref, gn_ref, o_ref, st_ref):
    @pl.when(pl.program_id(1) == 0)
    def _():
        st_ref[...] = jnp.zeros_like(st_ref)

    ch = GLA_CHUNK
    tc = q_ref.shape[0]
    n_chunks = tc // ch
    row = lax.broadcasted_iota(jnp.int32, (tc, tc), 0)
    col = lax.broadcasted_iota(jnp.int32, (tc, tc), 1)
    causal = (row // ch == col // ch) & (row >= col)
    tri = jnp.where(causal, 1.0, 0.0).astype(BF16)
    lane = lax.broadcasted_iota(jnp.int32, (1, GLA_KEY_WIDTH), 1)
    la = la_ref[...]
    la_hi = la.astype(BF16)
    rest = la - la_hi.astype(F32)
    la_mid = rest.astype(BF16)
    la_lo = (rest - la_mid.astype(F32)).astype(BF16)
    g = _dot(tri, la_hi) + _dot(tri, la_mid) + _dot(tri, la_lo)
    ends = [g[(ci + 1) * ch - 1:(ci + 1) * ch, :] for ci in range(n_chunks)]
    g_last = jnp.concatenate([jnp.broadcast_to(e, (ch, GLA_KEY_WIDTH)) for e in ends], axis=0)
    qd = q_ref[...] * (GLA_DK ** -0.5) * jnp.exp(g)
    k = k_ref[...]
    kd = (k * jnp.exp(-g)).astype(BF16)
    kt = k * jnp.exp(g_last - g)
    qhs, kths, vhs, intra = [], [], [], []
    for h in range(GLA_HEADS):
        hm = lane // GLA_DK == h
        qhs.append(jnp.where(hm, qd, 0.0).astype(BF16))
        kths.append(jnp.where(hm, kt, 0.0).astype(BF16))
        vhs.append(v_ref[:, h * GLA_DV:(h + 1) * GLA_DV].astype(BF16))
        a = jnp.where(causal, _dot(qhs[h], kd, _NT), 0.0)
        intra.append(_dot(a.astype(BF16), vhs[h]))
    state = st_ref[...]
    inter = [[] for _ in range(GLA_HEADS)]
    for ci in range(n_chunks):
        rows = slice(ci * ch, (ci + 1) * ch)
        state_b = state.astype(BF16)
        new_state = state * jnp.exp(ends[ci])
        for h in range(GLA_HEADS):
            inter[h].append(_dot(qhs[h][rows], state_b, _NT))
            new_state = new_state + _dot(vhs[h][rows], kths[h][rows], _TN)
        state = new_state
    st_ref[...] = state
    gn = gn_ref[...]
    for h in range(GLA_HEADS):
        o = intra[h] + jnp.concatenate(inter[h], axis=0)
        rh = r_ref[:, h * GLA_DV:(h + 1) * GLA_DV]
        o_ref[:, h * GLA_DV:(h + 1) * GLA_DV] = (_rms(o, gn) * (rh * jax.nn.sigmoid(rh))).astype(o_ref.dtype)


def _gla(qg, kg, vg, rg, la, g_norm, *, batch, seq, tc=256):
    t = qg.shape[0]
    steps = seq // tc
    row = lambda b, i: (b * steps + i, 0)
    kspec = pl.BlockSpec((tc, GLA_KEY_WIDTH), row)
    vspec = pl.BlockSpec((tc, GLA_VAL_WIDTH), row)
    return pl.pallas_call(
        _gla_kernel, grid=(batch, steps),
        in_specs=[kspec, kspec, vspec, vspec, kspec, pl.BlockSpec((1, GLA_DV), lambda b, i: (0, 0))],
        out_specs=vspec, out_shape=jax.ShapeDtypeStruct((t, GLA_VAL_WIDTH), BF16),
        scratch_shapes=[pltpu.VMEM((GLA_DV, GLA_KEY_WIDTH), F32)],
        compiler_params=_params("arbitrary", "arbitrary"), name="gla",
    )(qg, kg, vg, rg, la, g_norm.reshape(1, GLA_DV))


def _outproj_kernel(x_ref, om_ref, og_ref, wm_ref, wg_ref, gf_ref, wr_ref, br_ref, h_ref, f_ref, lg_ref):
    h = x_ref[...] + _dot(om_ref[...], wm_ref[...]) + _dot(og_ref[...], wg_ref[...])
    h_ref[...] = h
    f = _rms(h, gf_ref[...]).astype(BF16)
    f_ref[...] = f
    lg_ref[...] = _dot(wr_ref[...], f, _NT) + br_ref[...]


def _outproj(x2d, o_moba, o_gla, w_out, g_ffn, w_router, b_router, *, tm=512):
    t, d = x2d.shape
    wm = w_out[:MOBA_WIDTH].astype(BF16)
    wg = w_out[MOBA_WIDTH:].astype(BF16)
    wr = w_router.T.astype(BF16)
    br = b_router.reshape(N_EXPERTS, 1)
    row = lambda i: (i, 0)
    const = lambda i: (0, 0)
    full = lambda a: pl.BlockSpec(a.shape, const)
    args = (x2d, o_moba, o_gla, wm, wg, g_ffn.reshape(1, d), wr, br)
    in_specs = [pl.BlockSpec((tm, d), row), pl.BlockSpec((tm, MOBA_WIDTH), row),
                pl.BlockSpec((tm, GLA_VAL_WIDTH), row)] + [full(a) for a in args[3:]]
    return pl.pallas_call(
        _outproj_kernel, grid=(t // tm,), in_specs=in_specs,
        out_specs=(pl.BlockSpec((tm, d), row), pl.BlockSpec((tm, d), row),
                   pl.BlockSpec((N_EXPERTS, tm), lambda i: (0, i))),
        out_shape=(jax.ShapeDtypeStruct((t, d), F32), jax.ShapeDtypeStruct((t, d), BF16),
                   jax.ShapeDtypeStruct((N_EXPERTS, t), F32)),
        compiler_params=_params("arbitrary"), name="outproj",
    )(*args)


PAIR_CHUNK = 2 * LANES


def _moe_kernel(be_ref, nused_ref, x_ref, w1_ref, b1_ref, w2_ref, b2_ref, o_ref, w1s_ref, w2s_ref):
    i = pl.program_id(0)
    f = w2_ref.shape[1]

    @pl.when((i == 0) | (be_ref[i] != be_ref[jnp.maximum(i - 1, 0)]))
    def _():
        r = lax.broadcasted_iota(jnp.int32, (PAIR_CHUNK, PAIR_CHUNK), 0)
        c = lax.broadcasted_iota(jnp.int32, (PAIR_CHUNK, PAIR_CHUNK), 1)
        src_col = jnp.where(c < LANES, 2 * c, 2 * (c - LANES) + 1)
        perm = jnp.where(r == src_col, 1.0, 0.0).astype(BF16)
        for cc in range(2 * f // PAIR_CHUNK):
            chunk = w1_ref[0, :, cc * PAIR_CHUNK:(cc + 1) * PAIR_CHUNK].astype(BF16)
            res = _dot(chunk, perm).astype(BF16)
            w1s_ref[:, cc * LANES:(cc + 1) * LANES] = res[:, :LANES]
            w1s_ref[:, f + cc * LANES:f + (cc + 1) * LANES] = res[:, LANES:]
        w2s_ref[...] = w2_ref[0].astype(BF16)

    @pl.when(i < nused_ref[0])
    def _():
        hid = _dot(x_ref[...], w1s_ref[...]) + b1_ref[0]
        glu = jnp.minimum(hid[:, :f], SWIGLU_LIMIT)
        lin = jnp.clip(hid[:, f:], -SWIGLU_LIMIT, SWIGLU_LIMIT)
        act = glu * jax.nn.sigmoid(SWIGLU_ALPHA * glu) * (lin + 1.0)
        o_ref[...] = (_dot(act.astype(BF16), w2s_ref[...]) + b2_ref[0]).astype(o_ref.dtype)

    @pl.when(i >= nused_ref[0])
    def _():
        o_ref[...] = jnp.zeros_like(o_ref)


def _moe(x_rows, block_expert, n_used, w1, b1, w2, b2, *, tm):
    n_rows, d = x_rows.shape
    e, _, f2 = w1.shape
    f = f2 // 2
    b1p = jnp.concatenate([b1[:, 0::2], b1[:, 1::2]], axis=-1).reshape(e, 1, f2)
    b2r = b2.reshape(e, 1, d)
    grid_spec = pltpu.PrefetchScalarGridSpec(
        num_scalar_prefetch=2,
        grid=(n_rows // tm,),
        in_specs=[
            pl.BlockSpec((tm, d), lambda i, be, nu: (i, 0)),
            pl.BlockSpec((1, d, f2), lambda i, be, nu: (be[i], 0, 0)),
            pl.BlockSpec((1, 1, f2), lambda i, be, nu: (be[i], 0, 0)),
            pl.BlockSpec((1, f, d), lambda i, be, nu: (be[i], 0, 0)),
            pl.BlockSpec((1, 1, d), lambda i, be, nu: (be[i], 0, 0)),
        ],
        out_specs=pl.BlockSpec((tm, d), lambda i, be, nu: (i, 0)),
        scratch_shapes=[pltpu.VMEM((d, f2), BF16), pltpu.VMEM((f, d), BF16)],
    )
    return pl.pallas_call(
        _moe_kernel, grid_spec=grid_spec, out_shape=jax.ShapeDtypeStruct((n_rows, d), BF16),
        compiler_params=_params("arbitrary"), name="moe",
    )(block_expert, n_used, x_rows, w1, b1p, w2, b2r)


ROUTE_CHUNK = 2 * LANES


def _router_kernel(lg_ref, idx_ref, gate_ref, rank_ref, cnt_ref, carry_ref):
    @pl.when(pl.program_id(0) == 0)
    def _():
        carry_ref[...] = jnp.zeros_like(carry_ref)

    x = lg_ref[...]
    ne, tt = x.shape
    eidx = lax.broadcasted_iota(jnp.int32, (ne, tt), 0)
    member = jnp.zeros((ne, tt), F32)
    vals, ids = [], []
    for _ in range(TOP_K):
        m = jnp.max(x, axis=0, keepdims=True)
        sel = jnp.min(jnp.where(x == m, eidx, ne), axis=0, keepdims=True)
        hit = eidx == sel
        member = jnp.where(hit, 1.0, member)
        x = jnp.where(hit, -jnp.inf, x)
        vals.append(m)
        ids.append(sel)
    e = jnp.exp(jnp.concatenate(vals, axis=0) - vals[0])
    gate_ref[...] = e / jnp.sum(e, axis=0, keepdims=True)
    idx_ref[...] = jnp.concatenate(ids, axis=0)
    r = lax.broadcasted_iota(jnp.int32, (ROUTE_CHUNK, ROUTE_CHUNK), 0)
    c = lax.broadcasted_iota(jnp.int32, (ROUTE_CHUNK, ROUTE_CHUNK), 1)
    earlier = jnp.where(r < c, 1.0, 0.0).astype(BF16)
    carry = carry_ref[:, :1]
    before = []
    for ch in range(tt // ROUTE_CHUNK):
        mc = member[:, ch * ROUTE_CHUNK:(ch + 1) * ROUTE_CHUNK]
        before.append(_dot(mc.astype(BF16), earlier) + carry)
        carry = carry + jnp.sum(mc, axis=1, keepdims=True)
    before = jnp.concatenate(before, axis=1)
    ranks = [jnp.sum(jnp.where(eidx == ids[k], before, 0.0), axis=0, keepdims=True) for k in range(TOP_K)]
    rank_ref[...] = jnp.concatenate(ranks, axis=0).astype(jnp.int32)
    carry_ref[...] = jnp.broadcast_to(carry, carry_ref.shape)
    cnt_ref[...] = jnp.broadcast_to(carry, cnt_ref.shape).astype(jnp.int32)


def _router(logits_t, *, tt=2048):
    ne, t = logits_t.shape
    tok = lambda i: (0, i)
    kspec = pl.BlockSpec((TOP_K, tt), tok)
    return pl.pallas_call(
        _router_kernel, grid=(t // tt,), in_specs=[pl.BlockSpec((ne, tt), tok)],
        out_specs=(kspec, kspec, kspec, pl.BlockSpec((ne, LANES), lambda i: (0, 0))),
        out_shape=(jax.ShapeDtypeStruct((TOP_K, t), jnp.int32), jax.ShapeDtypeStruct((TOP_K, t), F32),
                   jax.ShapeDtypeStruct((TOP_K, t), jnp.int32), jax.ShapeDtypeStruct((ne, LANES), jnp.int32)),
        scratch_shapes=[pltpu.VMEM((ne, LANES), F32)],
        compiler_params=_params("arbitrary"), name="router",
    )(logits_t)


def _route(logits_t, *, tm):
    ne, t = logits_t.shape
    idx, gates, rank, cnt = _router(logits_t)
    counts = cnt[:, 0]
    padded = (counts + tm - 1) // tm * tm
    pad_end = jnp.cumsum(padded)
    pad_start = pad_end - padded
    experts = jnp.arange(ne, dtype=jnp.int32).reshape(ne, 1, 1)
    slot = jnp.sum(jnp.where(idx[None] == experts, pad_start.reshape(ne, 1, 1), 0), axis=0) + rank
    n_blocks = -(-TOP_K * t // tm) + ne
    flat_slot = slot.reshape(-1)
    tok = jnp.broadcast_to(jnp.arange(t, dtype=jnp.int32), (TOP_K, t)).reshape(-1)
    row_tok = (jnp.arange(n_blocks * tm, dtype=jnp.int32) % t).at[flat_slot].add(
        tok - flat_slot % t, unique_indices=True, mode="promise_in_bounds")
    n_used = pad_end[-1] // tm
    blk_start = jnp.minimum(jnp.arange(n_blocks, dtype=jnp.int32), n_used - 1) * tm
    block_expert = jnp.minimum(jnp.sum(pad_end[:, None] <= blk_start[None, :], axis=0), ne - 1)
    return gates, row_tok, slot, block_expert.astype(jnp.int32), n_used.reshape(1).astype(jnp.int32)


def _final_kernel(h_ref, y_ref, gate_ref, p_ref, wp_ref, gp_ref, wgate_ref, gfin_ref, o_ref, *, last_layer):
    h = h_ref[...]
    g = gate_ref[...]
    gates = jnp.concatenate([g, jnp.zeros((8 - TOP_K, g.shape[1]), F32)], axis=0).T
    for k in range(TOP_K):
        h = h + gates[:, k:k + 1] * y_ref[k].astype(F32)
    ple = _rms(_dot(p_ref[...].astype(BF16), wp_ref[...]), gp_ref[...])
    h = h + jax.nn.sigmoid(_dot(h.astype(BF16), wgate_ref[...])) * ple
    o_ref[...] = _rms(h, gfin_ref[...]) if last_layer else h


def _final(h1, y_tok, gates, p2d, w_ple_proj, g_ple, w_ple_gate, g_final, *, last_layer, tm=512):
    t, d = h1.shape
    row = lambda i: (i, 0)
    const = lambda i: (0, 0)
    full = lambda a: pl.BlockSpec(a.shape, const)
    args = (h1, y_tok, gates, p2d, w_ple_proj.astype(BF16), g_ple.reshape(1, d), w_ple_gate.astype(BF16),
            g_final.reshape(1, d))
    in_specs = [pl.BlockSpec((tm, d), row), pl.BlockSpec((TOP_K, tm, d), lambda i: (0, i, 0)),
                pl.BlockSpec((TOP_K, tm), lambda i: (0, i)), pl.BlockSpec((tm, p2d.shape[1]), row)]
    in_specs += [full(a) for a in args[4:]]
    return pl.pallas_call(
        functools.partial(_final_kernel, last_layer=last_layer), grid=(t // tm,), in_specs=in_specs,
        out_specs=pl.BlockSpec((tm, d), row),
        out_shape=jax.ShapeDtypeStruct((t, d), F32),
        compiler_params=_params("arbitrary"), name="final",
    )(*args)


def kernel(x, p, g_mix, w_in, w_gla_gate, b_gla_gate, g_gla_norm, w_out, g_ffn, w_router, b_router,
           w1, b1, w2, b2, w_ple_gate, w_ple_proj, g_ple, g_final, *, moe_tm=512):
    batch, seq, d = x.shape
    depth = p.shape[0]
    t = batch * seq
    h = x.reshape(t, d)
    for i in range(depth):
        qa, ka, vt, kmean, qg, kg, vg, rg, la = _inproj(h, g_mix[i], w_in[i], w_gla_gate[i], b_gla_gate[i],
                                                        batch=batch, seq=seq)
        o_moba = _moba(qa, ka, vt, kmean, batch=batch, seq=seq)
        o_gla = _gla(qg, kg, vg, rg, la, g_gla_norm[i], batch=batch, seq=seq)
        h1, f, logits = _outproj(h, o_moba, o_gla, w_out[i], g_ffn[i], w_router[i], b_router[i])
        gates, row_tok, slot, block_expert, n_used = _route(logits, tm=moe_tm)
        x_rows = jnp.take(f, row_tok, axis=0, mode="clip")
        y = _moe(x_rows, block_expert, n_used, w1[i], b1[i], w2[i], b2[i], tm=moe_tm)
        y_tok = jnp.take(y, slot, axis=0, mode="clip")
        h = _final(h1, y_tok, gates, p[i].reshape(t, -1), w_ple_proj[i], g_ple[i], w_ple_gate[i],
                   g_final, last_layer=i == depth - 1)
    return h.reshape(batch, seq, d)
```

```python
import functools

import jax
import jax.numpy as jnp
import numpy as np
from jax import lax
from jax.experimental import pallas as pl
from jax.experimental.pallas import tpu as pltpu

HEAD_DIM = 64
MOBA_HEADS = 8
MOBA_WIDTH = MOBA_HEADS * HEAD_DIM
MOBA_BLOCK = 256
MOBA_TOPK = 3
MOBA_GROUP = 4
ALIBI_MAX = 8.0
GLA_HEADS = 4
GLA_DK = 64
GLA_DV = 128
GLA_KEY_WIDTH = GLA_HEADS * GLA_DK
GLA_VAL_WIDTH = GLA_HEADS * GLA_DV
GLA_GATE_RANK = 16
GLA_GATE_TAU = 16.0
GLA_CHUNK = 64
GLA_SEQS = 4
N_EXPERTS = 32
TOP_K = 4
SWIGLU_ALPHA = 1.702
SWIGLU_LIMIT = 7.0
EPS = 1e-6

LANES = 128
MASKED = -1e30
VMEM_LIMIT = 56 * 1024 * 1024

BF16 = jnp.bfloat16
F32 = jnp.float32

_NT = (((1,), (1,)), ((), ()))
_TN = (((0,), (0,)), ((), ()))


def _dot(a, b, dims=None, precision=None):
    if dims is None:
        return jnp.dot(a, b, preferred_element_type=F32, precision=precision)
    return lax.dot_general(a, b, dims, preferred_element_type=F32, precision=precision)


def _rms(x, g):
    return x * lax.rsqrt(jnp.mean(x * x, axis=-1, keepdims=True) + EPS) * g


def _col_reduce(x, pair_op, final_op, rows=64):
    parts = [x[i:i + rows] for i in range(0, x.shape[0], rows)]
    while len(parts) > 1:
        parts = [pair_op(parts[i], parts[i + 1]) for i in range(0, len(parts), 2)]
    return final_op(parts[0], axis=0, keepdims=True)


def _params(*sem):
    return pltpu.CompilerParams(dimension_semantics=sem, vmem_limit_bytes=VMEM_LIMIT)


def _inproj_kernel(x_ref, g_ref, wq_ref, wk_ref, wvt_ref, wgla_ref, wag_ref, wgate_ref, bgate_ref,
                   qa_ref, ka_ref, vt_ref, km_ref, qg_ref, kg_ref, vg_ref, rg_ref, la_ref):
    i = pl.program_id(1)
    tm = x_ref.shape[0]
    nblk = tm // MOBA_BLOCK
    @pl.when(i == 0)
    def _():
        km_ref[...] = jnp.zeros_like(km_ref)

    ab = _rms(x_ref[...], g_ref[...]).astype(BF16)
    ag = _dot(ab, wag_ref[...]).astype(BF16)
    z = _dot(ag, wgate_ref[...]) + bgate_ref[...]
    la_ref[...] = jax.nn.log_sigmoid(z) / GLA_GATE_TAU
    qa_ref[...] = (_dot(ab, wq_ref[...]) * (HEAD_DIM ** -0.5)).astype(BF16)
    ka = _dot(ab, wk_ref[...])
    ka_ref[...] = ka.astype(BF16)
    km = km_ref[...]
    blk_row = lax.broadcasted_iota(jnp.int32, km.shape, 0)
    for j in range(nblk):
        mean_j = jnp.mean(ka[j * MOBA_BLOCK:(j + 1) * MOBA_BLOCK], axis=0, keepdims=True)
        km = jnp.where(blk_row == i * nblk + j, mean_j, km)
    km_ref[...] = km
    vt = _dot(wvt_ref[...], ab, _NT).astype(BF16)
    for j in range(nblk):
        vt_ref[j] = vt[:, j * MOBA_BLOCK:(j + 1) * MOBA_BLOCK]
    gla = _dot(ab, wgla_ref[...])
    o = 0
    for ref in (qg_ref, kg_ref, vg_ref, rg_ref):
        w = ref.shape[1]
        ref[...] = gla[:, o:o + w]
        o += w


def _inproj(x2d, g_mix, w_in, w_gla_gate, b_gla_gate, *, batch, seq, tm=512):
    t, d = x2d.shape
    nb = seq // MOBA_BLOCK
    o_k, o_v, o_qg = MOBA_WIDTH, 2 * MOBA_WIDTH, 3 * MOBA_WIDTH
    o_ag = o_qg + 2 * GLA_KEY_WIDTH + 2 * GLA_VAL_WIDTH
    wq = w_in[:, :o_k].astype(BF16)
    wk = w_in[:, o_k:o_v].astype(BF16)
    wvt = w_in[:, o_v:o_qg].T.astype(BF16)
    wgla = w_in[:, o_qg:o_ag].astype(BF16)
    wag = jnp.pad(w_in[:, o_ag:], ((0, 0), (0, LANES - GLA_GATE_RANK))).astype(BF16)
    wgate = jnp.pad(w_gla_gate, ((0, LANES - GLA_GATE_RANK), (0, 0))).astype(BF16)
    steps = seq // tm
    row = lambda b, i: (b * steps + i, 0)
    const = lambda b, i: (0, 0)
    full = lambda a: pl.BlockSpec(a.shape, const)
    args = (x2d, g_mix.reshape(1, d), wq, wk, wvt, wgla, wag, wgate, b_gla_gate.reshape(1, -1))
    in_specs = [pl.BlockSpec((tm, d), row)] + [full(a) for a in args[1:]]
    out_shape = (
        jax.ShapeDtypeStruct((t, MOBA_WIDTH), BF16),
        jax.ShapeDtypeStruct((t, MOBA_WIDTH), BF16),
        jax.ShapeDtypeStruct((t // MOBA_BLOCK, MOBA_WIDTH, MOBA_BLOCK), BF16),
        jax.ShapeDtypeStruct((batch * nb, MOBA_WIDTH), F32),
        jax.ShapeDtypeStruct((t, GLA_KEY_WIDTH), F32),
        jax.ShapeDtypeStruct((t, GLA_KEY_WIDTH), F32),
        jax.ShapeDtypeStruct((t, GLA_VAL_WIDTH), F32),
        jax.ShapeDtypeStruct((t, GLA_VAL_WIDTH), F32),
        jax.ShapeDtypeStruct((t, GLA_KEY_WIDTH), F32),
    )
    out_specs = (
        pl.BlockSpec((tm, MOBA_WIDTH), row),
        pl.BlockSpec((tm, MOBA_WIDTH), row),
        pl.BlockSpec((tm // MOBA_BLOCK, MOBA_WIDTH, MOBA_BLOCK), lambda b, i: (b * steps + i, 0, 0)),
        pl.BlockSpec((nb, MOBA_WIDTH), lambda b, i: (b, 0)),
        pl.BlockSpec((tm, GLA_KEY_WIDTH), row),
        pl.BlockSpec((tm, GLA_KEY_WIDTH), row),
        pl.BlockSpec((tm, GLA_VAL_WIDTH), row),
        pl.BlockSpec((tm, GLA_VAL_WIDTH), row),
        pl.BlockSpec((tm, GLA_KEY_WIDTH), row),
    )
    return pl.pallas_call(
        _inproj_kernel, grid=(batch, steps), in_specs=in_specs, out_specs=out_specs, out_shape=out_shape,
        compiler_params=_params("arbitrary", "arbitrary"), name="inproj",
    )(*args)


def _moba_kernel(slopes_ref, q_ref, k_ref, vt_ref, km_ref, o_ref, kaug_ref, s_ref):
    hg = pl.program_id(1)
    c = pl.program_id(2)
    blk = MOBA_BLOCK
    nb = km_ref.shape[0]
    hw = 2 * HEAD_DIM
    heads = q_ref.shape[1] // HEAD_DIM
    feat_rows = nb + 8
    head_rows = lambda g: slice(g * HEAD_DIM, (g + 1) * HEAD_DIM)

    @pl.when(c == 0)
    def _():
        lane = lax.broadcasted_iota(jnp.int32, (blk, hw), 1)
        koff = lax.broadcasted_iota(jnp.int32, (blk, hw), 0).astype(F32)
        for g in range(heads):
            tile, hh = divmod(g, 2)
            slope = slopes_ref[hg * heads + g]
            base = HEAD_DIM * (1 - hh)
            for jb in range(nb):
                feat = jnp.where(lane == base + nb, slope * koff, 0.0)
                feat = jnp.where((lane == base + jb) | (lane == base + nb + 1), 1.0, feat)
                kb = k_ref[jb * blk:(jb + 1) * blk, tile * hw:(tile + 1) * hw]
                kaug_ref[g, jb * blk:(jb + 1) * blk, :] = jnp.where(lane // HEAD_DIM == hh, kb, feat.astype(BF16))

    qt = q_ref[...].astype(F32).T
    qt_b = qt.astype(BF16)
    kidx = lax.broadcasted_iota(jnp.int32, (blk, blk), 0)
    qidx = lax.broadcasted_iota(jnp.int32, (blk, blk), 1)
    causal = kidx <= qidx
    nidx = lax.broadcasted_iota(jnp.int32, (nb, blk), 0)
    r8 = lax.broadcasted_iota(jnp.int32, (8, blk), 0)
    qoff = lax.broadcasted_iota(jnp.int32, (8, blk), 1).astype(F32)
    km_lane = lax.broadcasted_iota(jnp.int32, km_ref.shape, 1)
    km = km_ref[...]
    own_rows = pl.ds(pl.multiple_of(c * blk, blk), blk)
    vt_own = vt_ref[c]
    ones = jnp.ones((16, blk), BF16)

    def values_and_ones(vt_blk, g):
        return jnp.concatenate([vt_blk[head_rows(g), :], ones], axis=0)

    hs = range(heads)
    slopes = [slopes_ref[hg * heads + g] for g in hs]
    gates = [jnp.where(nidx < c, _dot(jnp.where(km_lane // HEAD_DIM == g, km, 0.0).astype(BF16), qt_b), -jnp.inf)
             for g in hs]
    ranks = [jnp.zeros((nb, blk), jnp.int32) for _ in hs]
    for m in range(nb - 1):
        for g in hs:
            gm = gates[g][m:m + 1, :]
            beats = (gm > gates[g]) | ((gm == gates[g]) & (m < nidx))
            ranks[g] = ranks[g] + beats.astype(jnp.int32)
    fill = jnp.zeros((HEAD_DIM - feat_rows, blk), F32)

    def q_aug(g, bias):
        dist0 = -slopes[g] * ((c - nidx) * blk).astype(F32)
        tail = jnp.where(r8 == 0, 1.0, jnp.where(r8 == 1, -slopes[g] * qoff, 0.0))
        feats = [jnp.where(bias, dist0, MASKED), tail, fill]
        qh = qt[head_rows(g), :]
        return jnp.concatenate([qh] + feats if g % 2 == 0 else feats + [qh], axis=0).astype(BF16)

    q_past = [q_aug(g, (nidx < c) & (ranks[g] < MOBA_TOPK)) for g in hs]
    q_own = [q_aug(g, nidx == c) for g in hs]
    s_own = [jnp.where(causal, _dot(kaug_ref[g, own_rows, :], q_own[g]), MASKED) for g in hs]
    m_own = [jnp.max(s, axis=0, keepdims=True) for s in s_own]
    p_own = [jnp.exp((s_own[g] - m_own[g]).astype(BF16)) for g in hs]
    state = []
    for g in hs:
        state += [m_own[g], _dot(values_and_ones(vt_own, g), p_own[g])]

    def scores(pair, slot):
        rows = pl.ds(pl.multiple_of(pair * 2 * blk, 2 * blk), 2 * blk)
        tops = []
        for g in range(heads):
            s = _dot(kaug_ref[g, rows, :], q_past[g])
            s_ref[slot, g] = s
            tops.append(_col_reduce(s, jnp.maximum, jnp.max))
        return tops

    def absorb(pair, slot, tops, carry):
        vt0 = vt_ref[2 * pair]
        vt1 = vt_ref[2 * pair + 1]
        out = []
        for g in range(heads):
            m_i, acc = carry[2 * g:2 * g + 2]
            m_new = jnp.maximum(m_i, tops[g])
            p = jnp.exp((s_ref[slot, g] - m_new).astype(BF16))
            acc = (jnp.exp(m_i - m_new) * acc + _dot(values_and_ones(vt0, g), p[:blk])
                   + _dot(values_and_ones(vt1, g), p[blk:]))
            out += [m_new, acc]
        return out

    n_state = 2 * heads

    def step(i, carry, slot):
        tops = scores(i + 1, 1 - slot)
        return tuple(absorb(i, slot, carry[n_state:], carry[:n_state]) + tops)

    def body(i, carry):
        return lax.cond(i % 2 == 0, lambda: step(i, carry, 0), lambda: step(i, carry, 1))

    last = jnp.maximum((c + 1) // 2 - 1, 0)
    carry = lax.fori_loop(0, last, body, tuple(state + scores(0, 0)))
    state = lax.cond(last % 2 == 0, lambda: tuple(absorb(last, 0, carry[n_state:], carry[:n_state])),
                     lambda: tuple(absorb(last, 1, carry[n_state:], carry[:n_state])))
    accs = state[1::2]
    out_t = jnp.concatenate([a[:HEAD_DIM] / a[HEAD_DIM:HEAD_DIM + 1] for a in accs], axis=0)
    o_ref[...] = out_t.T.astype(o_ref.dtype)


def _alibi_slopes():
    slopes = np.exp2(-ALIBI_MAX * np.arange(1, MOBA_HEADS + 1, dtype=np.float32) / MOBA_HEADS).astype(np.float32)
    worst = slopes * np.float32(MOBA_BLOCK - 1)
    assert np.all(worst.astype(jnp.bfloat16).astype(np.float32) == worst), "ALiBi slopes not bf16-exact"
    return slopes


def _moba(qa, ka, vt, kmean, *, batch, seq):
    t = qa.shape[0]
    nb = seq // MOBA_BLOCK
    gw = MOBA_GROUP * HEAD_DIM
    assert nb % 8 == 0 and nb + 8 <= HEAD_DIM, "bias features must fit the spare contraction lanes"
    grid_spec = pltpu.PrefetchScalarGridSpec(
        num_scalar_prefetch=1,
        grid=(batch, MOBA_HEADS // MOBA_GROUP, nb),
        in_specs=[
            pl.BlockSpec((MOBA_BLOCK, gw), lambda b, h, c, s: (b * nb + c, h)),
            pl.BlockSpec((seq, gw), lambda b, h, c, s: (b, h)),
            pl.BlockSpec((nb, gw, MOBA_BLOCK), lambda b, h, c, s: (b, h, 0)),
            pl.BlockSpec((nb, gw), lambda b, h, c, s: (b, h)),
        ],
        out_specs=pl.BlockSpec((MOBA_BLOCK, gw), lambda b, h, c, s: (b * nb + c, h)),
        scratch_shapes=[pltpu.VMEM((MOBA_GROUP, seq, 2 * HEAD_DIM), BF16),
                        pltpu.VMEM((2, MOBA_GROUP, 2 * MOBA_BLOCK, MOBA_BLOCK), F32)],
    )
    return pl.pallas_call(
        _moba_kernel, grid_spec=grid_spec, out_shape=jax.ShapeDtypeStruct((t, MOBA_WIDTH), BF16),
        compiler_params=_params("arbitrary", "arbitrary", "arbitrary"), name="moba",
    )(jnp.asarray(_alibi_slopes()), qa, ka, vt, kmean)


def _gla_kernel(q_ref, k_ref, v_ref, r_ref, la_ref, gn_ref, o_ref, st_ref):
    @pl.when(pl.program_id(1) == 0)
    def _():
        st_ref[...] = jnp.zeros_like(st_ref)

    ch = GLA_CHUNK
    n_seqs, tc = q_ref.shape[:2]
    n_chunks = tc // ch
    row = lax.broadcasted_iota(jnp.int32, (tc, tc), 0)
    col = lax.broadcasted_iota(jnp.int32, (tc, tc), 1)
    causal = (row // ch == col // ch) & (row >= col)
    tri = jnp.where(causal, 1.0, 0.0).astype(BF16)
    lane = lax.broadcasted_iota(jnp.int32, (1, GLA_KEY_WIDTH), 1)
    gn = gn_ref[...]
    seqs = range(n_seqs)
    heads = range(GLA_HEADS)
    gs = []
    for s in seqs:
        la = la_ref[s]
        la_hi = la.astype(BF16)
        rest = la - la_hi.astype(F32)
        la_mid = rest.astype(BF16)
        la_lo = (rest - la_mid.astype(F32)).astype(BF16)
        gs.append(_dot(tri, la_hi) + _dot(tri, la_mid) + _dot(tri, la_lo))
    ends = [[g[(ci + 1) * ch - 1:(ci + 1) * ch, :] for ci in range(n_chunks)] for g in gs]
    qhs, kds, kths, vhs = [], [], [], []
    for s in seqs:
        g = gs[s]
        g_last = jnp.concatenate([jnp.broadcast_to(e, (ch, GLA_KEY_WIDTH)) for e in ends[s]], axis=0)
        qd = q_ref[s] * (GLA_DK ** -0.5) * jnp.exp(g)
        k = k_ref[s]
        kds.append((k * jnp.exp(-g)).astype(BF16))
        kt = k * jnp.exp(g_last - g)
        qhs.append([jnp.where(lane // GLA_DK == h, qd, 0.0).astype(BF16) for h in heads])
        kths.append([jnp.where(lane // GLA_DK == h, kt, 0.0).astype(BF16) for h in heads])
        vhs.append([v_ref[s, :, h * GLA_DV:(h + 1) * GLA_DV].astype(BF16) for h in heads])
    intra = [[None] * GLA_HEADS for _ in seqs]
    for h in heads:
        for s in seqs:
            a = jnp.where(causal, _dot(qhs[s][h], kds[s], _NT), 0.0)
            intra[s][h] = _dot(a.astype(BF16), vhs[s][h])
    states = [st_ref[s] for s in seqs]
    inter = [[[] for _ in heads] for _ in seqs]
    for ci in range(n_chunks):
        rows = slice(ci * ch, (ci + 1) * ch)
        for s in seqs:
            state_b = states[s].astype(BF16)
            new_state = states[s] * jnp.exp(ends[s][ci])
            for h in heads:
                inter[s][h].append(_dot(qhs[s][h][rows], state_b, _NT))
                new_state = new_state + _dot(vhs[s][h][rows], kths[s][h][rows], _TN)
            states[s] = new_state
    for s in seqs:
        st_ref[s] = states[s]
    for h in heads:
        for s in seqs:
            o = intra[s][h] + jnp.concatenate(inter[s][h], axis=0)
            rh = r_ref[s, :, h * GLA_DV:(h + 1) * GLA_DV]
            o_ref[s, :, h * GLA_DV:(h + 1) * GLA_DV] = (_rms(o, gn) * (rh * jax.nn.sigmoid(rh))).astype(o_ref.dtype)


def _gla(qg, kg, vg, rg, la, g_norm, *, batch, seq, tc=256):
    n_seqs = GLA_SEQS if batch % GLA_SEQS == 0 else 1
    by_seq = lambda a: a.reshape(batch, seq, a.shape[-1])
    blk = lambda i, j: (i, j, 0)
    kspec = pl.BlockSpec((n_seqs, tc, GLA_KEY_WIDTH), blk)
    vspec = pl.BlockSpec((n_seqs, tc, GLA_VAL_WIDTH), blk)
    out = pl.pallas_call(
        _gla_kernel, grid=(batch // n_seqs, seq // tc),
        in_specs=[kspec, kspec, vspec, vspec, kspec, pl.BlockSpec((1, GLA_DV), lambda i, j: (0, 0))],
        out_specs=vspec, out_shape=jax.ShapeDtypeStruct((batch, seq, GLA_VAL_WIDTH), BF16),
        scratch_shapes=[pltpu.VMEM((n_seqs, GLA_DV, GLA_KEY_WIDTH), F32)],
        compiler_params=_params("arbitrary", "arbitrary"), name="gla",
    )(by_seq(qg), by_seq(kg), by_seq(vg), by_seq(rg), by_seq(la), g_norm.reshape(1, GLA_DV))
    return out.reshape(batch * seq, GLA_VAL_WIDTH)


def _outproj_kernel(x_ref, om_ref, og_ref, wm_ref, wg_ref, gf_ref, wr_ref, br_ref, h_ref, f_ref, lg_ref):
    h = x_ref[...] + _dot(om_ref[...], wm_ref[...]) + _dot(og_ref[...], wg_ref[...])
    h_ref[...] = h
    f = _rms(h, gf_ref[...]).astype(BF16)
    f_ref[...] = f
    lg_ref[...] = _dot(wr_ref[...], f, _NT) + br_ref[...]


def _outproj(x2d, o_moba, o_gla, w_out, g_ffn, w_router, b_router, *, tm=512):
    t, d = x2d.shape
    wm = w_out[:MOBA_WIDTH].astype(BF16)
    wg = w_out[MOBA_WIDTH:].astype(BF16)
    wr = w_router.T.astype(BF16)
    br = b_router.reshape(N_EXPERTS, 1)
    row = lambda i: (i, 0)
    const = lambda i: (0, 0)
    full = lambda a: pl.BlockSpec(a.shape, const)
    args = (x2d, o_moba, o_gla, wm, wg, g_ffn.reshape(1, d), wr, br)
    in_specs = [pl.BlockSpec((tm, d), row), pl.BlockSpec((tm, MOBA_WIDTH), row),
                pl.BlockSpec((tm, GLA_VAL_WIDTH), row)] + [full(a) for a in args[3:]]
    return pl.pallas_call(
        _outproj_kernel, grid=(t // tm,), in_specs=in_specs,
        out_specs=(pl.BlockSpec((tm, d), row), pl.BlockSpec((tm, d), row),
                   pl.BlockSpec((N_EXPERTS, tm), lambda i: (0, i))),
        out_shape=(jax.ShapeDtypeStruct((t, d), F32), jax.ShapeDtypeStruct((t, d), BF16),
                   jax.ShapeDtypeStruct((N_EXPERTS, t), F32)),
        compiler_params=_params("arbitrary"), name="outproj",
    )(*args)


PAIR_CHUNK = 2 * LANES


def _moe_kernel(be_ref, nused_ref, x_ref, w1_ref, b1_ref, w2_ref, b2_ref, o_ref, w1s_ref, w2s_ref):
    i = pl.program_id(0)
    f = w2_ref.shape[1]

    @pl.when((i == 0) | (be_ref[i] != be_ref[jnp.maximum(i - 1, 0)]))
    def _():
        r = lax.broadcasted_iota(jnp.int32, (PAIR_CHUNK, PAIR_CHUNK), 0)
        c = lax.broadcasted_iota(jnp.int32, (PAIR_CHUNK, PAIR_CHUNK), 1)
        src_col = jnp.where(c < LANES, 2 * c, 2 * (c - LANES) + 1)
        perm = jnp.where(r == src_col, 1.0, 0.0).astype(BF16)
        for cc in range(2 * f // PAIR_CHUNK):
            chunk = w1_ref[0, :, cc * PAIR_CHUNK:(cc + 1) * PAIR_CHUNK].astype(BF16)
            res = _dot(chunk, perm).astype(BF16)
            w1s_ref[:, cc * LANES:(cc + 1) * LANES] = res[:, :LANES]
            w1s_ref[:, f + cc * LANES:f + (cc + 1) * LANES] = res[:, LANES:]
        w2s_ref[...] = w2_ref[0].astype(BF16)

    @pl.when(i < nused_ref[0])
    def _():
        hid = _dot(x_ref[...], w1s_ref[...]) + b1_ref[0]
        glu = jnp.minimum(hid[:, :f], SWIGLU_LIMIT)
        lin = jnp.clip(hid[:, f:], -SWIGLU_LIMIT, SWIGLU_LIMIT)
        act = glu * jax.nn.sigmoid(SWIGLU_ALPHA * glu) * (lin + 1.0)
        o_ref[...] = (_dot(act.astype(BF16), w2s_ref[...]) + b2_ref[0]).astype(o_ref.dtype)

    @pl.when(i >= nused_ref[0])
    def _():
        o_ref[...] = jnp.zeros_like(o_ref)


def _moe(x_rows, block_expert, n_used, w1, b1, w2, b2, *, tm):
    n_rows, d = x_rows.shape
    e, _, f2 = w1.shape
    f = f2 // 2
    b1p = jnp.concatenate([b1[:, 0::2], b1[:, 1::2]], axis=-1).reshape(e, 1, f2)
    b2r = b2.reshape(e, 1, d)
    grid_spec = pltpu.PrefetchScalarGridSpec(
        num_scalar_prefetch=2,
        grid=(n_rows // tm,),
        in_specs=[
            pl.BlockSpec((tm, d), lambda i, be, nu: (i, 0)),
            pl.BlockSpec((1, d, f2), lambda i, be, nu: (be[i], 0, 0)),
            pl.BlockSpec((1, 1, f2), lambda i, be, nu: (be[i], 0, 0)),
            pl.BlockSpec((1, f, d), lambda i, be, nu: (be[i], 0, 0)),
            pl.BlockSpec((1, 1, d), lambda i, be, nu: (be[i], 0, 0)),
        ],
        out_specs=pl.BlockSpec((tm, d), lambda i, be, nu: (i, 0)),
        scratch_shapes=[pltpu.VMEM((d, f2), BF16), pltpu.VMEM((f, d), BF16)],
    )
    return pl.pallas_call(
        _moe_kernel, grid_spec=grid_spec, out_shape=jax.ShapeDtypeStruct((n_rows, d), BF16),
        compiler_params=_params("arbitrary"), name="moe",
    )(block_expert, n_used, x_rows, w1, b1p, w2, b2r)


ROUTE_CHUNK = 2 * LANES


def _router_kernel(lg_ref, idx_ref, gate_ref, rank_ref, cnt_ref, carry_ref):
    @pl.when(pl.program_id(0) == 0)
    def _():
        carry_ref[...] = jnp.zeros_like(carry_ref)

    x = lg_ref[...]
    ne, tt = x.shape
    eidx = lax.broadcasted_iota(jnp.int32, (ne, tt), 0)
    member = jnp.zeros((ne, tt), F32)
    vals, ids = [], []
    for _ in range(TOP_K):
        m = jnp.max(x, axis=0, keepdims=True)
        sel = jnp.min(jnp.where(x == m, eidx, ne), axis=0, keepdims=True)
        hit = eidx == sel
        member = jnp.where(hit, 1.0, member)
        x = jnp.where(hit, -jnp.inf, x)
        vals.append(m)
        ids.append(sel)
    e = jnp.exp(jnp.concatenate(vals, axis=0) - vals[0])
    gate_ref[...] = e / jnp.sum(e, axis=0, keepdims=True)
    idx_ref[...] = jnp.concatenate(ids, axis=0)
    r = lax.broadcasted_iota(jnp.int32, (ROUTE_CHUNK, ROUTE_CHUNK), 0)
    c = lax.broadcasted_iota(jnp.int32, (ROUTE_CHUNK, ROUTE_CHUNK), 1)
    earlier = jnp.where(r < c, 1.0, 0.0).astype(BF16)
    carry = carry_ref[:, :1]
    before = []
    for ch in range(tt // ROUTE_CHUNK):
        mc = member[:, ch * ROUTE_CHUNK:(ch + 1) * ROUTE_CHUNK]
        before.append(_dot(mc.astype(BF16), earlier) + carry)
        carry = carry + jnp.sum(mc, axis=1, keepdims=True)
    before = jnp.concatenate(before, axis=1)
    ranks = [jnp.sum(jnp.where(eidx == ids[k], before, 0.0), axis=0, keepdims=True) for k in range(TOP_K)]
    rank_ref[...] = jnp.concatenate(ranks, axis=0).astype(jnp.int32)
    carry_ref[...] = jnp.broadcast_to(carry, carry_ref.shape)
    cnt_ref[...] = jnp.broadcast_to(carry, cnt_ref.shape).astype(jnp.int32)


def _router(logits_t, *, tt=2048):
    ne, t = logits_t.shape
    tok = lambda i: (0, i)
    kspec = pl.BlockSpec((TOP_K, tt), tok)
    return pl.pallas_call(
        _router_kernel, grid=(t // tt,), in_specs=[pl.BlockSpec((ne, tt), tok)],
        out_specs=(kspec, kspec, kspec, pl.BlockSpec((ne, LANES), lambda i: (0, 0))),
        out_shape=(jax.ShapeDtypeStruct((TOP_K, t), jnp.int32), jax.ShapeDtypeStruct((TOP_K, t), F32),
                   jax.ShapeDtypeStruct((TOP_K, t), jnp.int32), jax.ShapeDtypeStruct((ne, LANES), jnp.int32)),
        scratch_shapes=[pltpu.VMEM((ne, LANES), F32)],
        compiler_params=_params("arbitrary"), name="router",
    )(logits_t)


def _route(logits_t, *, tm):
    ne, t = logits_t.shape
    idx, gates, rank, cnt = _router(logits_t)
    counts = cnt[:, 0]
    padded = (counts + tm - 1) // tm * tm
    pad_end = jnp.cumsum(padded)
    pad_start = pad_end - padded
    experts = jnp.arange(ne, dtype=jnp.int32).reshape(ne, 1, 1)
    slot = jnp.sum(jnp.where(idx[None] == experts, pad_start.reshape(ne, 1, 1), 0), axis=0) + rank
    n_blocks = -(-TOP_K * t // tm) + ne
    flat_slot = slot.reshape(-1)
    tok = jnp.broadcast_to(jnp.arange(t, dtype=jnp.int32), (TOP_K, t)).reshape(-1)
    row_tok = (jnp.arange(n_blocks * tm, dtype=jnp.int32) % t).at[flat_slot].add(
        tok - flat_slot % t, unique_indices=True, mode="promise_in_bounds")
    n_used = pad_end[-1] // tm
    blk_start = jnp.minimum(jnp.arange(n_blocks, dtype=jnp.int32), n_used - 1) * tm
    block_expert = jnp.minimum(jnp.sum(pad_end[:, None] <= blk_start[None, :], axis=0), ne - 1)
    return gates, row_tok, slot, block_expert.astype(jnp.int32), n_used.reshape(1).astype(jnp.int32)


def _final_kernel(h_ref, y_ref, gate_ref, p_ref, wp_ref, gp_ref, wgate_ref, gfin_ref, o_ref, *, last_layer):
    h = h_ref[...]
    g = gate_ref[...]
    gates = jnp.concatenate([g, jnp.zeros((8 - TOP_K, g.shape[1]), F32)], axis=0).T
    for k in range(TOP_K):
        h = h + gates[:, k:k + 1] * y_ref[k].astype(F32)
    ple = _rms(_dot(p_ref[...].astype(BF16), wp_ref[...]), gp_ref[...])
    h = h + jax.nn.sigmoid(_dot(h.astype(BF16), wgate_ref[...])) * ple
    o_ref[...] = _rms(h, gfin_ref[...]) if last_layer else h


def _final(h1, y_tok, gates, p2d, w_ple_proj, g_ple, w_ple_gate, g_final, *, last_layer, tm=512):
    t, d = h1.shape
    row = lambda i: (i, 0)
    const = lambda i: (0, 0)
    full = lambda a: pl.BlockSpec(a.shape, const)
    args = (h1, y_tok, gates, p2d, w_ple_proj.astype(BF16), g_ple.reshape(1, d), w_ple_gate.astype(BF16),
            g_final.reshape(1, d))
    in_specs = [pl.BlockSpec((tm, d), row), pl.BlockSpec((TOP_K, tm, d), lambda i: (0, i, 0)),
                pl.BlockSpec((TOP_K, tm), lambda i: (0, i)), pl.BlockSpec((tm, p2d.shape[1]), row)]
    in_specs += [full(a) for a in args[4:]]
    return pl.pallas_call(
        functools.partial(_final_kernel, last_layer=last_layer), grid=(t // tm,), in_specs=in_specs,
        out_specs=pl.BlockSpec((tm, d), row),
        out_shape=jax.ShapeDtypeStruct((t, d), F32),
        compiler_params=_params("arbitrary"), name="final",
    )(*args)


def kernel(x, p, g_mix, w_in, w_gla_gate, b_gla_gate, g_gla_norm, w_out, g_ffn, w_router, b_router,
           w1, b1, w2, b2, w_ple_gate, w_ple_proj, g_ple, g_final, *, moe_tm=512):
    batch, seq, d = x.shape
    depth = p.shape[0]
    t = batch * seq
    h = x.reshape(t, d)
    for i in range(depth):
        qa, ka, vt, kmean, qg, kg, vg, rg, la = _inproj(h, g_mix[i], w_in[i], w_gla_gate[i], b_gla_gate[i],
                                                        batch=batch, seq=seq)
        o_moba = _moba(qa, ka, vt, kmean, batch=batch, seq=seq)
        o_gla = _gla(qg, kg, vg, rg, la, g_gla_norm[i], batch=batch, seq=seq)
        h1, f, logits = _outproj(h, o_moba, o_gla, w_out[i], g_ffn[i], w_router[i], b_router[i])
        gates, row_tok, slot, block_expert, n_used = _route(logits, tm=moe_tm)
        x_rows = jnp.take(f, row_tok, axis=0, mode="clip")
        y = _moe(x_rows, block_expert, n_used, w1[i], b1[i], w2[i], b2[i], tm=moe_tm)
        y_tok = jnp.take(y, slot, axis=0, mode="clip")
        h = _final(h1, y_tok, gates, p[i].reshape(t, -1), w_ple_proj[i], g_ple[i], w_ple_gate[i],
                   g_final, last_layer=i == depth - 1)
    return h.reshape(batch, seq, d)
```

```python
import functools

import jax
import jax.numpy as jnp
import numpy as np
from jax import lax
from jax.experimental import pallas as pl
from jax.experimental.pallas import tpu as pltpu

HEAD_DIM = 64
MOBA_HEADS = 8
MOBA_WIDTH = MOBA_HEADS * HEAD_DIM
MOBA_BLOCK = 256
MOBA_TOPK = 3
MOBA_GROUP = 4
ALIBI_MAX = 8.0
GLA_HEADS = 4
GLA_DK = 64
GLA_DV = 128
GLA_KEY_WIDTH = GLA_HEADS * GLA_DK
GLA_VAL_WIDTH = GLA_HEADS * GLA_DV
GLA_GATE_RANK = 16
GLA_GATE_TAU = 16.0
GLA_CHUNK = 64
GLA_SEQS = 4
N_EXPERTS = 32
TOP_K = 4
SWIGLU_ALPHA = 1.702
SWIGLU_LIMIT = 7.0
EPS = 1e-6

LANES = 128
ROW_SUB = 128
MASKED = -1e30
VMEM_LIMIT = 56 * 1024 * 1024

BF16 = jnp.bfloat16
F32 = jnp.float32

_NT = (((1,), (1,)), ((), ()))
_TN = (((0,), (0,)), ((), ()))


def _dot(a, b, dims=None, precision=None):
    if dims is None:
        return jnp.dot(a, b, preferred_element_type=F32, precision=precision)
    return lax.dot_general(a, b, dims, preferred_element_type=F32, precision=precision)


def _rms(x, g):
    return x * lax.rsqrt(jnp.mean(x * x, axis=-1, keepdims=True) + EPS) * g


def _col_reduce(x, pair_op, final_op, rows=64):
    parts = [x[i:i + rows] for i in range(0, x.shape[0], rows)]
    while len(parts) > 1:
        parts = [pair_op(parts[i], parts[i + 1]) for i in range(0, len(parts), 2)]
    return final_op(parts[0], axis=0, keepdims=True)


def _params(*sem):
    return pltpu.CompilerParams(dimension_semantics=sem, vmem_limit_bytes=VMEM_LIMIT)


def _inproj_kernel(x_ref, g_ref, wq_ref, wk_ref, wvt_ref, wgla_ref, wag_ref, wgate_ref, bgate_ref,
                   qa_ref, ka_ref, vt_ref, km_ref, qg_ref, kg_ref, vg_ref, rg_ref, la_ref):
    i = pl.program_id(1)
    tm = x_ref.shape[0]
    nblk = tm // MOBA_BLOCK
    @pl.when(i == 0)
    def _():
        km_ref[...] = jnp.zeros_like(km_ref)

    ab = _rms(x_ref[...], g_ref[...]).astype(BF16)
    ag = _dot(ab, wag_ref[...]).astype(BF16)
    z = _dot(ag, wgate_ref[...]) + bgate_ref[...]
    la_ref[...] = jax.nn.log_sigmoid(z) / GLA_GATE_TAU
    qa_ref[...] = (_dot(ab, wq_ref[...]) * (HEAD_DIM ** -0.5)).astype(BF16)
    ka = _dot(ab, wk_ref[...])
    ka_ref[...] = ka.astype(BF16)
    km = km_ref[...]
    blk_row = lax.broadcasted_iota(jnp.int32, km.shape, 0)
    for j in range(nblk):
        mean_j = jnp.mean(ka[j * MOBA_BLOCK:(j + 1) * MOBA_BLOCK], axis=0, keepdims=True)
        km = jnp.where(blk_row == i * nblk + j, mean_j, km)
    km_ref[...] = km
    vt = _dot(wvt_ref[...], ab, _NT).astype(BF16)
    for j in range(nblk):
        vt_ref[j] = vt[:, j * MOBA_BLOCK:(j + 1) * MOBA_BLOCK]
    gla = _dot(ab, wgla_ref[...])
    o = 0
    for ref in (qg_ref, kg_ref, vg_ref, rg_ref):
        w = ref.shape[1]
        ref[...] = gla[:, o:o + w]
        o += w


def _inproj(x2d, g_mix, w_in, w_gla_gate, b_gla_gate, *, batch, seq, tm=512):
    t, d = x2d.shape
    nb = seq // MOBA_BLOCK
    o_k, o_v, o_qg = MOBA_WIDTH, 2 * MOBA_WIDTH, 3 * MOBA_WIDTH
    o_ag = o_qg + 2 * GLA_KEY_WIDTH + 2 * GLA_VAL_WIDTH
    wq = w_in[:, :o_k].astype(BF16)
    wk = w_in[:, o_k:o_v].astype(BF16)
    wvt = w_in[:, o_v:o_qg].T.astype(BF16)
    wgla = w_in[:, o_qg:o_ag].astype(BF16)
    wag = jnp.pad(w_in[:, o_ag:], ((0, 0), (0, LANES - GLA_GATE_RANK))).astype(BF16)
    wgate = jnp.pad(w_gla_gate, ((0, LANES - GLA_GATE_RANK), (0, 0))).astype(BF16)
    steps = seq // tm
    row = lambda b, i: (b * steps + i, 0)
    const = lambda b, i: (0, 0)
    full = lambda a: pl.BlockSpec(a.shape, const)
    args = (x2d, g_mix.reshape(1, d), wq, wk, wvt, wgla, wag, wgate, b_gla_gate.reshape(1, -1))
    in_specs = [pl.BlockSpec((tm, d), row)] + [full(a) for a in args[1:]]
    out_shape = (
        jax.ShapeDtypeStruct((t, MOBA_WIDTH), BF16),
        jax.ShapeDtypeStruct((t, MOBA_WIDTH), BF16),
        jax.ShapeDtypeStruct((t // MOBA_BLOCK, MOBA_WIDTH, MOBA_BLOCK), BF16),
        jax.ShapeDtypeStruct((batch * nb, MOBA_WIDTH), F32),
        jax.ShapeDtypeStruct((t, GLA_KEY_WIDTH), F32),
        jax.ShapeDtypeStruct((t, GLA_KEY_WIDTH), F32),
        jax.ShapeDtypeStruct((t, GLA_VAL_WIDTH), F32),
        jax.ShapeDtypeStruct((t, GLA_VAL_WIDTH), F32),
        jax.ShapeDtypeStruct((t, GLA_KEY_WIDTH), F32),
    )
    out_specs = (
        pl.BlockSpec((tm, MOBA_WIDTH), row),
        pl.BlockSpec((tm, MOBA_WIDTH), row),
        pl.BlockSpec((tm // MOBA_BLOCK, MOBA_WIDTH, MOBA_BLOCK), lambda b, i: (b * steps + i, 0, 0)),
        pl.BlockSpec((nb, MOBA_WIDTH), lambda b, i: (b, 0)),
        pl.BlockSpec((tm, GLA_KEY_WIDTH), row),
        pl.BlockSpec((tm, GLA_KEY_WIDTH), row),
        pl.BlockSpec((tm, GLA_VAL_WIDTH), row),
        pl.BlockSpec((tm, GLA_VAL_WIDTH), row),
        pl.BlockSpec((tm, GLA_KEY_WIDTH), row),
    )
    return pl.pallas_call(
        _inproj_kernel, grid=(batch, steps), in_specs=in_specs, out_specs=out_specs, out_shape=out_shape,
        compiler_params=_params("arbitrary", "arbitrary"), name="inproj",
    )(*args)


def _moba_kernel(slopes_ref, q_ref, k_ref, vt_ref, km_ref, o_ref, kaug_ref, s_ref):
    hg = pl.program_id(1)
    c = pl.program_id(2)
    blk = MOBA_BLOCK
    nb = km_ref.shape[0]
    hw = 2 * HEAD_DIM
    heads = q_ref.shape[1] // HEAD_DIM
    feat_rows = nb + 8
    head_rows = lambda g: slice(g * HEAD_DIM, (g + 1) * HEAD_DIM)

    @pl.when(c == 0)
    def _():
        lane = lax.broadcasted_iota(jnp.int32, (blk, hw), 1)
        koff = lax.broadcasted_iota(jnp.int32, (blk, hw), 0).astype(F32)
        for g in range(heads):
            tile, hh = divmod(g, 2)
            slope = slopes_ref[hg * heads + g]
            base = HEAD_DIM * (1 - hh)
            for jb in range(nb):
                feat = jnp.where(lane == base + nb, slope * koff, 0.0)
                feat = jnp.where((lane == base + jb) | (lane == base + nb + 1), 1.0, feat)
                kb = k_ref[jb * blk:(jb + 1) * blk, tile * hw:(tile + 1) * hw]
                kaug_ref[g, jb * blk:(jb + 1) * blk, :] = jnp.where(lane // HEAD_DIM == hh, kb, feat.astype(BF16))

    qt = q_ref[...].astype(F32).T
    qt_b = qt.astype(BF16)
    kidx = lax.broadcasted_iota(jnp.int32, (blk, blk), 0)
    qidx = lax.broadcasted_iota(jnp.int32, (blk, blk), 1)
    causal = kidx <= qidx
    nidx = lax.broadcasted_iota(jnp.int32, (nb, blk), 0)
    r8 = lax.broadcasted_iota(jnp.int32, (8, blk), 0)
    qoff = lax.broadcasted_iota(jnp.int32, (8, blk), 1).astype(F32)
    km_lane = lax.broadcasted_iota(jnp.int32, km_ref.shape, 1)
    km = km_ref[...]
    own_rows = pl.ds(pl.multiple_of(c * blk, blk), blk)
    vt_own = vt_ref[c]
    ones = jnp.ones((16, blk), BF16)

    def values_and_ones(vt_blk, g):
        return jnp.concatenate([vt_blk[head_rows(g), :], ones], axis=0)

    hs = range(heads)
    slopes = [slopes_ref[hg * heads + g] for g in hs]
    gates = [jnp.where(nidx < c, _dot(jnp.where(km_lane // HEAD_DIM == g, km, 0.0).astype(BF16), qt_b), -jnp.inf)
             for g in hs]
    ranks = [jnp.zeros((nb, blk), jnp.int32) for _ in hs]
    for m in range(nb - 1):
        for g in hs:
            gm = gates[g][m:m + 1, :]
            beats = (gm > gates[g]) | ((gm == gates[g]) & (m < nidx))
            ranks[g] = ranks[g] + beats.astype(jnp.int32)
    fill = jnp.zeros((HEAD_DIM - feat_rows, blk), F32)

    def q_aug(g, bias):
        dist0 = -slopes[g] * ((c - nidx) * blk).astype(F32)
        tail = jnp.where(r8 == 0, 1.0, jnp.where(r8 == 1, -slopes[g] * qoff, 0.0))
        feats = [jnp.where(bias, dist0, MASKED), tail, fill]
        qh = qt[head_rows(g), :]
        return jnp.concatenate([qh] + feats if g % 2 == 0 else feats + [qh], axis=0).astype(BF16)

    q_past = [q_aug(g, (nidx < c) & (ranks[g] < MOBA_TOPK)) for g in hs]
    q_own = [q_aug(g, nidx == c) for g in hs]
    s_own = [jnp.where(causal, _dot(kaug_ref[g, own_rows, :], q_own[g]), MASKED) for g in hs]
    m_own = [jnp.max(s, axis=0, keepdims=True) for s in s_own]
    p_own = [jnp.exp((s_own[g] - m_own[g]).astype(BF16)) for g in hs]
    state = []
    for g in hs:
        state += [m_own[g], _dot(values_and_ones(vt_own, g), p_own[g])]

    def scores(pair, slot):
        rows = pl.ds(pl.multiple_of(pair * 2 * blk, 2 * blk), 2 * blk)
        tops = []
        for g in range(heads):
            s = _dot(kaug_ref[g, rows, :], q_past[g])
            s_ref[slot, g] = s
            tops.append(_col_reduce(s, jnp.maximum, jnp.max))
        return tops

    def absorb(pair, slot, tops, carry):
        vt0 = vt_ref[2 * pair]
        vt1 = vt_ref[2 * pair + 1]
        out = []
        for g in range(heads):
            m_i, acc = carry[2 * g:2 * g + 2]
            m_new = jnp.maximum(m_i, tops[g])
            p = jnp.exp((s_ref[slot, g] - m_new).astype(BF16))
            acc = (jnp.exp(m_i - m_new) * acc + _dot(values_and_ones(vt0, g), p[:blk])
                   + _dot(values_and_ones(vt1, g), p[blk:]))
            out += [m_new, acc]
        return out

    n_state = 2 * heads

    def step(i, carry, slot):
        tops = scores(i + 1, 1 - slot)
        return tuple(absorb(i, slot, carry[n_state:], carry[:n_state]) + tops)

    def body(i, carry):
        return lax.cond(i % 2 == 0, lambda: step(i, carry, 0), lambda: step(i, carry, 1))

    last = jnp.maximum((c + 1) // 2 - 1, 0)
    carry = lax.fori_loop(0, last, body, tuple(state + scores(0, 0)))
    state = lax.cond(last % 2 == 0, lambda: tuple(absorb(last, 0, carry[n_state:], carry[:n_state])),
                     lambda: tuple(absorb(last, 1, carry[n_state:], carry[:n_state])))
    accs = state[1::2]
    out_t = jnp.concatenate([a[:HEAD_DIM] / a[HEAD_DIM:HEAD_DIM + 1] for a in accs], axis=0)
    o_ref[...] = out_t.T.astype(o_ref.dtype)


def _alibi_slopes():
    slopes = np.exp2(-ALIBI_MAX * np.arange(1, MOBA_HEADS + 1, dtype=np.float32) / MOBA_HEADS).astype(np.float32)
    worst = slopes * np.float32(MOBA_BLOCK - 1)
    assert np.all(worst.astype(jnp.bfloat16).astype(np.float32) == worst), "ALiBi slopes not bf16-exact"
    return slopes


def _moba(qa, ka, vt, kmean, *, batch, seq):
    t = qa.shape[0]
    nb = seq // MOBA_BLOCK
    gw = MOBA_GROUP * HEAD_DIM
    assert nb % 8 == 0 and nb + 8 <= HEAD_DIM, "bias features must fit the spare contraction lanes"
    grid_spec = pltpu.PrefetchScalarGridSpec(
        num_scalar_prefetch=1,
        grid=(batch, MOBA_HEADS // MOBA_GROUP, nb),
        in_specs=[
            pl.BlockSpec((MOBA_BLOCK, gw), lambda b, h, c, s: (b * nb + c, h)),
            pl.BlockSpec((seq, gw), lambda b, h, c, s: (b, h)),
            pl.BlockSpec((nb, gw, MOBA_BLOCK), lambda b, h, c, s: (b, h, 0)),
            pl.BlockSpec((nb, gw), lambda b, h, c, s: (b, h)),
        ],
        out_specs=pl.BlockSpec((MOBA_BLOCK, gw), lambda b, h, c, s: (b * nb + c, h)),
        scratch_shapes=[pltpu.VMEM((MOBA_GROUP, seq, 2 * HEAD_DIM), BF16),
                        pltpu.VMEM((2, MOBA_GROUP, 2 * MOBA_BLOCK, MOBA_BLOCK), F32)],
    )
    return pl.pallas_call(
        _moba_kernel, grid_spec=grid_spec, out_shape=jax.ShapeDtypeStruct((t, MOBA_WIDTH), BF16),
        compiler_params=_params("arbitrary", "arbitrary", "arbitrary"), name="moba",
    )(jnp.asarray(_alibi_slopes()), qa, ka, vt, kmean)


def _gla_kernel(q_ref, k_ref, v_ref, r_ref, la_ref, gn_ref, o_ref, st_ref):
    @pl.when(pl.program_id(1) == 0)
    def _():
        st_ref[...] = jnp.zeros_like(st_ref)

    ch = GLA_CHUNK
    n_seqs, tc = q_ref.shape[:2]
    n_chunks = tc // ch
    row = lax.broadcasted_iota(jnp.int32, (tc, tc), 0)
    col = lax.broadcasted_iota(jnp.int32, (tc, tc), 1)
    causal = (row // ch == col // ch) & (row >= col)
    tri = jnp.where(causal, 1.0, 0.0).astype(BF16)
    lane = lax.broadcasted_iota(jnp.int32, (1, GLA_KEY_WIDTH), 1)
    gn = gn_ref[...]
    seqs = range(n_seqs)
    heads = range(GLA_HEADS)
    gs = []
    for s in seqs:
        la = la_ref[s]
        la_hi = la.astype(BF16)
        rest = la - la_hi.astype(F32)
        la_mid = rest.astype(BF16)
        la_lo = (rest - la_mid.astype(F32)).astype(BF16)
        gs.append(_dot(tri, la_hi) + _dot(tri, la_mid) + _dot(tri, la_lo))
    ends = [[g[(ci + 1) * ch - 1:(ci + 1) * ch, :] for ci in range(n_chunks)] for g in gs]
    qhs, kds, kths, vhs = [], [], [], []
    for s in seqs:
        g = gs[s]
        g_last = jnp.concatenate([jnp.broadcast_to(e, (ch, GLA_KEY_WIDTH)) for e in ends[s]], axis=0)
        qd = q_ref[s] * (GLA_DK ** -0.5) * jnp.exp(g)
        k = k_ref[s]
        kds.append((k * jnp.exp(-g)).astype(BF16))
        kt = k * jnp.exp(g_last - g)
        qhs.append([jnp.where(lane // GLA_DK == h, qd, 0.0).astype(BF16) for h in heads])
        kths.append([jnp.where(lane // GLA_DK == h, kt, 0.0).astype(BF16) for h in heads])
        vhs.append([v_ref[s, :, h * GLA_DV:(h + 1) * GLA_DV].astype(BF16) for h in heads])
    intra = [[None] * GLA_HEADS for _ in seqs]
    for h in heads:
        for s in seqs:
            a = jnp.where(causal, _dot(qhs[s][h], kds[s], _NT), 0.0)
            intra[s][h] = _dot(a.astype(BF16), vhs[s][h])
    states = [st_ref[s] for s in seqs]
    inter = [[[] for _ in heads] for _ in seqs]
    for ci in range(n_chunks):
        rows = slice(ci * ch, (ci + 1) * ch)
        for s in seqs:
            state_b = states[s].astype(BF16)
            new_state = states[s] * jnp.exp(ends[s][ci])
            for h in heads:
                inter[s][h].append(_dot(qhs[s][h][rows], state_b, _NT))
                new_state = new_state + _dot(vhs[s][h][rows], kths[s][h][rows], _TN)
            states[s] = new_state
    for s in seqs:
        st_ref[s] = states[s]
    for h in heads:
        for s in seqs:
            o = intra[s][h] + jnp.concatenate(inter[s][h], axis=0)
            rh = r_ref[s, :, h * GLA_DV:(h + 1) * GLA_DV]
            o_ref[s, :, h * GLA_DV:(h + 1) * GLA_DV] = (_rms(o, gn) * (rh * jax.nn.sigmoid(rh))).astype(o_ref.dtype)


def _gla(qg, kg, vg, rg, la, g_norm, *, batch, seq, tc=256):
    n_seqs = GLA_SEQS if batch % GLA_SEQS == 0 else 1
    by_seq = lambda a: a.reshape(batch, seq, a.shape[-1])
    blk = lambda i, j: (i, j, 0)
    kspec = pl.BlockSpec((n_seqs, tc, GLA_KEY_WIDTH), blk)
    vspec = pl.BlockSpec((n_seqs, tc, GLA_VAL_WIDTH), blk)
    out = pl.pallas_call(
        _gla_kernel, grid=(batch // n_seqs, seq // tc),
        in_specs=[kspec, kspec, vspec, vspec, kspec, pl.BlockSpec((1, GLA_DV), lambda i, j: (0, 0))],
        out_specs=vspec, out_shape=jax.ShapeDtypeStruct((batch, seq, GLA_VAL_WIDTH), BF16),
        scratch_shapes=[pltpu.VMEM((n_seqs, GLA_DV, GLA_KEY_WIDTH), F32)],
        compiler_params=_params("arbitrary", "arbitrary"), name="gla",
    )(by_seq(qg), by_seq(kg), by_seq(vg), by_seq(rg), by_seq(la), g_norm.reshape(1, GLA_DV))
    return out.reshape(batch * seq, GLA_VAL_WIDTH)


def _outproj_kernel(x_ref, om_ref, og_ref, wm_ref, wg_ref, gf_ref, wr_ref, br_ref, h_ref, f_ref, lg_ref):
    subs = [slice(r, r + ROW_SUB) for r in range(0, x_ref.shape[0], ROW_SUB)]
    hs = [x_ref[s, :] + _dot(om_ref[s, :], wm_ref[...]) + _dot(og_ref[s, :], wg_ref[...]) for s in subs]
    for s, h in zip(subs, hs):
        h_ref[s, :] = h
    fs = [_rms(h, gf_ref[...]).astype(BF16) for h in hs]
    for s, f in zip(subs, fs):
        f_ref[s, :] = f
        lg_ref[:, s] = _dot(wr_ref[...], f, _NT) + br_ref[...]


def _outproj(x2d, o_moba, o_gla, w_out, g_ffn, w_router, b_router, *, tm=512):
    t, d = x2d.shape
    wm = w_out[:MOBA_WIDTH].astype(BF16)
    wg = w_out[MOBA_WIDTH:].astype(BF16)
    wr = w_router.T.astype(BF16)
    br = b_router.reshape(N_EXPERTS, 1)
    row = lambda i: (i, 0)
    const = lambda i: (0, 0)
    full = lambda a: pl.BlockSpec(a.shape, const)
    args = (x2d, o_moba, o_gla, wm, wg, g_ffn.reshape(1, d), wr, br)
    in_specs = [pl.BlockSpec((tm, d), row), pl.BlockSpec((tm, MOBA_WIDTH), row),
                pl.BlockSpec((tm, GLA_VAL_WIDTH), row)] + [full(a) for a in args[3:]]
    return pl.pallas_call(
        _outproj_kernel, grid=(t // tm,), in_specs=in_specs,
        out_specs=(pl.BlockSpec((tm, d), row), pl.BlockSpec((tm, d), row),
                   pl.BlockSpec((N_EXPERTS, tm), lambda i: (0, i))),
        out_shape=(jax.ShapeDtypeStruct((t, d), F32), jax.ShapeDtypeStruct((t, d), BF16),
                   jax.ShapeDtypeStruct((N_EXPERTS, t), F32)),
        compiler_params=_params("arbitrary"), name="outproj",
    )(*args)


PAIR_CHUNK = 2 * LANES


def _moe_kernel(be_ref, nused_ref, x_ref, w1_ref, b1_ref, w2_ref, b2_ref, o_ref, w1s_ref, w2s_ref):
    i = pl.program_id(0)
    f = w2_ref.shape[1]

    @pl.when((i == 0) | (be_ref[i] != be_ref[jnp.maximum(i - 1, 0)]))
    def _():
        r = lax.broadcasted_iota(jnp.int32, (PAIR_CHUNK, PAIR_CHUNK), 0)
        c = lax.broadcasted_iota(jnp.int32, (PAIR_CHUNK, PAIR_CHUNK), 1)
        src_col = jnp.where(c < LANES, 2 * c, 2 * (c - LANES) + 1)
        perm = jnp.where(r == src_col, 1.0, 0.0).astype(BF16)
        for cc in range(2 * f // PAIR_CHUNK):
            chunk = w1_ref[0, :, cc * PAIR_CHUNK:(cc + 1) * PAIR_CHUNK].astype(BF16)
            res = _dot(chunk, perm).astype(BF16)
            w1s_ref[:, cc * LANES:(cc + 1) * LANES] = res[:, :LANES]
            w1s_ref[:, f + cc * LANES:f + (cc + 1) * LANES] = res[:, LANES:]
        w2s_ref[...] = w2_ref[0].astype(BF16)

    @pl.when(i < nused_ref[0])
    def _():
        hid = _dot(x_ref[...], w1s_ref[...]) + b1_ref[0]
        glu = jnp.minimum(hid[:, :f], SWIGLU_LIMIT)
        lin = jnp.clip(hid[:, f:], -SWIGLU_LIMIT, SWIGLU_LIMIT)
        act = glu * jax.nn.sigmoid(SWIGLU_ALPHA * glu) * (lin + 1.0)
        o_ref[...] = (_dot(act.astype(BF16), w2s_ref[...]) + b2_ref[0]).astype(o_ref.dtype)

    @pl.when(i >= nused_ref[0])
    def _():
        o_ref[...] = jnp.zeros_like(o_ref)


def _moe(x_rows, block_expert, n_used, w1, b1, w2, b2, *, tm):
    n_rows, d = x_rows.shape
    e, _, f2 = w1.shape
    f = f2 // 2
    b1p = jnp.concatenate([b1[:, 0::2], b1[:, 1::2]], axis=-1).reshape(e, 1, f2)
    b2r = b2.reshape(e, 1, d)
    grid_spec = pltpu.PrefetchScalarGridSpec(
        num_scalar_prefetch=2,
        grid=(n_rows // tm,),
        in_specs=[
            pl.BlockSpec((tm, d), lambda i, be, nu: (i, 0)),
            pl.BlockSpec((1, d, f2), lambda i, be, nu: (be[i], 0, 0)),
            pl.BlockSpec((1, 1, f2), lambda i, be, nu: (be[i], 0, 0)),
            pl.BlockSpec((1, f, d), lambda i, be, nu: (be[i], 0, 0)),
            pl.BlockSpec((1, 1, d), lambda i, be, nu: (be[i], 0, 0)),
        ],
        out_specs=pl.BlockSpec((tm, d), lambda i, be, nu: (i, 0)),
        scratch_shapes=[pltpu.VMEM((d, f2), BF16), pltpu.VMEM((f, d), BF16)],
    )
    return pl.pallas_call(
        _moe_kernel, grid_spec=grid_spec, out_shape=jax.ShapeDtypeStruct((n_rows, d), BF16),
        compiler_params=_params("arbitrary"), name="moe",
    )(block_expert, n_used, x_rows, w1, b1p, w2, b2r)


ROUTE_CHUNK = 2 * LANES


def _router_kernel(lg_ref, idx_ref, gate_ref, rank_ref, cnt_ref, carry_ref):
    @pl.when(pl.program_id(0) == 0)
    def _():
        carry_ref[...] = jnp.zeros_like(carry_ref)

    x = lg_ref[...]
    ne, tt = x.shape
    eidx = lax.broadcasted_iota(jnp.int32, (ne, tt), 0)
    member = jnp.zeros((ne, tt), F32)
    vals, ids = [], []
    for _ in range(TOP_K):
        m = jnp.max(x, axis=0, keepdims=True)
        sel = jnp.min(jnp.where(x == m, eidx, ne), axis=0, keepdims=True)
        hit = eidx == sel
        member = jnp.where(hit, 1.0, member)
        x = jnp.where(hit, -jnp.inf, x)
        vals.append(m)
        ids.append(sel)
    e = jnp.exp(jnp.concatenate(vals, axis=0) - vals[0])
    gate_ref[...] = e / jnp.sum(e, axis=0, keepdims=True)
    idx_ref[...] = jnp.concatenate(ids, axis=0)
    r = lax.broadcasted_iota(jnp.int32, (ROUTE_CHUNK, ROUTE_CHUNK), 0)
    c = lax.broadcasted_iota(jnp.int32, (ROUTE_CHUNK, ROUTE_CHUNK), 1)
    earlier = jnp.where(r < c, 1.0, 0.0).astype(BF16)
    carry = carry_ref[:, :1]
    before = []
    for ch in range(tt // ROUTE_CHUNK):
        mc = member[:, ch * ROUTE_CHUNK:(ch + 1) * ROUTE_CHUNK]
        before.append(_dot(mc.astype(BF16), earlier) + carry)
        carry = carry + jnp.sum(mc, axis=1, keepdims=True)
    before = jnp.concatenate(before, axis=1)
    ranks = [jnp.sum(jnp.where(eidx == ids[k], before, 0.0), axis=0, keepdims=True) for k in range(TOP_K)]
    rank_ref[...] = jnp.concatenate(ranks, axis=0).astype(jnp.int32)
    carry_ref[...] = jnp.broadcast_to(carry, carry_ref.shape)
    cnt_ref[...] = jnp.broadcast_to(carry, cnt_ref.shape).astype(jnp.int32)


def _router(logits_t, *, tt=2048):
    ne, t = logits_t.shape
    tok = lambda i: (0, i)
    kspec = pl.BlockSpec((TOP_K, tt), tok)
    return pl.pallas_call(
        _router_kernel, grid=(t // tt,), in_specs=[pl.BlockSpec((ne, tt), tok)],
        out_specs=(kspec, kspec, kspec, pl.BlockSpec((ne, LANES), lambda i: (0, 0))),
        out_shape=(jax.ShapeDtypeStruct((TOP_K, t), jnp.int32), jax.ShapeDtypeStruct((TOP_K, t), F32),
                   jax.ShapeDtypeStruct((TOP_K, t), jnp.int32), jax.ShapeDtypeStruct((ne, LANES), jnp.int32)),
        scratch_shapes=[pltpu.VMEM((ne, LANES), F32)],
        compiler_params=_params("arbitrary"), name="router",
    )(logits_t)


def _route(logits_t, *, tm):
    ne, t = logits_t.shape
    idx, gates, rank, cnt = _router(logits_t)
    counts = cnt[:, 0]
    padded = (counts + tm - 1) // tm * tm
    pad_end = jnp.cumsum(padded)
    pad_start = pad_end - padded
    experts = jnp.arange(ne, dtype=jnp.int32).reshape(ne, 1, 1)
    slot = jnp.sum(jnp.where(idx[None] == experts, pad_start.reshape(ne, 1, 1), 0), axis=0) + rank
    n_blocks = -(-TOP_K * t // tm) + ne
    flat_slot = slot.reshape(-1)
    tok = jnp.broadcast_to(jnp.arange(t, dtype=jnp.int32), (TOP_K, t)).reshape(-1)
    row_tok = (jnp.arange(n_blocks * tm, dtype=jnp.int32) % t).at[flat_slot].add(
        tok - flat_slot % t, unique_indices=True, mode="promise_in_bounds")
    n_used = pad_end[-1] // tm
    blk_start = jnp.minimum(jnp.arange(n_blocks, dtype=jnp.int32), n_used - 1) * tm
    block_expert = jnp.minimum(jnp.sum(pad_end[:, None] <= blk_start[None, :], axis=0), ne - 1)
    return gates, row_tok, slot, block_expert.astype(jnp.int32), n_used.reshape(1).astype(jnp.int32)


def _final_kernel(h_ref, y_ref, gate_ref, p_ref, wp_ref, gp_ref, wgate_ref, gfin_ref, o_ref, *, last_layer):
    g = gate_ref[...]
    gates = jnp.concatenate([g, jnp.zeros((8 - TOP_K, g.shape[1]), F32)], axis=0).T
    subs = [slice(r, r + ROW_SUB) for r in range(0, h_ref.shape[0], ROW_SUB)]
    ples = [_rms(_dot(p_ref[s, :].astype(BF16), wp_ref[...]), gp_ref[...]) for s in subs]
    hs = []
    for s in subs:
        h = h_ref[s, :]
        for k in range(TOP_K):
            h = h + gates[s, k:k + 1] * y_ref[k, s, :].astype(F32)
        hs.append(h)
    hs = [h + jax.nn.sigmoid(_dot(h.astype(BF16), wgate_ref[...])) * ple for h, ple in zip(hs, ples)]
    for s, h in zip(subs, hs):
        o_ref[s, :] = _rms(h, gfin_ref[...]) if last_layer else h


def _final(h1, y_tok, gates, p2d, w_ple_proj, g_ple, w_ple_gate, g_final, *, last_layer, tm=512):
    t, d = h1.shape
    row = lambda i: (i, 0)
    const = lambda i: (0, 0)
    full = lambda a: pl.BlockSpec(a.shape, const)
    args = (h1, y_tok, gates, p2d, w_ple_proj.astype(BF16), g_ple.reshape(1, d), w_ple_gate.astype(BF16),
            g_final.reshape(1, d))
    in_specs = [pl.BlockSpec((tm, d), row), pl.BlockSpec((TOP_K, tm, d), lambda i: (0, i, 0)),
                pl.BlockSpec((TOP_K, tm), lambda i: (0, i)), pl.BlockSpec((tm, p2d.shape[1]), row)]
    in_specs += [full(a) for a in args[4:]]
    return pl.pallas_call(
        functools.partial(_final_kernel, last_layer=last_layer), grid=(t // tm,), in_specs=in_specs,
        out_specs=pl.BlockSpec((tm, d), row),
        out_shape=jax.ShapeDtypeStruct((t, d), F32),
        compiler_params=_params("arbitrary"), name="final",
    )(*args)


def kernel(x, p, g_mix, w_in, w_gla_gate, b_gla_gate, g_gla_norm, w_out, g_ffn, w_router, b_router,
           w1, b1, w2, b2, w_ple_gate, w_ple_proj, g_ple, g_final, *, moe_tm=512):
    batch, seq, d = x.shape
    depth = p.shape[0]
    t = batch * seq
    h = x.reshape(t, d)
    for i in range(depth):
        qa, ka, vt, kmean, qg, kg, vg, rg, la = _inproj(h, g_mix[i], w_in[i], w_gla_gate[i], b_gla_gate[i],
                                                        batch=batch, seq=seq)
        o_moba = _moba(qa, ka, vt, kmean, batch=batch, seq=seq)
        o_gla = _gla(qg, kg, vg, rg, la, g_gla_norm[i], batch=batch, seq=seq)
        h1, f, logits = _outproj(h, o_moba, o_gla, w_out[i], g_ffn[i], w_router[i], b_router[i])
        gates, row_tok, slot, block_expert, n_used = _route(logits, tm=moe_tm)
        x_rows = jnp.take(f, row_tok, axis=0, mode="clip")
        y = _moe(x_rows, block_expert, n_used, w1[i], b1[i], w2[i], b2[i], tm=moe_tm)
        y_tok = jnp.take(y, slot, axis=0, mode="clip")
        h = _final(h1, y_tok, gates, p[i].reshape(t, -1), w_ple_proj[i], g_ple[i], w_ple_gate[i],
                   g_final, last_layer=i == depth - 1)
    return h.reshape(batch, seq, d)
```

```python
import functools

import jax
import jax.numpy as jnp
import numpy as np
from jax import lax
from jax.experimental import pallas as pl
from jax.experimental.pallas import tpu as pltpu

HEAD_DIM = 64
MOBA_HEADS = 8
MOBA_WIDTH = MOBA_HEADS * HEAD_DIM
MOBA_BLOCK = 256
MOBA_TOPK = 3
MOBA_GROUP = 4
ALIBI_MAX = 8.0
GLA_HEADS = 4
GLA_DK = 64
GLA_DV = 128
GLA_KEY_WIDTH = GLA_HEADS * GLA_DK
GLA_VAL_WIDTH = GLA_HEADS * GLA_DV
GLA_GATE_RANK = 16
GLA_GATE_TAU = 16.0
GLA_CHUNK = 64
GLA_SEQS = 4
N_EXPERTS = 32
TOP_K = 4
SWIGLU_ALPHA = 1.702
SWIGLU_LIMIT = 7.0
EPS = 1e-6

LANES = 128
MASKED = -1e30
VMEM_LIMIT = 56 * 1024 * 1024

BF16 = jnp.bfloat16
F32 = jnp.float32

_NT = (((1,), (1,)), ((), ()))
_TN = (((0,), (0,)), ((), ()))


def _dot(a, b, dims=None, precision=None):
    if dims is None:
        return jnp.dot(a, b, preferred_element_type=F32, precision=precision)
    return lax.dot_general(a, b, dims, preferred_element_type=F32, precision=precision)


def _rms(x, g):
    return x * lax.rsqrt(jnp.mean(x * x, axis=-1, keepdims=True) + EPS) * g


def _col_reduce(x, pair_op, final_op, rows=64):
    parts = [x[i:i + rows] for i in range(0, x.shape[0], rows)]
    while len(parts) > 1:
        parts = [pair_op(parts[i], parts[i + 1]) for i in range(0, len(parts), 2)]
    return final_op(parts[0], axis=0, keepdims=True)


def _params(*sem):
    return pltpu.CompilerParams(dimension_semantics=sem, vmem_limit_bytes=VMEM_LIMIT)


def _inproj_kernel(x_ref, g_ref, wq_ref, wk_ref, wvt_ref, wgla_ref, wag_ref, wgate_ref, bgate_ref,
                   qa_ref, ka_ref, vt_ref, km_ref, qg_ref, kg_ref, vg_ref, rg_ref, la_ref):
    i = pl.program_id(1)
    tm = x_ref.shape[0]
    nblk = tm // MOBA_BLOCK
    @pl.when(i == 0)
    def _():
        km_ref[...] = jnp.zeros_like(km_ref)

    ab = _rms(x_ref[...], g_ref[...]).astype(BF16)
    ag = _dot(ab, wag_ref[...]).astype(BF16)
    z = _dot(ag, wgate_ref[...]) + bgate_ref[...]
    la_ref[...] = jax.nn.log_sigmoid(z) / GLA_GATE_TAU
    qa_ref[...] = (_dot(ab, wq_ref[...]) * (HEAD_DIM ** -0.5)).astype(BF16)
    ka = _dot(ab, wk_ref[...])
    ka_ref[...] = ka.astype(BF16)
    km = km_ref[...]
    blk_row = lax.broadcasted_iota(jnp.int32, km.shape, 0)
    for j in range(nblk):
        mean_j = jnp.mean(ka[j * MOBA_BLOCK:(j + 1) * MOBA_BLOCK], axis=0, keepdims=True)
        km = jnp.where(blk_row == i * nblk + j, mean_j, km)
    km_ref[...] = km
    vt = _dot(wvt_ref[...], ab, _NT).astype(BF16)
    for j in range(nblk):
        vt_ref[j] = vt[:, j * MOBA_BLOCK:(j + 1) * MOBA_BLOCK]
    gla = _dot(ab, wgla_ref[...])
    o = 0
    for ref in (qg_ref, kg_ref, vg_ref, rg_ref):
        w = ref.shape[1]
        ref[...] = gla[:, o:o + w].astype(ref.dtype)
        o += w


def _inproj(x2d, g_mix, w_in, w_gla_gate, b_gla_gate, *, batch, seq, tm=512):
    t, d = x2d.shape
    nb = seq // MOBA_BLOCK
    o_k, o_v, o_qg = MOBA_WIDTH, 2 * MOBA_WIDTH, 3 * MOBA_WIDTH
    o_ag = o_qg + 2 * GLA_KEY_WIDTH + 2 * GLA_VAL_WIDTH
    wq = w_in[:, :o_k].astype(BF16)
    wk = w_in[:, o_k:o_v].astype(BF16)
    wvt = w_in[:, o_v:o_qg].T.astype(BF16)
    wgla = w_in[:, o_qg:o_ag].astype(BF16)
    wag = jnp.pad(w_in[:, o_ag:], ((0, 0), (0, LANES - GLA_GATE_RANK))).astype(BF16)
    wgate = jnp.pad(w_gla_gate, ((0, LANES - GLA_GATE_RANK), (0, 0))).astype(BF16)
    steps = seq // tm
    row = lambda b, i: (b * steps + i, 0)
    const = lambda b, i: (0, 0)
    full = lambda a: pl.BlockSpec(a.shape, const)
    args = (x2d, g_mix.reshape(1, d), wq, wk, wvt, wgla, wag, wgate, b_gla_gate.reshape(1, -1))
    in_specs = [pl.BlockSpec((tm, d), row)] + [full(a) for a in args[1:]]
    out_shape = (
        jax.ShapeDtypeStruct((t, MOBA_WIDTH), BF16),
        jax.ShapeDtypeStruct((t, MOBA_WIDTH), BF16),
        jax.ShapeDtypeStruct((t // MOBA_BLOCK, MOBA_WIDTH, MOBA_BLOCK), BF16),
        jax.ShapeDtypeStruct((batch * nb, MOBA_WIDTH), F32),
        jax.ShapeDtypeStruct((t, GLA_KEY_WIDTH), F32),
        jax.ShapeDtypeStruct((t, GLA_KEY_WIDTH), F32),
        jax.ShapeDtypeStruct((t, GLA_VAL_WIDTH), BF16),
        jax.ShapeDtypeStruct((t, GLA_VAL_WIDTH), F32),
        jax.ShapeDtypeStruct((t, GLA_KEY_WIDTH), F32),
    )
    out_specs = (
        pl.BlockSpec((tm, MOBA_WIDTH), row),
        pl.BlockSpec((tm, MOBA_WIDTH), row),
        pl.BlockSpec((tm // MOBA_BLOCK, MOBA_WIDTH, MOBA_BLOCK), lambda b, i: (b * steps + i, 0, 0)),
        pl.BlockSpec((nb, MOBA_WIDTH), lambda b, i: (b, 0)),
        pl.BlockSpec((tm, GLA_KEY_WIDTH), row),
        pl.BlockSpec((tm, GLA_KEY_WIDTH), row),
        pl.BlockSpec((tm, GLA_VAL_WIDTH), row),
        pl.BlockSpec((tm, GLA_VAL_WIDTH), row),
        pl.BlockSpec((tm, GLA_KEY_WIDTH), row),
    )
    return pl.pallas_call(
        _inproj_kernel, grid=(batch, steps), in_specs=in_specs, out_specs=out_specs, out_shape=out_shape,
        compiler_params=_params("arbitrary", "arbitrary"), name="inproj",
    )(*args)


def _moba_kernel(slopes_ref, q_ref, k_ref, vt_ref, km_ref, o_ref, kaug_ref, s_ref):
    hg = pl.program_id(1)
    c = pl.program_id(2)
    blk = MOBA_BLOCK
    nb = km_ref.shape[0]
    hw = 2 * HEAD_DIM
    heads = q_ref.shape[1] // HEAD_DIM
    feat_rows = nb + 8
    head_rows = lambda g: slice(g * HEAD_DIM, (g + 1) * HEAD_DIM)

    @pl.when(c == 0)
    def _():
        lane = lax.broadcasted_iota(jnp.int32, (blk, hw), 1)
        koff = lax.broadcasted_iota(jnp.int32, (blk, hw), 0).astype(F32)
        for g in range(heads):
            tile, hh = divmod(g, 2)
            slope = slopes_ref[hg * heads + g]
            base = HEAD_DIM * (1 - hh)
            for jb in range(nb):
                feat = jnp.where(lane == base + nb, slope * koff, 0.0)
                feat = jnp.where((lane == base + jb) | (lane == base + nb + 1), 1.0, feat)
                kb = k_ref[jb * blk:(jb + 1) * blk, tile * hw:(tile + 1) * hw]
                kaug_ref[g, jb * blk:(jb + 1) * blk, :] = jnp.where(lane // HEAD_DIM == hh, kb, feat.astype(BF16))

    qt = q_ref[...].astype(F32).T
    qt_b = qt.astype(BF16)
    kidx = lax.broadcasted_iota(jnp.int32, (blk, blk), 0)
    qidx = lax.broadcasted_iota(jnp.int32, (blk, blk), 1)
    causal = kidx <= qidx
    nidx = lax.broadcasted_iota(jnp.int32, (nb, blk), 0)
    r8 = lax.broadcasted_iota(jnp.int32, (8, blk), 0)
    qoff = lax.broadcasted_iota(jnp.int32, (8, blk), 1).astype(F32)
    km_lane = lax.broadcasted_iota(jnp.int32, km_ref.shape, 1)
    km = km_ref[...]
    own_rows = pl.ds(pl.multiple_of(c * blk, blk), blk)
    vt_own = vt_ref[c]
    ones = jnp.ones((16, blk), BF16)

    def values_and_ones(vt_blk, g):
        return jnp.concatenate([vt_blk[head_rows(g), :], ones], axis=0)

    hs = range(heads)
    slopes = [slopes_ref[hg * heads + g] for g in hs]
    gates = [jnp.where(nidx < c, _dot(jnp.where(km_lane // HEAD_DIM == g, km, 0.0).astype(BF16), qt_b), -jnp.inf)
             for g in hs]
    ranks = [jnp.zeros((nb, blk), jnp.int32) for _ in hs]
    for m in range(nb - 1):
        for g in hs:
            gm = gates[g][m:m + 1, :]
            beats = (gm > gates[g]) | ((gm == gates[g]) & (m < nidx))
            ranks[g] = ranks[g] + beats.astype(jnp.int32)
    fill = jnp.zeros((HEAD_DIM - feat_rows, blk), F32)

    def q_aug(g, bias):
        dist0 = -slopes[g] * ((c - nidx) * blk).astype(F32)
        tail = jnp.where(r8 == 0, 1.0, jnp.where(r8 == 1, -slopes[g] * qoff, 0.0))
        feats = [jnp.where(bias, dist0, MASKED), tail, fill]
        qh = qt[head_rows(g), :]
        return jnp.concatenate([qh] + feats if g % 2 == 0 else feats + [qh], axis=0).astype(BF16)

    q_past = [q_aug(g, (nidx < c) & (ranks[g] < MOBA_TOPK)) for g in hs]
    q_own = [q_aug(g, nidx == c) for g in hs]
    s_own = [jnp.where(causal, _dot(kaug_ref[g, own_rows, :], q_own[g]), MASKED) for g in hs]
    m_own = [jnp.max(s, axis=0, keepdims=True) for s in s_own]
    p_own = [jnp.exp((s_own[g] - m_own[g]).astype(BF16)) for g in hs]
    state = []
    for g in hs:
        state += [m_own[g], _dot(values_and_ones(vt_own, g), p_own[g])]

    def scores(pair, slot):
        rows = pl.ds(pl.multiple_of(pair * 2 * blk, 2 * blk), 2 * blk)
        tops = []
        for g in range(heads):
            s = _dot(kaug_ref[g, rows, :], q_past[g])
            s_ref[slot, g] = s
            tops.append(_col_reduce(s, jnp.maximum, jnp.max))
        return tops

    def absorb(pair, slot, tops, carry):
        vt0 = vt_ref[2 * pair]
        vt1 = vt_ref[2 * pair + 1]
        out = []
        for g in range(heads):
            m_i, acc = carry[2 * g:2 * g + 2]
            m_new = jnp.maximum(m_i, tops[g])
            p = jnp.exp((s_ref[slot, g] - m_new).astype(BF16))
            acc = (jnp.exp(m_i - m_new) * acc + _dot(values_and_ones(vt0, g), p[:blk])
                   + _dot(values_and_ones(vt1, g), p[blk:]))
            out += [m_new, acc]
        return out

    n_state = 2 * heads

    def step(i, carry, slot):
        tops = scores(i + 1, 1 - slot)
        return tuple(absorb(i, slot, carry[n_state:], carry[:n_state]) + tops)

    def body(i, carry):
        return lax.cond(i % 2 == 0, lambda: step(i, carry, 0), lambda: step(i, carry, 1))

    last = jnp.maximum((c + 1) // 2 - 1, 0)
    carry = lax.fori_loop(0, last, body, tuple(state + scores(0, 0)))
    state = lax.cond(last % 2 == 0, lambda: tuple(absorb(last, 0, carry[n_state:], carry[:n_state])),
                     lambda: tuple(absorb(last, 1, carry[n_state:], carry[:n_state])))
    accs = state[1::2]
    out_t = jnp.concatenate([a[:HEAD_DIM] / a[HEAD_DIM:HEAD_DIM + 1] for a in accs], axis=0)
    o_ref[...] = out_t.T.astype(o_ref.dtype)


def _alibi_slopes():
    slopes = np.exp2(-ALIBI_MAX * np.arange(1, MOBA_HEADS + 1, dtype=np.float32) / MOBA_HEADS).astype(np.float32)
    worst = slopes * np.float32(MOBA_BLOCK - 1)
    assert np.all(worst.astype(jnp.bfloat16).astype(np.float32) == worst), "ALiBi slopes not bf16-exact"
    return slopes


def _moba(qa, ka, vt, kmean, *, batch, seq):
    t = qa.shape[0]
    nb = seq // MOBA_BLOCK
    gw = MOBA_GROUP * HEAD_DIM
    assert nb % 8 == 0 and nb + 8 <= HEAD_DIM, "bias features must fit the spare contraction lanes"
    grid_spec = pltpu.PrefetchScalarGridSpec(
        num_scalar_prefetch=1,
        grid=(batch, MOBA_HEADS // MOBA_GROUP, nb),
        in_specs=[
            pl.BlockSpec((MOBA_BLOCK, gw), lambda b, h, c, s: (b * nb + c, h)),
            pl.BlockSpec((seq, gw), lambda b, h, c, s: (b, h)),
            pl.BlockSpec((nb, gw, MOBA_BLOCK), lambda b, h, c, s: (b, h, 0)),
            pl.BlockSpec((nb, gw), lambda b, h, c, s: (b, h)),
        ],
        out_specs=pl.BlockSpec((MOBA_BLOCK, gw), lambda b, h, c, s: (b * nb + c, h)),
        scratch_shapes=[pltpu.VMEM((MOBA_GROUP, seq, 2 * HEAD_DIM), BF16),
                        pltpu.VMEM((2, MOBA_GROUP, 2 * MOBA_BLOCK, MOBA_BLOCK), F32)],
    )
    return pl.pallas_call(
        _moba_kernel, grid_spec=grid_spec, out_shape=jax.ShapeDtypeStruct((t, MOBA_WIDTH), BF16),
        compiler_params=_params("arbitrary", "arbitrary", "arbitrary"), name="moba",
    )(jnp.asarray(_alibi_slopes()), qa, ka, vt, kmean)


def _gla_kernel(q_ref, k_ref, v_ref, r_ref, la_ref, gn_ref, o_ref, st_ref):
    @pl.when(pl.program_id(1) == 0)
    def _():
        st_ref[...] = jnp.zeros_like(st_ref)

    ch = GLA_CHUNK
    n_seqs, tc = q_ref.shape[:2]
    n_chunks = tc // ch
    row = lax.broadcasted_iota(jnp.int32, (tc, tc), 0)
    col = lax.broadcasted_iota(jnp.int32, (tc, tc), 1)
    causal = (row // ch == col // ch) & (row >= col)
    tri = jnp.where(causal, 1.0, 0.0).astype(BF16)
    lane = lax.broadcasted_iota(jnp.int32, (1, GLA_KEY_WIDTH), 1)
    gn = gn_ref[...]
    seqs = range(n_seqs)
    heads = range(GLA_HEADS)
    gs = []
    for s in seqs:
        la = la_ref[s]
        la_hi = la.astype(BF16)
        rest = la - la_hi.astype(F32)
        la_mid = rest.astype(BF16)
        la_lo = (rest - la_mid.astype(F32)).astype(BF16)
        gs.append(_dot(tri, la_hi) + _dot(tri, la_mid) + _dot(tri, la_lo))
    ends = [[g[(ci + 1) * ch - 1:(ci + 1) * ch, :] for ci in range(n_chunks)] for g in gs]
    qhs, kds, kths, vhs = [], [], [], []
    for s in seqs:
        g = gs[s]
        g_last = jnp.concatenate([jnp.broadcast_to(e, (ch, GLA_KEY_WIDTH)) for e in ends[s]], axis=0)
        qd = q_ref[s] * (GLA_DK ** -0.5) * jnp.exp(g)
        k = k_ref[s]
        kds.append((k * jnp.exp(-g)).astype(BF16))
        kt = k * jnp.exp(g_last - g)
        qhs.append([jnp.where(lane // GLA_DK == h, qd, 0.0).astype(BF16) for h in heads])
        kths.append([jnp.where(lane // GLA_DK == h, kt, 0.0).astype(BF16) for h in heads])
        vhs.append([v_ref[s, :, h * GLA_DV:(h + 1) * GLA_DV] for h in heads])
    intra = [[None] * GLA_HEADS for _ in seqs]
    for h in heads:
        for s in seqs:
            a = jnp.where(causal, _dot(qhs[s][h], kds[s], _NT), 0.0)
            intra[s][h] = _dot(a.astype(BF16), vhs[s][h])
    states = [st_ref[s] for s in seqs]
    inter = [[[] for _ in heads] for _ in seqs]
    for ci in range(n_chunks):
        rows = slice(ci * ch, (ci + 1) * ch)
        for s in seqs:
            state_b = states[s].astype(BF16)
            new_state = states[s] * jnp.exp(ends[s][ci])
            for h in heads:
                inter[s][h].append(_dot(qhs[s][h][rows], state_b, _NT))
                new_state = new_state + _dot(vhs[s][h][rows], kths[s][h][rows], _TN)
            states[s] = new_state
    for s in seqs:
        st_ref[s] = states[s]
    for h in heads:
        for s in seqs:
            o = intra[s][h] + jnp.concatenate(inter[s][h], axis=0)
            rh = r_ref[s, :, h * GLA_DV:(h + 1) * GLA_DV]
            o_ref[s, :, h * GLA_DV:(h + 1) * GLA_DV] = (_rms(o, gn) * (rh * jax.nn.sigmoid(rh))).astype(o_ref.dtype)


def _gla(qg, kg, vg, rg, la, g_norm, *, batch, seq, tc=256):
    n_seqs = GLA_SEQS if batch % GLA_SEQS == 0 else 1
    by_seq = lambda a: a.reshape(batch, seq, a.shape[-1])
    blk = lambda i, j: (i, j, 0)
    kspec = pl.BlockSpec((n_seqs, tc, GLA_KEY_WIDTH), blk)
    vspec = pl.BlockSpec((n_seqs, tc, GLA_VAL_WIDTH), blk)
    out = pl.pallas_call(
        _gla_kernel, grid=(batch // n_seqs, seq // tc),
        in_specs=[kspec, kspec, vspec, vspec, kspec, pl.BlockSpec((1, GLA_DV), lambda i, j: (0, 0))],
        out_specs=vspec, out_shape=jax.ShapeDtypeStruct((batch, seq, GLA_VAL_WIDTH), BF16),
        scratch_shapes=[pltpu.VMEM((n_seqs, GLA_DV, GLA_KEY_WIDTH), F32)],
        compiler_params=_params("arbitrary", "arbitrary"), name="gla",
    )(by_seq(qg), by_seq(kg), by_seq(vg), by_seq(rg), by_seq(la), g_norm.reshape(1, GLA_DV))
    return out.reshape(batch * seq, GLA_VAL_WIDTH)


def _outproj_kernel(x_ref, om_ref, og_ref, wm_ref, wg_ref, gf_ref, wr_ref, br_ref, h_ref, f_ref, lg_ref):
    h = x_ref[...] + _dot(om_ref[...], wm_ref[...]) + _dot(og_ref[...], wg_ref[...])
    h_ref[...] = h
    f = _rms(h, gf_ref[...]).astype(BF16)
    f_ref[...] = f
    lg_ref[...] = _dot(wr_ref[...], f, _NT) + br_ref[...]


def _outproj(x2d, o_moba, o_gla, w_out, g_ffn, w_router, b_router, *, tm=1024):
    t, d = x2d.shape
    wm = w_out[:MOBA_WIDTH].astype(BF16)
    wg = w_out[MOBA_WIDTH:].astype(BF16)
    wr = w_router.T.astype(BF16)
    br = b_router.reshape(N_EXPERTS, 1)
    row = lambda i: (i, 0)
    const = lambda i: (0, 0)
    full = lambda a: pl.BlockSpec(a.shape, const)
    args = (x2d, o_moba, o_gla, wm, wg, g_ffn.reshape(1, d), wr, br)
    in_specs = [pl.BlockSpec((tm, d), row), pl.BlockSpec((tm, MOBA_WIDTH), row),
                pl.BlockSpec((tm, GLA_VAL_WIDTH), row)] + [full(a) for a in args[3:]]
    return pl.pallas_call(
        _outproj_kernel, grid=(t // tm,), in_specs=in_specs,
        out_specs=(pl.BlockSpec((tm, d), row), pl.BlockSpec((tm, d), row),
                   pl.BlockSpec((N_EXPERTS, tm), lambda i: (0, i))),
        out_shape=(jax.ShapeDtypeStruct((t, d), F32), jax.ShapeDtypeStruct((t, d), BF16),
                   jax.ShapeDtypeStruct((N_EXPERTS, t), F32)),
        compiler_params=_params("arbitrary"), name="outproj",
    )(*args)


PAIR_CHUNK = 2 * LANES


def _moe_kernel(be_ref, nused_ref, x_ref, w1_ref, b1_ref, w2_ref, b2_ref, o_ref, w1s_ref, w2s_ref):
    i = pl.program_id(0)
    f = w2_ref.shape[1]

    @pl.when((i == 0) | (be_ref[i] != be_ref[jnp.maximum(i - 1, 0)]))
    def _():
        r = lax.broadcasted_iota(jnp.int32, (PAIR_CHUNK, PAIR_CHUNK), 0)
        c = lax.broadcasted_iota(jnp.int32, (PAIR_CHUNK, PAIR_CHUNK), 1)
        src_col = jnp.where(c < LANES, 2 * c, 2 * (c - LANES) + 1)
        perm = jnp.where(r == src_col, 1.0, 0.0).astype(BF16)
        for cc in range(2 * f // PAIR_CHUNK):
            chunk = w1_ref[0, :, cc * PAIR_CHUNK:(cc + 1) * PAIR_CHUNK].astype(BF16)
            res = _dot(chunk, perm).astype(BF16)
            w1s_ref[:, cc * LANES:(cc + 1) * LANES] = res[:, :LANES]
            w1s_ref[:, f + cc * LANES:f + (cc + 1) * LANES] = res[:, LANES:]
        w2s_ref[...] = w2_ref[0].astype(BF16)

    @pl.when(i < nused_ref[0])
    def _():
        hid = _dot(x_ref[...], w1s_ref[...]) + b1_ref[0]
        glu = jnp.minimum(hid[:, :f], SWIGLU_LIMIT)
        lin = jnp.clip(hid[:, f:], -SWIGLU_LIMIT, SWIGLU_LIMIT)
        act = glu * jax.nn.sigmoid(SWIGLU_ALPHA * glu) * (lin + 1.0)
        o_ref[...] = (_dot(act.astype(BF16), w2s_ref[...]) + b2_ref[0]).astype(o_ref.dtype)

    @pl.when(i >= nused_ref[0])
    def _():
        o_ref[...] = jnp.zeros_like(o_ref)


def _moe(x_rows, block_expert, n_used, w1, b1, w2, b2, *, tm):
    n_rows, d = x_rows.shape
    e, _, f2 = w1.shape
    f = f2 // 2
    b1p = jnp.concatenate([b1[:, 0::2], b1[:, 1::2]], axis=-1).reshape(e, 1, f2)
    b2r = b2.reshape(e, 1, d)
    grid_spec = pltpu.PrefetchScalarGridSpec(
        num_scalar_prefetch=2,
        grid=(n_rows // tm,),
        in_specs=[
            pl.BlockSpec((tm, d), lambda i, be, nu: (i, 0)),
            pl.BlockSpec((1, d, f2), lambda i, be, nu: (be[i], 0, 0)),
            pl.BlockSpec((1, 1, f2), lambda i, be, nu: (be[i], 0, 0)),
            pl.BlockSpec((1, f, d), lambda i, be, nu: (be[i], 0, 0)),
            pl.BlockSpec((1, 1, d), lambda i, be, nu: (be[i], 0, 0)),
        ],
        out_specs=pl.BlockSpec((tm, d), lambda i, be, nu: (i, 0)),
        scratch_shapes=[pltpu.VMEM((d, f2), BF16), pltpu.VMEM((f, d), BF16)],
    )
    return pl.pallas_call(
        _moe_kernel, grid_spec=grid_spec, out_shape=jax.ShapeDtypeStruct((n_rows, d), BF16),
        compiler_params=_params("arbitrary"), name="moe",
    )(block_expert, n_used, x_rows, w1, b1p, w2, b2r)


ROUTE_CHUNK = 2 * LANES


def _router_kernel(lg_ref, idx_ref, gate_ref, rank_ref, cnt_ref, carry_ref):
    @pl.when(pl.program_id(0) == 0)
    def _():
        carry_ref[...] = jnp.zeros_like(carry_ref)

    x = lg_ref[...]
    ne, tt = x.shape
    eidx = lax.broadcasted_iota(jnp.int32, (ne, tt), 0)
    member = jnp.zeros((ne, tt), F32)
    vals, ids = [], []
    for _ in range(TOP_K):
        m = jnp.max(x, axis=0, keepdims=True)
        sel = jnp.min(jnp.where(x == m, eidx, ne), axis=0, keepdims=True)
        hit = eidx == sel
        member = jnp.where(hit, 1.0, member)
        x = jnp.where(hit, -jnp.inf, x)
        vals.append(m)
        ids.append(sel)
    e = jnp.exp(jnp.concatenate(vals, axis=0) - vals[0])
    gate_ref[...] = e / jnp.sum(e, axis=0, keepdims=True)
    idx_ref[...] = jnp.concatenate(ids, axis=0)
    r = lax.broadcasted_iota(jnp.int32, (ROUTE_CHUNK, ROUTE_CHUNK), 0)
    c = lax.broadcasted_iota(jnp.int32, (ROUTE_CHUNK, ROUTE_CHUNK), 1)
    earlier = jnp.where(r < c, 1.0, 0.0).astype(BF16)
    carry = carry_ref[:, :1]
    before = []
    for ch in range(tt // ROUTE_CHUNK):
        mc = member[:, ch * ROUTE_CHUNK:(ch + 1) * ROUTE_CHUNK]
        before.append(_dot(mc.astype(BF16), earlier) + carry)
        carry = carry + jnp.sum(mc, axis=1, keepdims=True)
    before = jnp.concatenate(before, axis=1)
    ranks = [jnp.sum(jnp.where(eidx == ids[k], before, 0.0), axis=0, keepdims=True) for k in range(TOP_K)]
    rank_ref[...] = jnp.concatenate(ranks, axis=0).astype(jnp.int32)
    carry_ref[...] = jnp.broadcast_to(carry, carry_ref.shape)
    cnt_ref[...] = jnp.broadcast_to(carry, cnt_ref.shape).astype(jnp.int32)


def _router(logits_t, *, tt=2048):
    ne, t = logits_t.shape
    tok = lambda i: (0, i)
    kspec = pl.BlockSpec((TOP_K, tt), tok)
    return pl.pallas_call(
        _router_kernel, grid=(t // tt,), in_specs=[pl.BlockSpec((ne, tt), tok)],
        out_specs=(kspec, kspec, kspec, pl.BlockSpec((ne, LANES), lambda i: (0, 0))),
        out_shape=(jax.ShapeDtypeStruct((TOP_K, t), jnp.int32), jax.ShapeDtypeStruct((TOP_K, t), F32),
                   jax.ShapeDtypeStruct((TOP_K, t), jnp.int32), jax.ShapeDtypeStruct((ne, LANES), jnp.int32)),
        scratch_shapes=[pltpu.VMEM((ne, LANES), F32)],
        compiler_params=_params("arbitrary"), name="router",
    )(logits_t)


def _route(logits_t, *, tm):
    ne, t = logits_t.shape
    idx, gates, rank, cnt = _router(logits_t)
    counts = cnt[:, 0]
    padded = (counts + tm - 1) // tm * tm
    pad_end = jnp.cumsum(padded)
    pad_start = pad_end - padded
    experts = jnp.arange(ne, dtype=jnp.int32).reshape(ne, 1, 1)
    slot = jnp.sum(jnp.where(idx[None] == experts, pad_start.reshape(ne, 1, 1), 0), axis=0) + rank
    n_blocks = -(-TOP_K * t // tm) + ne
    flat_slot = slot.reshape(-1)
    tok = jnp.broadcast_to(jnp.arange(t, dtype=jnp.int32), (TOP_K, t)).reshape(-1)
    row_tok = (jnp.arange(n_blocks * tm, dtype=jnp.int32) % t).at[flat_slot].add(
        tok - flat_slot % t, unique_indices=True, mode="promise_in_bounds")
    n_used = pad_end[-1] // tm
    blk_start = jnp.minimum(jnp.arange(n_blocks, dtype=jnp.int32), n_used - 1) * tm
    block_expert = jnp.minimum(jnp.sum(pad_end[:, None] <= blk_start[None, :], axis=0), ne - 1)
    return gates, row_tok, slot, block_expert.astype(jnp.int32), n_used.reshape(1).astype(jnp.int32)


def _final_kernel(h_ref, y_ref, gate_ref, p_ref, wp_ref, gp_ref, wgate_ref, gfin_ref, o_ref, *, last_layer):
    h = h_ref[...]
    g = gate_ref[...]
    gates = jnp.concatenate([g, jnp.zeros((8 - TOP_K, g.shape[1]), F32)], axis=0).T
    for k in range(TOP_K):
        h = h + gates[:, k:k + 1] * y_ref[k].astype(F32)
    ple = _rms(_dot(p_ref[...].astype(BF16), wp_ref[...]), gp_ref[...])
    h = h + jax.nn.sigmoid(_dot(h.astype(BF16), wgate_ref[...])) * ple
    o_ref[...] = _rms(h, gfin_ref[...]) if last_layer else h


def _final(h1, y_tok, gates, p2d, w_ple_proj, g_ple, w_ple_gate, g_final, *, last_layer, tm=1024):
    t, d = h1.shape
    row = lambda i: (i, 0)
    const = lambda i: (0, 0)
    full = lambda a: pl.BlockSpec(a.shape, const)
    args = (h1, y_tok, gates, p2d, w_ple_proj.astype(BF16), g_ple.reshape(1, d), w_ple_gate.astype(BF16),
            g_final.reshape(1, d))
    in_specs = [pl.BlockSpec((tm, d), row), pl.BlockSpec((TOP_K, tm, d), lambda i: (0, i, 0)),
                pl.BlockSpec((TOP_K, tm), lambda i: (0, i)), pl.BlockSpec((tm, p2d.shape[1]), row)]
    in_specs += [full(a) for a in args[4:]]
    return pl.pallas_call(
        functools.partial(_final_kernel, last_layer=last_layer), grid=(t // tm,), in_specs=in_specs,
        out_specs=pl.BlockSpec((tm, d), row),
        out_shape=jax.ShapeDtypeStruct((t, d), F32),
        compiler_params=_params("arbitrary"), name="final",
    )(*args)


def kernel(x, p, g_mix, w_in, w_gla_gate, b_gla_gate, g_gla_norm, w_out, g_ffn, w_router, b_router,
           w1, b1, w2, b2, w_ple_gate, w_ple_proj, g_ple, g_final, *, moe_tm=512):
    batch, seq, d = x.shape
    depth = p.shape[0]
    t = batch * seq
    h = x.reshape(t, d)
    for i in range(depth):
        qa, ka, vt, kmean, qg, kg, vg, rg, la = _inproj(h, g_mix[i], w_in[i], w_gla_gate[i], b_gla_gate[i],
                                                        batch=batch, seq=seq)
        o_moba = _moba(qa, ka, vt, kmean, batch=batch, seq=seq)
        o_gla = _gla(qg, kg, vg, rg, la, g_gla_norm[i], batch=batch, seq=seq)
        h1, f, logits = _outproj(h, o_moba, o_gla, w_out[i], g_ffn[i], w_router[i], b_router[i])
        gates, row_tok, slot, block_expert, n_used = _route(logits, tm=moe_tm)
        x_rows = jnp.take(f, row_tok, axis=0, mode="clip")
        y = _moe(x_rows, block_expert, n_used, w1[i], b1[i], w2[i], b2[i], tm=moe_tm)
        y_tok = jnp.take(y, slot, axis=0, mode="clip")
        h = _final(h1, y_tok, gates, p[i].reshape(t, -1), w_ple_proj[i], g_ple[i], w_ple_gate[i],
                   g_final, last_layer=i == depth - 1)
    return h.reshape(batch, seq, d)
```

```python
import functools

import jax
import jax.numpy as jnp
import numpy as np
from jax import lax
from jax.experimental import pallas as pl
from jax.experimental.pallas import tpu as pltpu

HEAD_DIM = 64
MOBA_HEADS = 8
MOBA_WIDTH = MOBA_HEADS * HEAD_DIM
MOBA_BLOCK = 256
MOBA_TOPK = 3
MOBA_GROUP = 4
MOBA_QBLOCKS = 2
ALIBI_MAX = 8.0
GLA_HEADS = 4
GLA_DK = 64
GLA_DV = 128
GLA_KEY_WIDTH = GLA_HEADS * GLA_DK
GLA_VAL_WIDTH = GLA_HEADS * GLA_DV
GLA_GATE_RANK = 16
GLA_GATE_TAU = 16.0
GLA_CHUNK = 64
GLA_SEQS = 4
N_EXPERTS = 32
TOP_K = 4
SWIGLU_ALPHA = 1.702
SWIGLU_LIMIT = 7.0
EPS = 1e-6

LANES = 128
MASKED = -1e30
VMEM_LIMIT = 56 * 1024 * 1024

BF16 = jnp.bfloat16
F32 = jnp.float32

_NT = (((1,), (1,)), ((), ()))
_TN = (((0,), (0,)), ((), ()))


def _dot(a, b, dims=None, precision=None):
    if dims is None:
        return jnp.dot(a, b, preferred_element_type=F32, precision=precision)
    return lax.dot_general(a, b, dims, preferred_element_type=F32, precision=precision)


def _rms(x, g):
    return x * lax.rsqrt(jnp.mean(x * x, axis=-1, keepdims=True) + EPS) * g


def _col_reduce(x, pair_op, final_op, rows=64):
    parts = [x[i:i + rows] for i in range(0, x.shape[0], rows)]
    while len(parts) > 1:
        parts = [pair_op(parts[i], parts[i + 1]) for i in range(0, len(parts), 2)]
    return final_op(parts[0], axis=0, keepdims=True)


def _params(*sem):
    return pltpu.CompilerParams(dimension_semantics=sem, vmem_limit_bytes=VMEM_LIMIT)


def _inproj_kernel(x_ref, g_ref, wq_ref, wk_ref, wvt_ref, wgla_ref, wag_ref, wgate_ref, bgate_ref,
                   qa_ref, ka_ref, vt_ref, km_ref, qg_ref, kg_ref, vg_ref, rg_ref, la_ref):
    i = pl.program_id(1)
    tm = x_ref.shape[0]
    nblk = tm // MOBA_BLOCK
    @pl.when(i == 0)
    def _():
        km_ref[...] = jnp.zeros_like(km_ref)

    ab = _rms(x_ref[...], g_ref[...]).astype(BF16)
    ag = _dot(ab, wag_ref[...]).astype(BF16)
    z = _dot(ag, wgate_ref[...]) + bgate_ref[...]
    la_ref[...] = jax.nn.log_sigmoid(z) / GLA_GATE_TAU
    qa_ref[...] = (_dot(ab, wq_ref[...]) * (HEAD_DIM ** -0.5)).astype(BF16)
    ka = _dot(ab, wk_ref[...])
    ka_ref[...] = ka.astype(BF16)
    km = km_ref[...]
    blk_row = lax.broadcasted_iota(jnp.int32, km.shape, 0)
    for j in range(nblk):
        mean_j = jnp.mean(ka[j * MOBA_BLOCK:(j + 1) * MOBA_BLOCK], axis=0, keepdims=True)
        km = jnp.where(blk_row == i * nblk + j, mean_j, km)
    km_ref[...] = km
    vt = _dot(wvt_ref[...], ab, _NT).astype(BF16)
    for j in range(nblk):
        vt_ref[j] = vt[:, j * MOBA_BLOCK:(j + 1) * MOBA_BLOCK]
    gla = _dot(ab, wgla_ref[...])
    o = 0
    for ref in (qg_ref, kg_ref, vg_ref, rg_ref):
        w = ref.shape[1]
        ref[...] = gla[:, o:o + w].astype(ref.dtype)
        o += w


def _inproj(x2d, g_mix, w_in, w_gla_gate, b_gla_gate, *, batch, seq, tm=512):
    t, d = x2d.shape
    nb = seq // MOBA_BLOCK
    o_k, o_v, o_qg = MOBA_WIDTH, 2 * MOBA_WIDTH, 3 * MOBA_WIDTH
    o_ag = o_qg + 2 * GLA_KEY_WIDTH + 2 * GLA_VAL_WIDTH
    wq = w_in[:, :o_k].astype(BF16)
    wk = w_in[:, o_k:o_v].astype(BF16)
    wvt = w_in[:, o_v:o_qg].T.astype(BF16)
    wgla = w_in[:, o_qg:o_ag].astype(BF16)
    wag = jnp.pad(w_in[:, o_ag:], ((0, 0), (0, LANES - GLA_GATE_RANK))).astype(BF16)
    wgate = jnp.pad(w_gla_gate, ((0, LANES - GLA_GATE_RANK), (0, 0))).astype(BF16)
    steps = seq // tm
    row = lambda b, i: (b * steps + i, 0)
    const = lambda b, i: (0, 0)
    full = lambda a: pl.BlockSpec(a.shape, const)
    args = (x2d, g_mix.reshape(1, d), wq, wk, wvt, wgla, wag, wgate, b_gla_gate.reshape(1, -1))
    in_specs = [pl.BlockSpec((tm, d), row)] + [full(a) for a in args[1:]]
    out_shape = (
        jax.ShapeDtypeStruct((t, MOBA_WIDTH), BF16),
        jax.ShapeDtypeStruct((t, MOBA_WIDTH), BF16),
        jax.ShapeDtypeStruct((t // MOBA_BLOCK, MOBA_WIDTH, MOBA_BLOCK), BF16),
        jax.ShapeDtypeStruct((batch * nb, MOBA_WIDTH), F32),
        jax.ShapeDtypeStruct((t, GLA_KEY_WIDTH), F32),
        jax.ShapeDtypeStruct((t, GLA_KEY_WIDTH), F32),
        jax.ShapeDtypeStruct((t, GLA_VAL_WIDTH), BF16),
        jax.ShapeDtypeStruct((t, GLA_VAL_WIDTH), F32),
        jax.ShapeDtypeStruct((t, GLA_KEY_WIDTH), F32),
    )
    out_specs = (
        pl.BlockSpec((tm, MOBA_WIDTH), row),
        pl.BlockSpec((tm, MOBA_WIDTH), row),
        pl.BlockSpec((tm // MOBA_BLOCK, MOBA_WIDTH, MOBA_BLOCK), lambda b, i: (b * steps + i, 0, 0)),
        pl.BlockSpec((nb, MOBA_WIDTH), lambda b, i: (b, 0)),
        pl.BlockSpec((tm, GLA_KEY_WIDTH), row),
        pl.BlockSpec((tm, GLA_KEY_WIDTH), row),
        pl.BlockSpec((tm, GLA_VAL_WIDTH), row),
        pl.BlockSpec((tm, GLA_VAL_WIDTH), row),
        pl.BlockSpec((tm, GLA_KEY_WIDTH), row),
    )
    return pl.pallas_call(
        _inproj_kernel, grid=(batch, steps), in_specs=in_specs, out_specs=out_specs, out_shape=out_shape,
        compiler_params=_params("arbitrary", "arbitrary"), name="inproj",
    )(*args)


def _moba_kernel(slopes_ref, q_ref, k_ref, vt_ref, km_ref, o_ref, kaug_ref, s_ref):
    hg = pl.program_id(1)
    step = pl.program_id(2)
    blk = MOBA_BLOCK
    nb = km_ref.shape[0]
    hw = 2 * HEAD_DIM
    heads = q_ref.shape[1] // HEAD_DIM
    n_q = q_ref.shape[0] // blk
    feat_rows = nb + 8
    head_rows = lambda g: slice(g * HEAD_DIM, (g + 1) * HEAD_DIM)
    hs = range(heads)

    @pl.when(step == 0)
    def _():
        lane = lax.broadcasted_iota(jnp.int32, (blk, hw), 1)
        koff = lax.broadcasted_iota(jnp.int32, (blk, hw), 0).astype(F32)
        for g in hs:
            tile, hh = divmod(g, 2)
            slope = slopes_ref[hg * heads + g]
            base = HEAD_DIM * (1 - hh)
            for jb in range(nb):
                feat = jnp.where(lane == base + nb, slope * koff, 0.0)
                feat = jnp.where((lane == base + jb) | (lane == base + nb + 1), 1.0, feat)
                kb = k_ref[jb * blk:(jb + 1) * blk, tile * hw:(tile + 1) * hw]
                kaug_ref[g, jb * blk:(jb + 1) * blk, :] = jnp.where(lane // HEAD_DIM == hh, kb, feat.astype(BF16))

    kidx = lax.broadcasted_iota(jnp.int32, (blk, blk), 0)
    qidx = lax.broadcasted_iota(jnp.int32, (blk, blk), 1)
    causal = kidx <= qidx
    nidx = lax.broadcasted_iota(jnp.int32, (nb, blk), 0)
    r8 = lax.broadcasted_iota(jnp.int32, (8, blk), 0)
    qoff = lax.broadcasted_iota(jnp.int32, (8, blk), 1).astype(F32)
    km_lane = lax.broadcasted_iota(jnp.int32, km_ref.shape, 1)
    km = km_ref[...]
    slopes = [slopes_ref[hg * heads + g] for g in hs]
    ones = jnp.ones((16, blk), BF16)
    fill = jnp.zeros((HEAD_DIM - feat_rows, blk), F32)

    def values_and_ones(vt_blk, g):
        return jnp.concatenate([vt_blk[head_rows(g), :], ones], axis=0)

    def scores(q_past, pair, buf, slot):
        rows = pl.ds(pl.multiple_of(pair * 2 * blk, 2 * blk), 2 * blk)
        tops = []
        for g in hs:
            s = _dot(kaug_ref[g, rows, :], q_past[g])
            s_ref[buf, slot, g] = s
            tops.append(_col_reduce(s, jnp.maximum, jnp.max))
        return tops

    def absorb(pair, buf, slot, tops, carry):
        vt0 = vt_ref[2 * pair]
        vt1 = vt_ref[2 * pair + 1]
        out = []
        for g in hs:
            m_i, acc = carry[2 * g:2 * g + 2]
            m_new = jnp.maximum(m_i, tops[g])
            p = jnp.exp((s_ref[buf, slot, g] - m_new).astype(BF16))
            acc = (jnp.exp(m_i - m_new) * acc + _dot(values_and_ones(vt0, g), p[:blk])
                   + _dot(values_and_ones(vt1, g), p[blk:]))
            out += [m_new, acc]
        return out

    def prologue(i):
        c = step * n_q + i
        qt = q_ref[i * blk:(i + 1) * blk, :].astype(F32).T
        qt_b = qt.astype(BF16)
        gates = [jnp.where(nidx < c, _dot(jnp.where(km_lane // HEAD_DIM == g, km, 0.0).astype(BF16), qt_b), -jnp.inf)
                 for g in hs]
        ranks = [jnp.zeros((nb, blk), jnp.int32) for _ in hs]
        for m in range(nb - 1):
            for g in hs:
                gm = gates[g][m:m + 1, :]
                beats = (gm > gates[g]) | ((gm == gates[g]) & (m < nidx))
                ranks[g] = ranks[g] + beats.astype(jnp.int32)

        def q_aug(g, bias):
            dist0 = -slopes[g] * ((c - nidx) * blk).astype(F32)
            tail = jnp.where(r8 == 0, 1.0, jnp.where(r8 == 1, -slopes[g] * qoff, 0.0))
            feats = [jnp.where(bias, dist0, MASKED), tail, fill]
            qh = qt[head_rows(g), :]
            return jnp.concatenate([qh] + feats if g % 2 == 0 else feats + [qh], axis=0).astype(BF16)

        q_past = [q_aug(g, (nidx < c) & (ranks[g] < MOBA_TOPK)) for g in hs]
        own_rows = pl.ds(pl.multiple_of(c * blk, blk), blk)
        vt_own = vt_ref[c]
        q_own = [q_aug(g, nidx == c) for g in hs]
        s_own = [jnp.where(causal, _dot(kaug_ref[g, own_rows, :], q_own[g]), MASKED) for g in hs]
        m_own = [jnp.max(s, axis=0, keepdims=True) for s in s_own]
        p_own = [jnp.exp((s_own[g] - m_own[g]).astype(BF16)) for g in hs]
        state = []
        for g in hs:
            state += [m_own[g], _dot(values_and_ones(vt_own, g), p_own[g])]
        return c, q_past, state + scores(q_past, 0, i, 0)

    def attend(i, c, q_past, carry):
        n_state = 2 * heads

        def advance(n, carry, slot):
            tops = scores(q_past, n + 1, i, 1 - slot)
            return tuple(absorb(n, i, slot, carry[n_state:], carry[:n_state]) + tops)

        def body(n, carry):
            return lax.cond(n % 2 == 0, lambda: advance(n, carry, 0), lambda: advance(n, carry, 1))

        last = jnp.maximum((c + 1) // 2 - 1, 0)
        carry = lax.fori_loop(0, last, body, tuple(carry))
        state = lax.cond(last % 2 == 0, lambda: tuple(absorb(last, i, 0, carry[n_state:], carry[:n_state])),
                         lambda: tuple(absorb(last, i, 1, carry[n_state:], carry[:n_state])))
        out_t = jnp.concatenate([a[:HEAD_DIM] / a[HEAD_DIM:HEAD_DIM + 1] for a in state[1::2]], axis=0)
        o_ref[i * blk:(i + 1) * blk, :] = out_t.T.astype(o_ref.dtype)

    started = [prologue(i) for i in range(n_q)]
    for i, (c, q_past, carry) in enumerate(started):
        attend(i, c, q_past, carry)


def _alibi_slopes():
    slopes = np.exp2(-ALIBI_MAX * np.arange(1, MOBA_HEADS + 1, dtype=np.float32) / MOBA_HEADS).astype(np.float32)
    worst = slopes * np.float32(MOBA_BLOCK - 1)
    assert np.all(worst.astype(jnp.bfloat16).astype(np.float32) == worst), "ALiBi slopes not bf16-exact"
    return slopes


def _moba(qa, ka, vt, kmean, *, batch, seq):
    t = qa.shape[0]
    nb = seq // MOBA_BLOCK
    gw = MOBA_GROUP * HEAD_DIM
    assert nb % 8 == 0 and nb + 8 <= HEAD_DIM, "bias features must fit the spare contraction lanes"
    grid_spec = pltpu.PrefetchScalarGridSpec(
        num_scalar_prefetch=1,
        grid=(batch, MOBA_HEADS // MOBA_GROUP, nb // MOBA_QBLOCKS),
        in_specs=[
            pl.BlockSpec((MOBA_QBLOCKS * MOBA_BLOCK, gw), lambda b, h, c, s: (b * (nb // MOBA_QBLOCKS) + c, h)),
            pl.BlockSpec((seq, gw), lambda b, h, c, s: (b, h)),
            pl.BlockSpec((nb, gw, MOBA_BLOCK), lambda b, h, c, s: (b, h, 0)),
            pl.BlockSpec((nb, gw), lambda b, h, c, s: (b, h)),
        ],
        out_specs=pl.BlockSpec((MOBA_QBLOCKS * MOBA_BLOCK, gw), lambda b, h, c, s: (b * (nb // MOBA_QBLOCKS) + c, h)),
        scratch_shapes=[pltpu.VMEM((MOBA_GROUP, seq, 2 * HEAD_DIM), BF16),
                        pltpu.VMEM((MOBA_QBLOCKS, 2, MOBA_GROUP, 2 * MOBA_BLOCK, MOBA_BLOCK), F32)],
    )
    return pl.pallas_call(
        _moba_kernel, grid_spec=grid_spec, out_shape=jax.ShapeDtypeStruct((t, MOBA_WIDTH), BF16),
        compiler_params=_params("arbitrary", "arbitrary", "arbitrary"), name="moba",
    )(jnp.asarray(_alibi_slopes()), qa, ka, vt, kmean)


def _gla_kernel(q_ref, k_ref, v_ref, r_ref, la_ref, gn_ref, o_ref, st_ref):
    @pl.when(pl.program_id(1) == 0)
    def _():
        st_ref[...] = jnp.zeros_like(st_ref)

    ch = GLA_CHUNK
    n_seqs, tc = q_ref.shape[:2]
    n_chunks = tc // ch
    row = lax.broadcasted_iota(jnp.int32, (tc, tc), 0)
    col = lax.broadcasted_iota(jnp.int32, (tc, tc), 1)
    causal = (row // ch == col // ch) & (row >= col)
    tri = jnp.where(causal, 1.0, 0.0).astype(BF16)
    lane = lax.broadcasted_iota(jnp.int32, (1, GLA_KEY_WIDTH), 1)
    gn = gn_ref[...]
    seqs = range(n_seqs)
    heads = range(GLA_HEADS)
    gs = []
    for s in seqs:
        la = la_ref[s]
        la_hi = la.astype(BF16)
        rest = la - la_hi.astype(F32)
        la_mid = rest.astype(BF16)
        la_lo = (rest - la_mid.astype(F32)).astype(BF16)
        gs.append(_dot(tri, la_hi) + _dot(tri, la_mid) + _dot(tri, la_lo))
    ends = [[g[(ci + 1) * ch - 1:(ci + 1) * ch, :] for ci in range(n_chunks)] for g in gs]
    qhs, kds, kths, vhs = [], [], [], []
    for s in seqs:
        g = gs[s]
        g_last = jnp.concatenate([jnp.broadcast_to(e, (ch, GLA_KEY_WIDTH)) for e in ends[s]], axis=0)
        qd = q_ref[s] * (GLA_DK ** -0.5) * jnp.exp(g)
        k = k_ref[s]
        kds.append((k * jnp.exp(-g)).astype(BF16))
        kt = k * jnp.exp(g_last - g)
        qhs.append([jnp.where(lane // GLA_DK == h, qd, 0.0).astype(BF16) for h in heads])
        kths.append([jnp.where(lane // GLA_DK == h, kt, 0.0).astype(BF16) for h in heads])
        vhs.append([v_ref[s, :, h * GLA_DV:(h + 1) * GLA_DV] for h in heads])
    intra = [[None] * GLA_HEADS for _ in seqs]
    for h in heads:
        for s in seqs:
            a = jnp.where(causal, _dot(qhs[s][h], kds[s], _NT), 0.0)
            intra[s][h] = _dot(a.astype(BF16), vhs[s][h])
    states = [st_ref[s] for s in seqs]
    inter = [[[] for _ in heads] for _ in seqs]
    for ci in range(n_chunks):
        rows = slice(ci * ch, (ci + 1) * ch)
        for s in seqs:
            state_b = states[s].astype(BF16)
            new_state = states[s] * jnp.exp(ends[s][ci])
            for h in heads:
                inter[s][h].append(_dot(qhs[s][h][rows], state_b, _NT))
                new_state = new_state + _dot(vhs[s][h][rows], kths[s][h][rows], _TN)
            states[s] = new_state
    for s in seqs:
        st_ref[s] = states[s]
    for h in heads:
        for s in seqs:
            o = intra[s][h] + jnp.concatenate(inter[s][h], axis=0)
            rh = r_ref[s, :, h * GLA_DV:(h + 1) * GLA_DV]
            o_ref[s, :, h * GLA_DV:(h + 1) * GLA_DV] = (_rms(o, gn) * (rh * jax.nn.sigmoid(rh))).astype(o_ref.dtype)


def _gla(qg, kg, vg, rg, la, g_norm, *, batch, seq, tc=256):
    n_seqs = GLA_SEQS if batch % GLA_SEQS == 0 else 1
    by_seq = lambda a: a.reshape(batch, seq, a.shape[-1])
    blk = lambda i, j: (i, j, 0)
    kspec = pl.BlockSpec((n_seqs, tc, GLA_KEY_WIDTH), blk)
    vspec = pl.BlockSpec((n_seqs, tc, GLA_VAL_WIDTH), blk)
    out = pl.pallas_call(
        _gla_kernel, grid=(batch // n_seqs, seq // tc),
        in_specs=[kspec, kspec, vspec, vspec, kspec, pl.BlockSpec((1, GLA_DV), lambda i, j: (0, 0))],
        out_specs=vspec, out_shape=jax.ShapeDtypeStruct((batch, seq, GLA_VAL_WIDTH), BF16),
        scratch_shapes=[pltpu.VMEM((n_seqs, GLA_DV, GLA_KEY_WIDTH), F32)],
        compiler_params=_params("arbitrary", "arbitrary"), name="gla",
    )(by_seq(qg), by_seq(kg), by_seq(vg), by_seq(rg), by_seq(la), g_norm.reshape(1, GLA_DV))
    return out.reshape(batch * seq, GLA_VAL_WIDTH)


def _outproj_kernel(x_ref, om_ref, og_ref, wm_ref, wg_ref, gf_ref, wr_ref, br_ref, h_ref, f_ref, lg_ref):
    h = x_ref[...] + _dot(om_ref[...], wm_ref[...]) + _dot(og_ref[...], wg_ref[...])
    h_ref[...] = h
    f = _rms(h, gf_ref[...]).astype(BF16)
    f_ref[...] = f
    lg_ref[...] = _dot(wr_ref[...], f, _NT) + br_ref[...]


def _outproj(x2d, o_moba, o_gla, w_out, g_ffn, w_router, b_router, *, tm=1024):
    t, d = x2d.shape
    wm = w_out[:MOBA_WIDTH].astype(BF16)
    wg = w_out[MOBA_WIDTH:].astype(BF16)
    wr = w_router.T.astype(BF16)
    br = b_router.reshape(N_EXPERTS, 1)
    row = lambda i: (i, 0)
    const = lambda i: (0, 0)
    full = lambda a: pl.BlockSpec(a.shape, const)
    args = (x2d, o_moba, o_gla, wm, wg, g_ffn.reshape(1, d), wr, br)
    in_specs = [pl.BlockSpec((tm, d), row), pl.BlockSpec((tm, MOBA_WIDTH), row),
                pl.BlockSpec((tm, GLA_VAL_WIDTH), row)] + [full(a) for a in args[3:]]
    return pl.pallas_call(
        _outproj_kernel, grid=(t // tm,), in_specs=in_specs,
        out_specs=(pl.BlockSpec((tm, d), row), pl.BlockSpec((tm, d), row),
                   pl.BlockSpec((N_EXPERTS, tm), lambda i: (0, i))),
        out_shape=(jax.ShapeDtypeStruct((t, d), F32), jax.ShapeDtypeStruct((t, d), BF16),
                   jax.ShapeDtypeStruct((N_EXPERTS, t), F32)),
        compiler_params=_params("arbitrary"), name="outproj",
    )(*args)


PAIR_CHUNK = 2 * LANES


def _moe_kernel(be_ref, nused_ref, x_ref, w1_ref, b1_ref, w2_ref, b2_ref, o_ref, w1s_ref, w2s_ref):
    i = pl.program_id(0)
    f = w2_ref.shape[1]

    @pl.when((i == 0) | (be_ref[i] != be_ref[jnp.maximum(i - 1, 0)]))
    def _():
        r = lax.broadcasted_iota(jnp.int32, (PAIR_CHUNK, PAIR_CHUNK), 0)
        c = lax.broadcasted_iota(jnp.int32, (PAIR_CHUNK, PAIR_CHUNK), 1)
        src_col = jnp.where(c < LANES, 2 * c, 2 * (c - LANES) + 1)
        perm = jnp.where(r == src_col, 1.0, 0.0).astype(BF16)
        for cc in range(2 * f // PAIR_CHUNK):
            chunk = w1_ref[0, :, cc * PAIR_CHUNK:(cc + 1) * PAIR_CHUNK].astype(BF16)
            res = _dot(chunk, perm).astype(BF16)
            w1s_ref[:, cc * LANES:(cc + 1) * LANES] = res[:, :LANES]
            w1s_ref[:, f + cc * LANES:f + (cc + 1) * LANES] = res[:, LANES:]
        w2s_ref[...] = w2_ref[0].astype(BF16)

    @pl.when(i < nused_ref[0])
    def _():
        hid = _dot(x_ref[...], w1s_ref[...]) + b1_ref[0]
        glu = jnp.minimum(hid[:, :f], SWIGLU_LIMIT)
        lin = jnp.clip(hid[:, f:], -SWIGLU_LIMIT, SWIGLU_LIMIT)
        act = glu * jax.nn.sigmoid(SWIGLU_ALPHA * glu) * (lin + 1.0)
        o_ref[...] = (_dot(act.astype(BF16), w2s_ref[...]) + b2_ref[0]).astype(o_ref.dtype)

    @pl.when(i >= nused_ref[0])
    def _():
        o_ref[...] = jnp.zeros_like(o_ref)


def _moe(x_rows, block_expert, n_used, w1, b1, w2, b2, *, tm):
    n_rows, d = x_rows.shape
    e, _, f2 = w1.shape
    f = f2 // 2
    b1p = jnp.concatenate([b1[:, 0::2], b1[:, 1::2]], axis=-1).reshape(e, 1, f2)
    b2r = b2.reshape(e, 1, d)
    grid_spec = pltpu.PrefetchScalarGridSpec(
        num_scalar_prefetch=2,
        grid=(n_rows // tm,),
        in_specs=[
            pl.BlockSpec((tm, d), lambda i, be, nu: (i, 0)),
            pl.BlockSpec((1, d, f2), lambda i, be, nu: (be[i], 0, 0)),
            pl.BlockSpec((1, 1, f2), lambda i, be, nu: (be[i], 0, 0)),
            pl.BlockSpec((1, f, d), lambda i, be, nu: (be[i], 0, 0)),
            pl.BlockSpec((1, 1, d), lambda i, be, nu: (be[i], 0, 0)),
        ],
        out_specs=pl.BlockSpec((tm, d), lambda i, be, nu: (i, 0)),
        scratch_shapes=[pltpu.VMEM((d, f2), BF16), pltpu.VMEM((f, d), BF16)],
    )
    return pl.pallas_call(
        _moe_kernel, grid_spec=grid_spec, out_shape=jax.ShapeDtypeStruct((n_rows, d), BF16),
        compiler_params=_params("arbitrary"), name="moe",
    )(block_expert, n_used, x_rows, w1, b1p, w2, b2r)


ROUTE_CHUNK = 2 * LANES


def _router_kernel(lg_ref, idx_ref, gate_ref, rank_ref, cnt_ref, carry_ref):
    @pl.when(pl.program_id(0) == 0)
    def _():
        carry_ref[...] = jnp.zeros_like(carry_ref)

    x = lg_ref[...]
    ne, tt = x.shape
    eidx = lax.broadcasted_iota(jnp.int32, (ne, tt), 0)
    member = jnp.zeros((ne, tt), F32)
    vals, ids = [], []
    for _ in range(TOP_K):
        m = jnp.max(x, axis=0, keepdims=True)
        sel = jnp.min(jnp.where(x == m, eidx, ne), axis=0, keepdims=True)
        hit = eidx == sel
        member = jnp.where(hit, 1.0, member)
        x = jnp.where(hit, -jnp.inf, x)
        vals.append(m)
        ids.append(sel)
    e = jnp.exp(jnp.concatenate(vals, axis=0) - vals[0])
    gate_ref[...] = e / jnp.sum(e, axis=0, keepdims=True)
    idx_ref[...] = jnp.concatenate(ids, axis=0)
    r = lax.broadcasted_iota(jnp.int32, (ROUTE_CHUNK, ROUTE_CHUNK), 0)
    c = lax.broadcasted_iota(jnp.int32, (ROUTE_CHUNK, ROUTE_CHUNK), 1)
    earlier = jnp.where(r < c, 1.0, 0.0).astype(BF16)
    carry = carry_ref[:, :1]
    before = []
    for ch in range(tt // ROUTE_CHUNK):
        mc = member[:, ch * ROUTE_CHUNK:(ch + 1) * ROUTE_CHUNK]
        before.append(_dot(mc.astype(BF16), earlier) + carry)
        carry = carry + jnp.sum(mc, axis=1, keepdims=True)
    before = jnp.concatenate(before, axis=1)
    ranks = [jnp.sum(jnp.where(eidx == ids[k], before, 0.0), axis=0, keepdims=True) for k in range(TOP_K)]
    rank_ref[...] = jnp.concatenate(ranks, axis=0).astype(jnp.int32)
    carry_ref[...] = jnp.broadcast_to(carry, carry_ref.shape)
    cnt_ref[...] = jnp.broadcast_to(carry, cnt_ref.shape).astype(jnp.int32)


def _router(logits_t, *, tt=2048):
    ne, t = logits_t.shape
    tok = lambda i: (0, i)
    kspec = pl.BlockSpec((TOP_K, tt), tok)
    return pl.pallas_call(
        _router_kernel, grid=(t // tt,), in_specs=[pl.BlockSpec((ne, tt), tok)],
        out_specs=(kspec, kspec, kspec, pl.BlockSpec((ne, LANES), lambda i: (0, 0))),
        out_shape=(jax.ShapeDtypeStruct((TOP_K, t), jnp.int32), jax.ShapeDtypeStruct((TOP_K, t), F32),
                   jax.ShapeDtypeStruct((TOP_K, t), jnp.int32), jax.ShapeDtypeStruct((ne, LANES), jnp.int32)),
        scratch_shapes=[pltpu.VMEM((ne, LANES), F32)],
        compiler_params=_params("arbitrary"), name="router",
    )(logits_t)


def _route(logits_t, *, tm):
    ne, t = logits_t.shape
    idx, gates, rank, cnt = _router(logits_t)
    counts = cnt[:, 0]
    padded = (counts + tm - 1) // tm * tm
    pad_end = jnp.cumsum(padded)
    pad_start = pad_end - padded
    experts = jnp.arange(ne, dtype=jnp.int32).reshape(ne, 1, 1)
    slot = jnp.sum(jnp.where(idx[None] == experts, pad_start.reshape(ne, 1, 1), 0), axis=0) + rank
    n_blocks = -(-TOP_K * t // tm) + ne
    flat_slot = slot.reshape(-1)
    tok = jnp.broadcast_to(jnp.arange(t, dtype=jnp.int32), (TOP_K, t)).reshape(-1)
    row_tok = (jnp.arange(n_blocks * tm, dtype=jnp.int32) % t).at[flat_slot].add(
        tok - flat_slot % t, unique_indices=True, mode="promise_in_bounds")
    n_used = pad_end[-1] // tm
    blk_start = jnp.minimum(jnp.arange(n_blocks, dtype=jnp.int32), n_used - 1) * tm
    block_expert = jnp.minimum(jnp.sum(pad_end[:, None] <= blk_start[None, :], axis=0), ne - 1)
    return gates, row_tok, slot, block_expert.astype(jnp.int32), n_used.reshape(1).astype(jnp.int32)


def _final_kernel(h_ref, y_ref, gate_ref, p_ref, wp_ref, gp_ref, wgate_ref, gfin_ref, o_ref, *, last_layer):
    h = h_ref[...]
    g = gate_ref[...]
    gates = jnp.concatenate([g, jnp.zeros((8 - TOP_K, g.shape[1]), F32)], axis=0).T
    for k in range(TOP_K):
        h = h + gates[:, k:k + 1] * y_ref[k].astype(F32)
    ple = _rms(_dot(p_ref[...].astype(BF16), wp_ref[...]), gp_ref[...])
    h = h + jax.nn.sigmoid(_dot(h.astype(BF16), wgate_ref[...])) * ple
    o_ref[...] = _rms(h, gfin_ref[...]) if last_layer else h


def _final(h1, y_tok, gates, p2d, w_ple_proj, g_ple, w_ple_gate, g_final, *, last_layer, tm=1024):
    t, d = h1.shape
    row = lambda i: (i, 0)
    const = lambda i: (0, 0)
    full = lambda a: pl.BlockSpec(a.shape, const)
    args = (h1, y_tok, gates, p2d, w_ple_proj.astype(BF16), g_ple.reshape(1, d), w_ple_gate.astype(BF16),
            g_final.reshape(1, d))
    in_specs = [pl.BlockSpec((tm, d), row), pl.BlockSpec((TOP_K, tm, d), lambda i: (0, i, 0)),
                pl.BlockSpec((TOP_K, tm), lambda i: (0, i)), pl.BlockSpec((tm, p2d.shape[1]), row)]
    in_specs += [full(a) for a in args[4:]]
    return pl.pallas_call(
        functools.partial(_final_kernel, last_layer=last_layer), grid=(t // tm,), in_specs=in_specs,
        out_specs=pl.BlockSpec((tm, d), row),
        out_shape=jax.ShapeDtypeStruct((t, d), F32),
        compiler_params=_params("arbitrary"), name="final",
    )(*args)


def kernel(x, p, g_mix, w_in, w_gla_gate, b_gla_gate, g_gla_norm, w_out, g_ffn, w_router, b_router,
           w1, b1, w2, b2, w_ple_gate, w_ple_proj, g_ple, g_final, *, moe_tm=512):
    batch, seq, d = x.shape
    depth = p.shape[0]
    t = batch * seq
    h = x.reshape(t, d)
    for i in range(depth):
        qa, ka, vt, kmean, qg, kg, vg, rg, la = _inproj(h, g_mix[i], w_in[i], w_gla_gate[i], b_gla_gate[i],
                                                        batch=batch, seq=seq)
        o_moba = _moba(qa, ka, vt, kmean, batch=batch, seq=seq)
        o_gla = _gla(qg, kg, vg, rg, la, g_gla_norm[i], batch=batch, seq=seq)
        h1, f, logits = _outproj(h, o_moba, o_gla, w_out[i], g_ffn[i], w_router[i], b_router[i])
        gates, row_tok, slot, block_expert, n_used = _route(logits, tm=moe_tm)
        x_rows = jnp.take(f, row_tok, axis=0, mode="clip")
        y = _moe(x_rows, block_expert, n_used, w1[i], b1[i], w2[i], b2[i], tm=moe_tm)
        y_tok = jnp.take(y, slot, axis=0, mode="clip")
        h = _final(h1, y_tok, gates, p[i].reshape(t, -1), w_ple_proj[i], g_ple[i], w_ple_gate[i],
                   g_final, last_layer=i == depth - 1)
    return h.reshape(batch, seq, d)
```

```python
import functools

import jax
import jax.numpy as jnp
import numpy as np
from jax import lax
from jax.experimental import pallas as pl
from jax.experimental.pallas import tpu as pltpu

HEAD_DIM = 64
MOBA_HEADS = 8
MOBA_WIDTH = MOBA_HEADS * HEAD_DIM
MOBA_BLOCK = 256
MOBA_TOPK = 3
MOBA_GROUP = 4
MOBA_QBLOCKS = 4
ALIBI_MAX = 8.0
GLA_HEADS = 4
GLA_DK = 64
GLA_DV = 128
GLA_KEY_WIDTH = GLA_HEADS * GLA_DK
GLA_VAL_WIDTH = GLA_HEADS * GLA_DV
GLA_GATE_RANK = 16
GLA_GATE_TAU = 16.0
GLA_CHUNK = 64
GLA_SEQS = 4
N_EXPERTS = 32
TOP_K = 4
SWIGLU_ALPHA = 1.702
SWIGLU_LIMIT = 7.0
EPS = 1e-6

LANES = 128
MASKED = -1e30
VMEM_LIMIT = 56 * 1024 * 1024

BF16 = jnp.bfloat16
F32 = jnp.float32

_NT = (((1,), (1,)), ((), ()))
_TN = (((0,), (0,)), ((), ()))


def _dot(a, b, dims=None, precision=None):
    if dims is None:
        return jnp.dot(a, b, preferred_element_type=F32, precision=precision)
    return lax.dot_general(a, b, dims, preferred_element_type=F32, precision=precision)


def _rms(x, g):
    return x * lax.rsqrt(jnp.mean(x * x, axis=-1, keepdims=True) + EPS) * g


def _col_reduce(x, pair_op, final_op, rows=64):
    parts = [x[i:i + rows] for i in range(0, x.shape[0], rows)]
    while len(parts) > 1:
        parts = [pair_op(parts[i], parts[i + 1]) for i in range(0, len(parts), 2)]
    return final_op(parts[0], axis=0, keepdims=True)


def _params(*sem):
    return pltpu.CompilerParams(dimension_semantics=sem, vmem_limit_bytes=VMEM_LIMIT)


def _inproj_kernel(x_ref, g_ref, wq_ref, wk_ref, wvt_ref, wgla_ref, wag_ref, wgate_ref, bgate_ref,
                   qa_ref, ka_ref, vt_ref, km_ref, qg_ref, kg_ref, vg_ref, rg_ref, la_ref):
    i = pl.program_id(1)
    tm = x_ref.shape[0]
    nblk = tm // MOBA_BLOCK
    @pl.when(i == 0)
    def _():
        km_ref[...] = jnp.zeros_like(km_ref)

    ab = _rms(x_ref[...], g_ref[...]).astype(BF16)
    ag = _dot(ab, wag_ref[...]).astype(BF16)
    z = _dot(ag, wgate_ref[...]) + bgate_ref[...]
    la_ref[...] = jax.nn.log_sigmoid(z) / GLA_GATE_TAU
    qa_ref[...] = (_dot(ab, wq_ref[...]) * (HEAD_DIM ** -0.5)).astype(BF16)
    ka = _dot(ab, wk_ref[...])
    ka_ref[...] = ka.astype(BF16)
    km = km_ref[...]
    blk_row = lax.broadcasted_iota(jnp.int32, km.shape, 0)
    for j in range(nblk):
        mean_j = jnp.mean(ka[j * MOBA_BLOCK:(j + 1) * MOBA_BLOCK], axis=0, keepdims=True)
        km = jnp.where(blk_row == i * nblk + j, mean_j, km)
    km_ref[...] = km
    vt = _dot(wvt_ref[...], ab, _NT).astype(BF16)
    for j in range(nblk):
        vt_ref[j] = vt[:, j * MOBA_BLOCK:(j + 1) * MOBA_BLOCK]
    gla = _dot(ab, wgla_ref[...])
    o = 0
    for ref in (qg_ref, kg_ref, vg_ref, rg_ref):
        w = ref.shape[1]
        ref[...] = gla[:, o:o + w].astype(ref.dtype)
        o += w


def _inproj(x2d, g_mix, w_in, w_gla_gate, b_gla_gate, *, batch, seq, tm=512):
    t, d = x2d.shape
    nb = seq // MOBA_BLOCK
    o_k, o_v, o_qg = MOBA_WIDTH, 2 * MOBA_WIDTH, 3 * MOBA_WIDTH
    o_ag = o_qg + 2 * GLA_KEY_WIDTH + 2 * GLA_VAL_WIDTH
    wq = w_in[:, :o_k].astype(BF16)
    wk = w_in[:, o_k:o_v].astype(BF16)
    wvt = w_in[:, o_v:o_qg].T.astype(BF16)
    wgla = w_in[:, o_qg:o_ag].astype(BF16)
    wag = jnp.pad(w_in[:, o_ag:], ((0, 0), (0, LANES - GLA_GATE_RANK))).astype(BF16)
    wgate = jnp.pad(w_gla_gate, ((0, LANES - GLA_GATE_RANK), (0, 0))).astype(BF16)
    steps = seq // tm
    row = lambda b, i: (b * steps + i, 0)
    const = lambda b, i: (0, 0)
    full = lambda a: pl.BlockSpec(a.shape, const)
    args = (x2d, g_mix.reshape(1, d), wq, wk, wvt, wgla, wag, wgate, b_gla_gate.reshape(1, -1))
    in_specs = [pl.BlockSpec((tm, d), row)] + [full(a) for a in args[1:]]
    out_shape = (
        jax.ShapeDtypeStruct((t, MOBA_WIDTH), BF16),
        jax.ShapeDtypeStruct((t, MOBA_WIDTH), BF16),
        jax.ShapeDtypeStruct((t // MOBA_BLOCK, MOBA_WIDTH, MOBA_BLOCK), BF16),
        jax.ShapeDtypeStruct((batch * nb, MOBA_WIDTH), F32),
        jax.ShapeDtypeStruct((t, GLA_KEY_WIDTH), F32),
        jax.ShapeDtypeStruct((t, GLA_KEY_WIDTH), F32),
        jax.ShapeDtypeStruct((t, GLA_VAL_WIDTH), BF16),
        jax.ShapeDtypeStruct((t, GLA_VAL_WIDTH), F32),
        jax.ShapeDtypeStruct((t, GLA_KEY_WIDTH), F32),
    )
    out_specs = (
        pl.BlockSpec((tm, MOBA_WIDTH), row),
        pl.BlockSpec((tm, MOBA_WIDTH), row),
        pl.BlockSpec((tm // MOBA_BLOCK, MOBA_WIDTH, MOBA_BLOCK), lambda b, i: (b * steps + i, 0, 0)),
        pl.BlockSpec((nb, MOBA_WIDTH), lambda b, i: (b, 0)),
        pl.BlockSpec((tm, GLA_KEY_WIDTH), row),
        pl.BlockSpec((tm, GLA_KEY_WIDTH), row),
        pl.BlockSpec((tm, GLA_VAL_WIDTH), row),
        pl.BlockSpec((tm, GLA_VAL_WIDTH), row),
        pl.BlockSpec((tm, GLA_KEY_WIDTH), row),
    )
    return pl.pallas_call(
        _inproj_kernel, grid=(batch, steps), in_specs=in_specs, out_specs=out_specs, out_shape=out_shape,
        compiler_params=_params("arbitrary", "arbitrary"), name="inproj",
    )(*args)


def _moba_kernel(slopes_ref, q_ref, k_ref, vt_ref, km_ref, o_ref, kaug_ref, s_ref):
    hg = pl.program_id(1)
    step = pl.program_id(2)
    blk = MOBA_BLOCK
    nb = km_ref.shape[0]
    hw = 2 * HEAD_DIM
    heads = q_ref.shape[1] // HEAD_DIM
    n_q = q_ref.shape[0] // blk
    feat_rows = nb + 8
    head_rows = lambda g: slice(g * HEAD_DIM, (g + 1) * HEAD_DIM)
    hs = range(heads)

    @pl.when(step == 0)
    def _():
        lane = lax.broadcasted_iota(jnp.int32, (blk, hw), 1)
        koff = lax.broadcasted_iota(jnp.int32, (blk, hw), 0).astype(F32)
        for g in hs:
            tile, hh = divmod(g, 2)
            slope = slopes_ref[hg * heads + g]
            base = HEAD_DIM * (1 - hh)
            for jb in range(nb):
                feat = jnp.where(lane == base + nb, slope * koff, 0.0)
                feat = jnp.where((lane == base + jb) | (lane == base + nb + 1), 1.0, feat)
                kb = k_ref[jb * blk:(jb + 1) * blk, tile * hw:(tile + 1) * hw]
                kaug_ref[g, jb * blk:(jb + 1) * blk, :] = jnp.where(lane // HEAD_DIM == hh, kb, feat.astype(BF16))

    kidx = lax.broadcasted_iota(jnp.int32, (blk, blk), 0)
    qidx = lax.broadcasted_iota(jnp.int32, (blk, blk), 1)
    causal = kidx <= qidx
    nidx = lax.broadcasted_iota(jnp.int32, (nb, blk), 0)
    r8 = lax.broadcasted_iota(jnp.int32, (8, blk), 0)
    qoff = lax.broadcasted_iota(jnp.int32, (8, blk), 1).astype(F32)
    km_lane = lax.broadcasted_iota(jnp.int32, km_ref.shape, 1)
    km = km_ref[...]
    slopes = [slopes_ref[hg * heads + g] for g in hs]
    ones = jnp.ones((16, blk), BF16)
    fill = jnp.zeros((HEAD_DIM - feat_rows, blk), F32)

    def values_and_ones(vt_blk, g):
        return jnp.concatenate([vt_blk[head_rows(g), :], ones], axis=0)

    def scores(q_past, pair, buf, slot):
        rows = pl.ds(pl.multiple_of(pair * 2 * blk, 2 * blk), 2 * blk)
        tops = []
        for g in hs:
            s = _dot(kaug_ref[g, rows, :], q_past[g])
            s_ref[buf, slot, g] = s
            tops.append(_col_reduce(s, jnp.maximum, jnp.max))
        return tops

    def absorb(pair, buf, slot, tops, carry):
        vt0 = vt_ref[2 * pair]
        vt1 = vt_ref[2 * pair + 1]
        out = []
        for g in hs:
            m_i, acc = carry[2 * g:2 * g + 2]
            m_new = jnp.maximum(m_i, tops[g])
            p = jnp.exp((s_ref[buf, slot, g] - m_new).astype(BF16))
            acc = (jnp.exp(m_i - m_new) * acc + _dot(values_and_ones(vt0, g), p[:blk])
                   + _dot(values_and_ones(vt1, g), p[blk:]))
            out += [m_new, acc]
        return out

    def prologue(i):
        c = step * n_q + i
        qt = q_ref[i * blk:(i + 1) * blk, :].astype(F32).T
        qt_b = qt.astype(BF16)
        gates = [jnp.where(nidx < c, _dot(jnp.where(km_lane // HEAD_DIM == g, km, 0.0).astype(BF16), qt_b), -jnp.inf)
                 for g in hs]
        ranks = [jnp.zeros((nb, blk), jnp.int32) for _ in hs]
        for m in range(nb - 1):
            for g in hs:
                gm = gates[g][m:m + 1, :]
                beats = (gm > gates[g]) | ((gm == gates[g]) & (m < nidx))
                ranks[g] = ranks[g] + beats.astype(jnp.int32)

        def q_aug(g, bias):
            dist0 = -slopes[g] * ((c - nidx) * blk).astype(F32)
            tail = jnp.where(r8 == 0, 1.0, jnp.where(r8 == 1, -slopes[g] * qoff, 0.0))
            feats = [jnp.where(bias, dist0, MASKED), tail, fill]
            qh = qt[head_rows(g), :]
            return jnp.concatenate([qh] + feats if g % 2 == 0 else feats + [qh], axis=0).astype(BF16)

        q_past = [q_aug(g, (nidx < c) & (ranks[g] < MOBA_TOPK)) for g in hs]
        own_rows = pl.ds(pl.multiple_of(c * blk, blk), blk)
        vt_own = vt_ref[c]
        q_own = [q_aug(g, nidx == c) for g in hs]
        s_own = [jnp.where(causal, _dot(kaug_ref[g, own_rows, :], q_own[g]), MASKED) for g in hs]
        m_own = [jnp.max(s, axis=0, keepdims=True) for s in s_own]
        p_own = [jnp.exp((s_own[g] - m_own[g]).astype(BF16)) for g in hs]
        state = []
        for g in hs:
            state += [m_own[g], _dot(values_and_ones(vt_own, g), p_own[g])]
        return c, q_past, state + scores(q_past, 0, i, 0)

    def attend(i, c, q_past, carry):
        n_state = 2 * heads

        def advance(n, carry, slot):
            tops = scores(q_past, n + 1, i, 1 - slot)
            return tuple(absorb(n, i, slot, carry[n_state:], carry[:n_state]) + tops)

        def body(n, carry):
            return lax.cond(n % 2 == 0, lambda: advance(n, carry, 0), lambda: advance(n, carry, 1))

        last = jnp.maximum((c + 1) // 2 - 1, 0)
        carry = lax.fori_loop(0, last, body, tuple(carry))
        state = lax.cond(last % 2 == 0, lambda: tuple(absorb(last, i, 0, carry[n_state:], carry[:n_state])),
                         lambda: tuple(absorb(last, i, 1, carry[n_state:], carry[:n_state])))
        out_t = jnp.concatenate([a[:HEAD_DIM] / a[HEAD_DIM:HEAD_DIM + 1] for a in state[1::2]], axis=0)
        o_ref[i * blk:(i + 1) * blk, :] = out_t.T.astype(o_ref.dtype)

    started = [prologue(i) for i in range(n_q)]
    for i, (c, q_past, carry) in enumerate(started):
        attend(i, c, q_past, carry)


def _alibi_slopes():
    slopes = np.exp2(-ALIBI_MAX * np.arange(1, MOBA_HEADS + 1, dtype=np.float32) / MOBA_HEADS).astype(np.float32)
    worst = slopes * np.float32(MOBA_BLOCK - 1)
    assert np.all(worst.astype(jnp.bfloat16).astype(np.float32) == worst), "ALiBi slopes not bf16-exact"
    return slopes


def _moba(qa, ka, vt, kmean, *, batch, seq):
    t = qa.shape[0]
    nb = seq // MOBA_BLOCK
    gw = MOBA_GROUP * HEAD_DIM
    assert nb % 8 == 0 and nb + 8 <= HEAD_DIM, "bias features must fit the spare contraction lanes"
    grid_spec = pltpu.PrefetchScalarGridSpec(
        num_scalar_prefetch=1,
        grid=(batch, MOBA_HEADS // MOBA_GROUP, nb // MOBA_QBLOCKS),
        in_specs=[
            pl.BlockSpec((MOBA_QBLOCKS * MOBA_BLOCK, gw), lambda b, h, c, s: (b * (nb // MOBA_QBLOCKS) + c, h)),
            pl.BlockSpec((seq, gw), lambda b, h, c, s: (b, h)),
            pl.BlockSpec((nb, gw, MOBA_BLOCK), lambda b, h, c, s: (b, h, 0)),
            pl.BlockSpec((nb, gw), lambda b, h, c, s: (b, h)),
        ],
        out_specs=pl.BlockSpec((MOBA_QBLOCKS * MOBA_BLOCK, gw), lambda b, h, c, s: (b * (nb // MOBA_QBLOCKS) + c, h)),
        scratch_shapes=[pltpu.VMEM((MOBA_GROUP, seq, 2 * HEAD_DIM), BF16),
                        pltpu.VMEM((MOBA_QBLOCKS, 2, MOBA_GROUP, 2 * MOBA_BLOCK, MOBA_BLOCK), F32)],
    )
    return pl.pallas_call(
        _moba_kernel, grid_spec=grid_spec, out_shape=jax.ShapeDtypeStruct((t, MOBA_WIDTH), BF16),
        compiler_params=_params("arbitrary", "arbitrary", "arbitrary"), name="moba",
    )(jnp.asarray(_alibi_slopes()), qa, ka, vt, kmean)


def _gla_kernel(q_ref, k_ref, v_ref, r_ref, la_ref, gn_ref, o_ref, st_ref):
    @pl.when(pl.program_id(1) == 0)
    def _():
        st_ref[...] = jnp.zeros_like(st_ref)

    ch = GLA_CHUNK
    n_seqs, tc = q_ref.shape[:2]
    n_chunks = tc // ch
    row = lax.broadcasted_iota(jnp.int32, (tc, tc), 0)
    col = lax.broadcasted_iota(jnp.int32, (tc, tc), 1)
    causal = (row // ch == col // ch) & (row >= col)
    tri = jnp.where(causal, 1.0, 0.0).astype(BF16)
    lane = lax.broadcasted_iota(jnp.int32, (1, GLA_KEY_WIDTH), 1)
    gn = gn_ref[...]
    seqs = range(n_seqs)
    heads = range(GLA_HEADS)
    gs = []
    for s in seqs:
        la = la_ref[s]
        la_hi = la.astype(BF16)
        rest = la - la_hi.astype(F32)
        la_mid = rest.astype(BF16)
        la_lo = (rest - la_mid.astype(F32)).astype(BF16)
        gs.append(_dot(tri, la_hi) + _dot(tri, la_mid) + _dot(tri, la_lo))
    ends = [[g[(ci + 1) * ch - 1:(ci + 1) * ch, :] for ci in range(n_chunks)] for g in gs]
    qhs, kds, kths, vhs = [], [], [], []
    for s in seqs:
        g = gs[s]
        g_last = jnp.concatenate([jnp.broadcast_to(e, (ch, GLA_KEY_WIDTH)) for e in ends[s]], axis=0)
        qd = q_ref[s] * (GLA_DK ** -0.5) * jnp.exp(g)
        k = k_ref[s]
        kds.append((k * jnp.exp(-g)).astype(BF16))
        kt = k * jnp.exp(g_last - g)
        qhs.append([jnp.where(lane // GLA_DK == h, qd, 0.0).astype(BF16) for h in heads])
        kths.append([jnp.where(lane // GLA_DK == h, kt, 0.0).astype(BF16) for h in heads])
        vhs.append([v_ref[s, :, h * GLA_DV:(h + 1) * GLA_DV] for h in heads])
    intra = [[None] * GLA_HEADS for _ in seqs]
    for h in heads:
        for s in seqs:
            a = jnp.where(causal, _dot(qhs[s][h], kds[s], _NT), 0.0)
            intra[s][h] = _dot(a.astype(BF16), vhs[s][h])
    states = [st_ref[s] for s in seqs]
    inter = [[[] for _ in heads] for _ in seqs]
    for ci in range(n_chunks):
        rows = slice(ci * ch, (ci + 1) * ch)
        for s in seqs:
            state_b = states[s].astype(BF16)
            new_state = states[s] * jnp.exp(ends[s][ci])
            for h in heads:
                inter[s][h].append(_dot(qhs[s][h][rows], state_b, _NT))
                new_state = new_state + _dot(vhs[s][h][rows], kths[s][h][rows], _TN)
            states[s] = new_state
    for s in seqs:
        st_ref[s] = states[s]
    for h in heads:
        for s in seqs:
            o = intra[s][h] + jnp.concatenate(inter[s][h], axis=0)
            rh = r_ref[s, :, h * GLA_DV:(h + 1) * GLA_DV]
            o_ref[s, :, h * GLA_DV:(h + 1) * GLA_DV] = (_rms(o, gn) * (rh * jax.nn.sigmoid(rh))).astype(o_ref.dtype)


def _gla(qg, kg, vg, rg, la, g_norm, *, batch, seq, tc=256):
    n_seqs = GLA_SEQS if batch % GLA_SEQS == 0 else 1
    by_seq = lambda a: a.reshape(batch, seq, a.shape[-1])
    blk = lambda i, j: (i, j, 0)
    kspec = pl.BlockSpec((n_seqs, tc, GLA_KEY_WIDTH), blk)
    vspec = pl.BlockSpec((n_seqs, tc, GLA_VAL_WIDTH), blk)
    out = pl.pallas_call(
        _gla_kernel, grid=(batch // n_seqs, seq // tc),
        in_specs=[kspec, kspec, vspec, vspec, kspec, pl.BlockSpec((1, GLA_DV), lambda i, j: (0, 0))],
        out_specs=vspec, out_shape=jax.ShapeDtypeStruct((batch, seq, GLA_VAL_WIDTH), BF16),
        scratch_shapes=[pltpu.VMEM((n_seqs, GLA_DV, GLA_KEY_WIDTH), F32)],
        compiler_params=_params("arbitrary", "arbitrary"), name="gla",
    )(by_seq(qg), by_seq(kg), by_seq(vg), by_seq(rg), by_seq(la), g_norm.reshape(1, GLA_DV))
    return out.reshape(batch * seq, GLA_VAL_WIDTH)


def _outproj_kernel(x_ref, om_ref, og_ref, wm_ref, wg_ref, gf_ref, wr_ref, br_ref, h_ref, f_ref, lg_ref):
    h = x_ref[...] + _dot(om_ref[...], wm_ref[...]) + _dot(og_ref[...], wg_ref[...])
    h_ref[...] = h
    f = _rms(h, gf_ref[...]).astype(BF16)
    f_ref[...] = f
    lg_ref[...] = _dot(wr_ref[...], f, _NT) + br_ref[...]


def _outproj(x2d, o_moba, o_gla, w_out, g_ffn, w_router, b_router, *, tm=1024):
    t, d = x2d.shape
    wm = w_out[:MOBA_WIDTH].astype(BF16)
    wg = w_out[MOBA_WIDTH:].astype(BF16)
    wr = w_router.T.astype(BF16)
    br = b_router.reshape(N_EXPERTS, 1)
    row = lambda i: (i, 0)
    const = lambda i: (0, 0)
    full = lambda a: pl.BlockSpec(a.shape, const)
    args = (x2d, o_moba, o_gla, wm, wg, g_ffn.reshape(1, d), wr, br)
    in_specs = [pl.BlockSpec((tm, d), row), pl.BlockSpec((tm, MOBA_WIDTH), row),
                pl.BlockSpec((tm, GLA_VAL_WIDTH), row)] + [full(a) for a in args[3:]]
    return pl.pallas_call(
        _outproj_kernel, grid=(t // tm,), in_specs=in_specs,
        out_specs=(pl.BlockSpec((tm, d), row), pl.BlockSpec((tm, d), row),
                   pl.BlockSpec((N_EXPERTS, tm), lambda i: (0, i))),
        out_shape=(jax.ShapeDtypeStruct((t, d), F32), jax.ShapeDtypeStruct((t, d), BF16),
                   jax.ShapeDtypeStruct((N_EXPERTS, t), F32)),
        compiler_params=_params("arbitrary"), name="outproj",
    )(*args)


PAIR_CHUNK = 2 * LANES


def _moe_kernel(be_ref, nused_ref, x_ref, w1_ref, b1_ref, w2_ref, b2_ref, o_ref, w1s_ref, w2s_ref):
    i = pl.program_id(0)
    f = w2_ref.shape[1]

    @pl.when((i == 0) | (be_ref[i] != be_ref[jnp.maximum(i - 1, 0)]))
    def _():
        r = lax.broadcasted_iota(jnp.int32, (PAIR_CHUNK, PAIR_CHUNK), 0)
        c = lax.broadcasted_iota(jnp.int32, (PAIR_CHUNK, PAIR_CHUNK), 1)
        src_col = jnp.where(c < LANES, 2 * c, 2 * (c - LANES) + 1)
        perm = jnp.where(r == src_col, 1.0, 0.0).astype(BF16)
        for cc in range(2 * f // PAIR_CHUNK):
            chunk = w1_ref[0, :, cc * PAIR_CHUNK:(cc + 1) * PAIR_CHUNK].astype(BF16)
            res = _dot(chunk, perm).astype(BF16)
            w1s_ref[:, cc * LANES:(cc + 1) * LANES] = res[:, :LANES]
            w1s_ref[:, f + cc * LANES:f + (cc + 1) * LANES] = res[:, LANES:]
        w2s_ref[...] = w2_ref[0].astype(BF16)

    @pl.when(i < nused_ref[0])
    def _():
        hid = _dot(x_ref[...], w1s_ref[...]) + b1_ref[0]
        glu = jnp.minimum(hid[:, :f], SWIGLU_LIMIT)
        lin = jnp.clip(hid[:, f:], -SWIGLU_LIMIT, SWIGLU_LIMIT)
        act = glu * jax.nn.sigmoid(SWIGLU_ALPHA * glu) * (lin + 1.0)
        o_ref[...] = (_dot(act.astype(BF16), w2s_ref[...]) + b2_ref[0]).astype(o_ref.dtype)

    @pl.when(i >= nused_ref[0])
    def _():
        o_ref[...] = jnp.zeros_like(o_ref)


def _moe(x_rows, block_expert, n_used, w1, b1, w2, b2, *, tm):
    n_rows, d = x_rows.shape
    e, _, f2 = w1.shape
    f = f2 // 2
    b1p = jnp.concatenate([b1[:, 0::2], b1[:, 1::2]], axis=-1).reshape(e, 1, f2)
    b2r = b2.reshape(e, 1, d)
    grid_spec = pltpu.PrefetchScalarGridSpec(
        num_scalar_prefetch=2,
        grid=(n_rows // tm,),
        in_specs=[
            pl.BlockSpec((tm, d), lambda i, be, nu: (i, 0)),
            pl.BlockSpec((1, d, f2), lambda i, be, nu: (be[i], 0, 0)),
            pl.BlockSpec((1, 1, f2), lambda i, be, nu: (be[i], 0, 0)),
            pl.BlockSpec((1, f, d), lambda i, be, nu: (be[i], 0, 0)),
            pl.BlockSpec((1, 1, d), lambda i, be, nu: (be[i], 0, 0)),
        ],
        out_specs=pl.BlockSpec((tm, d), lambda i, be, nu: (i, 0)),
        scratch_shapes=[pltpu.VMEM((d, f2), BF16), pltpu.VMEM((f, d), BF16)],
    )
    return pl.pallas_call(
        _moe_kernel, grid_spec=grid_spec, out_shape=jax.ShapeDtypeStruct((n_rows, d), BF16),
        compiler_params=_params("arbitrary"), name="moe",
    )(block_expert, n_used, x_rows, w1, b1p, w2, b2r)


ROUTE_CHUNK = 2 * LANES


def _router_kernel(lg_ref, idx_ref, gate_ref, rank_ref, cnt_ref, carry_ref):
    @pl.when(pl.program_id(0) == 0)
    def _():
        carry_ref[...] = jnp.zeros_like(carry_ref)

    x = lg_ref[...]
    ne, tt = x.shape
    eidx = lax.broadcasted_iota(jnp.int32, (ne, tt), 0)
    member = jnp.zeros((ne, tt), F32)
    vals, ids = [], []
    for _ in range(TOP_K):
        m = jnp.max(x, axis=0, keepdims=True)
        sel = jnp.min(jnp.where(x == m, eidx, ne), axis=0, keepdims=True)
        hit = eidx == sel
        member = jnp.where(hit, 1.0, member)
        x = jnp.where(hit, -jnp.inf, x)
        vals.append(m)
        ids.append(sel)
    e = jnp.exp(jnp.concatenate(vals, axis=0) - vals[0])
    gate_ref[...] = e / jnp.sum(e, axis=0, keepdims=True)
    idx_ref[...] = jnp.concatenate(ids, axis=0)
    r = lax.broadcasted_iota(jnp.int32, (ROUTE_CHUNK, ROUTE_CHUNK), 0)
    c = lax.broadcasted_iota(jnp.int32, (ROUTE_CHUNK, ROUTE_CHUNK), 1)
    earlier = jnp.where(r < c, 1.0, 0.0).astype(BF16)
    carry = carry_ref[:, :1]
    before = []
    for ch in range(tt // ROUTE_CHUNK):
        mc = member[:, ch * ROUTE_CHUNK:(ch + 1) * ROUTE_CHUNK]
        before.append(_dot(mc.astype(BF16), earlier) + carry)
        carry = carry + jnp.sum(mc, axis=1, keepdims=True)
    before = jnp.concatenate(before, axis=1)
    ranks = [jnp.sum(jnp.where(eidx == ids[k], before, 0.0), axis=0, keepdims=True) for k in range(TOP_K)]
    rank_ref[...] = jnp.concatenate(ranks, axis=0).astype(jnp.int32)
    carry_ref[...] = jnp.broadcast_to(carry, carry_ref.shape)
    cnt_ref[...] = jnp.broadcast_to(carry, cnt_ref.shape).astype(jnp.int32)


def _router(logits_t, *, tt=2048):
    ne, t = logits_t.shape
    tok = lambda i: (0, i)
    kspec = pl.BlockSpec((TOP_K, tt), tok)
    return pl.pallas_call(
        _router_kernel, grid=(t // tt,), in_specs=[pl.BlockSpec((ne, tt), tok)],
        out_specs=(kspec, kspec, kspec, pl.BlockSpec((ne, LANES), lambda i: (0, 0))),
        out_shape=(jax.ShapeDtypeStruct((TOP_K, t), jnp.int32), jax.ShapeDtypeStruct((TOP_K, t), F32),
                   jax.ShapeDtypeStruct((TOP_K, t), jnp.int32), jax.ShapeDtypeStruct((ne, LANES), jnp.int32)),
        scratch_shapes=[pltpu.VMEM((ne, LANES), F32)],
        compiler_params=_params("arbitrary"), name="router",
    )(logits_t)


def _route(logits_t, *, tm):
    ne, t = logits_t.shape
    idx, gates, rank, cnt = _router(logits_t)
    counts = cnt[:, 0]
    padded = (counts + tm - 1) // tm * tm
    pad_end = jnp.cumsum(padded)
    pad_start = pad_end - padded
    experts = jnp.arange(ne, dtype=jnp.int32).reshape(ne, 1, 1)
    slot = jnp.sum(jnp.where(idx[None] == experts, pad_start.reshape(ne, 1, 1), 0), axis=0) + rank
    n_blocks = -(-TOP_K * t // tm) + ne
    flat_slot = slot.reshape(-1)
    tok = jnp.broadcast_to(jnp.arange(t, dtype=jnp.int32), (TOP_K, t)).reshape(-1)
    row_tok = (jnp.arange(n_blocks * tm, dtype=jnp.int32) % t).at[flat_slot].add(
        tok - flat_slot % t, unique_indices=True, mode="promise_in_bounds")
    n_used = pad_end[-1] // tm
    blk_start = jnp.minimum(jnp.arange(n_blocks, dtype=jnp.int32), n_used - 1) * tm
    block_expert = jnp.minimum(jnp.sum(pad_end[:, None] <= blk_start[None, :], axis=0), ne - 1)
    return gates, row_tok, slot, block_expert.astype(jnp.int32), n_used.reshape(1).astype(jnp.int32)


def _final_kernel(h_ref, y_ref, gate_ref, p_ref, wp_ref, gp_ref, wgate_ref, gfin_ref, o_ref, *, last_layer):
    h = h_ref[...]
    g = gate_ref[...]
    gates = jnp.concatenate([g, jnp.zeros((8 - TOP_K, g.shape[1]), F32)], axis=0).T
    for k in range(TOP_K):
        h = h + gates[:, k:k + 1] * y_ref[k].astype(F32)
    ple = _rms(_dot(p_ref[...].astype(BF16), wp_ref[...]), gp_ref[...])
    h = h + jax.nn.sigmoid(_dot(h.astype(BF16), wgate_ref[...])) * ple
    o_ref[...] = _rms(h, gfin_ref[...]) if last_layer else h


def _final(h1, y_tok, gates, p2d, w_ple_proj, g_ple, w_ple_gate, g_final, *, last_layer, tm=1024):
    t, d = h1.shape
    row = lambda i: (i, 0)
    const = lambda i: (0, 0)
    full = lambda a: pl.BlockSpec(a.shape, const)
    args = (h1, y_tok, gates, p2d, w_ple_proj.astype(BF16), g_ple.reshape(1, d), w_ple_gate.astype(BF16),
            g_final.reshape(1, d))
    in_specs = [pl.BlockSpec((tm, d), row), pl.BlockSpec((TOP_K, tm, d), lambda i: (0, i, 0)),
                pl.BlockSpec((TOP_K, tm), lambda i: (0, i)), pl.BlockSpec((tm, p2d.shape[1]), row)]
    in_specs += [full(a) for a in args[4:]]
    return pl.pallas_call(
        functools.partial(_final_kernel, last_layer=last_layer), grid=(t // tm,), in_specs=in_specs,
        out_specs=pl.BlockSpec((tm, d), row),
        out_shape=jax.ShapeDtypeStruct((t, d), F32),
        compiler_params=_params("arbitrary"), name="final",
    )(*args)


def kernel(x, p, g_mix, w_in, w_gla_gate, b_gla_gate, g_gla_norm, w_out, g_ffn, w_router, b_router,
           w1, b1, w2, b2, w_ple_gate, w_ple_proj, g_ple, g_final, *, moe_tm=512):
    batch, seq, d = x.shape
    depth = p.shape[0]
    t = batch * seq
    h = x.reshape(t, d)
    for i in range(depth):
        qa, ka, vt, kmean, qg, kg, vg, rg, la = _inproj(h, g_mix[i], w_in[i], w_gla_gate[i], b_gla_gate[i],
                                                        batch=batch, seq=seq)
        o_moba = _moba(qa, ka, vt, kmean, batch=batch, seq=seq)
        o_gla = _gla(qg, kg, vg, rg, la, g_gla_norm[i], batch=batch, seq=seq)
        h1, f, logits = _outproj(h, o_moba, o_gla, w_out[i], g_ffn[i], w_router[i], b_router[i])
        gates, row_tok, slot, block_expert, n_used = _route(logits, tm=moe_tm)
        x_rows = jnp.take(f, row_tok, axis=0, mode="clip")
        y = _moe(x_rows, block_expert, n_used, w1[i], b1[i], w2[i], b2[i], tm=moe_tm)
        y_tok = jnp.take(y, slot, axis=0, mode="clip")
        h = _final(h1, y_tok, gates, p[i].reshape(t, -1), w_ple_proj[i], g_ple[i], w_ple_gate[i],
                   g_final, last_layer=i == depth - 1)
    return h.reshape(batch, seq, d)
```

```python
import functools

import jax
import jax.numpy as jnp
import numpy as np
from jax import lax
from jax.experimental import pallas as pl
from jax.experimental.pallas import tpu as pltpu

HEAD_DIM = 64
MOBA_HEADS = 8
MOBA_WIDTH = MOBA_HEADS * HEAD_DIM
MOBA_BLOCK = 256
MOBA_TOPK = 3
MOBA_GROUP = 4
MOBA_QBLOCKS = 4
ALIBI_MAX = 8.0
GLA_HEADS = 4
GLA_DK = 64
GLA_DV = 128
GLA_KEY_WIDTH = GLA_HEADS * GLA_DK
GLA_VAL_WIDTH = GLA_HEADS * GLA_DV
GLA_GATE_RANK = 16
GLA_GATE_TAU = 16.0
GLA_CHUNK = 64
GLA_SEQS = 4
N_EXPERTS = 32
TOP_K = 4
SWIGLU_ALPHA = 1.702
SWIGLU_LIMIT = 7.0
EPS = 1e-6

LANES = 128
SUBLANES = 8
BF16_SUBLANES = 16

INPROJ_ROWS = 512
OUTPROJ_ROWS = 1024
FINAL_ROWS = 1024
GLA_TILE = 256
ROUTER_TOKENS = 2048
MOE_ROWS = 512
MASKED = -1e30
VMEM_LIMIT = 56 * 1024 * 1024

BF16 = jnp.bfloat16
F32 = jnp.float32

_NT = (((1,), (1,)), ((), ()))
_TN = (((0,), (0,)), ((), ()))


def _dot(a, b, dims=None):
    if dims is None:
        return jnp.dot(a, b, preferred_element_type=F32)
    return lax.dot_general(a, b, dims, preferred_element_type=F32)


def _rms(x, g):
    return x * lax.rsqrt(jnp.mean(x * x, axis=-1, keepdims=True) + EPS) * g


def _col_reduce(x, pair_op, final_op, rows=64):
    parts = [x[i:i + rows] for i in range(0, x.shape[0], rows)]
    while len(parts) > 1:
        parts = [pair_op(parts[i], parts[i + 1]) for i in range(0, len(parts), 2)]
    return final_op(parts[0], axis=0, keepdims=True)


def _params(*sem):
    return pltpu.CompilerParams(dimension_semantics=sem, vmem_limit_bytes=VMEM_LIMIT)


def _inproj_kernel(x_ref, g_ref, wq_ref, wk_ref, wvt_ref, wgla_ref, wag_ref, wgate_ref, bgate_ref,
                   qa_ref, ka_ref, vt_ref, km_ref, qg_ref, kg_ref, vg_ref, rg_ref, la_ref):
    i = pl.program_id(1)
    tm = x_ref.shape[0]
    nblk = tm // MOBA_BLOCK
    @pl.when(i == 0)
    def _():
        km_ref[...] = jnp.zeros_like(km_ref)

    ab = _rms(x_ref[...], g_ref[...]).astype(BF16)
    ag = _dot(ab, wag_ref[...]).astype(BF16)
    z = _dot(ag, wgate_ref[...]) + bgate_ref[...]
    la_ref[...] = jax.nn.log_sigmoid(z) / GLA_GATE_TAU
    qa_ref[...] = (_dot(ab, wq_ref[...]) * (HEAD_DIM ** -0.5)).astype(BF16)
    ka = _dot(ab, wk_ref[...])
    ka_ref[...] = ka.astype(BF16)
    km = km_ref[...]
    blk_row = lax.broadcasted_iota(jnp.int32, km.shape, 0)
    for j in range(nblk):
        mean_j = jnp.mean(ka[j * MOBA_BLOCK:(j + 1) * MOBA_BLOCK], axis=0, keepdims=True)
        km = jnp.where(blk_row == i * nblk + j, mean_j, km)
    km_ref[...] = km
    vt = _dot(wvt_ref[...], ab, _NT).astype(BF16)
    for j in range(nblk):
        vt_ref[j] = vt[:, j * MOBA_BLOCK:(j + 1) * MOBA_BLOCK]
    gla = _dot(ab, wgla_ref[...])
    o = 0
    for ref in (qg_ref, kg_ref, vg_ref, rg_ref):
        w = ref.shape[1]
        ref[...] = gla[:, o:o + w].astype(ref.dtype)
        o += w


def _inproj(x2d, g_mix, w_in, w_gla_gate, b_gla_gate, *, batch, seq, tm=INPROJ_ROWS):
    t, d = x2d.shape
    nb = seq // MOBA_BLOCK
    o_k, o_v, o_qg = MOBA_WIDTH, 2 * MOBA_WIDTH, 3 * MOBA_WIDTH
    o_ag = o_qg + 2 * GLA_KEY_WIDTH + 2 * GLA_VAL_WIDTH
    wq = w_in[:, :o_k].astype(BF16)
    wk = w_in[:, o_k:o_v].astype(BF16)
    wvt = w_in[:, o_v:o_qg].T.astype(BF16)
    wgla = w_in[:, o_qg:o_ag].astype(BF16)
    wag = jnp.pad(w_in[:, o_ag:], ((0, 0), (0, LANES - GLA_GATE_RANK))).astype(BF16)
    wgate = jnp.pad(w_gla_gate, ((0, LANES - GLA_GATE_RANK), (0, 0))).astype(BF16)
    steps = seq // tm
    row = lambda b, i: (b * steps + i, 0)
    const = lambda b, i: (0, 0)
    full = lambda a: pl.BlockSpec(a.shape, const)
    args = (x2d, g_mix.reshape(1, d), wq, wk, wvt, wgla, wag, wgate, b_gla_gate.reshape(1, -1))
    in_specs = [pl.BlockSpec((tm, d), row)] + [full(a) for a in args[1:]]
    out_shape = (
        jax.ShapeDtypeStruct((t, MOBA_WIDTH), BF16),
        jax.ShapeDtypeStruct((t, MOBA_WIDTH), BF16),
        jax.ShapeDtypeStruct((t // MOBA_BLOCK, MOBA_WIDTH, MOBA_BLOCK), BF16),
        jax.ShapeDtypeStruct((batch * nb, MOBA_WIDTH), F32),
        jax.ShapeDtypeStruct((t, GLA_KEY_WIDTH), F32),
        jax.ShapeDtypeStruct((t, GLA_KEY_WIDTH), F32),
        jax.ShapeDtypeStruct((t, GLA_VAL_WIDTH), BF16),
        jax.ShapeDtypeStruct((t, GLA_VAL_WIDTH), F32),
        jax.ShapeDtypeStruct((t, GLA_KEY_WIDTH), F32),
    )
    out_specs = (
        pl.BlockSpec((tm, MOBA_WIDTH), row),
        pl.BlockSpec((tm, MOBA_WIDTH), row),
        pl.BlockSpec((tm // MOBA_BLOCK, MOBA_WIDTH, MOBA_BLOCK), lambda b, i: (b * steps + i, 0, 0)),
        pl.BlockSpec((nb, MOBA_WIDTH), lambda b, i: (b, 0)),
        pl.BlockSpec((tm, GLA_KEY_WIDTH), row),
        pl.BlockSpec((tm, GLA_KEY_WIDTH), row),
        pl.BlockSpec((tm, GLA_VAL_WIDTH), row),
        pl.BlockSpec((tm, GLA_VAL_WIDTH), row),
        pl.BlockSpec((tm, GLA_KEY_WIDTH), row),
    )
    return pl.pallas_call(
        _inproj_kernel, grid=(batch, steps), in_specs=in_specs, out_specs=out_specs, out_shape=out_shape,
        compiler_params=_params("arbitrary", "arbitrary"), name="inproj",
    )(*args)


def _moba_kernel(slopes_ref, q_ref, k_ref, vt_ref, km_ref, o_ref, kaug_ref, s_ref):
    hg = pl.program_id(1)
    step = pl.program_id(2)
    blk = MOBA_BLOCK
    nb = km_ref.shape[0]
    hw = 2 * HEAD_DIM
    heads = q_ref.shape[1] // HEAD_DIM
    n_q = q_ref.shape[0] // blk
    feat_rows = nb + SUBLANES
    head_rows = lambda g: slice(g * HEAD_DIM, (g + 1) * HEAD_DIM)
    hs = range(heads)

    @pl.when(step == 0)
    def _():
        lane = lax.broadcasted_iota(jnp.int32, (blk, hw), 1)
        koff = lax.broadcasted_iota(jnp.int32, (blk, hw), 0).astype(F32)
        for g in hs:
            tile, hh = divmod(g, 2)
            slope = slopes_ref[hg * heads + g]
            base = HEAD_DIM * (1 - hh)
            for jb in range(nb):
                feat = jnp.where(lane == base + nb, slope * koff, 0.0)
                feat = jnp.where((lane == base + jb) | (lane == base + nb + 1), 1.0, feat)
                kb = k_ref[jb * blk:(jb + 1) * blk, tile * hw:(tile + 1) * hw]
                kaug_ref[g, jb * blk:(jb + 1) * blk, :] = jnp.where(lane // HEAD_DIM == hh, kb, feat.astype(BF16))

    kidx = lax.broadcasted_iota(jnp.int32, (blk, blk), 0)
    qidx = lax.broadcasted_iota(jnp.int32, (blk, blk), 1)
    causal = kidx <= qidx
    nidx = lax.broadcasted_iota(jnp.int32, (nb, blk), 0)
    r8 = lax.broadcasted_iota(jnp.int32, (SUBLANES, blk), 0)
    qoff = lax.broadcasted_iota(jnp.int32, (SUBLANES, blk), 1).astype(F32)
    km_lane = lax.broadcasted_iota(jnp.int32, km_ref.shape, 1)
    km = km_ref[...]
    slopes = [slopes_ref[hg * heads + g] for g in hs]
    ones = jnp.ones((BF16_SUBLANES, blk), BF16)
    fill = jnp.zeros((HEAD_DIM - feat_rows, blk), F32)

    def values_and_ones(vt_blk, g):
        return jnp.concatenate([vt_blk[head_rows(g), :], ones], axis=0)

    def scores(q_past, pair, buf, slot):
        rows = pl.ds(pl.multiple_of(pair * 2 * blk, 2 * blk), 2 * blk)
        tops = []
        for g in hs:
            s = _dot(kaug_ref[g, rows, :], q_past[g])
            s_ref[buf, slot, g] = s
            tops.append(_col_reduce(s, jnp.maximum, jnp.max))
        return tops

    def absorb_head(pair, buf, slot, top, m_i, acc, g):
        m_new = jnp.maximum(m_i, top)
        p = jnp.exp((s_ref[buf, slot, g] - m_new).astype(BF16))
        acc = (jnp.exp(m_i - m_new) * acc + _dot(values_and_ones(vt_ref[2 * pair], g), p[:blk])
               + _dot(values_and_ones(vt_ref[2 * pair + 1], g), p[blk:]))
        return [m_new, acc]

    def absorb(pair, buf, slot, tops, carry):
        out = []
        for g in hs:
            out += absorb_head(pair, buf, slot, tops[g], carry[2 * g], carry[2 * g + 1], g)
        return out

    def prologue(i):
        c = step * n_q + i
        qt = q_ref[i * blk:(i + 1) * blk, :].astype(F32).T
        qt_b = qt.astype(BF16)
        gates = [jnp.where(nidx < c, _dot(jnp.where(km_lane // HEAD_DIM == g, km, 0.0).astype(BF16), qt_b), -jnp.inf)
                 for g in hs]
        ranks = [jnp.zeros((nb, blk), jnp.int32) for _ in hs]
        for m in range(nb - 1):
            for g in hs:
                gm = gates[g][m:m + 1, :]
                beats = (gm > gates[g]) | ((gm == gates[g]) & (m < nidx))
                ranks[g] = ranks[g] + beats.astype(jnp.int32)

        def q_aug(g, bias):
            dist0 = -slopes[g] * ((c - nidx) * blk).astype(F32)
            tail = jnp.where(r8 == 0, 1.0, jnp.where(r8 == 1, -slopes[g] * qoff, 0.0))
            feats = [jnp.where(bias, dist0, MASKED), tail, fill]
            qh = qt[head_rows(g), :]
            return jnp.concatenate([qh] + feats if g % 2 == 0 else feats + [qh], axis=0).astype(BF16)

        q_past = [q_aug(g, (nidx < c) & (ranks[g] < MOBA_TOPK)) for g in hs]
        own_rows = pl.ds(pl.multiple_of(c * blk, blk), blk)
        vt_own = vt_ref[c]
        q_own = [q_aug(g, nidx == c) for g in hs]
        s_own = [jnp.where(causal, _dot(kaug_ref[g, own_rows, :], q_own[g]), MASKED) for g in hs]
        m_own = [jnp.max(s, axis=0, keepdims=True) for s in s_own]
        p_own = [jnp.exp((s_own[g] - m_own[g]).astype(BF16)) for g in hs]
        state = []
        for g in hs:
            state += [m_own[g], _dot(values_and_ones(vt_own, g), p_own[g])]
        return c, q_past, state + scores(q_past, 0, i, 0)

    def attend(i, c, q_past, carry):
        n_state = 2 * heads

        def advance(n, carry, slot):
            tops = scores(q_past, n + 1, i, 1 - slot)
            return tuple(absorb(n, i, slot, carry[n_state:], carry[:n_state]) + tops)

        def body(n, carry):
            return lax.cond(n % 2 == 0, lambda: advance(n, carry, 0), lambda: advance(n, carry, 1))

        last = jnp.maximum((c + 1) // 2 - 1, 0)
        return last, lax.fori_loop(0, last, body, tuple(carry))

    started = [prologue(i) for i in range(n_q)]
    looped = [attend(i, c, q_past, carry) for i, (c, q_past, carry) in enumerate(started)]
    accs = [[None] * heads for _ in looped]
    for g in hs:
        for i, (last, carry) in enumerate(looped):
            accs[i][g] = absorb_head(last, i, last % 2, carry[2 * heads + g], carry[2 * g], carry[2 * g + 1], g)[1]
    for i in range(n_q):
        out_t = jnp.concatenate([a[:HEAD_DIM] / a[HEAD_DIM:HEAD_DIM + 1] for a in accs[i]], axis=0)
        o_ref[i * blk:(i + 1) * blk, :] = out_t.T.astype(o_ref.dtype)


def _alibi_slopes():
    slopes = np.exp2(-ALIBI_MAX * np.arange(1, MOBA_HEADS + 1, dtype=np.float32) / MOBA_HEADS).astype(np.float32)
    worst = slopes * np.float32(MOBA_BLOCK - 1)
    assert np.all(worst.astype(jnp.bfloat16).astype(np.float32) == worst), "ALiBi slopes not bf16-exact"
    return slopes


def _moba(qa, ka, vt, kmean, *, batch, seq):
    t = qa.shape[0]
    nb = seq // MOBA_BLOCK
    gw = MOBA_GROUP * HEAD_DIM
    assert nb % SUBLANES == 0 and nb + SUBLANES <= HEAD_DIM, "bias features must fit the spare contraction lanes"
    grid_spec = pltpu.PrefetchScalarGridSpec(
        num_scalar_prefetch=1,
        grid=(batch, MOBA_HEADS // MOBA_GROUP, nb // MOBA_QBLOCKS),
        in_specs=[
            pl.BlockSpec((MOBA_QBLOCKS * MOBA_BLOCK, gw), lambda b, h, c, s: (b * (nb // MOBA_QBLOCKS) + c, h)),
            pl.BlockSpec((seq, gw), lambda b, h, c, s: (b, h)),
            pl.BlockSpec((nb, gw, MOBA_BLOCK), lambda b, h, c, s: (b, h, 0)),
            pl.BlockSpec((nb, gw), lambda b, h, c, s: (b, h)),
        ],
        out_specs=pl.BlockSpec((MOBA_QBLOCKS * MOBA_BLOCK, gw), lambda b, h, c, s: (b * (nb // MOBA_QBLOCKS) + c, h)),
        scratch_shapes=[pltpu.VMEM((MOBA_GROUP, seq, 2 * HEAD_DIM), BF16),
                        pltpu.VMEM((MOBA_QBLOCKS, 2, MOBA_GROUP, 2 * MOBA_BLOCK, MOBA_BLOCK), F32)],
    )
    return pl.pallas_call(
        _moba_kernel, grid_spec=grid_spec, out_shape=jax.ShapeDtypeStruct((t, MOBA_WIDTH), BF16),
        compiler_params=_params("arbitrary", "arbitrary", "arbitrary"), name="moba",
    )(jnp.asarray(_alibi_slopes()), qa, ka, vt, kmean)


def _gla_kernel(q_ref, k_ref, v_ref, r_ref, la_ref, gn_ref, o_ref, st_ref):
    @pl.when(pl.program_id(1) == 0)
    def _():
        st_ref[...] = jnp.zeros_like(st_ref)

    ch = GLA_CHUNK
    n_seqs, tc = q_ref.shape[:2]
    n_chunks = tc // ch
    row = lax.broadcasted_iota(jnp.int32, (tc, tc), 0)
    col = lax.broadcasted_iota(jnp.int32, (tc, tc), 1)
    causal = (row // ch == col // ch) & (row >= col)
    tri = jnp.where(causal, 1.0, 0.0).astype(BF16)
    lane = lax.broadcasted_iota(jnp.int32, (1, GLA_KEY_WIDTH), 1)
    gn = gn_ref[...]
    seqs = range(n_seqs)
    heads = range(GLA_HEADS)
    gs = []
    for s in seqs:
        la = la_ref[s]
        la_hi = la.astype(BF16)
        rest = la - la_hi.astype(F32)
        la_mid = rest.astype(BF16)
        la_lo = (rest - la_mid.astype(F32)).astype(BF16)
        gs.append(_dot(tri, la_hi) + _dot(tri, la_mid) + _dot(tri, la_lo))
    ends = [[g[(ci + 1) * ch - 1:(ci + 1) * ch, :] for ci in range(n_chunks)] for g in gs]
    qhs, kds, kths, vhs = [], [], [], []
    for s in seqs:
        g = gs[s]
        g_last = jnp.concatenate([jnp.broadcast_to(e, (ch, GLA_KEY_WIDTH)) for e in ends[s]], axis=0)
        qd = q_ref[s] * (GLA_DK ** -0.5) * jnp.exp(g)
        k = k_ref[s]
        kds.append((k * jnp.exp(-g)).astype(BF16))
        kt = k * jnp.exp(g_last - g)
        qhs.append([jnp.where(lane // GLA_DK == h, qd, 0.0).astype(BF16) for h in heads])
        kths.append([jnp.where(lane // GLA_DK == h, kt, 0.0).astype(BF16) for h in heads])
        vhs.append([v_ref[s, :, h * GLA_DV:(h + 1) * GLA_DV] for h in heads])
    intra = [[None] * GLA_HEADS for _ in seqs]
    for h in heads:
        for s in seqs:
            a = jnp.where(causal, _dot(qhs[s][h], kds[s], _NT), 0.0)
            intra[s][h] = _dot(a.astype(BF16), vhs[s][h])
    states = [st_ref[s] for s in seqs]
    inter = [[[] for _ in heads] for _ in seqs]
    for ci in range(n_chunks):
        rows = slice(ci * ch, (ci + 1) * ch)
        for s in seqs:
            state_b = states[s].astype(BF16)
            new_state = states[s] * jnp.exp(ends[s][ci])
            for h in heads:
                inter[s][h].append(_dot(qhs[s][h][rows], state_b, _NT))
                new_state = new_state + _dot(vhs[s][h][rows], kths[s][h][rows], _TN)
            states[s] = new_state
    for s in seqs:
        st_ref[s] = states[s]
    for h in heads:
        for s in seqs:
            o = intra[s][h] + jnp.concatenate(inter[s][h], axis=0)
            rh = r_ref[s, :, h * GLA_DV:(h + 1) * GLA_DV]
            o_ref[s, :, h * GLA_DV:(h + 1) * GLA_DV] = (_rms(o, gn) * (rh * jax.nn.sigmoid(rh))).astype(o_ref.dtype)


def _gla(qg, kg, vg, rg, la, g_norm, *, batch, seq, tc=GLA_TILE):
    n_seqs = GLA_SEQS if batch % GLA_SEQS == 0 else 1
    by_seq = lambda a: a.reshape(batch, seq, a.shape[-1])
    blk = lambda i, j: (i, j, 0)
    kspec = pl.BlockSpec((n_seqs, tc, GLA_KEY_WIDTH), blk)
    vspec = pl.BlockSpec((n_seqs, tc, GLA_VAL_WIDTH), blk)
    out = pl.pallas_call(
        _gla_kernel, grid=(batch // n_seqs, seq // tc),
        in_specs=[kspec, kspec, vspec, vspec, kspec, pl.BlockSpec((1, GLA_DV), lambda i, j: (0, 0))],
        out_specs=vspec, out_shape=jax.ShapeDtypeStruct((batch, seq, GLA_VAL_WIDTH), BF16),
        scratch_shapes=[pltpu.VMEM((n_seqs, GLA_DV, GLA_KEY_WIDTH), F32)],
        compiler_params=_params("arbitrary", "arbitrary"), name="gla",
    )(by_seq(qg), by_seq(kg), by_seq(vg), by_seq(rg), by_seq(la), g_norm.reshape(1, GLA_DV))
    return out.reshape(batch * seq, GLA_VAL_WIDTH)


def _outproj_kernel(x_ref, om_ref, og_ref, wm_ref, wg_ref, gf_ref, wr_ref, br_ref, h_ref, f_ref, lg_ref):
    h = x_ref[...] + _dot(om_ref[...], wm_ref[...]) + _dot(og_ref[...], wg_ref[...])
    h_ref[...] = h
    f = _rms(h, gf_ref[...]).astype(BF16)
    f_ref[...] = f
    lg_ref[...] = _dot(wr_ref[...], f, _NT) + br_ref[...]


def _outproj(x2d, o_moba, o_gla, w_out, g_ffn, w_router, b_router, *, tm=OUTPROJ_ROWS):
    t, d = x2d.shape
    wm = w_out[:MOBA_WIDTH].astype(BF16)
    wg = w_out[MOBA_WIDTH:].astype(BF16)
    wr = w_router.T.astype(BF16)
    br = b_router.reshape(N_EXPERTS, 1)
    row = lambda i: (i, 0)
    const = lambda i: (0, 0)
    full = lambda a: pl.BlockSpec(a.shape, const)
    args = (x2d, o_moba, o_gla, wm, wg, g_ffn.reshape(1, d), wr, br)
    in_specs = [pl.BlockSpec((tm, d), row), pl.BlockSpec((tm, MOBA_WIDTH), row),
                pl.BlockSpec((tm, GLA_VAL_WIDTH), row)] + [full(a) for a in args[3:]]
    return pl.pallas_call(
        _outproj_kernel, grid=(t // tm,), in_specs=in_specs,
        out_specs=(pl.BlockSpec((tm, d), row), pl.BlockSpec((tm, d), row),
                   pl.BlockSpec((N_EXPERTS, tm), lambda i: (0, i))),
        out_shape=(jax.ShapeDtypeStruct((t, d), F32), jax.ShapeDtypeStruct((t, d), BF16),
                   jax.ShapeDtypeStruct((N_EXPERTS, t), F32)),
        compiler_params=_params("arbitrary"), name="outproj",
    )(*args)


PAIR_CHUNK = 2 * LANES


def _moe_kernel(be_ref, nused_ref, x_ref, w1_ref, b1_ref, w2_ref, b2_ref, o_ref, w1s_ref, w2s_ref):
    i = pl.program_id(0)
    f = w2_ref.shape[1]

    @pl.when((i == 0) | (be_ref[i] != be_ref[jnp.maximum(i - 1, 0)]))
    def _():
        r = lax.broadcasted_iota(jnp.int32, (PAIR_CHUNK, PAIR_CHUNK), 0)
        c = lax.broadcasted_iota(jnp.int32, (PAIR_CHUNK, PAIR_CHUNK), 1)
        src_col = jnp.where(c < LANES, 2 * c, 2 * (c - LANES) + 1)
        perm = jnp.where(r == src_col, 1.0, 0.0).astype(BF16)
        for cc in range(2 * f // PAIR_CHUNK):
            chunk = w1_ref[0, :, cc * PAIR_CHUNK:(cc + 1) * PAIR_CHUNK].astype(BF16)
            res = _dot(chunk, perm).astype(BF16)
            w1s_ref[:, cc * LANES:(cc + 1) * LANES] = res[:, :LANES]
            w1s_ref[:, f + cc * LANES:f + (cc + 1) * LANES] = res[:, LANES:]
        w2s_ref[...] = w2_ref[0].astype(BF16)

    @pl.when(i < nused_ref[0])
    def _():
        hid = _dot(x_ref[...], w1s_ref[...]) + b1_ref[0]
        glu = jnp.minimum(hid[:, :f], SWIGLU_LIMIT)
        lin = jnp.clip(hid[:, f:], -SWIGLU_LIMIT, SWIGLU_LIMIT)
        act = glu * jax.nn.sigmoid(SWIGLU_ALPHA * glu) * (lin + 1.0)
        o_ref[...] = (_dot(act.astype(BF16), w2s_ref[...]) + b2_ref[0]).astype(o_ref.dtype)

    @pl.when(i >= nused_ref[0])
    def _():
        o_ref[...] = jnp.zeros_like(o_ref)


def _moe(x_rows, block_expert, n_used, w1, b1, w2, b2, *, tm):
    n_rows, d = x_rows.shape
    e, _, f2 = w1.shape
    f = f2 // 2
    b1p = jnp.concatenate([b1[:, 0::2], b1[:, 1::2]], axis=-1).reshape(e, 1, f2)
    b2r = b2.reshape(e, 1, d)
    grid_spec = pltpu.PrefetchScalarGridSpec(
        num_scalar_prefetch=2,
        grid=(n_rows // tm,),
        in_specs=[
            pl.BlockSpec((tm, d), lambda i, be, nu: (i, 0)),
            pl.BlockSpec((1, d, f2), lambda i, be, nu: (be[i], 0, 0)),
            pl.BlockSpec((1, 1, f2), lambda i, be, nu: (be[i], 0, 0)),
            pl.BlockSpec((1, f, d), lambda i, be, nu: (be[i], 0, 0)),
            pl.BlockSpec((1, 1, d), lambda i, be, nu: (be[i], 0, 0)),
        ],
        out_specs=pl.BlockSpec((tm, d), lambda i, be, nu: (i, 0)),
        scratch_shapes=[pltpu.VMEM((d, f2), BF16), pltpu.VMEM((f, d), BF16)],
    )
    return pl.pallas_call(
        _moe_kernel, grid_spec=grid_spec, out_shape=jax.ShapeDtypeStruct((n_rows, d), BF16),
        compiler_params=_params("arbitrary"), name="moe",
    )(block_expert, n_used, x_rows, w1, b1p, w2, b2r)


ROUTE_CHUNK = 2 * LANES


def _router_kernel(lg_ref, idx_ref, gate_ref, rank_ref, cnt_ref, carry_ref):
    @pl.when(pl.program_id(0) == 0)
    def _():
        carry_ref[...] = jnp.zeros_like(carry_ref)

    x = lg_ref[...]
    ne, tt = x.shape
    eidx = lax.broadcasted_iota(jnp.int32, (ne, tt), 0)
    member = jnp.zeros((ne, tt), F32)
    vals, ids = [], []
    for _ in range(TOP_K):
        m = jnp.max(x, axis=0, keepdims=True)
        sel = jnp.min(jnp.where(x == m, eidx, ne), axis=0, keepdims=True)
        hit = eidx == sel
        member = jnp.where(hit, 1.0, member)
        x = jnp.where(hit, -jnp.inf, x)
        vals.append(m)
        ids.append(sel)
    e = jnp.exp(jnp.concatenate(vals, axis=0) - vals[0])
    gate_ref[...] = e / jnp.sum(e, axis=0, keepdims=True)
    idx_ref[...] = jnp.concatenate(ids, axis=0)
    r = lax.broadcasted_iota(jnp.int32, (ROUTE_CHUNK, ROUTE_CHUNK), 0)
    c = lax.broadcasted_iota(jnp.int32, (ROUTE_CHUNK, ROUTE_CHUNK), 1)
    earlier = jnp.where(r < c, 1.0, 0.0).astype(BF16)
    carry = carry_ref[:, :1]
    before = []
    for ch in range(tt // ROUTE_CHUNK):
        mc = member[:, ch * ROUTE_CHUNK:(ch + 1) * ROUTE_CHUNK]
        before.append(_dot(mc.astype(BF16), earlier) + carry)
        carry = carry + jnp.sum(mc, axis=1, keepdims=True)
    before = jnp.concatenate(before, axis=1)
    ranks = [jnp.sum(jnp.where(eidx == ids[k], before, 0.0), axis=0, keepdims=True) for k in range(TOP_K)]
    rank_ref[...] = jnp.concatenate(ranks, axis=0).astype(jnp.int32)
    carry_ref[...] = jnp.broadcast_to(carry, carry_ref.shape)
    cnt_ref[...] = jnp.broadcast_to(carry, cnt_ref.shape).astype(jnp.int32)


def _router(logits_t, *, tt=ROUTER_TOKENS):
    ne, t = logits_t.shape
    tok = lambda i: (0, i)
    kspec = pl.BlockSpec((TOP_K, tt), tok)
    return pl.pallas_call(
        _router_kernel, grid=(t // tt,), in_specs=[pl.BlockSpec((ne, tt), tok)],
        out_specs=(kspec, kspec, kspec, pl.BlockSpec((ne, LANES), lambda i: (0, 0))),
        out_shape=(jax.ShapeDtypeStruct((TOP_K, t), jnp.int32), jax.ShapeDtypeStruct((TOP_K, t), F32),
                   jax.ShapeDtypeStruct((TOP_K, t), jnp.int32), jax.ShapeDtypeStruct((ne, LANES), jnp.int32)),
        scratch_shapes=[pltpu.VMEM((ne, LANES), F32)],
        compiler_params=_params("arbitrary"), name="router",
    )(logits_t)


def _route(logits_t, *, tm):
    ne, t = logits_t.shape
    idx, gates, rank, cnt = _router(logits_t)
    counts = cnt[:, 0]
    padded = (counts + tm - 1) // tm * tm
    pad_end = jnp.cumsum(padded)
    pad_start = pad_end - padded
    experts = jnp.arange(ne, dtype=jnp.int32).reshape(ne, 1, 1)
    slot = jnp.sum(jnp.where(idx[None] == experts, pad_start.reshape(ne, 1, 1), 0), axis=0) + rank
    n_blocks = -(-TOP_K * t // tm) + ne
    flat_slot = slot.reshape(-1)
    tok = jnp.broadcast_to(jnp.arange(t, dtype=jnp.int32), (TOP_K, t)).reshape(-1)
    row_tok = (jnp.arange(n_blocks * tm, dtype=jnp.int32) % t).at[flat_slot].add(
        tok - flat_slot % t, unique_indices=True, mode="promise_in_bounds")
    n_used = pad_end[-1] // tm
    blk_start = jnp.minimum(jnp.arange(n_blocks, dtype=jnp.int32), n_used - 1) * tm
    block_expert = jnp.minimum(jnp.sum(pad_end[:, None] <= blk_start[None, :], axis=0), ne - 1)
    return gates, row_tok, slot, block_expert.astype(jnp.int32), n_used.reshape(1).astype(jnp.int32)


def _final_kernel(h_ref, y_ref, gate_ref, p_ref, wp_ref, gp_ref, wgate_ref, gfin_ref, o_ref, *, last_layer):
    h = h_ref[...]
    g = gate_ref[...]
    gates = jnp.concatenate([g, jnp.zeros((SUBLANES - TOP_K, g.shape[1]), F32)], axis=0).T
    for k in range(TOP_K):
        h = h + gates[:, k:k + 1] * y_ref[k].astype(F32)
    ple = _rms(_dot(p_ref[...].astype(BF16), wp_ref[...]), gp_ref[...])
    h = h + jax.nn.sigmoid(_dot(h.astype(BF16), wgate_ref[...])) * ple
    o_ref[...] = _rms(h, gfin_ref[...]) if last_layer else h


def _final(h1, y_tok, gates, p2d, w_ple_proj, g_ple, w_ple_gate, g_final, *, last_layer, tm=FINAL_ROWS):
    t, d = h1.shape
    row = lambda i: (i, 0)
    const = lambda i: (0, 0)
    full = lambda a: pl.BlockSpec(a.shape, const)
    args = (h1, y_tok, gates, p2d, w_ple_proj.astype(BF16), g_ple.reshape(1, d), w_ple_gate.astype(BF16),
            g_final.reshape(1, d))
    in_specs = [pl.BlockSpec((tm, d), row), pl.BlockSpec((TOP_K, tm, d), lambda i: (0, i, 0)),
                pl.BlockSpec((TOP_K, tm), lambda i: (0, i)), pl.BlockSpec((tm, p2d.shape[1]), row)]
    in_specs += [full(a) for a in args[4:]]
    return pl.pallas_call(
        functools.partial(_final_kernel, last_layer=last_layer), grid=(t // tm,), in_specs=in_specs,
        out_specs=pl.BlockSpec((tm, d), row),
        out_shape=jax.ShapeDtypeStruct((t, d), F32),
        compiler_params=_params("arbitrary"), name="final",
    )(*args)


def kernel(x, p, g_mix, w_in, w_gla_gate, b_gla_gate, g_gla_norm, w_out, g_ffn, w_router, b_router,
           w1, b1, w2, b2, w_ple_gate, w_ple_proj, g_ple, g_final):
    batch, seq, d = x.shape
    depth = p.shape[0]
    t = batch * seq
    h = x.reshape(t, d)
    for i in range(depth):
        qa, ka, vt, kmean, qg, kg, vg, rg, la = _inproj(h, g_mix[i], w_in[i], w_gla_gate[i], b_gla_gate[i],
                                                        batch=batch, seq=seq)
        o_moba = _moba(qa, ka, vt, kmean, batch=batch, seq=seq)
        o_gla = _gla(qg, kg, vg, rg, la, g_gla_norm[i], batch=batch, seq=seq)
        h1, f, logits = _outproj(h, o_moba, o_gla, w_out[i], g_ffn[i], w_router[i], b_router[i])
        gates, row_tok, slot, block_expert, n_used = _route(logits, tm=MOE_ROWS)
        x_rows = jnp.take(f, row_tok, axis=0, mode="clip")
        y = _moe(x_rows, block_expert, n_used, w1[i], b1[i], w2[i], b2[i], tm=MOE_ROWS)
        y_tok = jnp.take(y, slot, axis=0, mode="clip")
        h = _final(h1, y_tok, gates, p[i].reshape(t, -1), w_ple_proj[i], g_ple[i], w_ple_gate[i],
                   g_final, last_layer=i == depth - 1)
    return h.reshape(batch, seq, d)
```

```python
import functools

import jax
import jax.numpy as jnp
import numpy as np
from jax import lax
from jax.experimental import pallas as pl
from jax.experimental.pallas import tpu as pltpu

HEAD_DIM = 64
MOBA_HEADS = 8
MOBA_WIDTH = MOBA_HEADS * HEAD_DIM
MOBA_BLOCK = 256
MOBA_TOPK = 3
MOBA_GROUP = 4
MOBA_QBLOCKS = 4
ALIBI_MAX = 8.0
GLA_HEADS = 4
GLA_DK = 64
GLA_DV = 128
GLA_KEY_WIDTH = GLA_HEADS * GLA_DK
GLA_VAL_WIDTH = GLA_HEADS * GLA_DV
GLA_GATE_RANK = 16
GLA_GATE_TAU = 16.0
GLA_CHUNK = 64
GLA_SEQS = 4
N_EXPERTS = 32
TOP_K = 4
SWIGLU_ALPHA = 1.702
SWIGLU_LIMIT = 7.0
EPS = 1e-6

LANES = 128
SUBLANES = 8
BF16_SUBLANES = 16

INPROJ_ROWS = 512
OUTPROJ_ROWS = 1024
FINAL_ROWS = 1024
GLA_TILE = 256
ROUTER_TOKENS = 2048
MOE_ROWS = 512
MASKED = -1e30
VMEM_LIMIT = 56 * 1024 * 1024

BF16 = jnp.bfloat16
F32 = jnp.float32

_NT = (((1,), (1,)), ((), ()))
_TN = (((0,), (0,)), ((), ()))


def _dot(a, b, dims=None):
    if dims is None:
        return jnp.dot(a, b, preferred_element_type=F32)
    return lax.dot_general(a, b, dims, preferred_element_type=F32)


def _rms(x, g):
    return x * lax.rsqrt(jnp.mean(x * x, axis=-1, keepdims=True) + EPS) * g


def _col_reduce(x, pair_op, final_op, rows=64):
    parts = [x[i:i + rows] for i in range(0, x.shape[0], rows)]
    while len(parts) > 1:
        parts = [pair_op(parts[i], parts[i + 1]) for i in range(0, len(parts), 2)]
    return final_op(parts[0], axis=0, keepdims=True)


def _params(*sem):
    return pltpu.CompilerParams(dimension_semantics=sem, vmem_limit_bytes=VMEM_LIMIT)


def _inproj_kernel(x_ref, g_ref, wq_ref, wk_ref, wvt_ref, wgla_ref, wag_ref, wgate_ref, bgate_ref,
                   qa_ref, ka_ref, vt_ref, km_ref, qg_ref, kg_ref, vg_ref, rg_ref, la_ref):
    i = pl.program_id(1)
    tm = x_ref.shape[0]
    nblk = tm // MOBA_BLOCK
    @pl.when(i == 0)
    def _():
        km_ref[...] = jnp.zeros_like(km_ref)

    ab = _rms(x_ref[...], g_ref[...]).astype(BF16)
    ag = _dot(ab, wag_ref[...]).astype(BF16)
    z = _dot(ag, wgate_ref[...]) + bgate_ref[...]
    la_ref[...] = jax.nn.log_sigmoid(z) / GLA_GATE_TAU
    qa_ref[...] = (_dot(ab, wq_ref[...]) * (HEAD_DIM ** -0.5)).astype(BF16)
    ka = _dot(ab, wk_ref[...])
    ka_ref[...] = ka.astype(BF16)
    km = km_ref[...]
    blk_row = lax.broadcasted_iota(jnp.int32, km.shape, 0)
    for j in range(nblk):
        mean_j = jnp.mean(ka[j * MOBA_BLOCK:(j + 1) * MOBA_BLOCK], axis=0, keepdims=True)
        km = jnp.where(blk_row == i * nblk + j, mean_j, km)
    km_ref[...] = km
    vt = _dot(wvt_ref[...], ab, _NT).astype(BF16)
    for j in range(nblk):
        vt_ref[j] = vt[:, j * MOBA_BLOCK:(j + 1) * MOBA_BLOCK]
    gla = _dot(ab, wgla_ref[...])
    o = 0
    for ref in (qg_ref, kg_ref, vg_ref, rg_ref):
        w = ref.shape[1]
        ref[...] = gla[:, o:o + w].astype(ref.dtype)
        o += w


def _inproj(x2d, g_mix, w_in, w_gla_gate, b_gla_gate, *, batch, seq, tm=INPROJ_ROWS):
    t, d = x2d.shape
    nb = seq // MOBA_BLOCK
    o_k, o_v, o_qg = MOBA_WIDTH, 2 * MOBA_WIDTH, 3 * MOBA_WIDTH
    o_ag = o_qg + 2 * GLA_KEY_WIDTH + 2 * GLA_VAL_WIDTH
    wq = w_in[:, :o_k].astype(BF16)
    wk = w_in[:, o_k:o_v].astype(BF16)
    wvt = w_in[:, o_v:o_qg].T.astype(BF16)
    wgla = w_in[:, o_qg:o_ag].astype(BF16)
    wag = jnp.pad(w_in[:, o_ag:], ((0, 0), (0, LANES - GLA_GATE_RANK))).astype(BF16)
    wgate = jnp.pad(w_gla_gate, ((0, LANES - GLA_GATE_RANK), (0, 0))).astype(BF16)
    steps = seq // tm
    row = lambda b, i: (b * steps + i, 0)
    const = lambda b, i: (0, 0)
    full = lambda a: pl.BlockSpec(a.shape, const)
    args = (x2d, g_mix.reshape(1, d), wq, wk, wvt, wgla, wag, wgate, b_gla_gate.reshape(1, -1))
    in_specs = [pl.BlockSpec((tm, d), row)] + [full(a) for a in args[1:]]
    out_shape = (
        jax.ShapeDtypeStruct((t, MOBA_WIDTH), BF16),
        jax.ShapeDtypeStruct((t, MOBA_WIDTH), BF16),
        jax.ShapeDtypeStruct((t // MOBA_BLOCK, MOBA_WIDTH, MOBA_BLOCK), BF16),
        jax.ShapeDtypeStruct((batch * nb, MOBA_WIDTH), F32),
        jax.ShapeDtypeStruct((t, GLA_KEY_WIDTH), F32),
        jax.ShapeDtypeStruct((t, GLA_KEY_WIDTH), F32),
        jax.ShapeDtypeStruct((t, GLA_VAL_WIDTH), BF16),
        jax.ShapeDtypeStruct((t, GLA_VAL_WIDTH), F32),
        jax.ShapeDtypeStruct((t, GLA_KEY_WIDTH), F32),
    )
    out_specs = (
        pl.BlockSpec((tm, MOBA_WIDTH), row),
        pl.BlockSpec((tm, MOBA_WIDTH), row),
        pl.BlockSpec((tm // MOBA_BLOCK, MOBA_WIDTH, MOBA_BLOCK), lambda b, i: (b * steps + i, 0, 0)),
        pl.BlockSpec((nb, MOBA_WIDTH), lambda b, i: (b, 0)),
        pl.BlockSpec((tm, GLA_KEY_WIDTH), row),
        pl.BlockSpec((tm, GLA_KEY_WIDTH), row),
        pl.BlockSpec((tm, GLA_VAL_WIDTH), row),
        pl.BlockSpec((tm, GLA_VAL_WIDTH), row),
        pl.BlockSpec((tm, GLA_KEY_WIDTH), row),
    )
    return pl.pallas_call(
        _inproj_kernel, grid=(batch, steps), in_specs=in_specs, out_specs=out_specs, out_shape=out_shape,
        compiler_params=_params("arbitrary", "arbitrary"), name="inproj",
    )(*args)


def _moba_kernel(slopes_ref, q_ref, k_ref, vt_ref, km_ref, o_ref, kaug_ref, s_ref):
    hg = pl.program_id(1)
    step = pl.program_id(2)
    blk = MOBA_BLOCK
    nb = km_ref.shape[0]
    hw = 2 * HEAD_DIM
    heads = q_ref.shape[1] // HEAD_DIM
    n_q = q_ref.shape[0] // blk
    feat_rows = nb + SUBLANES
    head_rows = lambda g: slice(g * HEAD_DIM, (g + 1) * HEAD_DIM)
    hs = range(heads)

    @pl.when(step == 0)
    def _():
        lane = lax.broadcasted_iota(jnp.int32, (blk, hw), 1)
        koff = lax.broadcasted_iota(jnp.int32, (blk, hw), 0).astype(F32)
        for g in hs:
            tile, hh = divmod(g, 2)
            slope = slopes_ref[hg * heads + g]
            base = HEAD_DIM * (1 - hh)
            for jb in range(nb):
                feat = jnp.where(lane == base + nb, slope * koff, 0.0)
                feat = jnp.where((lane == base + jb) | (lane == base + nb + 1), 1.0, feat)
                kb = k_ref[jb * blk:(jb + 1) * blk, tile * hw:(tile + 1) * hw]
                kaug_ref[g, jb * blk:(jb + 1) * blk, :] = jnp.where(lane // HEAD_DIM == hh, kb, feat.astype(BF16))

    kidx = lax.broadcasted_iota(jnp.int32, (blk, blk), 0)
    qidx = lax.broadcasted_iota(jnp.int32, (blk, blk), 1)
    causal = kidx <= qidx
    nidx = lax.broadcasted_iota(jnp.int32, (nb, blk), 0)
    r8 = lax.broadcasted_iota(jnp.int32, (SUBLANES, blk), 0)
    qoff = lax.broadcasted_iota(jnp.int32, (SUBLANES, blk), 1).astype(F32)
    km_lane = lax.broadcasted_iota(jnp.int32, km_ref.shape, 1)
    km = km_ref[...]
    slopes = [slopes_ref[hg * heads + g] for g in hs]
    ones = jnp.ones((BF16_SUBLANES, blk), BF16)
    fill = jnp.zeros((HEAD_DIM - feat_rows, blk), F32)

    def values_and_ones(vt_blk, g):
        return jnp.concatenate([vt_blk[head_rows(g), :], ones], axis=0)

    def scores(q_past, pair, buf, slot):
        rows = pl.ds(pl.multiple_of(pair * 2 * blk, 2 * blk), 2 * blk)
        tops = []
        for g in hs:
            s = _dot(kaug_ref[g, rows, :], q_past[g])
            s_ref[buf, slot, g] = s
            tops.append(_col_reduce(s, jnp.maximum, jnp.max))
        return tops

    def absorb_head(pair, buf, slot, top, m_i, acc, g):
        m_new = jnp.maximum(m_i, top)
        p = jnp.exp((s_ref[buf, slot, g] - m_new).astype(BF16))
        acc = (jnp.exp(m_i - m_new) * acc + _dot(values_and_ones(vt_ref[2 * pair], g), p[:blk])
               + _dot(values_and_ones(vt_ref[2 * pair + 1], g), p[blk:]))
        return [m_new, acc]

    def absorb(pair, buf, slot, tops, carry):
        out = []
        for g in hs:
            out += absorb_head(pair, buf, slot, tops[g], carry[2 * g], carry[2 * g + 1], g)
        return out

    def prologues():
        qs = range(n_q)
        cs = [step * n_q + i for i in qs]
        qts = [q_ref[i * blk:(i + 1) * blk, :].astype(F32).T for i in qs]
        qtbs = [qt.astype(BF16) for qt in qts]
        kmh = [jnp.where(km_lane // HEAD_DIM == g, km, 0.0).astype(BF16) for g in hs]
        gates = [[jnp.where(nidx < cs[i], _dot(kmh[g], qtbs[i]), -jnp.inf) for g in hs] for i in qs]
        ranks = [[jnp.zeros((nb, blk), jnp.int32) for _ in hs] for _ in qs]
        for m in range(nb - 1):
            for i in qs:
                for g in hs:
                    gm = gates[i][g][m:m + 1, :]
                    beats = (gm > gates[i][g]) | ((gm == gates[i][g]) & (m < nidx))
                    ranks[i][g] = ranks[i][g] + beats.astype(jnp.int32)

        def q_aug(i, g, bias):
            dist0 = -slopes[g] * ((cs[i] - nidx) * blk).astype(F32)
            tail = jnp.where(r8 == 0, 1.0, jnp.where(r8 == 1, -slopes[g] * qoff, 0.0))
            feats = [jnp.where(bias, dist0, MASKED), tail, fill]
            qh = qts[i][head_rows(g), :]
            return jnp.concatenate([qh] + feats if g % 2 == 0 else feats + [qh], axis=0).astype(BF16)

        q_past = [[q_aug(i, g, (nidx < cs[i]) & (ranks[i][g] < MOBA_TOPK)) for g in hs] for i in qs]
        q_own = [[q_aug(i, g, nidx == cs[i]) for g in hs] for i in qs]
        own_rows = [pl.ds(pl.multiple_of(c * blk, blk), blk) for c in cs]
        s_own = [[jnp.where(causal, _dot(kaug_ref[g, own_rows[i], :], q_own[i][g]), MASKED) for g in hs] for i in qs]
        m_own = [[jnp.max(s, axis=0, keepdims=True) for s in row] for row in s_own]
        p_own = [[jnp.exp((s_own[i][g] - m_own[i][g]).astype(BF16)) for g in hs] for i in qs]
        started = []
        for i in qs:
            state = []
            for g in hs:
                state += [m_own[i][g], _dot(values_and_ones(vt_ref[cs[i]], g), p_own[i][g])]
            started.append((cs[i], q_past[i], state))
        return [(c, q_past_i, state + scores(q_past_i, 0, i, 0)) for i, (c, q_past_i, state) in enumerate(started)]

    def attend(i, c, q_past, carry):
        n_state = 2 * heads

        def advance(n, carry, slot):
            tops = scores(q_past, n + 1, i, 1 - slot)
            return tuple(absorb(n, i, slot, carry[n_state:], carry[:n_state]) + tops)

        def body(n, carry):
            return lax.cond(n % 2 == 0, lambda: advance(n, carry, 0), lambda: advance(n, carry, 1))

        last = jnp.maximum((c + 1) // 2 - 1, 0)
        return last, lax.fori_loop(0, last, body, tuple(carry))

    started = prologues()
    looped = [attend(i, c, q_past, carry) for i, (c, q_past, carry) in enumerate(started)]
    accs = [[None] * heads for _ in looped]
    for g in hs:
        for i, (last, carry) in enumerate(looped):
            accs[i][g] = absorb_head(last, i, last % 2, carry[2 * heads + g], carry[2 * g], carry[2 * g + 1], g)[1]
    for i in range(n_q):
        out_t = jnp.concatenate([a[:HEAD_DIM] / a[HEAD_DIM:HEAD_DIM + 1] for a in accs[i]], axis=0)
        o_ref[i * blk:(i + 1) * blk, :] = out_t.T.astype(o_ref.dtype)


def _alibi_slopes():
    slopes = np.exp2(-ALIBI_MAX * np.arange(1, MOBA_HEADS + 1, dtype=np.float32) / MOBA_HEADS).astype(np.float32)
    worst = slopes * np.float32(MOBA_BLOCK - 1)
    assert np.all(worst.astype(jnp.bfloat16).astype(np.float32) == worst), "ALiBi slopes not bf16-exact"
    return slopes


def _moba(qa, ka, vt, kmean, *, batch, seq):
    t = qa.shape[0]
    nb = seq // MOBA_BLOCK
    gw = MOBA_GROUP * HEAD_DIM
    assert nb % SUBLANES == 0 and nb + SUBLANES <= HEAD_DIM, "bias features must fit the spare contraction lanes"
    grid_spec = pltpu.PrefetchScalarGridSpec(
        num_scalar_prefetch=1,
        grid=(batch, MOBA_HEADS // MOBA_GROUP, nb // MOBA_QBLOCKS),
        in_specs=[
            pl.BlockSpec((MOBA_QBLOCKS * MOBA_BLOCK, gw), lambda b, h, c, s: (b * (nb // MOBA_QBLOCKS) + c, h)),
            pl.BlockSpec((seq, gw), lambda b, h, c, s: (b, h)),
            pl.BlockSpec((nb, gw, MOBA_BLOCK), lambda b, h, c, s: (b, h, 0)),
            pl.BlockSpec((nb, gw), lambda b, h, c, s: (b, h)),
        ],
        out_specs=pl.BlockSpec((MOBA_QBLOCKS * MOBA_BLOCK, gw), lambda b, h, c, s: (b * (nb // MOBA_QBLOCKS) + c, h)),
        scratch_shapes=[pltpu.VMEM((MOBA_GROUP, seq, 2 * HEAD_DIM), BF16),
                        pltpu.VMEM((MOBA_QBLOCKS, 2, MOBA_GROUP, 2 * MOBA_BLOCK, MOBA_BLOCK), F32)],
    )
    return pl.pallas_call(
        _moba_kernel, grid_spec=grid_spec, out_shape=jax.ShapeDtypeStruct((t, MOBA_WIDTH), BF16),
        compiler_params=_params("arbitrary", "arbitrary", "arbitrary"), name="moba",
    )(jnp.asarray(_alibi_slopes()), qa, ka, vt, kmean)


def _gla_kernel(q_ref, k_ref, v_ref, r_ref, la_ref, gn_ref, o_ref, st_ref):
    @pl.when(pl.program_id(1) == 0)
    def _():
        st_ref[...] = jnp.zeros_like(st_ref)

    ch = GLA_CHUNK
    n_seqs, tc = q_ref.shape[:2]
    n_chunks = tc // ch
    row = lax.broadcasted_iota(jnp.int32, (tc, tc), 0)
    col = lax.broadcasted_iota(jnp.int32, (tc, tc), 1)
    causal = (row // ch == col // ch) & (row >= col)
    tri = jnp.where(causal, 1.0, 0.0).astype(BF16)
    lane = lax.broadcasted_iota(jnp.int32, (1, GLA_KEY_WIDTH), 1)
    gn = gn_ref[...]
    seqs = range(n_seqs)
    heads = range(GLA_HEADS)
    gs = []
    for s in seqs:
        la = la_ref[s]
        la_hi = la.astype(BF16)
        rest = la - la_hi.astype(F32)
        la_mid = rest.astype(BF16)
        la_lo = (rest - la_mid.astype(F32)).astype(BF16)
        gs.append(_dot(tri, la_hi) + _dot(tri, la_mid) + _dot(tri, la_lo))
    ends = [[g[(ci + 1) * ch - 1:(ci + 1) * ch, :] for ci in range(n_chunks)] for g in gs]
    qhs, kds, kths, vhs = [], [], [], []
    for s in seqs:
        g = gs[s]
        g_last = jnp.concatenate([jnp.broadcast_to(e, (ch, GLA_KEY_WIDTH)) for e in ends[s]], axis=0)
        qd = q_ref[s] * (GLA_DK ** -0.5) * jnp.exp(g)
        k = k_ref[s]
        kds.append((k * jnp.exp(-g)).astype(BF16))
        kt = k * jnp.exp(g_last - g)
        qhs.append([jnp.where(lane // GLA_DK == h, qd, 0.0).astype(BF16) for h in heads])
        kths.append([jnp.where(lane // GLA_DK == h, kt, 0.0).astype(BF16) for h in heads])
        vhs.append([v_ref[s, :, h * GLA_DV:(h + 1) * GLA_DV] for h in heads])
    intra = [[None] * GLA_HEADS for _ in seqs]
    for h in heads:
        for s in seqs:
            a = jnp.where(causal, _dot(qhs[s][h], kds[s], _NT), 0.0)
            intra[s][h] = _dot(a.astype(BF16), vhs[s][h])
    states = [st_ref[s] for s in seqs]
    inter = [[[] for _ in heads] for _ in seqs]
    for ci in range(n_chunks):
        rows = slice(ci * ch, (ci + 1) * ch)
        for s in seqs:
            state_b = states[s].astype(BF16)
            new_state = states[s] * jnp.exp(ends[s][ci])
            for h in heads:
                inter[s][h].append(_dot(qhs[s][h][rows], state_b, _NT))
                new_state = new_state + _dot(vhs[s][h][rows], kths[s][h][rows], _TN)
            states[s] = new_state
    for s in seqs:
        st_ref[s] = states[s]
    for h in heads:
        for s in seqs:
            o = intra[s][h] + jnp.concatenate(inter[s][h], axis=0)
            rh = r_ref[s, :, h * GLA_DV:(h + 1) * GLA_DV]
            o_ref[s, :, h * GLA_DV:(h + 1) * GLA_DV] = (_rms(o, gn) * (rh * jax.nn.sigmoid(rh))).astype(o_ref.dtype)


def _gla(qg, kg, vg, rg, la, g_norm, *, batch, seq, tc=GLA_TILE):
    n_seqs = GLA_SEQS if batch % GLA_SEQS == 0 else 1
    by_seq = lambda a: a.reshape(batch, seq, a.shape[-1])
    blk = lambda i, j: (i, j, 0)
    kspec = pl.BlockSpec((n_seqs, tc, GLA_KEY_WIDTH), blk)
    vspec = pl.BlockSpec((n_seqs, tc, GLA_VAL_WIDTH), blk)
    out = pl.pallas_call(
        _gla_kernel, grid=(batch // n_seqs, seq // tc),
        in_specs=[kspec, kspec, vspec, vspec, kspec, pl.BlockSpec((1, GLA_DV), lambda i, j: (0, 0))],
        out_specs=vspec, out_shape=jax.ShapeDtypeStruct((batch, seq, GLA_VAL_WIDTH), BF16),
        scratch_shapes=[pltpu.VMEM((n_seqs, GLA_DV, GLA_KEY_WIDTH), F32)],
        compiler_params=_params("arbitrary", "arbitrary"), name="gla",
    )(by_seq(qg), by_seq(kg), by_seq(vg), by_seq(rg), by_seq(la), g_norm.reshape(1, GLA_DV))
    return out.reshape(batch * seq, GLA_VAL_WIDTH)


def _outproj_kernel(x_ref, om_ref, og_ref, wm_ref, wg_ref, gf_ref, wr_ref, br_ref, h_ref, f_ref, lg_ref):
    h = x_ref[...] + _dot(om_ref[...], wm_ref[...]) + _dot(og_ref[...], wg_ref[...])
    h_ref[...] = h
    f = _rms(h, gf_ref[...]).astype(BF16)
    f_ref[...] = f
    lg_ref[...] = _dot(wr_ref[...], f, _NT) + br_ref[...]


def _outproj(x2d, o_moba, o_gla, w_out, g_ffn, w_router, b_router, *, tm=OUTPROJ_ROWS):
    t, d = x2d.shape
    wm = w_out[:MOBA_WIDTH].astype(BF16)
    wg = w_out[MOBA_WIDTH:].astype(BF16)
    wr = w_router.T.astype(BF16)
    br = b_router.reshape(N_EXPERTS, 1)
    row = lambda i: (i, 0)
    const = lambda i: (0, 0)
    full = lambda a: pl.BlockSpec(a.shape, const)
    args = (x2d, o_moba, o_gla, wm, wg, g_ffn.reshape(1, d), wr, br)
    in_specs = [pl.BlockSpec((tm, d), row), pl.BlockSpec((tm, MOBA_WIDTH), row),
                pl.BlockSpec((tm, GLA_VAL_WIDTH), row)] + [full(a) for a in args[3:]]
    return pl.pallas_call(
        _outproj_kernel, grid=(t // tm,), in_specs=in_specs,
        out_specs=(pl.BlockSpec((tm, d), row), pl.BlockSpec((tm, d), row),
                   pl.BlockSpec((N_EXPERTS, tm), lambda i: (0, i))),
        out_shape=(jax.ShapeDtypeStruct((t, d), F32), jax.ShapeDtypeStruct((t, d), BF16),
                   jax.ShapeDtypeStruct((N_EXPERTS, t), F32)),
        compiler_params=_params("arbitrary"), name="outproj",
    )(*args)


PAIR_CHUNK = 2 * LANES


def _moe_kernel(be_ref, nused_ref, x_ref, w1_ref, b1_ref, w2_ref, b2_ref, o_ref, w1s_ref, w2s_ref):
    i = pl.program_id(0)
    f = w2_ref.shape[1]

    @pl.when((i == 0) | (be_ref[i] != be_ref[jnp.maximum(i - 1, 0)]))
    def _():
        r = lax.broadcasted_iota(jnp.int32, (PAIR_CHUNK, PAIR_CHUNK), 0)
        c = lax.broadcasted_iota(jnp.int32, (PAIR_CHUNK, PAIR_CHUNK), 1)
        src_col = jnp.where(c < LANES, 2 * c, 2 * (c - LANES) + 1)
        perm = jnp.where(r == src_col, 1.0, 0.0).astype(BF16)
        for cc in range(2 * f // PAIR_CHUNK):
            chunk = w1_ref[0, :, cc * PAIR_CHUNK:(cc + 1) * PAIR_CHUNK].astype(BF16)
            res = _dot(chunk, perm).astype(BF16)
            w1s_ref[:, cc * LANES:(cc + 1) * LANES] = res[:, :LANES]
            w1s_ref[:, f + cc * LANES:f + (cc + 1) * LANES] = res[:, LANES:]
        w2s_ref[...] = w2_ref[0].astype(BF16)

    @pl.when(i < nused_ref[0])
    def _():
        hid = _dot(x_ref[...], w1s_ref[...]) + b1_ref[0]
        glu = jnp.minimum(hid[:, :f], SWIGLU_LIMIT)
        lin = jnp.clip(hid[:, f:], -SWIGLU_LIMIT, SWIGLU_LIMIT)
        act = glu * jax.nn.sigmoid(SWIGLU_ALPHA * glu) * (lin + 1.0)
        o_ref[...] = (_dot(act.astype(BF16), w2s_ref[...]) + b2_ref[0]).astype(o_ref.dtype)

    @pl.when(i >= nused_ref[0])
    def _():
        o_ref[...] = jnp.zeros_like(o_ref)


def _moe(x_rows, block_expert, n_used, w1, b1, w2, b2, *, tm):
    n_rows, d = x_rows.shape
    e, _, f2 = w1.shape
    f = f2 // 2
    b1p = jnp.concatenate([b1[:, 0::2], b1[:, 1::2]], axis=-1).reshape(e, 1, f2)
    b2r = b2.reshape(e, 1, d)
    grid_spec = pltpu.PrefetchScalarGridSpec(
        num_scalar_prefetch=2,
        grid=(n_rows // tm,),
        in_specs=[
            pl.BlockSpec((tm, d), lambda i, be, nu: (i, 0)),
            pl.BlockSpec((1, d, f2), lambda i, be, nu: (be[i], 0, 0)),
            pl.BlockSpec((1, 1, f2), lambda i, be, nu: (be[i], 0, 0)),
            pl.BlockSpec((1, f, d), lambda i, be, nu: (be[i], 0, 0)),
            pl.BlockSpec((1, 1, d), lambda i, be, nu: (be[i], 0, 0)),
        ],
        out_specs=pl.BlockSpec((tm, d), lambda i, be, nu: (i, 0)),
        scratch_shapes=[pltpu.VMEM((d, f2), BF16), pltpu.VMEM((f, d), BF16)],
    )
    return pl.pallas_call(
        _moe_kernel, grid_spec=grid_spec, out_shape=jax.ShapeDtypeStruct((n_rows, d), BF16),
        compiler_params=_params("arbitrary"), name="moe",
    )(block_expert, n_used, x_rows, w1, b1p, w2, b2r)


ROUTE_CHUNK = 2 * LANES


def _router_kernel(lg_ref, idx_ref, gate_ref, rank_ref, cnt_ref, carry_ref):
    @pl.when(pl.program_id(0) == 0)
    def _():
        carry_ref[...] = jnp.zeros_like(carry_ref)

    x = lg_ref[...]
    ne, tt = x.shape
    eidx = lax.broadcasted_iota(jnp.int32, (ne, tt), 0)
    member = jnp.zeros((ne, tt), F32)
    vals, ids = [], []
    for _ in range(TOP_K):
        m = jnp.max(x, axis=0, keepdims=True)
        sel = jnp.min(jnp.where(x == m, eidx, ne), axis=0, keepdims=True)
        hit = eidx == sel
        member = jnp.where(hit, 1.0, member)
        x = jnp.where(hit, -jnp.inf, x)
        vals.append(m)
        ids.append(sel)
    e = jnp.exp(jnp.concatenate(vals, axis=0) - vals[0])
    gate_ref[...] = e / jnp.sum(e, axis=0, keepdims=True)
    idx_ref[...] = jnp.concatenate(ids, axis=0)
    r = lax.broadcasted_iota(jnp.int32, (ROUTE_CHUNK, ROUTE_CHUNK), 0)
    c = lax.broadcasted_iota(jnp.int32, (ROUTE_CHUNK, ROUTE_CHUNK), 1)
    earlier = jnp.where(r < c, 1.0, 0.0).astype(BF16)
    carry = carry_ref[:, :1]
    before = []
    for ch in range(tt // ROUTE_CHUNK):
        mc = member[:, ch * ROUTE_CHUNK:(ch + 1) * ROUTE_CHUNK]
        before.append(_dot(mc.astype(BF16), earlier) + carry)
        carry = carry + jnp.sum(mc, axis=1, keepdims=True)
    before = jnp.concatenate(before, axis=1)
    ranks = [jnp.sum(jnp.where(eidx == ids[k], before, 0.0), axis=0, keepdims=True) for k in range(TOP_K)]
    rank_ref[...] = jnp.concatenate(ranks, axis=0).astype(jnp.int32)
    carry_ref[...] = jnp.broadcast_to(carry, carry_ref.shape)
    cnt_ref[...] = jnp.broadcast_to(carry, cnt_ref.shape).astype(jnp.int32)


def _router(logits_t, *, tt=ROUTER_TOKENS):
    ne, t = logits_t.shape
    tok = lambda i: (0, i)
    kspec = pl.BlockSpec((TOP_K, tt), tok)
    return pl.pallas_call(
        _router_kernel, grid=(t // tt,), in_specs=[pl.BlockSpec((ne, tt), tok)],
        out_specs=(kspec, kspec, kspec, pl.BlockSpec((ne, LANES), lambda i: (0, 0))),
        out_shape=(jax.ShapeDtypeStruct((TOP_K, t), jnp.int32), jax.ShapeDtypeStruct((TOP_K, t), F32),
                   jax.ShapeDtypeStruct((TOP_K, t), jnp.int32), jax.ShapeDtypeStruct((ne, LANES), jnp.int32)),
        scratch_shapes=[pltpu.VMEM((ne, LANES), F32)],
        compiler_params=_params("arbitrary"), name="router",
    )(logits_t)


def _route(logits_t, *, tm):
    ne, t = logits_t.shape
    idx, gates, rank, cnt = _router(logits_t)
    counts = cnt[:, 0]
    padded = (counts + tm - 1) // tm * tm
    pad_end = jnp.cumsum(padded)
    pad_start = pad_end - padded
    experts = jnp.arange(ne, dtype=jnp.int32).reshape(ne, 1, 1)
    slot = jnp.sum(jnp.where(idx[None] == experts, pad_start.reshape(ne, 1, 1), 0), axis=0) + rank
    n_blocks = -(-TOP_K * t // tm) + ne
    flat_slot = slot.reshape(-1)
    tok = jnp.broadcast_to(jnp.arange(t, dtype=jnp.int32), (TOP_K, t)).reshape(-1)
    row_tok = (jnp.arange(n_blocks * tm, dtype=jnp.int32) % t).at[flat_slot].add(
        tok - flat_slot % t, unique_indices=True, mode="promise_in_bounds")
    n_used = pad_end[-1] // tm
    blk_start = jnp.minimum(jnp.arange(n_blocks, dtype=jnp.int32), n_used - 1) * tm
    block_expert = jnp.minimum(jnp.sum(pad_end[:, None] <= blk_start[None, :], axis=0), ne - 1)
    return gates, row_tok, slot, block_expert.astype(jnp.int32), n_used.reshape(1).astype(jnp.int32)


def _final_kernel(h_ref, y_ref, gate_ref, p_ref, wp_ref, gp_ref, wgate_ref, gfin_ref, o_ref, *, last_layer):
    h = h_ref[...]
    g = gate_ref[...]
    gates = jnp.concatenate([g, jnp.zeros((SUBLANES - TOP_K, g.shape[1]), F32)], axis=0).T
    for k in range(TOP_K):
        h = h + gates[:, k:k + 1] * y_ref[k].astype(F32)
    ple = _rms(_dot(p_ref[...].astype(BF16), wp_ref[...]), gp_ref[...])
    h = h + jax.nn.sigmoid(_dot(h.astype(BF16), wgate_ref[...])) * ple
    o_ref[...] = _rms(h, gfin_ref[...]) if last_layer else h


def _final(h1, y_tok, gates, p2d, w_ple_proj, g_ple, w_ple_gate, g_final, *, last_layer, tm=FINAL_ROWS):
    t, d = h1.shape
    row = lambda i: (i, 0)
    const = lambda i: (0, 0)
    full = lambda a: pl.BlockSpec(a.shape, const)
    args = (h1, y_tok, gates, p2d, w_ple_proj.astype(BF16), g_ple.reshape(1, d), w_ple_gate.astype(BF16),
            g_final.reshape(1, d))
    in_specs = [pl.BlockSpec((tm, d), row), pl.BlockSpec((TOP_K, tm, d), lambda i: (0, i, 0)),
                pl.BlockSpec((TOP_K, tm), lambda i: (0, i)), pl.BlockSpec((tm, p2d.shape[1]), row)]
    in_specs += [full(a) for a in args[4:]]
    return pl.pallas_call(
        functools.partial(_final_kernel, last_layer=last_layer), grid=(t // tm,), in_specs=in_specs,
        out_specs=pl.BlockSpec((tm, d), row),
        out_shape=jax.ShapeDtypeStruct((t, d), F32),
        compiler_params=_params("arbitrary"), name="final",
    )(*args)


def kernel(x, p, g_mix, w_in, w_gla_gate, b_gla_gate, g_gla_norm, w_out, g_ffn, w_router, b_router,
           w1, b1, w2, b2, w_ple_gate, w_ple_proj, g_ple, g_final):
    batch, seq, d = x.shape
    depth = p.shape[0]
    t = batch * seq
    h = x.reshape(t, d)
    for i in range(depth):
        qa, ka, vt, kmean, qg, kg, vg, rg, la = _inproj(h, g_mix[i], w_in[i], w_gla_gate[i], b_gla_gate[i],
                                                        batch=batch, seq=seq)
        o_moba = _moba(qa, ka, vt, kmean, batch=batch, seq=seq)
        o_gla = _gla(qg, kg, vg, rg, la, g_gla_norm[i], batch=batch, seq=seq)
        h1, f, logits = _outproj(h, o_moba, o_gla, w_out[i], g_ffn[i], w_router[i], b_router[i])
        gates, row_tok, slot, block_expert, n_used = _route(logits, tm=MOE_ROWS)
        x_rows = jnp.take(f, row_tok, axis=0, mode="clip")
        y = _moe(x_rows, block_expert, n_used, w1[i], b1[i], w2[i], b2[i], tm=MOE_ROWS)
        y_tok = jnp.take(y, slot, axis=0, mode="clip")
        h = _final(h1, y_tok, gates, p[i].reshape(t, -1), w_ple_proj[i], g_ple[i], w_ple_gate[i],
                   g_final, last_layer=i == depth - 1)
    return h.reshape(batch, seq, d)
```

```python
import functools

import jax
import jax.numpy as jnp
import numpy as np
from jax import lax
from jax.experimental import pallas as pl
from jax.experimental.pallas import tpu as pltpu

HEAD_DIM = 64
MOBA_HEADS = 8
MOBA_WIDTH = MOBA_HEADS * HEAD_DIM
MOBA_BLOCK = 256
MOBA_TOPK = 3
MOBA_GROUP = 4
MOBA_QBLOCKS = 4
ALIBI_MAX = 8.0
GLA_HEADS = 4
GLA_DK = 64
GLA_DV = 128
GLA_KEY_WIDTH = GLA_HEADS * GLA_DK
GLA_VAL_WIDTH = GLA_HEADS * GLA_DV
GLA_GATE_RANK = 16
GLA_GATE_TAU = 16.0
GLA_CHUNK = 64
GLA_SEQS = 4
N_EXPERTS = 32
TOP_K = 4
SWIGLU_ALPHA = 1.702
SWIGLU_LIMIT = 7.0
EPS = 1e-6

LANES = 128
SUBLANES = 8
BF16_SUBLANES = 16

INPROJ_ROWS = 512
OUTPROJ_ROWS = 1024
FINAL_ROWS = 1024
GLA_TILE = 256
ROUTER_TOKENS = 2048
MOE_ROWS = 512
MASKED = -1e30
VMEM_LIMIT = 56 * 1024 * 1024

BF16 = jnp.bfloat16
F32 = jnp.float32

_NT = (((1,), (1,)), ((), ()))
_TN = (((0,), (0,)), ((), ()))


def _dot(a, b, dims=None):
    if dims is None:
        return jnp.dot(a, b, preferred_element_type=F32)
    return lax.dot_general(a, b, dims, preferred_element_type=F32)


def _rms(x, g):
    return x * lax.rsqrt(jnp.mean(x * x, axis=-1, keepdims=True) + EPS) * g


def _col_reduce(x, pair_op, final_op, rows=64):
    parts = [x[i:i + rows] for i in range(0, x.shape[0], rows)]
    while len(parts) > 1:
        parts = [pair_op(parts[i], parts[i + 1]) for i in range(0, len(parts), 2)]
    return final_op(parts[0], axis=0, keepdims=True)


def _params(*sem):
    return pltpu.CompilerParams(dimension_semantics=sem, vmem_limit_bytes=VMEM_LIMIT)


def _inproj_kernel(x_ref, g_ref, w_ref, wvt_ref, wag_ref, wgate_ref, bgate_ref,
                   qa_ref, ka_ref, vt_ref, km_ref, qg_ref, kg_ref, vg_ref, rg_ref, la_ref):
    i = pl.program_id(1)
    tm = x_ref.shape[0]
    nblk = tm // MOBA_BLOCK
    @pl.when(i == 0)
    def _():
        km_ref[...] = jnp.zeros_like(km_ref)

    ab = _rms(x_ref[...], g_ref[...]).astype(BF16)
    ag = _dot(ab, wag_ref[...]).astype(BF16)
    z = _dot(ag, wgate_ref[...]) + bgate_ref[...]
    la_ref[...] = jax.nn.log_sigmoid(z) / GLA_GATE_TAU
    qa_ref[...] = (_dot(ab, w_ref[:, :MOBA_WIDTH]) * (HEAD_DIM ** -0.5)).astype(BF16)
    ka = _dot(ab, w_ref[:, MOBA_WIDTH:2 * MOBA_WIDTH])
    ka_ref[...] = ka.astype(BF16)
    km = km_ref[...]
    blk_row = lax.broadcasted_iota(jnp.int32, km.shape, 0)
    for j in range(nblk):
        mean_j = jnp.mean(ka[j * MOBA_BLOCK:(j + 1) * MOBA_BLOCK], axis=0, keepdims=True)
        km = jnp.where(blk_row == i * nblk + j, mean_j, km)
    km_ref[...] = km
    vt = _dot(wvt_ref[...], ab, _NT).astype(BF16)
    for j in range(nblk):
        vt_ref[j] = vt[:, j * MOBA_BLOCK:(j + 1) * MOBA_BLOCK]
    gla_cols = 2 * GLA_KEY_WIDTH + 2 * GLA_VAL_WIDTH
    gla = _dot(ab, w_ref[:, 3 * MOBA_WIDTH:3 * MOBA_WIDTH + gla_cols])
    o = 0
    for ref in (qg_ref, kg_ref, vg_ref, rg_ref):
        w = ref.shape[1]
        ref[...] = gla[:, o:o + w].astype(ref.dtype)
        o += w


def _inproj(x2d, g_mix, w_in, w_gla_gate, b_gla_gate, *, batch, seq, tm=INPROJ_ROWS):
    t, d = x2d.shape
    nb = seq // MOBA_BLOCK
    o_v, o_qg = 2 * MOBA_WIDTH, 3 * MOBA_WIDTH
    o_ag = o_qg + 2 * GLA_KEY_WIDTH + 2 * GLA_VAL_WIDTH
    wb = w_in.astype(BF16)
    wvt = wb[:, o_v:o_qg].T
    wag = jnp.pad(wb[:, o_ag:], ((0, 0), (0, LANES - GLA_GATE_RANK)))
    wgate = jnp.pad(w_gla_gate, ((0, LANES - GLA_GATE_RANK), (0, 0))).astype(BF16)
    steps = seq // tm
    row = lambda b, i: (b * steps + i, 0)
    const = lambda b, i: (0, 0)
    full = lambda a: pl.BlockSpec(a.shape, const)
    args = (x2d, g_mix.reshape(1, d), wb, wvt, wag, wgate, b_gla_gate.reshape(1, -1))
    in_specs = [pl.BlockSpec((tm, d), row)] + [full(a) for a in args[1:]]
    out_shape = (
        jax.ShapeDtypeStruct((t, MOBA_WIDTH), BF16),
        jax.ShapeDtypeStruct((t, MOBA_WIDTH), BF16),
        jax.ShapeDtypeStruct((t // MOBA_BLOCK, MOBA_WIDTH, MOBA_BLOCK), BF16),
        jax.ShapeDtypeStruct((batch * nb, MOBA_WIDTH), F32),
        jax.ShapeDtypeStruct((t, GLA_KEY_WIDTH), F32),
        jax.ShapeDtypeStruct((t, GLA_KEY_WIDTH), F32),
        jax.ShapeDtypeStruct((t, GLA_VAL_WIDTH), BF16),
        jax.ShapeDtypeStruct((t, GLA_VAL_WIDTH), F32),
        jax.ShapeDtypeStruct((t, GLA_KEY_WIDTH), F32),
    )
    out_specs = (
        pl.BlockSpec((tm, MOBA_WIDTH), row),
        pl.BlockSpec((tm, MOBA_WIDTH), row),
        pl.BlockSpec((tm // MOBA_BLOCK, MOBA_WIDTH, MOBA_BLOCK), lambda b, i: (b * steps + i, 0, 0)),
        pl.BlockSpec((nb, MOBA_WIDTH), lambda b, i: (b, 0)),
        pl.BlockSpec((tm, GLA_KEY_WIDTH), row),
        pl.BlockSpec((tm, GLA_KEY_WIDTH), row),
        pl.BlockSpec((tm, GLA_VAL_WIDTH), row),
        pl.BlockSpec((tm, GLA_VAL_WIDTH), row),
        pl.BlockSpec((tm, GLA_KEY_WIDTH), row),
    )
    return pl.pallas_call(
        _inproj_kernel, grid=(batch, steps), in_specs=in_specs, out_specs=out_specs, out_shape=out_shape,
        compiler_params=_params("arbitrary", "arbitrary"), name="inproj",
    )(*args)


def _moba_kernel(slopes_ref, q_ref, k_ref, vt_ref, km_ref, o_ref, kaug_ref, s_ref):
    hg = pl.program_id(1)
    step = pl.program_id(2)
    blk = MOBA_BLOCK
    nb = km_ref.shape[0]
    hw = 2 * HEAD_DIM
    heads = q_ref.shape[1] // HEAD_DIM
    n_q = q_ref.shape[0] // blk
    feat_rows = nb + SUBLANES
    head_rows = lambda g: slice(g * HEAD_DIM, (g + 1) * HEAD_DIM)
    hs = range(heads)

    @pl.when(step == 0)
    def _():
        lane = lax.broadcasted_iota(jnp.int32, (blk, hw), 1)
        koff = lax.broadcasted_iota(jnp.int32, (blk, hw), 0).astype(F32)
        for g in hs:
            tile, hh = divmod(g, 2)
            slope = slopes_ref[hg * heads + g]
            base = HEAD_DIM * (1 - hh)
            for jb in range(nb):
                feat = jnp.where(lane == base + nb, slope * koff, 0.0)
                feat = jnp.where((lane == base + jb) | (lane == base + nb + 1), 1.0, feat)
                kb = k_ref[jb * blk:(jb + 1) * blk, tile * hw:(tile + 1) * hw]
                kaug_ref[g, jb * blk:(jb + 1) * blk, :] = jnp.where(lane // HEAD_DIM == hh, kb, feat.astype(BF16))

    kidx = lax.broadcasted_iota(jnp.int32, (blk, blk), 0)
    qidx = lax.broadcasted_iota(jnp.int32, (blk, blk), 1)
    causal = kidx <= qidx
    nidx = lax.broadcasted_iota(jnp.int32, (nb, blk), 0)
    r8 = lax.broadcasted_iota(jnp.int32, (SUBLANES, blk), 0)
    qoff = lax.broadcasted_iota(jnp.int32, (SUBLANES, blk), 1).astype(F32)
    km_lane = lax.broadcasted_iota(jnp.int32, km_ref.shape, 1)
    km = km_ref[...]
    slopes = [slopes_ref[hg * heads + g] for g in hs]
    ones = jnp.ones((BF16_SUBLANES, blk), BF16)
    fill = jnp.zeros((HEAD_DIM - feat_rows, blk), F32)

    def values_and_ones(vt_blk, g):
        return jnp.concatenate([vt_blk[head_rows(g), :], ones], axis=0)

    def scores(q_past, pair, buf, slot):
        rows = pl.ds(pl.multiple_of(pair * 2 * blk, 2 * blk), 2 * blk)
        tops = []
        for g in hs:
            s = _dot(kaug_ref[g, rows, :], q_past[g])
            s_ref[buf, slot, g] = s
            tops.append(_col_reduce(s, jnp.maximum, jnp.max))
        return tops

    def absorb_head(pair, buf, slot, top, m_i, acc, g):
        m_new = jnp.maximum(m_i, top)
        p = jnp.exp((s_ref[buf, slot, g] - m_new).astype(BF16))
        acc = (jnp.exp(m_i - m_new) * acc + _dot(values_and_ones(vt_ref[2 * pair], g), p[:blk])
               + _dot(values_and_ones(vt_ref[2 * pair + 1], g), p[blk:]))
        return [m_new, acc]

    def absorb(pair, buf, slot, tops, carry):
        out = []
        for g in hs:
            out += absorb_head(pair, buf, slot, tops[g], carry[2 * g], carry[2 * g + 1], g)
        return out

    def prologues():
        qs = range(n_q)
        cs = [step * n_q + i for i in qs]
        qts = [q_ref[i * blk:(i + 1) * blk, :].astype(F32).T for i in qs]
        qtbs = [qt.astype(BF16) for qt in qts]
        kmh = [jnp.where(km_lane // HEAD_DIM == g, km, 0.0).astype(BF16) for g in hs]
        gates = [[jnp.where(nidx < cs[i], _dot(kmh[g], qtbs[i]), -jnp.inf) for g in hs] for i in qs]
        ranks = [[jnp.zeros((nb, blk), jnp.int32) for _ in hs] for _ in qs]
        for m in range(nb - 1):
            for i in qs:
                for g in hs:
                    gm = gates[i][g][m:m + 1, :]
                    beats = (gm > gates[i][g]) | ((gm == gates[i][g]) & (m < nidx))
                    ranks[i][g] = ranks[i][g] + beats.astype(jnp.int32)

        def q_aug(i, g, bias):
            dist0 = -slopes[g] * ((cs[i] - nidx) * blk).astype(F32)
            tail = jnp.where(r8 == 0, 1.0, jnp.where(r8 == 1, -slopes[g] * qoff, 0.0))
            feats = [jnp.where(bias, dist0, MASKED), tail, fill]
            qh = qts[i][head_rows(g), :]
            return jnp.concatenate([qh] + feats if g % 2 == 0 else feats + [qh], axis=0).astype(BF16)

        q_past = [[q_aug(i, g, (nidx < cs[i]) & (ranks[i][g] < MOBA_TOPK)) for g in hs] for i in qs]
        q_own = [[q_aug(i, g, nidx == cs[i]) for g in hs] for i in qs]
        own_rows = [pl.ds(pl.multiple_of(c * blk, blk), blk) for c in cs]
        s_own = [[jnp.where(causal, _dot(kaug_ref[g, own_rows[i], :], q_own[i][g]), MASKED) for g in hs] for i in qs]
        m_own = [[jnp.max(s, axis=0, keepdims=True) for s in row] for row in s_own]
        p_own = [[jnp.exp((s_own[i][g] - m_own[i][g]).astype(BF16)) for g in hs] for i in qs]
        started = []
        for i in qs:
            state = []
            for g in hs:
                state += [m_own[i][g], _dot(values_and_ones(vt_ref[cs[i]], g), p_own[i][g])]
            started.append((cs[i], q_past[i], state))
        return [(c, q_past_i, state + scores(q_past_i, 0, i, 0)) for i, (c, q_past_i, state) in enumerate(started)]

    def attend(i, c, q_past, carry):
        n_state = 2 * heads

        def advance(n, carry, slot):
            tops = scores(q_past, n + 1, i, 1 - slot)
            return tuple(absorb(n, i, slot, carry[n_state:], carry[:n_state]) + tops)

        def body(n, carry):
            return lax.cond(n % 2 == 0, lambda: advance(n, carry, 0), lambda: advance(n, carry, 1))

        last = jnp.maximum((c + 1) // 2 - 1, 0)
        return last, lax.fori_loop(0, last, body, tuple(carry))

    started = prologues()
    looped = [attend(i, c, q_past, carry) for i, (c, q_past, carry) in enumerate(started)]
    accs = [[None] * heads for _ in looped]
    for g in hs:
        for i, (last, carry) in enumerate(looped):
            accs[i][g] = absorb_head(last, i, last % 2, carry[2 * heads + g], carry[2 * g], carry[2 * g + 1], g)[1]
    for i in range(n_q):
        out_t = jnp.concatenate([a[:HEAD_DIM] / a[HEAD_DIM:HEAD_DIM + 1] for a in accs[i]], axis=0)
        o_ref[i * blk:(i + 1) * blk, :] = out_t.T.astype(o_ref.dtype)


def _alibi_slopes():
    slopes = np.exp2(-ALIBI_MAX * np.arange(1, MOBA_HEADS + 1, dtype=np.float32) / MOBA_HEADS).astype(np.float32)
    worst = slopes * np.float32(MOBA_BLOCK - 1)
    assert np.all(worst.astype(jnp.bfloat16).astype(np.float32) == worst), "ALiBi slopes not bf16-exact"
    return slopes


def _moba(qa, ka, vt, kmean, *, batch, seq):
    t = qa.shape[0]
    nb = seq // MOBA_BLOCK
    gw = MOBA_GROUP * HEAD_DIM
    assert nb % SUBLANES == 0 and nb + SUBLANES <= HEAD_DIM, "bias features must fit the spare contraction lanes"
    grid_spec = pltpu.PrefetchScalarGridSpec(
        num_scalar_prefetch=1,
        grid=(batch, MOBA_HEADS // MOBA_GROUP, nb // MOBA_QBLOCKS),
        in_specs=[
            pl.BlockSpec((MOBA_QBLOCKS * MOBA_BLOCK, gw), lambda b, h, c, s: (b * (nb // MOBA_QBLOCKS) + c, h)),
            pl.BlockSpec((seq, gw), lambda b, h, c, s: (b, h)),
            pl.BlockSpec((nb, gw, MOBA_BLOCK), lambda b, h, c, s: (b, h, 0)),
            pl.BlockSpec((nb, gw), lambda b, h, c, s: (b, h)),
        ],
        out_specs=pl.BlockSpec((MOBA_QBLOCKS * MOBA_BLOCK, gw), lambda b, h, c, s: (b * (nb // MOBA_QBLOCKS) + c, h)),
        scratch_shapes=[pltpu.VMEM((MOBA_GROUP, seq, 2 * HEAD_DIM), BF16),
                        pltpu.VMEM((MOBA_QBLOCKS, 2, MOBA_GROUP, 2 * MOBA_BLOCK, MOBA_BLOCK), F32)],
    )
    return pl.pallas_call(
        _moba_kernel, grid_spec=grid_spec, out_shape=jax.ShapeDtypeStruct((t, MOBA_WIDTH), BF16),
        compiler_params=_params("arbitrary", "arbitrary", "arbitrary"), name="moba",
    )(jnp.asarray(_alibi_slopes()), qa, ka, vt, kmean)


def _gla_kernel(q_ref, k_ref, v_ref, r_ref, la_ref, gn_ref, o_ref, st_ref):
    @pl.when(pl.program_id(1) == 0)
    def _():
        st_ref[...] = jnp.zeros_like(st_ref)

    ch = GLA_CHUNK
    n_seqs, tc = q_ref.shape[:2]
    n_chunks = tc // ch
    row = lax.broadcasted_iota(jnp.int32, (tc, tc), 0)
    col = lax.broadcasted_iota(jnp.int32, (tc, tc), 1)
    causal = (row // ch == col // ch) & (row >= col)
    tri = jnp.where(causal, 1.0, 0.0).astype(BF16)
    lane = lax.broadcasted_iota(jnp.int32, (1, GLA_KEY_WIDTH), 1)
    gn = gn_ref[...]
    seqs = range(n_seqs)
    heads = range(GLA_HEADS)
    gs = []
    for s in seqs:
        la = la_ref[s]
        la_hi = la.astype(BF16)
        rest = la - la_hi.astype(F32)
        la_mid = rest.astype(BF16)
        la_lo = (rest - la_mid.astype(F32)).astype(BF16)
        gs.append(_dot(tri, la_hi) + _dot(tri, la_mid) + _dot(tri, la_lo))
    ends = [[g[(ci + 1) * ch - 1:(ci + 1) * ch, :] for ci in range(n_chunks)] for g in gs]
    qhs, kds, kths, vhs = [], [], [], []
    for s in seqs:
        g = gs[s]
        g_last = jnp.concatenate([jnp.broadcast_to(e, (ch, GLA_KEY_WIDTH)) for e in ends[s]], axis=0)
        qd = q_ref[s] * (GLA_DK ** -0.5) * jnp.exp(g)
        k = k_ref[s]
        kds.append((k * jnp.exp(-g)).astype(BF16))
        kt = k * jnp.exp(g_last - g)
        qhs.append([jnp.where(lane // GLA_DK == h, qd, 0.0).astype(BF16) for h in heads])
        kths.append([jnp.where(lane // GLA_DK == h, kt, 0.0).astype(BF16) for h in heads])
        vhs.append([v_ref[s, :, h * GLA_DV:(h + 1) * GLA_DV] for h in heads])
    intra = [[None] * GLA_HEADS for _ in seqs]
    for h in heads:
        for s in seqs:
            a = jnp.where(causal, _dot(qhs[s][h], kds[s], _NT), 0.0)
            intra[s][h] = _dot(a.astype(BF16), vhs[s][h])
    states = [st_ref[s] for s in seqs]
    inter = [[[] for _ in heads] for _ in seqs]
    for ci in range(n_chunks):
        rows = slice(ci * ch, (ci + 1) * ch)
        for s in seqs:
            state_b = states[s].astype(BF16)
            new_state = states[s] * jnp.exp(ends[s][ci])
            for h in heads:
                inter[s][h].append(_dot(qhs[s][h][rows], state_b, _NT))
                new_state = new_state + _dot(vhs[s][h][rows], kths[s][h][rows], _TN)
            states[s] = new_state
    for s in seqs:
        st_ref[s] = states[s]
    for h in heads:
        for s in seqs:
            o = intra[s][h] + jnp.concatenate(inter[s][h], axis=0)
            rh = r_ref[s, :, h * GLA_DV:(h + 1) * GLA_DV]
            o_ref[s, :, h * GLA_DV:(h + 1) * GLA_DV] = (_rms(o, gn) * (rh * jax.nn.sigmoid(rh))).astype(o_ref.dtype)


def _gla(qg, kg, vg, rg, la, g_norm, *, batch, seq, tc=GLA_TILE):
    n_seqs = GLA_SEQS if batch % GLA_SEQS == 0 else 1
    by_seq = lambda a: a.reshape(batch, seq, a.shape[-1])
    blk = lambda i, j: (i, j, 0)
    kspec = pl.BlockSpec((n_seqs, tc, GLA_KEY_WIDTH), blk)
    vspec = pl.BlockSpec((n_seqs, tc, GLA_VAL_WIDTH), blk)
    out = pl.pallas_call(
        _gla_kernel, grid=(batch // n_seqs, seq // tc),
        in_specs=[kspec, kspec, vspec, vspec, kspec, pl.BlockSpec((1, GLA_DV), lambda i, j: (0, 0))],
        out_specs=vspec, out_shape=jax.ShapeDtypeStruct((batch, seq, GLA_VAL_WIDTH), BF16),
        scratch_shapes=[pltpu.VMEM((n_seqs, GLA_DV, GLA_KEY_WIDTH), F32)],
        compiler_params=_params("arbitrary", "arbitrary"), name="gla",
    )(by_seq(qg), by_seq(kg), by_seq(vg), by_seq(rg), by_seq(la), g_norm.reshape(1, GLA_DV))
    return out.reshape(batch * seq, GLA_VAL_WIDTH)


def _outproj_kernel(x_ref, om_ref, og_ref, wm_ref, wg_ref, gf_ref, wr_ref, br_ref, h_ref, f_ref, lg_ref):
    h = x_ref[...] + _dot(om_ref[...], wm_ref[...]) + _dot(og_ref[...], wg_ref[...])
    h_ref[...] = h
    f = _rms(h, gf_ref[...]).astype(BF16)
    f_ref[...] = f
    lg_ref[...] = _dot(wr_ref[...], f, _NT) + br_ref[...]


def _outproj(x2d, o_moba, o_gla, w_out, g_ffn, w_router, b_router, *, tm=OUTPROJ_ROWS):
    t, d = x2d.shape
    wm = w_out[:MOBA_WIDTH].astype(BF16)
    wg = w_out[MOBA_WIDTH:].astype(BF16)
    wr = w_router.T.astype(BF16)
    br = b_router.reshape(N_EXPERTS, 1)
    row = lambda i: (i, 0)
    const = lambda i: (0, 0)
    full = lambda a: pl.BlockSpec(a.shape, const)
    args = (x2d, o_moba, o_gla, wm, wg, g_ffn.reshape(1, d), wr, br)
    in_specs = [pl.BlockSpec((tm, d), row), pl.BlockSpec((tm, MOBA_WIDTH), row),
                pl.BlockSpec((tm, GLA_VAL_WIDTH), row)] + [full(a) for a in args[3:]]
    return pl.pallas_call(
        _outproj_kernel, grid=(t // tm,), in_specs=in_specs,
        out_specs=(pl.BlockSpec((tm, d), row), pl.BlockSpec((tm, d), row),
                   pl.BlockSpec((N_EXPERTS, tm), lambda i: (0, i))),
        out_shape=(jax.ShapeDtypeStruct((t, d), F32), jax.ShapeDtypeStruct((t, d), BF16),
                   jax.ShapeDtypeStruct((N_EXPERTS, t), F32)),
        compiler_params=_params("arbitrary"), name="outproj",
    )(*args)


PAIR_CHUNK = 2 * LANES


def _moe_kernel(be_ref, nused_ref, x_ref, w1_ref, b1_ref, w2_ref, b2_ref, o_ref, w1s_ref, w2s_ref):
    i = pl.program_id(0)
    f = w2_ref.shape[1]

    @pl.when((i == 0) | (be_ref[i] != be_ref[jnp.maximum(i - 1, 0)]))
    def _():
        r = lax.broadcasted_iota(jnp.int32, (PAIR_CHUNK, PAIR_CHUNK), 0)
        c = lax.broadcasted_iota(jnp.int32, (PAIR_CHUNK, PAIR_CHUNK), 1)
        src_col = jnp.where(c < LANES, 2 * c, 2 * (c - LANES) + 1)
        perm = jnp.where(r == src_col, 1.0, 0.0).astype(BF16)
        for cc in range(2 * f // PAIR_CHUNK):
            chunk = w1_ref[0, :, cc * PAIR_CHUNK:(cc + 1) * PAIR_CHUNK].astype(BF16)
            res = _dot(chunk, perm).astype(BF16)
            w1s_ref[:, cc * LANES:(cc + 1) * LANES] = res[:, :LANES]
            w1s_ref[:, f + cc * LANES:f + (cc + 1) * LANES] = res[:, LANES:]
        w2s_ref[...] = w2_ref[0].astype(BF16)

    @pl.when(i < nused_ref[0])
    def _():
        hid = _dot(x_ref[...], w1s_ref[...]) + b1_ref[0]
        glu = jnp.minimum(hid[:, :f], SWIGLU_LIMIT)
        lin = jnp.clip(hid[:, f:], -SWIGLU_LIMIT, SWIGLU_LIMIT)
        act = glu * jax.nn.sigmoid(SWIGLU_ALPHA * glu) * (lin + 1.0)
        o_ref[...] = (_dot(act.astype(BF16), w2s_ref[...]) + b2_ref[0]).astype(o_ref.dtype)

    @pl.when(i >= nused_ref[0])
    def _():
        o_ref[...] = jnp.zeros_like(o_ref)


def _moe(x_rows, block_expert, n_used, w1, b1, w2, b2, *, tm):
    n_rows, d = x_rows.shape
    e, _, f2 = w1.shape
    f = f2 // 2
    b1p = jnp.concatenate([b1[:, 0::2], b1[:, 1::2]], axis=-1).reshape(e, 1, f2)
    b2r = b2.reshape(e, 1, d)
    grid_spec = pltpu.PrefetchScalarGridSpec(
        num_scalar_prefetch=2,
        grid=(n_rows // tm,),
        in_specs=[
            pl.BlockSpec((tm, d), lambda i, be, nu: (i, 0)),
            pl.BlockSpec((1, d, f2), lambda i, be, nu: (be[i], 0, 0)),
            pl.BlockSpec((1, 1, f2), lambda i, be, nu: (be[i], 0, 0)),
            pl.BlockSpec((1, f, d), lambda i, be, nu: (be[i], 0, 0)),
            pl.BlockSpec((1, 1, d), lambda i, be, nu: (be[i], 0, 0)),
        ],
        out_specs=pl.BlockSpec((tm, d), lambda i, be, nu: (i, 0)),
        scratch_shapes=[pltpu.VMEM((d, f2), BF16), pltpu.VMEM((f, d), BF16)],
    )
    return pl.pallas_call(
        _moe_kernel, grid_spec=grid_spec, out_shape=jax.ShapeDtypeStruct((n_rows, d), BF16),
        compiler_params=_params("arbitrary"), name="moe",
    )(block_expert, n_used, x_rows, w1, b1p, w2, b2r)


ROUTE_CHUNK = 2 * LANES


def _router_kernel(lg_ref, idx_ref, gate_ref, rank_ref, cnt_ref, carry_ref):
    @pl.when(pl.program_id(0) == 0)
    def _():
        carry_ref[...] = jnp.zeros_like(carry_ref)

    x = lg_ref[...]
    ne, tt = x.shape
    eidx = lax.broadcasted_iota(jnp.int32, (ne, tt), 0)
    member = jnp.zeros((ne, tt), F32)
    vals, ids = [], []
    for _ in range(TOP_K):
        m = jnp.max(x, axis=0, keepdims=True)
        sel = jnp.min(jnp.where(x == m, eidx, ne), axis=0, keepdims=True)
        hit = eidx == sel
        member = jnp.where(hit, 1.0, member)
        x = jnp.where(hit, -jnp.inf, x)
        vals.append(m)
        ids.append(sel)
    e = jnp.exp(jnp.concatenate(vals, axis=0) - vals[0])
    gate_ref[...] = e / jnp.sum(e, axis=0, keepdims=True)
    idx_ref[...] = jnp.concatenate(ids, axis=0)
    r = lax.broadcasted_iota(jnp.int32, (ROUTE_CHUNK, ROUTE_CHUNK), 0)
    c = lax.broadcasted_iota(jnp.int32, (ROUTE_CHUNK, ROUTE_CHUNK), 1)
    earlier = jnp.where(r < c, 1.0, 0.0).astype(BF16)
    carry = carry_ref[:, :1]
    before = []
    for ch in range(tt // ROUTE_CHUNK):
        mc = member[:, ch * ROUTE_CHUNK:(ch + 1) * ROUTE_CHUNK]
        before.append(_dot(mc.astype(BF16), earlier) + carry)
        carry = carry + jnp.sum(mc, axis=1, keepdims=True)
    before = jnp.concatenate(before, axis=1)
    ranks = [jnp.sum(jnp.where(eidx == ids[k], before, 0.0), axis=0, keepdims=True) for k in range(TOP_K)]
    rank_ref[...] = jnp.concatenate(ranks, axis=0).astype(jnp.int32)
    carry_ref[...] = jnp.broadcast_to(carry, carry_ref.shape)
    cnt_ref[...] = jnp.broadcast_to(carry, cnt_ref.shape).astype(jnp.int32)


def _router(logits_t, *, tt=ROUTER_TOKENS):
    ne, t = logits_t.shape
    tok = lambda i: (0, i)
    kspec = pl.BlockSpec((TOP_K, tt), tok)
    return pl.pallas_call(
        _router_kernel, grid=(t // tt,), in_specs=[pl.BlockSpec((ne, tt), tok)],
        out_specs=(kspec, kspec, kspec, pl.BlockSpec((ne, LANES), lambda i: (0, 0))),
        out_shape=(jax.ShapeDtypeStruct((TOP_K, t), jnp.int32), jax.ShapeDtypeStruct((TOP_K, t), F32),
                   jax.ShapeDtypeStruct((TOP_K, t), jnp.int32), jax.ShapeDtypeStruct((ne, LANES), jnp.int32)),
        scratch_shapes=[pltpu.VMEM((ne, LANES), F32)],
        compiler_params=_params("arbitrary"), name="router",
    )(logits_t)


def _route(logits_t, *, tm):
    ne, t = logits_t.shape
    idx, gates, rank, cnt = _router(logits_t)
    counts = cnt[:, 0]
    padded = (counts + tm - 1) // tm * tm
    pad_end = jnp.cumsum(padded)
    pad_start = pad_end - padded
    experts = jnp.arange(ne, dtype=jnp.int32).reshape(ne, 1, 1)
    slot = jnp.sum(jnp.where(idx[None] == experts, pad_start.reshape(ne, 1, 1), 0), axis=0) + rank
    n_blocks = -(-TOP_K * t // tm) + ne
    flat_slot = slot.reshape(-1)
    tok = jnp.broadcast_to(jnp.arange(t, dtype=jnp.int32), (TOP_K, t)).reshape(-1)
    row_tok = (jnp.arange(n_blocks * tm, dtype=jnp.int32) % t).at[flat_slot].add(
        tok - flat_slot % t, unique_indices=True, mode="promise_in_bounds")
    n_used = pad_end[-1] // tm
    blk_start = jnp.minimum(jnp.arange(n_blocks, dtype=jnp.int32), n_used - 1) * tm
    block_expert = jnp.minimum(jnp.sum(pad_end[:, None] <= blk_start[None, :], axis=0), ne - 1)
    return gates, row_tok, slot, block_expert.astype(jnp.int32), n_used.reshape(1).astype(jnp.int32)


def _final_kernel(h_ref, y_ref, gate_ref, p_ref, wp_ref, gp_ref, wgate_ref, gfin_ref, o_ref, *, last_layer):
    h = h_ref[...]
    g = gate_ref[...]
    gates = jnp.concatenate([g, jnp.zeros((SUBLANES - TOP_K, g.shape[1]), F32)], axis=0).T
    for k in range(TOP_K):
        h = h + gates[:, k:k + 1] * y_ref[k].astype(F32)
    ple = _rms(_dot(p_ref[...].astype(BF16), wp_ref[...]), gp_ref[...])
    h = h + jax.nn.sigmoid(_dot(h.astype(BF16), wgate_ref[...])) * ple
    o_ref[...] = _rms(h, gfin_ref[...]) if last_layer else h


def _final(h1, y_tok, gates, p2d, w_ple_proj, g_ple, w_ple_gate, g_final, *, last_layer, tm=FINAL_ROWS):
    t, d = h1.shape
    row = lambda i: (i, 0)
    const = lambda i: (0, 0)
    full = lambda a: pl.BlockSpec(a.shape, const)
    args = (h1, y_tok, gates, p2d, w_ple_proj.astype(BF16), g_ple.reshape(1, d), w_ple_gate.astype(BF16),
            g_final.reshape(1, d))
    in_specs = [pl.BlockSpec((tm, d), row), pl.BlockSpec((TOP_K, tm, d), lambda i: (0, i, 0)),
                pl.BlockSpec((TOP_K, tm), lambda i: (0, i)), pl.BlockSpec((tm, p2d.shape[1]), row)]
    in_specs += [full(a) for a in args[4:]]
    return pl.pallas_call(
        functools.partial(_final_kernel, last_layer=last_layer), grid=(t // tm,), in_specs=in_specs,
        out_specs=pl.BlockSpec((tm, d), row),
        out_shape=jax.ShapeDtypeStruct((t, d), F32),
        compiler_params=_params("arbitrary"), name="final",
    )(*args)


def kernel(x, p, g_mix, w_in, w_gla_gate, b_gla_gate, g_gla_norm, w_out, g_ffn, w_router, b_router,
           w1, b1, w2, b2, w_ple_gate, w_ple_proj, g_ple, g_final):
    batch, seq, d = x.shape
    depth = p.shape[0]
    t = batch * seq
    h = x.reshape(t, d)
    for i in range(depth):
        qa, ka, vt, kmean, qg, kg, vg, rg, la = _inproj(h, g_mix[i], w_in[i], w_gla_gate[i], b_gla_gate[i],
                                                        batch=batch, seq=seq)
        o_moba = _moba(qa, ka, vt, kmean, batch=batch, seq=seq)
        o_gla = _gla(qg, kg, vg, rg, la, g_gla_norm[i], batch=batch, seq=seq)
        h1, f, logits = _outproj(h, o_moba, o_gla, w_out[i], g_ffn[i], w_router[i], b_router[i])
        gates, row_tok, slot, block_expert, n_used = _route(logits, tm=MOE_ROWS)
        x_rows = jnp.take(f, row_tok, axis=0, mode="clip")
        y = _moe(x_rows, block_expert, n_used, w1[i], b1[i], w2[i], b2[i], tm=MOE_ROWS)
        y_tok = jnp.take(y, slot, axis=0, mode="clip")
        h = _final(h1, y_tok, gates, p[i].reshape(t, -1), w_ple_proj[i], g_ple[i], w_ple_gate[i],
                   g_final, last_layer=i == depth - 1)
    return h.reshape(batch, seq, d)
```

```python
import functools

import jax
import jax.numpy as jnp
import numpy as np
from jax import lax
from jax.experimental import pallas as pl
from jax.experimental.pallas import tpu as pltpu

HEAD_DIM = 64
MOBA_HEADS = 8
MOBA_WIDTH = MOBA_HEADS * HEAD_DIM
MOBA_BLOCK = 256
MOBA_TOPK = 3
MOBA_GROUP = 4
MOBA_QBLOCKS = 8
ALIBI_MAX = 8.0
GLA_HEADS = 4
GLA_DK = 64
GLA_DV = 128
GLA_KEY_WIDTH = GLA_HEADS * GLA_DK
GLA_VAL_WIDTH = GLA_HEADS * GLA_DV
GLA_GATE_RANK = 16
GLA_GATE_TAU = 16.0
GLA_CHUNK = 64
GLA_SEQS = 4
N_EXPERTS = 32
TOP_K = 4
SWIGLU_ALPHA = 1.702
SWIGLU_LIMIT = 7.0
EPS = 1e-6

LANES = 128
SUBLANES = 8
BF16_SUBLANES = 16

INPROJ_ROWS = 512
OUTPROJ_ROWS = 1024
FINAL_ROWS = 1024
GLA_TILE = 256
ROUTER_TOKENS = 2048
MOE_ROWS = 512
MASKED = -1e30
VMEM_LIMIT = 56 * 1024 * 1024

BF16 = jnp.bfloat16
F32 = jnp.float32

_NT = (((1,), (1,)), ((), ()))
_TN = (((0,), (0,)), ((), ()))


def _dot(a, b, dims=None):
    if dims is None:
        return jnp.dot(a, b, preferred_element_type=F32)
    return lax.dot_general(a, b, dims, preferred_element_type=F32)


def _rms(x, g):
    return x * lax.rsqrt(jnp.mean(x * x, axis=-1, keepdims=True) + EPS) * g


def _col_reduce(x, pair_op, final_op, rows=64):
    parts = [x[i:i + rows] for i in range(0, x.shape[0], rows)]
    while len(parts) > 1:
        parts = [pair_op(parts[i], parts[i + 1]) for i in range(0, len(parts), 2)]
    return final_op(parts[0], axis=0, keepdims=True)


def _params(*sem):
    return pltpu.CompilerParams(dimension_semantics=sem, vmem_limit_bytes=VMEM_LIMIT)


def _inproj_kernel(x_ref, g_ref, w_ref, wvt_ref, wag_ref, wgate_ref, bgate_ref,
                   qa_ref, ka_ref, vt_ref, km_ref, qg_ref, kg_ref, vg_ref, rg_ref, la_ref):
    i = pl.program_id(1)
    tm = x_ref.shape[0]
    nblk = tm // MOBA_BLOCK
    @pl.when(i == 0)
    def _():
        km_ref[...] = jnp.zeros_like(km_ref)

    ab = _rms(x_ref[...], g_ref[...]).astype(BF16)
    ag = _dot(ab, wag_ref[...]).astype(BF16)
    z = _dot(ag, wgate_ref[...]) + bgate_ref[...]
    la_ref[...] = jax.nn.log_sigmoid(z) / GLA_GATE_TAU
    qa_ref[...] = (_dot(ab, w_ref[:, :MOBA_WIDTH]) * (HEAD_DIM ** -0.5)).astype(BF16)
    ka = _dot(ab, w_ref[:, MOBA_WIDTH:2 * MOBA_WIDTH])
    ka_ref[...] = ka.astype(BF16)
    km = km_ref[...]
    blk_row = lax.broadcasted_iota(jnp.int32, km.shape, 0)
    for j in range(nblk):
        mean_j = jnp.mean(ka[j * MOBA_BLOCK:(j + 1) * MOBA_BLOCK], axis=0, keepdims=True)
        km = jnp.where(blk_row == i * nblk + j, mean_j, km)
    km_ref[...] = km
    vt = _dot(wvt_ref[...], ab, _NT).astype(BF16)
    for j in range(nblk):
        vt_ref[j] = vt[:, j * MOBA_BLOCK:(j + 1) * MOBA_BLOCK]
    gla_cols = 2 * GLA_KEY_WIDTH + 2 * GLA_VAL_WIDTH
    gla = _dot(ab, w_ref[:, 3 * MOBA_WIDTH:3 * MOBA_WIDTH + gla_cols])
    o = 0
    for ref in (qg_ref, kg_ref, vg_ref, rg_ref):
        w = ref.shape[1]
        ref[...] = gla[:, o:o + w].astype(ref.dtype)
        o += w


def _inproj(x2d, g_mix, w_in, w_gla_gate, b_gla_gate, *, batch, seq, tm=INPROJ_ROWS):
    t, d = x2d.shape
    nb = seq // MOBA_BLOCK
    o_v, o_qg = 2 * MOBA_WIDTH, 3 * MOBA_WIDTH
    o_ag = o_qg + 2 * GLA_KEY_WIDTH + 2 * GLA_VAL_WIDTH
    wb = w_in.astype(BF16)
    wvt = wb[:, o_v:o_qg].T
    wag = jnp.pad(wb[:, o_ag:], ((0, 0), (0, LANES - GLA_GATE_RANK)))
    wgate = jnp.pad(w_gla_gate, ((0, LANES - GLA_GATE_RANK), (0, 0))).astype(BF16)
    steps = seq // tm
    row = lambda b, i: (b * steps + i, 0)
    const = lambda b, i: (0, 0)
    full = lambda a: pl.BlockSpec(a.shape, const)
    args = (x2d, g_mix.reshape(1, d), wb, wvt, wag, wgate, b_gla_gate.reshape(1, -1))
    in_specs = [pl.BlockSpec((tm, d), row)] + [full(a) for a in args[1:]]
    out_shape = (
        jax.ShapeDtypeStruct((t, MOBA_WIDTH), BF16),
        jax.ShapeDtypeStruct((t, MOBA_WIDTH), BF16),
        jax.ShapeDtypeStruct((t // MOBA_BLOCK, MOBA_WIDTH, MOBA_BLOCK), BF16),
        jax.ShapeDtypeStruct((batch * nb, MOBA_WIDTH), F32),
        jax.ShapeDtypeStruct((t, GLA_KEY_WIDTH), F32),
        jax.ShapeDtypeStruct((t, GLA_KEY_WIDTH), F32),
        jax.ShapeDtypeStruct((t, GLA_VAL_WIDTH), BF16),
        jax.ShapeDtypeStruct((t, GLA_VAL_WIDTH), F32),
        jax.ShapeDtypeStruct((t, GLA_KEY_WIDTH), F32),
    )
    out_specs = (
        pl.BlockSpec((tm, MOBA_WIDTH), row),
        pl.BlockSpec((tm, MOBA_WIDTH), row),
        pl.BlockSpec((tm // MOBA_BLOCK, MOBA_WIDTH, MOBA_BLOCK), lambda b, i: (b * steps + i, 0, 0)),
        pl.BlockSpec((nb, MOBA_WIDTH), lambda b, i: (b, 0)),
        pl.BlockSpec((tm, GLA_KEY_WIDTH), row),
        pl.BlockSpec((tm, GLA_KEY_WIDTH), row),
        pl.BlockSpec((tm, GLA_VAL_WIDTH), row),
        pl.BlockSpec((tm, GLA_VAL_WIDTH), row),
        pl.BlockSpec((tm, GLA_KEY_WIDTH), row),
    )
    return pl.pallas_call(
        _inproj_kernel, grid=(batch, steps), in_specs=in_specs, out_specs=out_specs, out_shape=out_shape,
        compiler_params=_params("arbitrary", "arbitrary"), name="inproj",
    )(*args)


def _moba_kernel(slopes_ref, q_ref, k_ref, vt_ref, km_ref, o_ref, kaug_ref, s_ref):
    hg = pl.program_id(1)
    step = pl.program_id(2)
    blk = MOBA_BLOCK
    nb = km_ref.shape[0]
    hw = 2 * HEAD_DIM
    heads = q_ref.shape[1] // HEAD_DIM
    n_q = q_ref.shape[0] // blk
    feat_rows = nb + SUBLANES
    head_rows = lambda g: slice(g * HEAD_DIM, (g + 1) * HEAD_DIM)
    hs = range(heads)

    @pl.when(step == 0)
    def _():
        lane = lax.broadcasted_iota(jnp.int32, (blk, hw), 1)
        koff = lax.broadcasted_iota(jnp.int32, (blk, hw), 0).astype(F32)
        for g in hs:
            tile, hh = divmod(g, 2)
            slope = slopes_ref[hg * heads + g]
            base = HEAD_DIM * (1 - hh)
            for jb in range(nb):
                feat = jnp.where(lane == base + nb, slope * koff, 0.0)
                feat = jnp.where((lane == base + jb) | (lane == base + nb + 1), 1.0, feat)
                kb = k_ref[jb * blk:(jb + 1) * blk, tile * hw:(tile + 1) * hw]
                kaug_ref[g, jb * blk:(jb + 1) * blk, :] = jnp.where(lane // HEAD_DIM == hh, kb, feat.astype(BF16))

    kidx = lax.broadcasted_iota(jnp.int32, (blk, blk), 0)
    qidx = lax.broadcasted_iota(jnp.int32, (blk, blk), 1)
    causal = kidx <= qidx
    nidx = lax.broadcasted_iota(jnp.int32, (nb, blk), 0)
    r8 = lax.broadcasted_iota(jnp.int32, (SUBLANES, blk), 0)
    qoff = lax.broadcasted_iota(jnp.int32, (SUBLANES, blk), 1).astype(F32)
    km_lane = lax.broadcasted_iota(jnp.int32, km_ref.shape, 1)
    km = km_ref[...]
    slopes = [slopes_ref[hg * heads + g] for g in hs]
    ones = jnp.ones((BF16_SUBLANES, blk), BF16)
    fill = jnp.zeros((HEAD_DIM - feat_rows, blk), F32)

    def values_and_ones(vt_blk, g):
        return jnp.concatenate([vt_blk[head_rows(g), :], ones], axis=0)

    def scores(q_past, pair, buf, slot):
        rows = pl.ds(pl.multiple_of(pair * 2 * blk, 2 * blk), 2 * blk)
        tops = []
        for g in hs:
            s = _dot(kaug_ref[g, rows, :], q_past[g])
            s_ref[buf, slot, g] = s
            tops.append(_col_reduce(s, jnp.maximum, jnp.max))
        return tops

    def absorb_head(pair, buf, slot, top, m_i, acc, g):
        m_new = jnp.maximum(m_i, top)
        p = jnp.exp((s_ref[buf, slot, g] - m_new).astype(BF16))
        acc = (jnp.exp(m_i - m_new) * acc + _dot(values_and_ones(vt_ref[2 * pair], g), p[:blk])
               + _dot(values_and_ones(vt_ref[2 * pair + 1], g), p[blk:]))
        return [m_new, acc]

    def absorb(pair, buf, slot, tops, carry):
        out = []
        for g in hs:
            out += absorb_head(pair, buf, slot, tops[g], carry[2 * g], carry[2 * g + 1], g)
        return out

    def prologues():
        qs = range(n_q)
        cs = [step * n_q + i for i in qs]
        qts = [q_ref[i * blk:(i + 1) * blk, :].astype(F32).T for i in qs]
        qtbs = [qt.astype(BF16) for qt in qts]
        kmh = [jnp.where(km_lane // HEAD_DIM == g, km, 0.0).astype(BF16) for g in hs]
        gates = [[jnp.where(nidx < cs[i], _dot(kmh[g], qtbs[i]), -jnp.inf) for g in hs] for i in qs]
        ranks = [[jnp.zeros((nb, blk), jnp.int32) for _ in hs] for _ in qs]
        for m in range(nb - 1):
            for i in qs:
                for g in hs:
                    gm = gates[i][g][m:m + 1, :]
                    beats = (gm > gates[i][g]) | ((gm == gates[i][g]) & (m < nidx))
                    ranks[i][g] = ranks[i][g] + beats.astype(jnp.int32)

        def q_aug(i, g, bias):
            dist0 = -slopes[g] * ((cs[i] - nidx) * blk).astype(F32)
            tail = jnp.where(r8 == 0, 1.0, jnp.where(r8 == 1, -slopes[g] * qoff, 0.0))
            feats = [jnp.where(bias, dist0, MASKED), tail, fill]
            qh = qts[i][head_rows(g), :]
            return jnp.concatenate([qh] + feats if g % 2 == 0 else feats + [qh], axis=0).astype(BF16)

        q_past = [[q_aug(i, g, (nidx < cs[i]) & (ranks[i][g] < MOBA_TOPK)) for g in hs] for i in qs]
        q_own = [[q_aug(i, g, nidx == cs[i]) for g in hs] for i in qs]
        own_rows = [pl.ds(pl.multiple_of(c * blk, blk), blk) for c in cs]
        s_own = [[jnp.where(causal, _dot(kaug_ref[g, own_rows[i], :], q_own[i][g]), MASKED) for g in hs] for i in qs]
        m_own = [[jnp.max(s, axis=0, keepdims=True) for s in row] for row in s_own]
        p_own = [[jnp.exp((s_own[i][g] - m_own[i][g]).astype(BF16)) for g in hs] for i in qs]
        started = []
        for i in qs:
            state = []
            for g in hs:
                state += [m_own[i][g], _dot(values_and_ones(vt_ref[cs[i]], g), p_own[i][g])]
            started.append((cs[i], q_past[i], state))
        return [(c, q_past_i, state + scores(q_past_i, 0, i, 0)) for i, (c, q_past_i, state) in enumerate(started)]

    def attend(i, c, q_past, carry):
        n_state = 2 * heads

        def advance(n, carry, slot):
            tops = scores(q_past, n + 1, i, 1 - slot)
            return tuple(absorb(n, i, slot, carry[n_state:], carry[:n_state]) + tops)

        def body(n, carry):
            return lax.cond(n % 2 == 0, lambda: advance(n, carry, 0), lambda: advance(n, carry, 1))

        last = jnp.maximum((c + 1) // 2 - 1, 0)
        return last, lax.fori_loop(0, last, body, tuple(carry))

    started = prologues()
    looped = [attend(i, c, q_past, carry) for i, (c, q_past, carry) in enumerate(started)]
    accs = [[None] * heads for _ in looped]
    for g in hs:
        for i, (last, carry) in enumerate(looped):
            accs[i][g] = absorb_head(last, i, last % 2, carry[2 * heads + g], carry[2 * g], carry[2 * g + 1], g)[1]
    for i in range(n_q):
        out_t = jnp.concatenate([a[:HEAD_DIM] / a[HEAD_DIM:HEAD_DIM + 1] for a in accs[i]], axis=0)
        o_ref[i * blk:(i + 1) * blk, :] = out_t.T.astype(o_ref.dtype)


def _alibi_slopes():
    slopes = np.exp2(-ALIBI_MAX * np.arange(1, MOBA_HEADS + 1, dtype=np.float32) / MOBA_HEADS).astype(np.float32)
    worst = slopes * np.float32(MOBA_BLOCK - 1)
    assert np.all(worst.astype(jnp.bfloat16).astype(np.float32) == worst), "ALiBi slopes not bf16-exact"
    return slopes


def _moba(qa, ka, vt, kmean, *, batch, seq):
    t = qa.shape[0]
    nb = seq // MOBA_BLOCK
    gw = MOBA_GROUP * HEAD_DIM
    assert nb % SUBLANES == 0 and nb + SUBLANES <= HEAD_DIM, "bias features must fit the spare contraction lanes"
    grid_spec = pltpu.PrefetchScalarGridSpec(
        num_scalar_prefetch=1,
        grid=(batch, MOBA_HEADS // MOBA_GROUP, nb // MOBA_QBLOCKS),
        in_specs=[
            pl.BlockSpec((MOBA_QBLOCKS * MOBA_BLOCK, gw), lambda b, h, c, s: (b * (nb // MOBA_QBLOCKS) + c, h)),
            pl.BlockSpec((seq, gw), lambda b, h, c, s: (b, h)),
            pl.BlockSpec((nb, gw, MOBA_BLOCK), lambda b, h, c, s: (b, h, 0)),
            pl.BlockSpec((nb, gw), lambda b, h, c, s: (b, h)),
        ],
        out_specs=pl.BlockSpec((MOBA_QBLOCKS * MOBA_BLOCK, gw), lambda b, h, c, s: (b * (nb // MOBA_QBLOCKS) + c, h)),
        scratch_shapes=[pltpu.VMEM((MOBA_GROUP, seq, 2 * HEAD_DIM), BF16),
                        pltpu.VMEM((MOBA_QBLOCKS, 2, MOBA_GROUP, 2 * MOBA_BLOCK, MOBA_BLOCK), F32)],
    )
    return pl.pallas_call(
        _moba_kernel, grid_spec=grid_spec, out_shape=jax.ShapeDtypeStruct((t, MOBA_WIDTH), BF16),
        compiler_params=_params("arbitrary", "arbitrary", "arbitrary"), name="moba",
    )(jnp.asarray(_alibi_slopes()), qa, ka, vt, kmean)


def _gla_kernel(q_ref, k_ref, v_ref, r_ref, la_ref, gn_ref, o_ref, st_ref):
    @pl.when(pl.program_id(1) == 0)
    def _():
        st_ref[...] = jnp.zeros_like(st_ref)

    ch = GLA_CHUNK
    n_seqs, tc = q_ref.shape[:2]
    n_chunks = tc // ch
    row = lax.broadcasted_iota(jnp.int32, (tc, tc), 0)
    col = lax.broadcasted_iota(jnp.int32, (tc, tc), 1)
    causal = (row // ch == col // ch) & (row >= col)
    tri = jnp.where(causal, 1.0, 0.0).astype(BF16)
    lane = lax.broadcasted_iota(jnp.int32, (1, GLA_KEY_WIDTH), 1)
    gn = gn_ref[...]
    seqs = range(n_seqs)
    heads = range(GLA_HEADS)
    gs = []
    for s in seqs:
        la = la_ref[s]
        la_hi = la.astype(BF16)
        rest = la - la_hi.astype(F32)
        la_mid = rest.astype(BF16)
        la_lo = (rest - la_mid.astype(F32)).astype(BF16)
        gs.append(_dot(tri, la_hi) + _dot(tri, la_mid) + _dot(tri, la_lo))
    ends = [[g[(ci + 1) * ch - 1:(ci + 1) * ch, :] for ci in range(n_chunks)] for g in gs]
    qhs, kds, kths, vhs = [], [], [], []
    for s in seqs:
        g = gs[s]
        g_last = jnp.concatenate([jnp.broadcast_to(e, (ch, GLA_KEY_WIDTH)) for e in ends[s]], axis=0)
        qd = q_ref[s] * (GLA_DK ** -0.5) * jnp.exp(g)
        k = k_ref[s]
        kds.append((k * jnp.exp(-g)).astype(BF16))
        kt = k * jnp.exp(g_last - g)
        qhs.append([jnp.where(lane // GLA_DK == h, qd, 0.0).astype(BF16) for h in heads])
        kths.append([jnp.where(lane // GLA_DK == h, kt, 0.0).astype(BF16) for h in heads])
        vhs.append([v_ref[s, :, h * GLA_DV:(h + 1) * GLA_DV] for h in heads])
    intra = [[None] * GLA_HEADS for _ in seqs]
    for h in heads:
        for s in seqs:
            a = jnp.where(causal, _dot(qhs[s][h], kds[s], _NT), 0.0)
            intra[s][h] = _dot(a.astype(BF16), vhs[s][h])
    states = [st_ref[s] for s in seqs]
    inter = [[[] for _ in heads] for _ in seqs]
    for ci in range(n_chunks):
        rows = slice(ci * ch, (ci + 1) * ch)
        for s in seqs:
            state_b = states[s].astype(BF16)
            new_state = states[s] * jnp.exp(ends[s][ci])
            for h in heads:
                inter[s][h].append(_dot(qhs[s][h][rows], state_b, _NT))
                new_state = new_state + _dot(vhs[s][h][rows], kths[s][h][rows], _TN)
            states[s] = new_state
    for s in seqs:
        st_ref[s] = states[s]
    for h in heads:
        for s in seqs:
            o = intra[s][h] + jnp.concatenate(inter[s][h], axis=0)
            rh = r_ref[s, :, h * GLA_DV:(h + 1) * GLA_DV]
            o_ref[s, :, h * GLA_DV:(h + 1) * GLA_DV] = (_rms(o, gn) * (rh * jax.nn.sigmoid(rh))).astype(o_ref.dtype)


def _gla(qg, kg, vg, rg, la, g_norm, *, batch, seq, tc=GLA_TILE):
    n_seqs = GLA_SEQS if batch % GLA_SEQS == 0 else 1
    by_seq = lambda a: a.reshape(batch, seq, a.shape[-1])
    blk = lambda i, j: (i, j, 0)
    kspec = pl.BlockSpec((n_seqs, tc, GLA_KEY_WIDTH), blk)
    vspec = pl.BlockSpec((n_seqs, tc, GLA_VAL_WIDTH), blk)
    out = pl.pallas_call(
        _gla_kernel, grid=(batch // n_seqs, seq // tc),
        in_specs=[kspec, kspec, vspec, vspec, kspec, pl.BlockSpec((1, GLA_DV), lambda i, j: (0, 0))],
        out_specs=vspec, out_shape=jax.ShapeDtypeStruct((batch, seq, GLA_VAL_WIDTH), BF16),
        scratch_shapes=[pltpu.VMEM((n_seqs, GLA_DV, GLA_KEY_WIDTH), F32)],
        compiler_params=_params("arbitrary", "arbitrary"), name="gla",
    )(by_seq(qg), by_seq(kg), by_seq(vg), by_seq(rg), by_seq(la), g_norm.reshape(1, GLA_DV))
    return out.reshape(batch * seq, GLA_VAL_WIDTH)


def _outproj_kernel(x_ref, om_ref, og_ref, wm_ref, wg_ref, gf_ref, wr_ref, br_ref, h_ref, f_ref, lg_ref):
    h = x_ref[...] + _dot(om_ref[...], wm_ref[...]) + _dot(og_ref[...], wg_ref[...])
    h_ref[...] = h
    f = _rms(h, gf_ref[...]).astype(BF16)
    f_ref[...] = f
    lg_ref[...] = _dot(wr_ref[...], f, _NT) + br_ref[...]


def _outproj(x2d, o_moba, o_gla, w_out, g_ffn, w_router, b_router, *, tm=OUTPROJ_ROWS):
    t, d = x2d.shape
    wm = w_out[:MOBA_WIDTH].astype(BF16)
    wg = w_out[MOBA_WIDTH:].astype(BF16)
    wr = w_router.T.astype(BF16)
    br = b_router.reshape(N_EXPERTS, 1)
    row = lambda i: (i, 0)
    const = lambda i: (0, 0)
    full = lambda a: pl.BlockSpec(a.shape, const)
    args = (x2d, o_moba, o_gla, wm, wg, g_ffn.reshape(1, d), wr, br)
    in_specs = [pl.BlockSpec((tm, d), row), pl.BlockSpec((tm, MOBA_WIDTH), row),
                pl.BlockSpec((tm, GLA_VAL_WIDTH), row)] + [full(a) for a in args[3:]]
    return pl.pallas_call(
        _outproj_kernel, grid=(t // tm,), in_specs=in_specs,
        out_specs=(pl.BlockSpec((tm, d), row), pl.BlockSpec((tm, d), row),
                   pl.BlockSpec((N_EXPERTS, tm), lambda i: (0, i))),
        out_shape=(jax.ShapeDtypeStruct((t, d), F32), jax.ShapeDtypeStruct((t, d), BF16),
                   jax.ShapeDtypeStruct((N_EXPERTS, t), F32)),
        compiler_params=_params("arbitrary"), name="outproj",
    )(*args)


PAIR_CHUNK = 2 * LANES


def _moe_kernel(be_ref, nused_ref, x_ref, w1_ref, b1_ref, w2_ref, b2_ref, o_ref, w1s_ref, w2s_ref):
    i = pl.program_id(0)
    f = w2_ref.shape[1]

    @pl.when((i == 0) | (be_ref[i] != be_ref[jnp.maximum(i - 1, 0)]))
    def _():
        r = lax.broadcasted_iota(jnp.int32, (PAIR_CHUNK, PAIR_CHUNK), 0)
        c = lax.broadcasted_iota(jnp.int32, (PAIR_CHUNK, PAIR_CHUNK), 1)
        src_col = jnp.where(c < LANES, 2 * c, 2 * (c - LANES) + 1)
        perm = jnp.where(r == src_col, 1.0, 0.0).astype(BF16)
        for cc in range(2 * f // PAIR_CHUNK):
            chunk = w1_ref[0, :, cc * PAIR_CHUNK:(cc + 1) * PAIR_CHUNK].astype(BF16)
            res = _dot(chunk, perm).astype(BF16)
            w1s_ref[:, cc * LANES:(cc + 1) * LANES] = res[:, :LANES]
            w1s_ref[:, f + cc * LANES:f + (cc + 1) * LANES] = res[:, LANES:]
        w2s_ref[...] = w2_ref[0].astype(BF16)

    @pl.when(i < nused_ref[0])
    def _():
        hid = _dot(x_ref[...], w1s_ref[...]) + b1_ref[0]
        glu = jnp.minimum(hid[:, :f], SWIGLU_LIMIT)
        lin = jnp.clip(hid[:, f:], -SWIGLU_LIMIT, SWIGLU_LIMIT)
        act = glu * jax.nn.sigmoid(SWIGLU_ALPHA * glu) * (lin + 1.0)
        o_ref[...] = (_dot(act.astype(BF16), w2s_ref[...]) + b2_ref[0]).astype(o_ref.dtype)

    @pl.when(i >= nused_ref[0])
    def _():
        o_ref[...] = jnp.zeros_like(o_ref)


def _moe(x_rows, block_expert, n_used, w1, b1, w2, b2, *, tm):
    n_rows, d = x_rows.shape
    e, _, f2 = w1.shape
    f = f2 // 2
    b1p = jnp.concatenate([b1[:, 0::2], b1[:, 1::2]], axis=-1).reshape(e, 1, f2)
    b2r = b2.reshape(e, 1, d)
    grid_spec = pltpu.PrefetchScalarGridSpec(
        num_scalar_prefetch=2,
        grid=(n_rows // tm,),
        in_specs=[
            pl.BlockSpec((tm, d), lambda i, be, nu: (i, 0)),
            pl.BlockSpec((1, d, f2), lambda i, be, nu: (be[i], 0, 0)),
            pl.BlockSpec((1, 1, f2), lambda i, be, nu: (be[i], 0, 0)),
            pl.BlockSpec((1, f, d), lambda i, be, nu: (be[i], 0, 0)),
            pl.BlockSpec((1, 1, d), lambda i, be, nu: (be[i], 0, 0)),
        ],
        out_specs=pl.BlockSpec((tm, d), lambda i, be, nu: (i, 0)),
        scratch_shapes=[pltpu.VMEM((d, f2), BF16), pltpu.VMEM((f, d), BF16)],
    )
    return pl.pallas_call(
        _moe_kernel, grid_spec=grid_spec, out_shape=jax.ShapeDtypeStruct((n_rows, d), BF16),
        compiler_params=_params("arbitrary"), name="moe",
    )(block_expert, n_used, x_rows, w1, b1p, w2, b2r)


ROUTE_CHUNK = 2 * LANES


def _router_kernel(lg_ref, idx_ref, gate_ref, rank_ref, cnt_ref, carry_ref):
    @pl.when(pl.program_id(0) == 0)
    def _():
        carry_ref[...] = jnp.zeros_like(carry_ref)

    x = lg_ref[...]
    ne, tt = x.shape
    eidx = lax.broadcasted_iota(jnp.int32, (ne, tt), 0)
    member = jnp.zeros((ne, tt), F32)
    vals, ids = [], []
    for _ in range(TOP_K):
        m = jnp.max(x, axis=0, keepdims=True)
        sel = jnp.min(jnp.where(x == m, eidx, ne), axis=0, keepdims=True)
        hit = eidx == sel
        member = jnp.where(hit, 1.0, member)
        x = jnp.where(hit, -jnp.inf, x)
        vals.append(m)
        ids.append(sel)
    e = jnp.exp(jnp.concatenate(vals, axis=0) - vals[0])
    gate_ref[...] = e / jnp.sum(e, axis=0, keepdims=True)
    idx_ref[...] = jnp.concatenate(ids, axis=0)
    r = lax.broadcasted_iota(jnp.int32, (ROUTE_CHUNK, ROUTE_CHUNK), 0)
    c = lax.broadcasted_iota(jnp.int32, (ROUTE_CHUNK, ROUTE_CHUNK), 1)
    earlier = jnp.where(r < c, 1.0, 0.0).astype(BF16)
    carry = carry_ref[:, :1]
    before = []
    for ch in range(tt // ROUTE_CHUNK):
        mc = member[:, ch * ROUTE_CHUNK:(ch + 1) * ROUTE_CHUNK]
        before.append(_dot(mc.astype(BF16), earlier) + carry)
        carry = carry + jnp.sum(mc, axis=1, keepdims=True)
    before = jnp.concatenate(before, axis=1)
    ranks = [jnp.sum(jnp.where(eidx == ids[k], before, 0.0), axis=0, keepdims=True) for k in range(TOP_K)]
    rank_ref[...] = jnp.concatenate(ranks, axis=0).astype(jnp.int32)
    carry_ref[...] = jnp.broadcast_to(carry, carry_ref.shape)
    cnt_ref[...] = jnp.broadcast_to(carry, cnt_ref.shape).astype(jnp.int32)


def _router(logits_t, *, tt=ROUTER_TOKENS):
    ne, t = logits_t.shape
    tok = lambda i: (0, i)
    kspec = pl.BlockSpec((TOP_K, tt), tok)
    return pl.pallas_call(
        _router_kernel, grid=(t // tt,), in_specs=[pl.BlockSpec((ne, tt), tok)],
        out_specs=(kspec, kspec, kspec, pl.BlockSpec((ne, LANES), lambda i: (0, 0))),
        out_shape=(jax.ShapeDtypeStruct((TOP_K, t), jnp.int32), jax.ShapeDtypeStruct((TOP_K, t), F32),
                   jax.ShapeDtypeStruct((TOP_K, t), jnp.int32), jax.ShapeDtypeStruct((ne, LANES), jnp.int32)),
        scratch_shapes=[pltpu.VMEM((ne, LANES), F32)],
        compiler_params=_params("arbitrary"), name="router",
    )(logits_t)


def _route(logits_t, *, tm):
    ne, t = logits_t.shape
    idx, gates, rank, cnt = _router(logits_t)
    counts = cnt[:, 0]
    padded = (counts + tm - 1) // tm * tm
    pad_end = jnp.cumsum(padded)
    pad_start = pad_end - padded
    experts = jnp.arange(ne, dtype=jnp.int32).reshape(ne, 1, 1)
    slot = jnp.sum(jnp.where(idx[None] == experts, pad_start.reshape(ne, 1, 1), 0), axis=0) + rank
    n_blocks = -(-TOP_K * t // tm) + ne
    flat_slot = slot.reshape(-1)
    tok = jnp.broadcast_to(jnp.arange(t, dtype=jnp.int32), (TOP_K, t)).reshape(-1)
    row_tok = (jnp.arange(n_blocks * tm, dtype=jnp.int32) % t).at[flat_slot].add(
        tok - flat_slot % t, unique_indices=True, mode="promise_in_bounds")
    n_used = pad_end[-1] // tm
    blk_start = jnp.minimum(jnp.arange(n_blocks, dtype=jnp.int32), n_used - 1) * tm
    block_expert = jnp.minimum(jnp.sum(pad_end[:, None] <= blk_start[None, :], axis=0), ne - 1)
    return gates, row_tok, slot, block_expert.astype(jnp.int32), n_used.reshape(1).astype(jnp.int32)


def _final_kernel(h_ref, y_ref, gate_ref, p_ref, wp_ref, gp_ref, wgate_ref, gfin_ref, o_ref, *, last_layer):
    h = h_ref[...]
    g = gate_ref[...]
    gates = jnp.concatenate([g, jnp.zeros((SUBLANES - TOP_K, g.shape[1]), F32)], axis=0).T
    for k in range(TOP_K):
        h = h + gates[:, k:k + 1] * y_ref[k].astype(F32)
    ple = _rms(_dot(p_ref[...].astype(BF16), wp_ref[...]), gp_ref[...])
    h = h + jax.nn.sigmoid(_dot(h.astype(BF16), wgate_ref[...])) * ple
    o_ref[...] = _rms(h, gfin_ref[...]) if last_layer else h


def _final(h1, y_tok, gates, p2d, w_ple_proj, g_ple, w_ple_gate, g_final, *, last_layer, tm=FINAL_ROWS):
    t, d = h1.shape
    row = lambda i: (i, 0)
    const = lambda i: (0, 0)
    full = lambda a: pl.BlockSpec(a.shape, const)
    args = (h1, y_tok, gates, p2d, w_ple_proj.astype(BF16), g_ple.reshape(1, d), w_ple_gate.astype(BF16),
            g_final.reshape(1, d))
    in_specs = [pl.BlockSpec((tm, d), row), pl.BlockSpec((TOP_K, tm, d), lambda i: (0, i, 0)),
                pl.BlockSpec((TOP_K, tm), lambda i: (0, i)), pl.BlockSpec((tm, p2d.shape[1]), row)]
    in_specs += [full(a) for a in args[4:]]
    return pl.pallas_call(
        functools.partial(_final_kernel, last_layer=last_layer), grid=(t // tm,), in_specs=in_specs,
        out_specs=pl.BlockSpec((tm, d), row),
        out_shape=jax.ShapeDtypeStruct((t, d), F32),
        compiler_params=_params("arbitrary"), name="final",
    )(*args)


def kernel(x, p, g_mix, w_in, w_gla_gate, b_gla_gate, g_gla_norm, w_out, g_ffn, w_router, b_router,
           w1, b1, w2, b2, w_ple_gate, w_ple_proj, g_ple, g_final):
    batch, seq, d = x.shape
    depth = p.shape[0]
    t = batch * seq
    h = x.reshape(t, d)
    for i in range(depth):
        qa, ka, vt, kmean, qg, kg, vg, rg, la = _inproj(h, g_mix[i], w_in[i], w_gla_gate[i], b_gla_gate[i],
                                                        batch=batch, seq=seq)
        o_moba = _moba(qa, ka, vt, kmean, batch=batch, seq=seq)
        o_gla = _gla(qg, kg, vg, rg, la, g_gla_norm[i], batch=batch, seq=seq)
        h1, f, logits = _outproj(h, o_moba, o_gla, w_out[i], g_ffn[i], w_router[i], b_router[i])
        gates, row_tok, slot, block_expert, n_used = _route(logits, tm=MOE_ROWS)
        x_rows = jnp.take(f, row_tok, axis=0, mode="clip")
        y = _moe(x_rows, block_expert, n_used, w1[i], b1[i], w2[i], b2[i], tm=MOE_ROWS)
        y_tok = jnp.take(y, slot, axis=0, mode="clip")
        h = _final(h1, y_tok, gates, p[i].reshape(t, -1), w_ple_proj[i], g_ple[i], w_ple_gate[i],
                   g_final, last_layer=i == depth - 1)
    return h.reshape(batch, seq, d)
```

```python
import functools

import jax
import jax.numpy as jnp
import numpy as np
from jax import lax
from jax.experimental import pallas as pl
from jax.experimental.pallas import tpu as pltpu

HEAD_DIM = 64
MOBA_HEADS = 8
MOBA_WIDTH = MOBA_HEADS * HEAD_DIM
MOBA_BLOCK = 256
MOBA_TOPK = 3
MOBA_GROUP = 4
MOBA_QBLOCKS = 4
ALIBI_MAX = 8.0
GLA_HEADS = 4
GLA_DK = 64
GLA_DV = 128
GLA_KEY_WIDTH = GLA_HEADS * GLA_DK
GLA_VAL_WIDTH = GLA_HEADS * GLA_DV
GLA_GATE_RANK = 16
GLA_GATE_TAU = 16.0
GLA_CHUNK = 64
GLA_SEQS = 4
N_EXPERTS = 32
TOP_K = 4
SWIGLU_ALPHA = 1.702
SWIGLU_LIMIT = 7.0
EPS = 1e-6

LANES = 128
SUBLANES = 8
BF16_SUBLANES = 16

INPROJ_ROWS = 512
OUTPROJ_ROWS = 1024
FINAL_ROWS = 1024
GLA_TILE = 256
MOE_ROWS = 512
MASKED = -1e30
VMEM_LIMIT = 56 * 1024 * 1024

BF16 = jnp.bfloat16
F32 = jnp.float32

_NT = (((1,), (1,)), ((), ()))
_TN = (((0,), (0,)), ((), ()))


def _dot(a, b, dims=None):
    if dims is None:
        return jnp.dot(a, b, preferred_element_type=F32)
    return lax.dot_general(a, b, dims, preferred_element_type=F32)


def _rms(x, g):
    return x * lax.rsqrt(jnp.mean(x * x, axis=-1, keepdims=True) + EPS) * g


def _col_reduce(x, pair_op, final_op, rows=64):
    parts = [x[i:i + rows] for i in range(0, x.shape[0], rows)]
    while len(parts) > 1:
        parts = [pair_op(parts[i], parts[i + 1]) for i in range(0, len(parts), 2)]
    return final_op(parts[0], axis=0, keepdims=True)


def _params(*sem):
    return pltpu.CompilerParams(dimension_semantics=sem, vmem_limit_bytes=VMEM_LIMIT)


def _inproj_kernel(x_ref, g_ref, w_ref, wvt_ref, wag_ref, wgate_ref, bgate_ref,
                   qa_ref, ka_ref, vt_ref, km_ref, qg_ref, kg_ref, vg_ref, rg_ref, la_ref):
    i = pl.program_id(1)
    tm = x_ref.shape[0]
    nblk = tm // MOBA_BLOCK
    @pl.when(i == 0)
    def _():
        km_ref[...] = jnp.zeros_like(km_ref)

    ab = _rms(x_ref[...], g_ref[...]).astype(BF16)
    ag = _dot(ab, wag_ref[...]).astype(BF16)
    z = _dot(ag, wgate_ref[...]) + bgate_ref[...]
    la_ref[...] = jax.nn.log_sigmoid(z) / GLA_GATE_TAU
    qa_ref[...] = (_dot(ab, w_ref[:, :MOBA_WIDTH]) * (HEAD_DIM ** -0.5)).astype(BF16)
    ka = _dot(ab, w_ref[:, MOBA_WIDTH:2 * MOBA_WIDTH])
    ka_ref[...] = ka.astype(BF16)
    km = km_ref[...]
    blk_row = lax.broadcasted_iota(jnp.int32, km.shape, 0)
    for j in range(nblk):
        mean_j = jnp.mean(ka[j * MOBA_BLOCK:(j + 1) * MOBA_BLOCK], axis=0, keepdims=True)
        km = jnp.where(blk_row == i * nblk + j, mean_j, km)
    km_ref[...] = km
    vt = _dot(wvt_ref[...], ab, _NT).astype(BF16)
    for j in range(nblk):
        vt_ref[j] = vt[:, j * MOBA_BLOCK:(j + 1) * MOBA_BLOCK]
    gla_cols = 2 * GLA_KEY_WIDTH + 2 * GLA_VAL_WIDTH
    gla = _dot(ab, w_ref[:, 3 * MOBA_WIDTH:3 * MOBA_WIDTH + gla_cols])
    o = 0
    for ref in (qg_ref, kg_ref, vg_ref, rg_ref):
        w = ref.shape[1]
        ref[...] = gla[:, o:o + w].astype(ref.dtype)
        o += w


def _inproj(x2d, g_mix, w_in, w_gla_gate, b_gla_gate, *, batch, seq, tm=INPROJ_ROWS):
    t, d = x2d.shape
    nb = seq // MOBA_BLOCK
    o_v, o_qg = 2 * MOBA_WIDTH, 3 * MOBA_WIDTH
    o_ag = o_qg + 2 * GLA_KEY_WIDTH + 2 * GLA_VAL_WIDTH
    wb = w_in.astype(BF16)
    wvt = wb[:, o_v:o_qg].T
    wag = jnp.pad(wb[:, o_ag:], ((0, 0), (0, LANES - GLA_GATE_RANK)))
    wgate = jnp.pad(w_gla_gate, ((0, LANES - GLA_GATE_RANK), (0, 0))).astype(BF16)
    steps = seq // tm
    row = lambda b, i: (b * steps + i, 0)
    const = lambda b, i: (0, 0)
    full = lambda a: pl.BlockSpec(a.shape, const)
    args = (x2d, g_mix.reshape(1, d), wb, wvt, wag, wgate, b_gla_gate.reshape(1, -1))
    in_specs = [pl.BlockSpec((tm, d), row)] + [full(a) for a in args[1:]]
    out_shape = (
        jax.ShapeDtypeStruct((t, MOBA_WIDTH), BF16),
        jax.ShapeDtypeStruct((t, MOBA_WIDTH), BF16),
        jax.ShapeDtypeStruct((t // MOBA_BLOCK, MOBA_WIDTH, MOBA_BLOCK), BF16),
        jax.ShapeDtypeStruct((batch * nb, MOBA_WIDTH), F32),
        jax.ShapeDtypeStruct((t, GLA_KEY_WIDTH), F32),
        jax.ShapeDtypeStruct((t, GLA_KEY_WIDTH), F32),
        jax.ShapeDtypeStruct((t, GLA_VAL_WIDTH), BF16),
        jax.ShapeDtypeStruct((t, GLA_VAL_WIDTH), F32),
        jax.ShapeDtypeStruct((t, GLA_KEY_WIDTH), F32),
    )
    out_specs = (
        pl.BlockSpec((tm, MOBA_WIDTH), row),
        pl.BlockSpec((tm, MOBA_WIDTH), row),
        pl.BlockSpec((tm // MOBA_BLOCK, MOBA_WIDTH, MOBA_BLOCK), lambda b, i: (b * steps + i, 0, 0)),
        pl.BlockSpec((nb, MOBA_WIDTH), lambda b, i: (b, 0)),
        pl.BlockSpec((tm, GLA_KEY_WIDTH), row),
        pl.BlockSpec((tm, GLA_KEY_WIDTH), row),
        pl.BlockSpec((tm, GLA_VAL_WIDTH), row),
        pl.BlockSpec((tm, GLA_VAL_WIDTH), row),
        pl.BlockSpec((tm, GLA_KEY_WIDTH), row),
    )
    return pl.pallas_call(
        _inproj_kernel, grid=(batch, steps), in_specs=in_specs, out_specs=out_specs, out_shape=out_shape,
        compiler_params=_params("arbitrary", "arbitrary"), name="inproj",
    )(*args)


def _moba_kernel(slopes_ref, q_ref, k_ref, vt_ref, km_ref, o_ref, kaug_ref, s_ref):
    hg = pl.program_id(1)
    step = pl.program_id(2)
    blk = MOBA_BLOCK
    nb = km_ref.shape[0]
    hw = 2 * HEAD_DIM
    heads = q_ref.shape[1] // HEAD_DIM
    n_q = q_ref.shape[0] // blk
    feat_rows = nb + SUBLANES
    head_rows = lambda g: slice(g * HEAD_DIM, (g + 1) * HEAD_DIM)
    hs = range(heads)

    @pl.when(step == 0)
    def _():
        lane = lax.broadcasted_iota(jnp.int32, (blk, hw), 1)
        koff = lax.broadcasted_iota(jnp.int32, (blk, hw), 0).astype(F32)
        for g in hs:
            tile, hh = divmod(g, 2)
            slope = slopes_ref[hg * heads + g]
            base = HEAD_DIM * (1 - hh)
            for jb in range(nb):
                feat = jnp.where(lane == base + nb, slope * koff, 0.0)
                feat = jnp.where((lane == base + jb) | (lane == base + nb + 1), 1.0, feat)
                kb = k_ref[jb * blk:(jb + 1) * blk, tile * hw:(tile + 1) * hw]
                kaug_ref[g, jb * blk:(jb + 1) * blk, :] = jnp.where(lane // HEAD_DIM == hh, kb, feat.astype(BF16))

    kidx = lax.broadcasted_iota(jnp.int32, (blk, blk), 0)
    qidx = lax.broadcasted_iota(jnp.int32, (blk, blk), 1)
    causal = kidx <= qidx
    nidx = lax.broadcasted_iota(jnp.int32, (nb, blk), 0)
    r8 = lax.broadcasted_iota(jnp.int32, (SUBLANES, blk), 0)
    qoff = lax.broadcasted_iota(jnp.int32, (SUBLANES, blk), 1).astype(F32)
    km_lane = lax.broadcasted_iota(jnp.int32, km_ref.shape, 1)
    km = km_ref[...]
    slopes = [slopes_ref[hg * heads + g] for g in hs]
    ones = jnp.ones((BF16_SUBLANES, blk), BF16)
    fill = jnp.zeros((HEAD_DIM - feat_rows, blk), F32)

    def values_and_ones(vt_blk, g):
        return jnp.concatenate([vt_blk[head_rows(g), :], ones], axis=0)

    def scores(q_past, pair, buf, slot):
        rows = pl.ds(pl.multiple_of(pair * 2 * blk, 2 * blk), 2 * blk)
        tops = []
        for g in hs:
            s = _dot(kaug_ref[g, rows, :], q_past[g])
            s_ref[buf, slot, g] = s
            tops.append(_col_reduce(s, jnp.maximum, jnp.max))
        return tops

    def absorb_head(pair, buf, slot, top, m_i, acc, g):
        m_new = jnp.maximum(m_i, top)
        p = jnp.exp((s_ref[buf, slot, g] - m_new).astype(BF16))
        acc = (jnp.exp(m_i - m_new) * acc + _dot(values_and_ones(vt_ref[2 * pair], g), p[:blk])
               + _dot(values_and_ones(vt_ref[2 * pair + 1], g), p[blk:]))
        return [m_new, acc]

    def absorb(pair, buf, slot, tops, carry):
        out = []
        for g in hs:
            out += absorb_head(pair, buf, slot, tops[g], carry[2 * g], carry[2 * g + 1], g)
        return out

    def prologues():
        qs = range(n_q)
        cs = [step * n_q + i for i in qs]
        qts = [q_ref[i * blk:(i + 1) * blk, :].astype(F32).T for i in qs]
        qtbs = [qt.astype(BF16) for qt in qts]
        kmh = [jnp.where(km_lane // HEAD_DIM == g, km, 0.0).astype(BF16) for g in hs]
        gates = [[jnp.where(nidx < cs[i], _dot(kmh[g], qtbs[i]), -jnp.inf) for g in hs] for i in qs]
        ranks = [[jnp.zeros((nb, blk), jnp.int32) for _ in hs] for _ in qs]
        for m in range(nb - 1):
            for i in qs:
                for g in hs:
                    gm = gates[i][g][m:m + 1, :]
                    beats = (gm > gates[i][g]) | ((gm == gates[i][g]) & (m < nidx))
                    ranks[i][g] = ranks[i][g] + beats.astype(jnp.int32)

        def q_aug(i, g, bias):
            dist0 = -slopes[g] * ((cs[i] - nidx) * blk).astype(F32)
            tail = jnp.where(r8 == 0, 1.0, jnp.where(r8 == 1, -slopes[g] * qoff, 0.0))
            feats = [jnp.where(bias, dist0, MASKED), tail, fill]
            qh = qts[i][head_rows(g), :]
            return jnp.concatenate([qh] + feats if g % 2 == 0 else feats + [qh], axis=0).astype(BF16)

        q_past = [[q_aug(i, g, (nidx < cs[i]) & (ranks[i][g] < MOBA_TOPK)) for g in hs] for i in qs]
        q_own = [[q_aug(i, g, nidx == cs[i]) for g in hs] for i in qs]
        own_rows = [pl.ds(pl.multiple_of(c * blk, blk), blk) for c in cs]
        s_own = [[jnp.where(causal, _dot(kaug_ref[g, own_rows[i], :], q_own[i][g]), MASKED) for g in hs] for i in qs]
        m_own = [[jnp.max(s, axis=0, keepdims=True) for s in row] for row in s_own]
        p_own = [[jnp.exp((s_own[i][g] - m_own[i][g]).astype(BF16)) for g in hs] for i in qs]
        started = []
        for i in qs:
            state = []
            for g in hs:
                state += [m_own[i][g], _dot(values_and_ones(vt_ref[cs[i]], g), p_own[i][g])]
            started.append((cs[i], q_past[i], state))
        return [(c, q_past_i, state + scores(q_past_i, 0, i, 0)) for i, (c, q_past_i, state) in enumerate(started)]

    def attend(i, c, q_past, carry):
        n_state = 2 * heads

        def advance(n, carry, slot):
            tops = scores(q_past, n + 1, i, 1 - slot)
            return tuple(absorb(n, i, slot, carry[n_state:], carry[:n_state]) + tops)

        def body(n, carry):
            return lax.cond(n % 2 == 0, lambda: advance(n, carry, 0), lambda: advance(n, carry, 1))

        last = jnp.maximum((c + 1) // 2 - 1, 0)
        return last, lax.fori_loop(0, last, body, tuple(carry))

    started = prologues()
    looped = [attend(i, c, q_past, carry) for i, (c, q_past, carry) in enumerate(started)]
    accs = [[None] * heads for _ in looped]
    for g in hs:
        for i, (last, carry) in enumerate(looped):
            accs[i][g] = absorb_head(last, i, last % 2, carry[2 * heads + g], carry[2 * g], carry[2 * g + 1], g)[1]
    for i in range(n_q):
        out_t = jnp.concatenate([a[:HEAD_DIM] / a[HEAD_DIM:HEAD_DIM + 1] for a in accs[i]], axis=0)
        o_ref[i * blk:(i + 1) * blk, :] = out_t.T.astype(o_ref.dtype)


def _alibi_slopes():
    slopes = np.exp2(-ALIBI_MAX * np.arange(1, MOBA_HEADS + 1, dtype=np.float32) / MOBA_HEADS).astype(np.float32)
    worst = slopes * np.float32(MOBA_BLOCK - 1)
    assert np.all(worst.astype(jnp.bfloat16).astype(np.float32) == worst), "ALiBi slopes not bf16-exact"
    return slopes


def _moba(qa, ka, vt, kmean, *, batch, seq):
    t = qa.shape[0]
    nb = seq // MOBA_BLOCK
    gw = MOBA_GROUP * HEAD_DIM
    assert nb % SUBLANES == 0 and nb + SUBLANES <= HEAD_DIM, "bias features must fit the spare contraction lanes"
    grid_spec = pltpu.PrefetchScalarGridSpec(
        num_scalar_prefetch=1,
        grid=(batch, MOBA_HEADS // MOBA_GROUP, nb // MOBA_QBLOCKS),
        in_specs=[
            pl.BlockSpec((MOBA_QBLOCKS * MOBA_BLOCK, gw), lambda b, h, c, s: (b * (nb // MOBA_QBLOCKS) + c, h)),
            pl.BlockSpec((seq, gw), lambda b, h, c, s: (b, h)),
            pl.BlockSpec((nb, gw, MOBA_BLOCK), lambda b, h, c, s: (b, h, 0)),
            pl.BlockSpec((nb, gw), lambda b, h, c, s: (b, h)),
        ],
        out_specs=pl.BlockSpec((MOBA_QBLOCKS * MOBA_BLOCK, gw), lambda b, h, c, s: (b * (nb // MOBA_QBLOCKS) + c, h)),
        scratch_shapes=[pltpu.VMEM((MOBA_GROUP, seq, 2 * HEAD_DIM), BF16),
                        pltpu.VMEM((MOBA_QBLOCKS, 2, MOBA_GROUP, 2 * MOBA_BLOCK, MOBA_BLOCK), F32)],
    )
    return pl.pallas_call(
        _moba_kernel, grid_spec=grid_spec, out_shape=jax.ShapeDtypeStruct((t, MOBA_WIDTH), BF16),
        compiler_params=_params("arbitrary", "arbitrary", "arbitrary"), name="moba",
    )(jnp.asarray(_alibi_slopes()), qa, ka, vt, kmean)


def _gla_kernel(q_ref, k_ref, v_ref, r_ref, la_ref, gn_ref, o_ref, st_ref):
    @pl.when(pl.program_id(1) == 0)
    def _():
        st_ref[...] = jnp.zeros_like(st_ref)

    ch = GLA_CHUNK
    n_seqs, tc = q_ref.shape[:2]
    n_chunks = tc // ch
    row = lax.broadcasted_iota(jnp.int32, (tc, tc), 0)
    col = lax.broadcasted_iota(jnp.int32, (tc, tc), 1)
    causal = (row // ch == col // ch) & (row >= col)
    tri = jnp.where(causal, 1.0, 0.0).astype(BF16)
    lane = lax.broadcasted_iota(jnp.int32, (1, GLA_KEY_WIDTH), 1)
    gn = gn_ref[...]
    seqs = range(n_seqs)
    heads = range(GLA_HEADS)
    gs = []
    for s in seqs:
        la = la_ref[s]
        la_hi = la.astype(BF16)
        rest = la - la_hi.astype(F32)
        la_mid = rest.astype(BF16)
        la_lo = (rest - la_mid.astype(F32)).astype(BF16)
        gs.append(_dot(tri, la_hi) + _dot(tri, la_mid) + _dot(tri, la_lo))
    ends = [[g[(ci + 1) * ch - 1:(ci + 1) * ch, :] for ci in range(n_chunks)] for g in gs]
    qhs, kds, kths, vhs = [], [], [], []
    for s in seqs:
        g = gs[s]
        g_last = jnp.concatenate([jnp.broadcast_to(e, (ch, GLA_KEY_WIDTH)) for e in ends[s]], axis=0)
        qd = q_ref[s] * (GLA_DK ** -0.5) * jnp.exp(g)
        k = k_ref[s]
        kds.append((k * jnp.exp(-g)).astype(BF16))
        kt = k * jnp.exp(g_last - g)
        qhs.append([jnp.where(lane // GLA_DK == h, qd, 0.0).astype(BF16) for h in heads])
        kths.append([jnp.where(lane // GLA_DK == h, kt, 0.0).astype(BF16) for h in heads])
        vhs.append([v_ref[s, :, h * GLA_DV:(h + 1) * GLA_DV] for h in heads])
    intra = [[None] * GLA_HEADS for _ in seqs]
    for h in heads:
        for s in seqs:
            a = jnp.where(causal, _dot(qhs[s][h], kds[s], _NT), 0.0)
            intra[s][h] = _dot(a.astype(BF16), vhs[s][h])
    states = [st_ref[s] for s in seqs]
    inter = [[[] for _ in heads] for _ in seqs]
    for ci in range(n_chunks):
        rows = slice(ci * ch, (ci + 1) * ch)
        for s in seqs:
            state_b = states[s].astype(BF16)
            new_state = states[s] * jnp.exp(ends[s][ci])
            for h in heads:
                inter[s][h].append(_dot(qhs[s][h][rows], state_b, _NT))
                new_state = new_state + _dot(vhs[s][h][rows], kths[s][h][rows], _TN)
            states[s] = new_state
    for s in seqs:
        st_ref[s] = states[s]
    for h in heads:
        for s in seqs:
            o = intra[s][h] + jnp.concatenate(inter[s][h], axis=0)
            rh = r_ref[s, :, h * GLA_DV:(h + 1) * GLA_DV]
            o_ref[s, :, h * GLA_DV:(h + 1) * GLA_DV] = (_rms(o, gn) * (rh * jax.nn.sigmoid(rh))).astype(o_ref.dtype)


def _gla(qg, kg, vg, rg, la, g_norm, *, batch, seq, tc=GLA_TILE):
    n_seqs = GLA_SEQS if batch % GLA_SEQS == 0 else 1
    by_seq = lambda a: a.reshape(batch, seq, a.shape[-1])
    blk = lambda i, j: (i, j, 0)
    kspec = pl.BlockSpec((n_seqs, tc, GLA_KEY_WIDTH), blk)
    vspec = pl.BlockSpec((n_seqs, tc, GLA_VAL_WIDTH), blk)
    out = pl.pallas_call(
        _gla_kernel, grid=(batch // n_seqs, seq // tc),
        in_specs=[kspec, kspec, vspec, vspec, kspec, pl.BlockSpec((1, GLA_DV), lambda i, j: (0, 0))],
        out_specs=vspec, out_shape=jax.ShapeDtypeStruct((batch, seq, GLA_VAL_WIDTH), BF16),
        scratch_shapes=[pltpu.VMEM((n_seqs, GLA_DV, GLA_KEY_WIDTH), F32)],
        compiler_params=_params("arbitrary", "arbitrary"), name="gla",
    )(by_seq(qg), by_seq(kg), by_seq(vg), by_seq(rg), by_seq(la), g_norm.reshape(1, GLA_DV))
    return out.reshape(batch * seq, GLA_VAL_WIDTH)


def _outproj_kernel(x_ref, om_ref, og_ref, wm_ref, wg_ref, gf_ref, wr_ref, br_ref,
                    h_ref, f_ref, idx_ref, gate_ref, rank_ref, cnt_ref, carry_ref):
    h = x_ref[...] + _dot(om_ref[...], wm_ref[...]) + _dot(og_ref[...], wg_ref[...])
    h_ref[...] = h
    f = _rms(h, gf_ref[...]).astype(BF16)
    f_ref[...] = f
    logits = _dot(wr_ref[...], f, _NT) + br_ref[...]
    _route_tile(logits, idx_ref, gate_ref, rank_ref, cnt_ref, carry_ref)


def _outproj(x2d, o_moba, o_gla, w_out, g_ffn, w_router, b_router, *, tm=OUTPROJ_ROWS):
    t, d = x2d.shape
    wm = w_out[:MOBA_WIDTH].astype(BF16)
    wg = w_out[MOBA_WIDTH:].astype(BF16)
    wr = w_router.T.astype(BF16)
    br = b_router.reshape(N_EXPERTS, 1)
    row = lambda i: (i, 0)
    const = lambda i: (0, 0)
    full = lambda a: pl.BlockSpec(a.shape, const)
    args = (x2d, o_moba, o_gla, wm, wg, g_ffn.reshape(1, d), wr, br)
    in_specs = [pl.BlockSpec((tm, d), row), pl.BlockSpec((tm, MOBA_WIDTH), row),
                pl.BlockSpec((tm, GLA_VAL_WIDTH), row)] + [full(a) for a in args[3:]]
    kspec = pl.BlockSpec((TOP_K, tm), lambda i: (0, i))
    return pl.pallas_call(
        _outproj_kernel, grid=(t // tm,), in_specs=in_specs,
        out_specs=(pl.BlockSpec((tm, d), row), pl.BlockSpec((tm, d), row), kspec, kspec, kspec,
                   pl.BlockSpec((N_EXPERTS, LANES), const)),
        out_shape=(jax.ShapeDtypeStruct((t, d), F32), jax.ShapeDtypeStruct((t, d), BF16),
                   jax.ShapeDtypeStruct((TOP_K, t), jnp.int32), jax.ShapeDtypeStruct((TOP_K, t), F32),
                   jax.ShapeDtypeStruct((TOP_K, t), jnp.int32), jax.ShapeDtypeStruct((N_EXPERTS, LANES), jnp.int32)),
        scratch_shapes=[pltpu.VMEM((N_EXPERTS, LANES), F32)],
        compiler_params=_params("arbitrary"), name="outproj",
    )(*args)


PAIR_CHUNK = 2 * LANES


def _moe_kernel(be_ref, nused_ref, x_ref, w1_ref, b1_ref, w2_ref, b2_ref, o_ref, w1s_ref, w2s_ref):
    i = pl.program_id(0)
    f = w2_ref.shape[1]

    @pl.when((i == 0) | (be_ref[i] != be_ref[jnp.maximum(i - 1, 0)]))
    def _():
        r = lax.broadcasted_iota(jnp.int32, (PAIR_CHUNK, PAIR_CHUNK), 0)
        c = lax.broadcasted_iota(jnp.int32, (PAIR_CHUNK, PAIR_CHUNK), 1)
        src_col = jnp.where(c < LANES, 2 * c, 2 * (c - LANES) + 1)
        perm = jnp.where(r == src_col, 1.0, 0.0).astype(BF16)
        for cc in range(2 * f // PAIR_CHUNK):
            chunk = w1_ref[0, :, cc * PAIR_CHUNK:(cc + 1) * PAIR_CHUNK].astype(BF16)
            res = _dot(chunk, perm).astype(BF16)
            w1s_ref[:, cc * LANES:(cc + 1) * LANES] = res[:, :LANES]
            w1s_ref[:, f + cc * LANES:f + (cc + 1) * LANES] = res[:, LANES:]
        w2s_ref[...] = w2_ref[0].astype(BF16)

    @pl.when(i < nused_ref[0])
    def _():
        hid = _dot(x_ref[...], w1s_ref[...]) + b1_ref[0]
        glu = jnp.minimum(hid[:, :f], SWIGLU_LIMIT)
        lin = jnp.clip(hid[:, f:], -SWIGLU_LIMIT, SWIGLU_LIMIT)
        act = glu * jax.nn.sigmoid(SWIGLU_ALPHA * glu) * (lin + 1.0)
        o_ref[...] = (_dot(act.astype(BF16), w2s_ref[...]) + b2_ref[0]).astype(o_ref.dtype)

    @pl.when(i >= nused_ref[0])
    def _():
        o_ref[...] = jnp.zeros_like(o_ref)


def _moe(x_rows, block_expert, n_used, w1, b1, w2, b2, *, tm):
    n_rows, d = x_rows.shape
    e, _, f2 = w1.shape
    f = f2 // 2
    b1p = jnp.concatenate([b1[:, 0::2], b1[:, 1::2]], axis=-1).reshape(e, 1, f2)
    b2r = b2.reshape(e, 1, d)
    grid_spec = pltpu.PrefetchScalarGridSpec(
        num_scalar_prefetch=2,
        grid=(n_rows // tm,),
        in_specs=[
            pl.BlockSpec((tm, d), lambda i, be, nu: (i, 0)),
            pl.BlockSpec((1, d, f2), lambda i, be, nu: (be[i], 0, 0)),
            pl.BlockSpec((1, 1, f2), lambda i, be, nu: (be[i], 0, 0)),
            pl.BlockSpec((1, f, d), lambda i, be, nu: (be[i], 0, 0)),
            pl.BlockSpec((1, 1, d), lambda i, be, nu: (be[i], 0, 0)),
        ],
        out_specs=pl.BlockSpec((tm, d), lambda i, be, nu: (i, 0)),
        scratch_shapes=[pltpu.VMEM((d, f2), BF16), pltpu.VMEM((f, d), BF16)],
    )
    return pl.pallas_call(
        _moe_kernel, grid_spec=grid_spec, out_shape=jax.ShapeDtypeStruct((n_rows, d), BF16),
        compiler_params=_params("arbitrary"), name="moe",
    )(block_expert, n_used, x_rows, w1, b1p, w2, b2r)


ROUTE_CHUNK = 2 * LANES


def _route_tile(x, idx_ref, gate_ref, rank_ref, cnt_ref, carry_ref):
    @pl.when(pl.program_id(0) == 0)
    def _():
        carry_ref[...] = jnp.zeros_like(carry_ref)

    ne, tt = x.shape
    eidx = lax.broadcasted_iota(jnp.int32, (ne, tt), 0)
    member = jnp.zeros((ne, tt), F32)
    vals, ids = [], []
    for _ in range(TOP_K):
        m = jnp.max(x, axis=0, keepdims=True)
        sel = jnp.min(jnp.where(x == m, eidx, ne), axis=0, keepdims=True)
        hit = eidx == sel
        member = jnp.where(hit, 1.0, member)
        x = jnp.where(hit, -jnp.inf, x)
        vals.append(m)
        ids.append(sel)
    e = jnp.exp(jnp.concatenate(vals, axis=0) - vals[0])
    gate_ref[...] = e / jnp.sum(e, axis=0, keepdims=True)
    idx_ref[...] = jnp.concatenate(ids, axis=0)
    r = lax.broadcasted_iota(jnp.int32, (ROUTE_CHUNK, ROUTE_CHUNK), 0)
    c = lax.broadcasted_iota(jnp.int32, (ROUTE_CHUNK, ROUTE_CHUNK), 1)
    earlier = jnp.where(r < c, 1.0, 0.0).astype(BF16)
    carry = carry_ref[:, :1]
    before = []
    for ch in range(tt // ROUTE_CHUNK):
        mc = member[:, ch * ROUTE_CHUNK:(ch + 1) * ROUTE_CHUNK]
        before.append(_dot(mc.astype(BF16), earlier) + carry)
        carry = carry + jnp.sum(mc, axis=1, keepdims=True)
    before = jnp.concatenate(before, axis=1)
    ranks = [jnp.sum(jnp.where(eidx == ids[k], before, 0.0), axis=0, keepdims=True) for k in range(TOP_K)]
    rank_ref[...] = jnp.concatenate(ranks, axis=0).astype(jnp.int32)
    carry_ref[...] = jnp.broadcast_to(carry, carry_ref.shape)
    cnt_ref[...] = jnp.broadcast_to(carry, cnt_ref.shape).astype(jnp.int32)


def _route(idx, gates, rank, cnt, *, tm):
    ne, t = cnt.shape[0], idx.shape[1]
    counts = cnt[:, 0]
    padded = (counts + tm - 1) // tm * tm
    pad_end = jnp.cumsum(padded)
    pad_start = pad_end - padded
    experts = jnp.arange(ne, dtype=jnp.int32).reshape(ne, 1, 1)
    slot = jnp.sum(jnp.where(idx[None] == experts, pad_start.reshape(ne, 1, 1), 0), axis=0) + rank
    n_blocks = -(-TOP_K * t // tm) + ne
    flat_slot = slot.reshape(-1)
    tok = jnp.broadcast_to(jnp.arange(t, dtype=jnp.int32), (TOP_K, t)).reshape(-1)
    row_tok = (jnp.arange(n_blocks * tm, dtype=jnp.int32) % t).at[flat_slot].add(
        tok - flat_slot % t, unique_indices=True, mode="promise_in_bounds")
    n_used = pad_end[-1] // tm
    blk_start = jnp.minimum(jnp.arange(n_blocks, dtype=jnp.int32), n_used - 1) * tm
    block_expert = jnp.minimum(jnp.sum(pad_end[:, None] <= blk_start[None, :], axis=0), ne - 1)
    return gates, row_tok, slot, block_expert.astype(jnp.int32), n_used.reshape(1).astype(jnp.int32)


def _final_kernel(h_ref, y_ref, gate_ref, p_ref, wp_ref, gp_ref, wgate_ref, gfin_ref, o_ref, *, last_layer):
    h = h_ref[...]
    g = gate_ref[...]
    gates = jnp.concatenate([g, jnp.zeros((SUBLANES - TOP_K, g.shape[1]), F32)], axis=0).T
    for k in range(TOP_K):
        h = h + gates[:, k:k + 1] * y_ref[k].astype(F32)
    ple = _rms(_dot(p_ref[...].astype(BF16), wp_ref[...]), gp_ref[...])
    h = h + jax.nn.sigmoid(_dot(h.astype(BF16), wgate_ref[...])) * ple
    o_ref[...] = _rms(h, gfin_ref[...]) if last_layer else h


def _final(h1, y_tok, gates, p2d, w_ple_proj, g_ple, w_ple_gate, g_final, *, last_layer, tm=FINAL_ROWS):
    t, d = h1.shape
    row = lambda i: (i, 0)
    const = lambda i: (0, 0)
    full = lambda a: pl.BlockSpec(a.shape, const)
    args = (h1, y_tok, gates, p2d, w_ple_proj.astype(BF16), g_ple.reshape(1, d), w_ple_gate.astype(BF16),
            g_final.reshape(1, d))
    in_specs = [pl.BlockSpec((tm, d), row), pl.BlockSpec((TOP_K, tm, d), lambda i: (0, i, 0)),
                pl.BlockSpec((TOP_K, tm), lambda i: (0, i)), pl.BlockSpec((tm, p2d.shape[1]), row)]
    in_specs += [full(a) for a in args[4:]]
    return pl.pallas_call(
        functools.partial(_final_kernel, last_layer=last_layer), grid=(t // tm,), in_specs=in_specs,
        out_specs=pl.BlockSpec((tm, d), row),
        out_shape=jax.ShapeDtypeStruct((t, d), F32),
        compiler_params=_params("arbitrary"), name="final",
    )(*args)


def kernel(x, p, g_mix, w_in, w_gla_gate, b_gla_gate, g_gla_norm, w_out, g_ffn, w_router, b_router,
           w1, b1, w2, b2, w_ple_gate, w_ple_proj, g_ple, g_final):
    batch, seq, d = x.shape
    depth = p.shape[0]
    t = batch * seq
    h = x.reshape(t, d)
    for i in range(depth):
        qa, ka, vt, kmean, qg, kg, vg, rg, la = _inproj(h, g_mix[i], w_in[i], w_gla_gate[i], b_gla_gate[i],
                                                        batch=batch, seq=seq)
        o_moba = _moba(qa, ka, vt, kmean, batch=batch, seq=seq)
        o_gla = _gla(qg, kg, vg, rg, la, g_gla_norm[i], batch=batch, seq=seq)
        h1, f, *routed = _outproj(h, o_moba, o_gla, w_out[i], g_ffn[i], w_router[i], b_router[i])
        gates, row_tok, slot, block_expert, n_used = _route(*routed, tm=MOE_ROWS)
        x_rows = jnp.take(f, row_tok, axis=0, mode="clip")
        y = _moe(x_rows, block_expert, n_used, w1[i], b1[i], w2[i], b2[i], tm=MOE_ROWS)
        y_tok = jnp.take(y, slot, axis=0, mode="clip")
        h = _final(h1, y_tok, gates, p[i].reshape(t, -1), w_ple_proj[i], g_ple[i], w_ple_gate[i],
                   g_final, last_layer=i == depth - 1)
    return h.reshape(batch, seq, d)
```

```python
import functools

import jax
import jax.numpy as jnp
import numpy as np
from jax import lax
from jax.experimental import pallas as pl
from jax.experimental.pallas import tpu as pltpu

HEAD_DIM = 64
MOBA_HEADS = 8
MOBA_WIDTH = MOBA_HEADS * HEAD_DIM
MOBA_BLOCK = 256
MOBA_TOPK = 3
MOBA_GROUP = 4
MOBA_QBLOCKS = 4
ALIBI_MAX = 8.0
GLA_HEADS = 4
GLA_DK = 64
GLA_DV = 128
GLA_KEY_WIDTH = GLA_HEADS * GLA_DK
GLA_VAL_WIDTH = GLA_HEADS * GLA_DV
GLA_GATE_RANK = 16
GLA_GATE_TAU = 16.0
GLA_CHUNK = 64
GLA_SEQS = 4
N_EXPERTS = 32
TOP_K = 4
SWIGLU_ALPHA = 1.702
SWIGLU_LIMIT = 7.0
EPS = 1e-6

LANES = 128
SUBLANES = 8
BF16_SUBLANES = 16

INPROJ_ROWS = 512
OUTPROJ_ROWS = 1024
FINAL_ROWS = 1024
GLA_TILE = 256
MOE_ROWS = 512
MASKED = -1e30
VMEM_LIMIT = 56 * 1024 * 1024

BF16 = jnp.bfloat16
F32 = jnp.float32

_NT = (((1,), (1,)), ((), ()))
_TN = (((0,), (0,)), ((), ()))


def _dot(a, b, dims=None):
    if dims is None:
        return jnp.dot(a, b, preferred_element_type=F32)
    return lax.dot_general(a, b, dims, preferred_element_type=F32)


def _rms(x, g):
    return x * lax.rsqrt(jnp.mean(x * x, axis=-1, keepdims=True) + EPS) * g


def _col_reduce(x, pair_op, final_op, rows=64):
    parts = [x[i:i + rows] for i in range(0, x.shape[0], rows)]
    while len(parts) > 1:
        parts = [pair_op(parts[i], parts[i + 1]) for i in range(0, len(parts), 2)]
    return final_op(parts[0], axis=0, keepdims=True)


def _params(*sem):
    return pltpu.CompilerParams(dimension_semantics=sem, vmem_limit_bytes=VMEM_LIMIT)


def _inproj_kernel(x_ref, g_ref, w_ref, wvt_ref, wag_ref, wgate_ref, bgate_ref,
                   qa_ref, ka_ref, vt_ref, km_ref, qg_ref, kg_ref, vg_ref, rg_ref, la_ref):
    i = pl.program_id(1)
    tm = x_ref.shape[0]
    nblk = tm // MOBA_BLOCK
    @pl.when(i == 0)
    def _():
        km_ref[...] = jnp.zeros_like(km_ref)

    ab = _rms(x_ref[...], g_ref[...]).astype(BF16)
    ag = _dot(ab, wag_ref[...]).astype(BF16)
    z = _dot(ag, wgate_ref[...]) + bgate_ref[...]
    la_ref[...] = jax.nn.log_sigmoid(z) / GLA_GATE_TAU
    qa_ref[...] = (_dot(ab, w_ref[:, :MOBA_WIDTH]) * (HEAD_DIM ** -0.5)).astype(BF16)
    ka = _dot(ab, w_ref[:, MOBA_WIDTH:2 * MOBA_WIDTH])
    ka_ref[...] = ka.astype(BF16)
    km = km_ref[...]
    blk_row = lax.broadcasted_iota(jnp.int32, km.shape, 0)
    for j in range(nblk):
        mean_j = jnp.mean(ka[j * MOBA_BLOCK:(j + 1) * MOBA_BLOCK], axis=0, keepdims=True)
        km = jnp.where(blk_row == i * nblk + j, mean_j, km)
    km_ref[...] = km
    vt = _dot(wvt_ref[...], ab, _NT).astype(BF16)
    for j in range(nblk):
        vt_ref[j] = vt[:, j * MOBA_BLOCK:(j + 1) * MOBA_BLOCK]
    gla_cols = 2 * GLA_KEY_WIDTH + 2 * GLA_VAL_WIDTH
    gla = _dot(ab, w_ref[:, 3 * MOBA_WIDTH:3 * MOBA_WIDTH + gla_cols])
    o = 0
    for ref in (qg_ref, kg_ref, vg_ref, rg_ref):
        w = ref.shape[1]
        ref[...] = gla[:, o:o + w].astype(ref.dtype)
        o += w


def _inproj(x2d, g_mix, w_in, w_gla_gate, b_gla_gate, *, batch, seq, tm=INPROJ_ROWS):
    t, d = x2d.shape
    nb = seq // MOBA_BLOCK
    o_v, o_qg = 2 * MOBA_WIDTH, 3 * MOBA_WIDTH
    o_ag = o_qg + 2 * GLA_KEY_WIDTH + 2 * GLA_VAL_WIDTH
    wb = w_in.astype(BF16)
    wvt = wb[:, o_v:o_qg].T
    wag = jnp.pad(wb[:, o_ag:], ((0, 0), (0, LANES - GLA_GATE_RANK)))
    wgate = jnp.pad(w_gla_gate, ((0, LANES - GLA_GATE_RANK), (0, 0))).astype(BF16)
    steps = seq // tm
    row = lambda b, i: (b * steps + i, 0)
    const = lambda b, i: (0, 0)
    full = lambda a: pl.BlockSpec(a.shape, const)
    args = (x2d, g_mix.reshape(1, d), wb, wvt, wag, wgate, b_gla_gate.reshape(1, -1))
    in_specs = [pl.BlockSpec((tm, d), row)] + [full(a) for a in args[1:]]
    out_shape = (
        jax.ShapeDtypeStruct((t, MOBA_WIDTH), BF16),
        jax.ShapeDtypeStruct((t, MOBA_WIDTH), BF16),
        jax.ShapeDtypeStruct((t // MOBA_BLOCK, MOBA_WIDTH, MOBA_BLOCK), BF16),
        jax.ShapeDtypeStruct((batch * nb, MOBA_WIDTH), F32),
        jax.ShapeDtypeStruct((t, GLA_KEY_WIDTH), F32),
        jax.ShapeDtypeStruct((t, GLA_KEY_WIDTH), F32),
        jax.ShapeDtypeStruct((t, GLA_VAL_WIDTH), BF16),
        jax.ShapeDtypeStruct((t, GLA_VAL_WIDTH), F32),
        jax.ShapeDtypeStruct((t, GLA_KEY_WIDTH), F32),
    )
    out_specs = (
        pl.BlockSpec((tm, MOBA_WIDTH), row),
        pl.BlockSpec((tm, MOBA_WIDTH), row),
        pl.BlockSpec((tm // MOBA_BLOCK, MOBA_WIDTH, MOBA_BLOCK), lambda b, i: (b * steps + i, 0, 0)),
        pl.BlockSpec((nb, MOBA_WIDTH), lambda b, i: (b, 0)),
        pl.BlockSpec((tm, GLA_KEY_WIDTH), row),
        pl.BlockSpec((tm, GLA_KEY_WIDTH), row),
        pl.BlockSpec((tm, GLA_VAL_WIDTH), row),
        pl.BlockSpec((tm, GLA_VAL_WIDTH), row),
        pl.BlockSpec((tm, GLA_KEY_WIDTH), row),
    )
    return pl.pallas_call(
        _inproj_kernel, grid=(batch, steps), in_specs=in_specs, out_specs=out_specs, out_shape=out_shape,
        compiler_params=_params("arbitrary", "arbitrary"), name="inproj",
    )(*args)


def _moba_kernel(slopes_ref, q_ref, k_ref, vt_ref, km_ref, o_ref, kaug_ref, s_ref):
    hg = pl.program_id(1)
    step = pl.program_id(2)
    blk = MOBA_BLOCK
    nb = km_ref.shape[0]
    hw = 2 * HEAD_DIM
    heads = q_ref.shape[1] // HEAD_DIM
    n_q = q_ref.shape[0] // blk
    feat_rows = nb + SUBLANES
    head_rows = lambda g: slice(g * HEAD_DIM, (g + 1) * HEAD_DIM)
    hs = range(heads)

    @pl.when(step == 0)
    def _():
        lane = lax.broadcasted_iota(jnp.int32, (blk, hw), 1)
        koff = lax.broadcasted_iota(jnp.int32, (blk, hw), 0).astype(F32)
        for g in hs:
            tile, hh = divmod(g, 2)
            slope = slopes_ref[hg * heads + g]
            base = HEAD_DIM * (1 - hh)
            for jb in range(nb):
                feat = jnp.where(lane == base + nb, slope * koff, 0.0)
                feat = jnp.where((lane == base + jb) | (lane == base + nb + 1), 1.0, feat)
                kb = k_ref[jb * blk:(jb + 1) * blk, tile * hw:(tile + 1) * hw]
                kaug_ref[g, jb * blk:(jb + 1) * blk, :] = jnp.where(lane // HEAD_DIM == hh, kb, feat.astype(BF16))

    kidx = lax.broadcasted_iota(jnp.int32, (blk, blk), 0)
    qidx = lax.broadcasted_iota(jnp.int32, (blk, blk), 1)
    causal = kidx <= qidx
    nidx = lax.broadcasted_iota(jnp.int32, (nb, blk), 0)
    r8 = lax.broadcasted_iota(jnp.int32, (SUBLANES, blk), 0)
    qoff = lax.broadcasted_iota(jnp.int32, (SUBLANES, blk), 1).astype(F32)
    km_lane = lax.broadcasted_iota(jnp.int32, km_ref.shape, 1)
    km = km_ref[...]
    slopes = [slopes_ref[hg * heads + g] for g in hs]
    ones = jnp.ones((BF16_SUBLANES, blk), BF16)
    fill = jnp.zeros((HEAD_DIM - feat_rows, blk), F32)

    def values_and_ones(vt_blk, g):
        return jnp.concatenate([vt_blk[head_rows(g), :], ones], axis=0)

    def scores(q_past, pair, buf, slot):
        rows = pl.ds(pl.multiple_of(pair * 2 * blk, 2 * blk), 2 * blk)
        tops = []
        for g in hs:
            s = _dot(kaug_ref[g, rows, :], q_past[g])
            s_ref[buf, slot, g] = s
            tops.append(_col_reduce(s, jnp.maximum, jnp.max))
        return tops

    def absorb_head(pair, buf, slot, top, m_i, acc, g):
        m_new = jnp.maximum(m_i, top)
        p = jnp.exp((s_ref[buf, slot, g] - m_new).astype(BF16))
        acc = (jnp.exp(m_i - m_new) * acc + _dot(values_and_ones(vt_ref[2 * pair], g), p[:blk])
               + _dot(values_and_ones(vt_ref[2 * pair + 1], g), p[blk:]))
        return [m_new, acc]

    def absorb(pair, buf, slot, tops, carry):
        out = []
        for g in hs:
            out += absorb_head(pair, buf, slot, tops[g], carry[2 * g], carry[2 * g + 1], g)
        return out

    def prologues():
        qs = range(n_q)
        cs = [step * n_q + i for i in qs]
        qts = [q_ref[i * blk:(i + 1) * blk, :].astype(F32).T for i in qs]
        qtbs = [qt.astype(BF16) for qt in qts]
        kmh = [jnp.where(km_lane // HEAD_DIM == g, km, 0.0).astype(BF16) for g in hs]
        gates = [[jnp.where(nidx < cs[i], _dot(kmh[g], qtbs[i]), -jnp.inf) for g in hs] for i in qs]
        ranks = [[jnp.zeros((nb, blk), jnp.int32) for _ in hs] for _ in qs]
        for m in range(nb - 1):
            for i in qs:
                for g in hs:
                    gm = gates[i][g][m:m + 1, :]
                    beats = (gm > gates[i][g]) | ((gm == gates[i][g]) & (m < nidx))
                    ranks[i][g] = ranks[i][g] + beats.astype(jnp.int32)

        def q_aug(i, g, bias):
            dist0 = -slopes[g] * ((cs[i] - nidx) * blk).astype(F32)
            tail = jnp.where(r8 == 0, 1.0, jnp.where(r8 == 1, -slopes[g] * qoff, 0.0))
            feats = [jnp.where(bias, dist0, MASKED), tail, fill]
            qh = qts[i][head_rows(g), :]
            return jnp.concatenate([qh] + feats if g % 2 == 0 else feats + [qh], axis=0).astype(BF16)

        q_past = [[q_aug(i, g, (nidx < cs[i]) & (ranks[i][g] < MOBA_TOPK)) for g in hs] for i in qs]
        q_own = [[q_aug(i, g, nidx == cs[i]) for g in hs] for i in qs]
        own_rows = [pl.ds(pl.multiple_of(c * blk, blk), blk) for c in cs]
        s_own = [[jnp.where(causal, _dot(kaug_ref[g, own_rows[i], :], q_own[i][g]), MASKED) for g in hs] for i in qs]
        m_own = [[jnp.max(s, axis=0, keepdims=True) for s in row] for row in s_own]
        p_own = [[jnp.exp((s_own[i][g] - m_own[i][g]).astype(BF16)) for g in hs] for i in qs]
        started = []
        for i in qs:
            state = []
            for g in hs:
                state += [m_own[i][g], _dot(values_and_ones(vt_ref[cs[i]], g), p_own[i][g])]
            started.append((cs[i], q_past[i], state))
        return [(c, q_past_i, state + scores(q_past_i, 0, i, 0)) for i, (c, q_past_i, state) in enumerate(started)]

    def attend(i, c, q_past, carry):
        n_state = 2 * heads

        def advance(n, carry, slot):
            tops = scores(q_past, n + 1, i, 1 - slot)
            return tuple(absorb(n, i, slot, carry[n_state:], carry[:n_state]) + tops)

        def body(n, carry):
            return lax.cond(n % 2 == 0, lambda: advance(n, carry, 0), lambda: advance(n, carry, 1))

        last = jnp.maximum((c + 1) // 2 - 1, 0)
        return last, lax.fori_loop(0, last, body, tuple(carry))

    started = prologues()
    looped = [attend(i, c, q_past, carry) for i, (c, q_past, carry) in enumerate(started)]
    accs = [[None] * heads for _ in looped]
    for g in hs:
        for i, (last, carry) in enumerate(looped):
            accs[i][g] = absorb_head(last, i, last % 2, carry[2 * heads + g], carry[2 * g], carry[2 * g + 1], g)[1]
    for i in range(n_q):
        out_t = jnp.concatenate([a[:HEAD_DIM] / a[HEAD_DIM:HEAD_DIM + 1] for a in accs[i]], axis=0)
        o_ref[i * blk:(i + 1) * blk, :] = out_t.T.astype(o_ref.dtype)


def _alibi_slopes():
    slopes = np.exp2(-ALIBI_MAX * np.arange(1, MOBA_HEADS + 1, dtype=np.float32) / MOBA_HEADS).astype(np.float32)
    worst = slopes * np.float32(MOBA_BLOCK - 1)
    assert np.all(worst.astype(jnp.bfloat16).astype(np.float32) == worst), "ALiBi slopes not bf16-exact"
    return slopes


def _moba(qa, ka, vt, kmean, *, batch, seq):
    t = qa.shape[0]
    nb = seq // MOBA_BLOCK
    gw = MOBA_GROUP * HEAD_DIM
    assert nb % SUBLANES == 0 and nb + SUBLANES <= HEAD_DIM, "bias features must fit the spare contraction lanes"
    grid_spec = pltpu.PrefetchScalarGridSpec(
        num_scalar_prefetch=1,
        grid=(batch, MOBA_HEADS // MOBA_GROUP, nb // MOBA_QBLOCKS),
        in_specs=[
            pl.BlockSpec((MOBA_QBLOCKS * MOBA_BLOCK, gw), lambda b, h, c, s: (b * (nb // MOBA_QBLOCKS) + c, h)),
            pl.BlockSpec((seq, gw), lambda b, h, c, s: (b, h)),
            pl.BlockSpec((nb, gw, MOBA_BLOCK), lambda b, h, c, s: (b, h, 0)),
            pl.BlockSpec((nb, gw), lambda b, h, c, s: (b, h)),
        ],
        out_specs=pl.BlockSpec((MOBA_QBLOCKS * MOBA_BLOCK, gw), lambda b, h, c, s: (b * (nb // MOBA_QBLOCKS) + c, h)),
        scratch_shapes=[pltpu.VMEM((MOBA_GROUP, seq, 2 * HEAD_DIM), BF16),
                        pltpu.VMEM((MOBA_QBLOCKS, 2, MOBA_GROUP, 2 * MOBA_BLOCK, MOBA_BLOCK), F32)],
    )
    return pl.pallas_call(
        _moba_kernel, grid_spec=grid_spec, out_shape=jax.ShapeDtypeStruct((t, MOBA_WIDTH), BF16),
        compiler_params=_params("arbitrary", "arbitrary", "arbitrary"), name="moba",
    )(jnp.asarray(_alibi_slopes()), qa, ka, vt, kmean)


def _gla_kernel(q_ref, k_ref, v_ref, r_ref, la_ref, gn_ref, o_ref, st_ref):
    @pl.when(pl.program_id(1) == 0)
    def _():
        st_ref[...] = jnp.zeros_like(st_ref)

    ch = GLA_CHUNK
    n_seqs, tc = q_ref.shape[:2]
    n_chunks = tc // ch
    row = lax.broadcasted_iota(jnp.int32, (tc, tc), 0)
    col = lax.broadcasted_iota(jnp.int32, (tc, tc), 1)
    causal = (row // ch == col // ch) & (row >= col)
    tri = jnp.where(causal, 1.0, 0.0).astype(BF16)
    lane = lax.broadcasted_iota(jnp.int32, (1, GLA_KEY_WIDTH), 1)
    gn = gn_ref[...]
    seqs = range(n_seqs)
    heads = range(GLA_HEADS)
    gs = []
    for s in seqs:
        la = la_ref[s]
        la_hi = la.astype(BF16)
        rest = la - la_hi.astype(F32)
        la_mid = rest.astype(BF16)
        la_lo = (rest - la_mid.astype(F32)).astype(BF16)
        gs.append(_dot(tri, la_hi) + _dot(tri, la_mid) + _dot(tri, la_lo))
    ends = [[g[(ci + 1) * ch - 1:(ci + 1) * ch, :] for ci in range(n_chunks)] for g in gs]
    qhs, kds, kths, vhs = [], [], [], []
    for s in seqs:
        g = gs[s]
        g_last = jnp.concatenate([jnp.broadcast_to(e, (ch, GLA_KEY_WIDTH)) for e in ends[s]], axis=0)
        qd = q_ref[s] * (GLA_DK ** -0.5) * jnp.exp(g)
        k = k_ref[s]
        kds.append((k * jnp.exp(-g)).astype(BF16))
        kt = k * jnp.exp(g_last - g)
        qhs.append([jnp.where(lane // GLA_DK == h, qd, 0.0).astype(BF16) for h in heads])
        kths.append([jnp.where(lane // GLA_DK == h, kt, 0.0).astype(BF16) for h in heads])
        vhs.append([v_ref[s, :, h * GLA_DV:(h + 1) * GLA_DV] for h in heads])
    intra = [[None] * GLA_HEADS for _ in seqs]
    for h in heads:
        for s in seqs:
            a = jnp.where(causal, _dot(qhs[s][h], kds[s], _NT), 0.0)
            intra[s][h] = _dot(a.astype(BF16), vhs[s][h])
    states = [st_ref[s] for s in seqs]
    inter = [[[] for _ in heads] for _ in seqs]
    for ci in range(n_chunks):
        rows = slice(ci * ch, (ci + 1) * ch)
        for s in seqs:
            state_b = states[s].astype(BF16)
            new_state = states[s] * jnp.exp(ends[s][ci])
            for h in heads:
                inter[s][h].append(_dot(qhs[s][h][rows], state_b, _NT))
                new_state = new_state + _dot(vhs[s][h][rows], kths[s][h][rows], _TN)
            states[s] = new_state
    for s in seqs:
        st_ref[s] = states[s]
    for h in heads:
        for s in seqs:
            o = intra[s][h] + jnp.concatenate(inter[s][h], axis=0)
            rh = r_ref[s, :, h * GLA_DV:(h + 1) * GLA_DV]
            o_ref[s, :, h * GLA_DV:(h + 1) * GLA_DV] = (_rms(o, gn) * (rh * jax.nn.sigmoid(rh))).astype(o_ref.dtype)


def _gla(qg, kg, vg, rg, la, g_norm, *, batch, seq, tc=GLA_TILE):
    n_seqs = GLA_SEQS if batch % GLA_SEQS == 0 else 1
    by_seq = lambda a: a.reshape(batch, seq, a.shape[-1])
    blk = lambda i, j: (i, j, 0)
    kspec = pl.BlockSpec((n_seqs, tc, GLA_KEY_WIDTH), blk)
    vspec = pl.BlockSpec((n_seqs, tc, GLA_VAL_WIDTH), blk)
    out = pl.pallas_call(
        _gla_kernel, grid=(batch // n_seqs, seq // tc),
        in_specs=[kspec, kspec, vspec, vspec, kspec, pl.BlockSpec((1, GLA_DV), lambda i, j: (0, 0))],
        out_specs=vspec, out_shape=jax.ShapeDtypeStruct((batch, seq, GLA_VAL_WIDTH), BF16),
        scratch_shapes=[pltpu.VMEM((n_seqs, GLA_DV, GLA_KEY_WIDTH), F32)],
        compiler_params=_params("arbitrary", "arbitrary"), name="gla",
    )(by_seq(qg), by_seq(kg), by_seq(vg), by_seq(rg), by_seq(la), g_norm.reshape(1, GLA_DV))
    return out.reshape(batch * seq, GLA_VAL_WIDTH)


def _outproj_kernel(x_ref, om_ref, og_ref, wm_ref, wg_ref, gf_ref, wr_ref, br_ref,
                    h_ref, f_ref, idx_ref, gate_ref, rank_ref, cnt_ref, carry_ref):
    h = x_ref[...] + _dot(om_ref[...], wm_ref[...]) + _dot(og_ref[...], wg_ref[...])
    h_ref[...] = h
    f = _rms(h, gf_ref[...]).astype(BF16)
    f_ref[...] = f
    logits = _dot(wr_ref[...], f, _NT) + br_ref[...]
    _route_tile(logits, idx_ref, gate_ref, rank_ref, cnt_ref, carry_ref)


def _outproj(x2d, o_moba, o_gla, w_out, g_ffn, w_router, b_router, *, tm=OUTPROJ_ROWS):
    t, d = x2d.shape
    wm = w_out[:MOBA_WIDTH].astype(BF16)
    wg = w_out[MOBA_WIDTH:].astype(BF16)
    wr = w_router.T.astype(BF16)
    br = b_router.reshape(N_EXPERTS, 1)
    row = lambda i: (i, 0)
    const = lambda i: (0, 0)
    full = lambda a: pl.BlockSpec(a.shape, const)
    args = (x2d, o_moba, o_gla, wm, wg, g_ffn.reshape(1, d), wr, br)
    in_specs = [pl.BlockSpec((tm, d), row), pl.BlockSpec((tm, MOBA_WIDTH), row),
                pl.BlockSpec((tm, GLA_VAL_WIDTH), row)] + [full(a) for a in args[3:]]
    kspec = pl.BlockSpec((TOP_K, tm), lambda i: (0, i))
    return pl.pallas_call(
        _outproj_kernel, grid=(t // tm,), in_specs=in_specs,
        out_specs=(pl.BlockSpec((tm, d), row), pl.BlockSpec((tm, d), row), kspec, kspec, kspec,
                   pl.BlockSpec((N_EXPERTS, LANES), const)),
        out_shape=(jax.ShapeDtypeStruct((t, d), F32), jax.ShapeDtypeStruct((t, d), BF16),
                   jax.ShapeDtypeStruct((TOP_K, t), jnp.int32), jax.ShapeDtypeStruct((TOP_K, t), F32),
                   jax.ShapeDtypeStruct((TOP_K, t), jnp.int32), jax.ShapeDtypeStruct((N_EXPERTS, LANES), jnp.int32)),
        scratch_shapes=[pltpu.VMEM((N_EXPERTS, LANES), F32)],
        compiler_params=_params("arbitrary"), name="outproj",
    )(*args)


PAIR_CHUNK = 2 * LANES


def _moe_kernel(be_ref, nused_ref, x_ref, w1_ref, b1_ref, w2_ref, b2_ref, o_ref, w1s_ref, w2s_ref):
    i = pl.program_id(0)
    f = w2_ref.shape[1]

    @pl.when((i == 0) | (be_ref[i] != be_ref[jnp.maximum(i - 1, 0)]))
    def _():
        r = lax.broadcasted_iota(jnp.int32, (PAIR_CHUNK, PAIR_CHUNK), 0)
        c = lax.broadcasted_iota(jnp.int32, (PAIR_CHUNK, PAIR_CHUNK), 1)
        src_col = jnp.where(c < LANES, 2 * c, 2 * (c - LANES) + 1)
        perm = jnp.where(r == src_col, 1.0, 0.0).astype(BF16)
        for cc in range(2 * f // PAIR_CHUNK):
            chunk = w1_ref[0, :, cc * PAIR_CHUNK:(cc + 1) * PAIR_CHUNK].astype(BF16)
            res = _dot(chunk, perm).astype(BF16)
            w1s_ref[:, cc * LANES:(cc + 1) * LANES] = res[:, :LANES]
            w1s_ref[:, f + cc * LANES:f + (cc + 1) * LANES] = res[:, LANES:]
        w2s_ref[...] = w2_ref[0].astype(BF16)

    @pl.when(i < nused_ref[0])
    def _():
        hid = _dot(x_ref[...], w1s_ref[...]) + b1_ref[0]
        glu = jnp.minimum(hid[:, :f], SWIGLU_LIMIT)
        lin = jnp.clip(hid[:, f:], -SWIGLU_LIMIT, SWIGLU_LIMIT)
        act = glu * jax.nn.sigmoid(SWIGLU_ALPHA * glu) * (lin + 1.0)
        o_ref[...] = (_dot(act.astype(BF16), w2s_ref[...]) + b2_ref[0]).astype(o_ref.dtype)

    @pl.when(i >= nused_ref[0])
    def _():
        o_ref[...] = jnp.zeros_like(o_ref)


def _moe(x_rows, block_expert, n_used, w1, b1, w2, b2, *, tm):
    n_rows, d = x_rows.shape
    e, _, f2 = w1.shape
    f = f2 // 2
    b1p = jnp.concatenate([b1[:, 0::2], b1[:, 1::2]], axis=-1).reshape(e, 1, f2)
    b2r = b2.reshape(e, 1, d)
    grid_spec = pltpu.PrefetchScalarGridSpec(
        num_scalar_prefetch=2,
        grid=(n_rows // tm,),
        in_specs=[
            pl.BlockSpec((tm, d), lambda i, be, nu: (jnp.minimum(i, nu[0] - 1), 0)),
            pl.BlockSpec((1, d, f2), lambda i, be, nu: (be[i], 0, 0)),
            pl.BlockSpec((1, 1, f2), lambda i, be, nu: (be[i], 0, 0)),
            pl.BlockSpec((1, f, d), lambda i, be, nu: (be[i], 0, 0)),
            pl.BlockSpec((1, 1, d), lambda i, be, nu: (be[i], 0, 0)),
        ],
        out_specs=pl.BlockSpec((tm, d), lambda i, be, nu: (i, 0)),
        scratch_shapes=[pltpu.VMEM((d, f2), BF16), pltpu.VMEM((f, d), BF16)],
    )
    return pl.pallas_call(
        _moe_kernel, grid_spec=grid_spec, out_shape=jax.ShapeDtypeStruct((n_rows, d), BF16),
        compiler_params=_params("arbitrary"), name="moe",
    )(block_expert, n_used, x_rows, w1, b1p, w2, b2r)


ROUTE_CHUNK = 2 * LANES


def _route_tile(x, idx_ref, gate_ref, rank_ref, cnt_ref, carry_ref):
    @pl.when(pl.program_id(0) == 0)
    def _():
        carry_ref[...] = jnp.zeros_like(carry_ref)

    ne, tt = x.shape
    eidx = lax.broadcasted_iota(jnp.int32, (ne, tt), 0)
    member = jnp.zeros((ne, tt), F32)
    vals, ids = [], []
    for _ in range(TOP_K):
        m = jnp.max(x, axis=0, keepdims=True)
        sel = jnp.min(jnp.where(x == m, eidx, ne), axis=0, keepdims=True)
        hit = eidx == sel
        member = jnp.where(hit, 1.0, member)
        x = jnp.where(hit, -jnp.inf, x)
        vals.append(m)
        ids.append(sel)
    e = jnp.exp(jnp.concatenate(vals, axis=0) - vals[0])
    gate_ref[...] = e / jnp.sum(e, axis=0, keepdims=True)
    idx_ref[...] = jnp.concatenate(ids, axis=0)
    r = lax.broadcasted_iota(jnp.int32, (ROUTE_CHUNK, ROUTE_CHUNK), 0)
    c = lax.broadcasted_iota(jnp.int32, (ROUTE_CHUNK, ROUTE_CHUNK), 1)
    earlier = jnp.where(r < c, 1.0, 0.0).astype(BF16)
    carry = carry_ref[:, :1]
    before = []
    for ch in range(tt // ROUTE_CHUNK):
        mc = member[:, ch * ROUTE_CHUNK:(ch + 1) * ROUTE_CHUNK]
        before.append(_dot(mc.astype(BF16), earlier) + carry)
        carry = carry + jnp.sum(mc, axis=1, keepdims=True)
    before = jnp.concatenate(before, axis=1)
    ranks = [jnp.sum(jnp.where(eidx == ids[k], before, 0.0), axis=0, keepdims=True) for k in range(TOP_K)]
    rank_ref[...] = jnp.concatenate(ranks, axis=0).astype(jnp.int32)
    carry_ref[...] = jnp.broadcast_to(carry, carry_ref.shape)
    cnt_ref[...] = jnp.broadcast_to(carry, cnt_ref.shape).astype(jnp.int32)


def _route(idx, gates, rank, cnt, *, tm):
    ne, t = cnt.shape[0], idx.shape[1]
    counts = cnt[:, 0]
    padded = (counts + tm - 1) // tm * tm
    pad_end = jnp.cumsum(padded)
    pad_start = pad_end - padded
    experts = jnp.arange(ne, dtype=jnp.int32).reshape(ne, 1, 1)
    slot = jnp.sum(jnp.where(idx[None] == experts, pad_start.reshape(ne, 1, 1), 0), axis=0) + rank
    n_blocks = -(-TOP_K * t // tm) + ne
    flat_slot = slot.reshape(-1)
    tok = jnp.broadcast_to(jnp.arange(t, dtype=jnp.int32), (TOP_K, t)).reshape(-1)
    row_tok = (jnp.arange(n_blocks * tm, dtype=jnp.int32) % t).at[flat_slot].add(
        tok - flat_slot % t, unique_indices=True, mode="promise_in_bounds")
    n_used = pad_end[-1] // tm
    blk_start = jnp.minimum(jnp.arange(n_blocks, dtype=jnp.int32), n_used - 1) * tm
    block_expert = jnp.minimum(jnp.sum(pad_end[:, None] <= blk_start[None, :], axis=0), ne - 1)
    return gates, row_tok, slot, block_expert.astype(jnp.int32), n_used.reshape(1).astype(jnp.int32)


def _final_kernel(h_ref, y_ref, gate_ref, p_ref, wp_ref, gp_ref, wgate_ref, gfin_ref, o_ref, *, last_layer):
    h = h_ref[...]
    g = gate_ref[...]
    gates = jnp.concatenate([g, jnp.zeros((SUBLANES - TOP_K, g.shape[1]), F32)], axis=0).T
    for k in range(TOP_K):
        h = h + gates[:, k:k + 1] * y_ref[k].astype(F32)
    ple = _rms(_dot(p_ref[...].astype(BF16), wp_ref[...]), gp_ref[...])
    h = h + jax.nn.sigmoid(_dot(h.astype(BF16), wgate_ref[...])) * ple
    o_ref[...] = _rms(h, gfin_ref[...]) if last_layer else h


def _final(h1, y_tok, gates, p2d, w_ple_proj, g_ple, w_ple_gate, g_final, *, last_layer, tm=FINAL_ROWS):
    t, d = h1.shape
    row = lambda i: (i, 0)
    const = lambda i: (0, 0)
    full = lambda a: pl.BlockSpec(a.shape, const)
    args = (h1, y_tok, gates, p2d, w_ple_proj.astype(BF16), g_ple.reshape(1, d), w_ple_gate.astype(BF16),
            g_final.reshape(1, d))
    in_specs = [pl.BlockSpec((tm, d), row), pl.BlockSpec((TOP_K, tm, d), lambda i: (0, i, 0)),
                pl.BlockSpec((TOP_K, tm), lambda i: (0, i)), pl.BlockSpec((tm, p2d.shape[1]), row)]
    in_specs += [full(a) for a in args[4:]]
    return pl.pallas_call(
        functools.partial(_final_kernel, last_layer=last_layer), grid=(t // tm,), in_specs=in_specs,
        out_specs=pl.BlockSpec((tm, d), row),
        out_shape=jax.ShapeDtypeStruct((t, d), F32),
        compiler_params=_params("arbitrary"), name="final",
    )(*args)


def kernel(x, p, g_mix, w_in, w_gla_gate, b_gla_gate, g_gla_norm, w_out, g_ffn, w_router, b_router,
           w1, b1, w2, b2, w_ple_gate, w_ple_proj, g_ple, g_final):
    batch, seq, d = x.shape
    depth = p.shape[0]
    t = batch * seq
    h = x.reshape(t, d)
    for i in range(depth):
        qa, ka, vt, kmean, qg, kg, vg, rg, la = _inproj(h, g_mix[i], w_in[i], w_gla_gate[i], b_gla_gate[i],
                                                        batch=batch, seq=seq)
        o_moba = _moba(qa, ka, vt, kmean, batch=batch, seq=seq)
        o_gla = _gla(qg, kg, vg, rg, la, g_gla_norm[i], batch=batch, seq=seq)
        h1, f, *routed = _outproj(h, o_moba, o_gla, w_out[i], g_ffn[i], w_router[i], b_router[i])
        gates, row_tok, slot, block_expert, n_used = _route(*routed, tm=MOE_ROWS)
        x_rows = jnp.take(f, row_tok, axis=0, mode="clip")
        y = _moe(x_rows, block_expert, n_used, w1[i], b1[i], w2[i], b2[i], tm=MOE_ROWS)
        y_tok = jnp.take(y, slot, axis=0, mode="clip")
        h = _final(h1, y_tok, gates, p[i].reshape(t, -1), w_ple_proj[i], g_ple[i], w_ple_gate[i],
                   g_final, last_layer=i == depth - 1)
    return h.reshape(batch, seq, d)
```
